```python
import functools
import jax, jax.numpy as jnp
from jax import lax
import numpy as np

D_MODEL = 1024
BATCH = 8
SEQ = 2048
DEPTH = 1
DEC_BATCH = 128
DEC_SEQ = 1
PAST_LEN = 16384
PAGE_SIZE = 128

N_META = 16
RMS_EPS = 1e-6
MLA_HEADS = 16
Q_LORA = 384
KV_LORA = 256
NOPE_DIM = 64
ROPE_DIM = 32
V_DIM = D_MODEL // MLA_HEADS
ROPE_THETA = 10000.0
Q_BLOCK = 128
MLA_SCALE = (NOPE_DIM + ROPE_DIM) ** -0.5
NEG_BIG = -1e30
GDN_HEADS = 8
GDN_DK = 128
GDN_DV = D_MODEL // GDN_HEADS
GDN_KEY = GDN_HEADS * GDN_DK
GDN_VAL = GDN_HEADS * GDN_DV
GDN_QKV = 2 * GDN_KEY + GDN_VAL
CONV_W = 4
CHUNK = 64
N_GROUPS = 4
EXPERTS_PER_GROUP = 8
N_EXPERTS = N_GROUPS * EXPERTS_PER_GROUP
TOP_K = 2
D_EXPERT = 256
MOE_BLOCK = 128
OFF_KV = Q_LORA
OFF_QKV = OFF_KV + KV_LORA + ROPE_DIM
OFF_Z = OFF_QKV + GDN_QKV
OFF_B = OFF_Z + GDN_VAL
OFF_A = OFF_B + GDN_HEADS
OFF_GM = OFF_A + GDN_HEADS
OFF_GG = OFF_GM + D_MODEL
D_IN = OFF_GG + D_MODEL
IN_SPLITS = [OFF_KV, OFF_QKV, OFF_Z, OFF_B, OFF_A, OFF_GM, OFF_GG]

kernel_name = 'hybrid_mla_gdn_hmoe_step'


def rms_norm(x, w):
    xf = x.astype(jnp.float32)
    y = xf * lax.rsqrt(jnp.mean(xf * xf, axis=-1, keepdims=True) + RMS_EPS)
    return (y * w.astype(jnp.float32)).astype(x.dtype)


def l2norm(x):
    return x * lax.rsqrt(jnp.sum(x * x, axis=-1, keepdims=True) + RMS_EPS)


def rope_tables(pos):
    inv_freq = ROPE_THETA ** (-jnp.arange(0, ROPE_DIM, 2, dtype=jnp.float32) / ROPE_DIM)
    ang = pos.astype(jnp.float32)[:, None] * inv_freq[None, :]
    return jnp.cos(ang), jnp.sin(ang)


def apply_rope(x, cos, sin):
    x1, x2 = jnp.split(x.astype(jnp.float32), 2, axis=-1)
    return jnp.concatenate([x1 * cos - x2 * sin, x2 * cos + x1 * sin], axis=-1).astype(x.dtype)


def causal_conv(x_pad, conv_w):
    c = x_pad.shape[-1]
    return lax.conv_general_dilated(x_pad, conv_w.astype(x_pad.dtype)[:, None, :], window_strides=(1,),
                                    padding='VALID', dimension_numbers=('NWC', 'WIO', 'NWC'),
                                    feature_group_count=c)


def mla_prompt(q_nope, q_rope, c_kv, k_rope, w_uk, w_uv):
    b, t = c_kv.shape[:2]
    k_nope = jnp.einsum('btc,chn->bthn', c_kv, w_uk)
    v = jnp.einsum('btc,chv->bthv', c_kv, w_uv)
    n_blk = -(-t // Q_BLOCK)
    pad = n_blk * Q_BLOCK - t
    qn = jnp.pad(q_nope, ((0, 0), (0, pad), (0, 0), (0, 0)))
    qr = jnp.pad(q_rope, ((0, 0), (0, pad), (0, 0), (0, 0)))
    key_pos = jnp.arange(t)

    def query_block(i):
        start = i * Q_BLOCK
        qn_b = lax.dynamic_slice_in_dim(qn, start, Q_BLOCK, axis=1)
        qr_b = lax.dynamic_slice_in_dim(qr, start, Q_BLOCK, axis=1)
        s = (jnp.einsum('bqhn,bkhn->bhqk', qn_b, k_nope).astype(jnp.float32)
             + jnp.einsum('bqhr,bkr->bhqk', qr_b, k_rope).astype(jnp.float32)) * MLA_SCALE
        q_pos = start + jnp.arange(Q_BLOCK)
        s = jnp.where(key_pos[None, :] <= q_pos[:, None], s, NEG_BIG)
        p = jax.nn.softmax(s, axis=-1)
        return jnp.einsum('bhqk,bkhv->bqhv', p.astype(v.dtype), v)

    o = lax.map(query_block, jnp.arange(n_blk))
    o = jnp.swapaxes(o, 0, 1).reshape(b, n_blk * Q_BLOCK, MLA_HEADS, V_DIM)
    return o[:, :t]


def mla_sample(q_nope, q_rope, c_new, kr_new, w_uk, w_uv, cache_kv_latent, cache_k_rope, page_table):
    f32 = jnp.float32
    q_lat = jnp.einsum('bshn,chn->bshc', q_nope, w_uk).astype(f32)
    qr = q_rope.astype(f32)
    db, s_len = q_lat.shape[:2]

    def page_step(carry, pages):
        m, l, acc = carry
        c = cache_kv_latent[pages].astype(f32)
        kr = cache_k_rope[pages].astype(f32)
        s = (jnp.einsum('bshc,bpc->bshp', q_lat, c) + jnp.einsum('bshr,bpr->bshp', qr, kr)) * MLA_SCALE
        m_new = jnp.maximum(m, s.max(-1))
        corr = jnp.exp(m - m_new)
        p = jnp.exp(s - m_new[..., None])
        return (m_new, l * corr + p.sum(-1), acc * corr[..., None] + jnp.einsum('bshp,bpc->bshc', p, c)), None

    init = (jnp.full((db, s_len, MLA_HEADS), NEG_BIG, f32), jnp.zeros((db, s_len, MLA_HEADS), f32),
            jnp.zeros((db, s_len, MLA_HEADS, KV_LORA), f32))
    (m, l, acc), _ = lax.scan(page_step, init, page_table.T)
    cn = c_new.astype(f32)
    s = (jnp.einsum('bshc,btc->bsht', q_lat, cn) + jnp.einsum('bshr,btr->bsht', qr, kr_new.astype(f32))) * MLA_SCALE
    causal = jnp.tril(jnp.ones((s_len, s_len), bool))[:, None, :]
    s = jnp.where(causal, s, NEG_BIG)
    m_new = jnp.maximum(m, s.max(-1))
    corr = jnp.exp(m - m_new)
    p = jnp.exp(s - m_new[..., None])
    l = l * corr + p.sum(-1)
    acc = acc * corr[..., None] + jnp.einsum('bsht,btc->bshc', p, cn)
    o = jnp.einsum('bshc,chv->bshv', acc / l[..., None], w_uv.astype(f32))
    return o.astype(q_nope.dtype)


def gdn_heads(xc):
    xf = jax.nn.silu(xc.astype(jnp.float32))
    b, t = xf.shape[:2]
    q = l2norm(xf[..., :GDN_KEY].reshape(b, t, GDN_HEADS, GDN_DK)) * (GDN_DK ** -0.5)
    k = l2norm(xf[..., GDN_KEY:2 * GDN_KEY].reshape(b, t, GDN_HEADS, GDN_DK))
    v = xf[..., 2 * GDN_KEY:].reshape(b, t, GDN_HEADS, GDN_DV)
    return q, k, v


def gdn_chunk(state, inp):
    q, k, v, g, beta = inp
    c = q.shape[2]
    G = jnp.cumsum(g, axis=-1)
    causal = jnp.tril(jnp.ones((c, c), bool))
    strict = jnp.tril(jnp.ones((c, c), bool), -1)
    decay = jnp.where(causal, jnp.exp(jnp.where(causal, G[..., :, None] - G[..., None, :], 0.0)), 0.0)
    kb = k * beta[..., None]
    m = jnp.where(strict, jnp.einsum('bhik,bhjk->bhij', kb, k) * decay, 0.0) + jnp.eye(c, dtype=q.dtype)
    rhs = jnp.concatenate([v * beta[..., None], kb * jnp.exp(G)[..., None]], axis=-1)
    sol = lax.linalg.triangular_solve(m, rhs, left_side=True, lower=True, unit_diagonal=True)
    u, w = sol[..., :GDN_DV], sol[..., GDN_DV:]
    v_new = u - jnp.einsum('bhck,bhkv->bhcv', w, state)
    attn = jnp.where(causal, jnp.einsum('bhik,bhjk->bhij', q, k) * decay, 0.0)
    out = jnp.einsum('bhck,bhkv->bhcv', q * jnp.exp(G)[..., None], state) + jnp.einsum('bhij,bhjv->bhiv', attn, v_new)
    g_last = G[..., -1]
    state = state * jnp.exp(g_last)[..., None, None] + jnp.einsum(
        'bhck,bhcv->bhkv', k * jnp.exp(g_last[..., None] - G)[..., None], v_new)
    return state, out


def gdn_prompt(qkv, g, beta, conv_w):
    b, t, _ = qkv.shape
    conv_state = qkv[:, t - (CONV_W - 1):]
    q, k, v = gdn_heads(causal_conv(jnp.pad(qkv, ((0, 0), (CONV_W - 1, 0), (0, 0))), conv_w))
    q, k, v = jnp.swapaxes(q, 1, 2), jnp.swapaxes(k, 1, 2), jnp.swapaxes(v, 1, 2)
    g, beta = jnp.swapaxes(g, 1, 2), jnp.swapaxes(beta, 1, 2)
    s0 = jnp.zeros((b, GDN_HEADS, GDN_DK, GDN_DV), jnp.float32)
    s_meta, o_meta = gdn_chunk(s0, (q[:, :, :N_META], k[:, :, :N_META], v[:, :, :N_META],
                                    g[:, :, :N_META], beta[:, :, :N_META]))
    n_chunks = (t - N_META) // CHUNK

    def to_chunks(a):
        a = a[:, :, N_META:]
        return jnp.moveaxis(a.reshape(a.shape[:2] + (n_chunks, CHUNK) + a.shape[3:]), 2, 0)

    s_fin, o_real = lax.scan(gdn_chunk, s_meta, (to_chunks(q), to_chunks(k), to_chunks(v),
                                                 to_chunks(g), to_chunks(beta)))
    o_real = jnp.moveaxis(o_real, 0, 2).reshape(b, GDN_HEADS, n_chunks * CHUNK, GDN_DV)
    o = jnp.concatenate([o_meta, o_real], axis=2)
    return jnp.swapaxes(o, 1, 2), conv_state, s_fin


def gdn_sample(qkv, g, beta, conv_w, conv_state, rec_state):
    xcat = jnp.concatenate([conv_state.astype(qkv.dtype), qkv], axis=1)
    q, k, v = gdn_heads(causal_conv(xcat, conv_w))

    def step(s, inp):
        q_t, k_t, v_t, g_t, b_t = inp
        s = s * jnp.exp(g_t)[..., None, None]
        delta = (v_t - jnp.einsum('bhkv,bhk->bhv', s, k_t)) * b_t[..., None]
        s = s + jnp.einsum('bhk,bhv->bhkv', k_t, delta)
        return s, jnp.einsum('bhkv,bhk->bhv', s, q_t)

    tm = lambda a: jnp.moveaxis(a, 1, 0)
    s_fin, o = lax.scan(step, rec_state.astype(jnp.float32), (tm(q), tm(k), tm(v), tm(g), tm(beta)))
    return jnp.moveaxis(o, 0, 1), xcat[:, xcat.shape[1] - (CONV_W - 1):], s_fin


def hier_route(h, w_group, b_group, w_router, b_router):
    lg = (h @ w_group).astype(jnp.float32) + b_group.astype(jnp.float32)
    grp = jnp.argmax(lg, axis=-1).astype(jnp.int32)
    gate_g = jnp.take_along_axis(jax.nn.softmax(lg, axis=-1), grp[:, None], axis=-1)
    le = ((h @ w_router).astype(jnp.float32) + b_router.astype(jnp.float32)).reshape(
        -1, N_GROUPS, EXPERTS_PER_GROUP)
    le = jnp.take_along_axis(le, grp[:, None, None], axis=1)[:, 0]
    top_v, top_i = lax.top_k(le, TOP_K)
    return grp[:, None] * EXPERTS_PER_GROUP + top_i, gate_g * jax.nn.softmax(top_v, axis=-1)


def moe_experts(h, expert_id, weight, w_gate, w_up, w_down):
    n_tok = h.shape[0]
    n_asg = n_tok * TOP_K
    n_blocks = (n_asg + N_EXPERTS * (MOE_BLOCK - 1) + MOE_BLOCK - 1) // MOE_BLOCK
    n_slots = n_blocks * MOE_BLOCK
    e_flat = expert_id.reshape(n_asg)
    t_flat = jnp.repeat(jnp.arange(n_tok, dtype=jnp.int32), TOP_K)
    w_flat = weight.reshape(n_asg).astype(h.dtype)
    order = jnp.argsort(e_flat)
    e_sorted = e_flat[order]
    counts = jnp.bincount(e_flat, length=N_EXPERTS)
    padded = (counts + MOE_BLOCK - 1) // MOE_BLOCK * MOE_BLOCK
    pad_end = jnp.cumsum(padded)
    dest = (pad_end - padded)[e_sorted] + jnp.arange(n_asg) - (jnp.cumsum(counts) - counts)[e_sorted]
    slot_tok = jnp.zeros((n_slots,), jnp.int32).at[dest].set(t_flat[order])
    slot_w = jnp.zeros((n_slots,), h.dtype).at[dest].set(w_flat[order])
    blk_expert = jnp.minimum(jnp.searchsorted(pad_end, jnp.arange(n_blocks) * MOE_BLOCK, side='right'),
                             N_EXPERTS - 1)

    def expert_block(args):
        tok, e = args
        xb = h[tok]
        return (jax.nn.silu(xb @ w_gate[e]) * (xb @ w_up[e])) @ w_down[e]

    y = lax.map(expert_block, (slot_tok.reshape(n_blocks, MOE_BLOCK), blk_expert))
    y = y.reshape(n_slots, -1) * slot_w[:, None]
    return jnp.zeros_like(h).at[slot_tok].add(y)


def block_forward(x, pos, mla_core, gdn_core, norm_mix, w_in, q_norm, w_uq, kv_norm, w_uk, w_uv,
                  conv_w, a_log, dt_bias, gdn_norm, w_out, norm_ffn, w_group, b_group,
                  w_router, b_router, w_gate, w_up, w_down):
    b, t, _ = x.shape
    hn = rms_norm(x, norm_mix)
    q_down, kv_down, qkv, z, b_raw, a_raw, gm, gg = jnp.split(hn @ w_in, IN_SPLITS, axis=-1)
    cos, sin = rope_tables(pos)
    q = jnp.einsum('btr,rhd->bthd', rms_norm(q_down, q_norm), w_uq)
    q_nope = q[..., :NOPE_DIM]
    q_rope = apply_rope(q[..., NOPE_DIM:], cos[:, None], sin[:, None])
    c_kv = rms_norm(kv_down[..., :KV_LORA], kv_norm)
    k_rope = apply_rope(kv_down[..., KV_LORA:], cos, sin)
    o_mla = mla_core(q_nope, q_rope, c_kv, k_rope, w_uk, w_uv).reshape(b, t, D_MODEL)
    g = -jnp.exp(a_log.astype(jnp.float32)) * jax.nn.softplus(a_raw.astype(jnp.float32) + dt_bias.astype(jnp.float32))
    beta = jax.nn.sigmoid(b_raw.astype(jnp.float32))
    o, conv_state, rec_state = gdn_core(qkv, g, beta, conv_w)
    zf = z.astype(jnp.float32).reshape(b, t, GDN_HEADS, GDN_DV)
    o_gdn = (rms_norm(o, gdn_norm) * jax.nn.silu(zf)).reshape(b, t, D_MODEL).astype(x.dtype)
    x = x + (jax.nn.sigmoid(gm) * o_mla + jax.nn.sigmoid(gg) * o_gdn) @ w_out
    hf = rms_norm(x, norm_ffn).reshape(b * t, D_MODEL)
    eid, ew = hier_route(hf, w_group, b_group, w_router, b_router)
    x = x + moe_experts(hf, eid, ew, w_gate, w_up, w_down).reshape(b, t, D_MODEL)
    return x, c_kv, k_rope, conv_state, rec_state


def setup_inputs(seed: int = 0) -> dict:
    key = jax.random.key(seed)
    ks = jax.random.split(key, 32)
    f32 = jnp.float32

    def nrm(k, shape, scale):
        return jax.random.normal(k, shape, f32) * scale

    def gain(k, shape):
        return 1.0 + 0.02 * jax.random.normal(k, shape, f32)

    n_pages = PAST_LEN // PAGE_SIZE
    n_pool = (DEC_BATCH * n_pages * 5) // 4
    page_table = jax.random.permutation(ks[4], n_pool)[:DEC_BATCH * n_pages].reshape(
        DEC_BATCH, n_pages).astype(jnp.int32)
    dt = jnp.exp(jax.random.uniform(ks[17], (DEPTH, GDN_HEADS), f32, np.log(1e-3), np.log(1e-1)))
    return {
        'x_prompt': nrm(ks[0], (BATCH, SEQ, D_MODEL), 1.0),
        'x_sample': nrm(ks[1], (DEC_BATCH, DEC_SEQ, D_MODEL), 1.0),
        'cache_kv_latent': nrm(ks[2], (DEPTH, n_pool, PAGE_SIZE, KV_LORA), 1.0),
        'cache_k_rope': nrm(ks[3], (DEPTH, n_pool, PAGE_SIZE, ROPE_DIM), 1.0),
        'page_table': page_table,
        'state_conv': nrm(ks[5], (DEPTH, DEC_BATCH, CONV_W - 1, GDN_QKV), 1.0),
        'state_gdn': nrm(ks[6], (DEPTH, DEC_BATCH, GDN_HEADS, GDN_DK, GDN_DV), 0.1),
        'meta_tokens': nrm(ks[7], (N_META, D_MODEL), 1.0),
        'norm_mix': gain(ks[8], (DEPTH, D_MODEL)),
        'w_in': nrm(ks[9], (DEPTH, D_MODEL, D_IN), D_MODEL ** -0.5),
        'q_norm': gain(ks[10], (DEPTH, Q_LORA)),
        'w_uq': nrm(ks[11], (DEPTH, Q_LORA, MLA_HEADS, NOPE_DIM + ROPE_DIM), Q_LORA ** -0.5),
        'kv_norm': gain(ks[12], (DEPTH, KV_LORA)),
        'w_uk': nrm(ks[13], (DEPTH, KV_LORA, MLA_HEADS, NOPE_DIM), KV_LORA ** -0.5),
        'w_uv': nrm(ks[14], (DEPTH, KV_LORA, MLA_HEADS, V_DIM), KV_LORA ** -0.5),
        'conv_w': nrm(ks[15], (DEPTH, CONV_W, GDN_QKV), CONV_W ** -0.5),
        'a_log': jnp.log(jax.random.uniform(ks[16], (DEPTH, GDN_HEADS), f32, 1.0, 16.0)),
        'dt_bias': dt + jnp.log(-jnp.expm1(-dt)),
        'gdn_norm': gain(ks[18], (DEPTH, GDN_DV)),
        'w_out': nrm(ks[19], (DEPTH, D_MODEL, D_MODEL), D_MODEL ** -0.5),
        'norm_ffn': gain(ks[20], (DEPTH, D_MODEL)),
        'w_group': nrm(ks[21], (DEPTH, D_MODEL, N_GROUPS), D_MODEL ** -0.5),
        'b_group': nrm(ks[22], (DEPTH, N_GROUPS), 0.01),
        'w_router': nrm(ks[23], (DEPTH, D_MODEL, N_EXPERTS), D_MODEL ** -0.5),
        'b_router': nrm(ks[24], (DEPTH, N_EXPERTS), 0.01),
        'w_gate': nrm(ks[25], (DEPTH, N_EXPERTS, D_MODEL, D_EXPERT), D_MODEL ** -0.5),
        'w_up': nrm(ks[26], (DEPTH, N_EXPERTS, D_MODEL, D_EXPERT), D_MODEL ** -0.5),
        'w_down': nrm(ks[27], (DEPTH, N_EXPERTS, D_EXPERT, D_MODEL), D_EXPERT ** -0.5),
        'norm_final': gain(ks[28], (D_MODEL,)),
    }


def reference(x_prompt, x_sample, cache_kv_latent, cache_k_rope, page_table, state_conv, state_gdn,
              meta_tokens, norm_mix, w_in, q_norm, w_uq, kv_norm, w_uk, w_uv, conv_w, a_log, dt_bias,
              gdn_norm, w_out, norm_ffn, w_group, b_group, w_router, b_router, w_gate, w_up, w_down,
              norm_final):
    b_p, s_p, _ = x_prompt.shape
    meta = jnp.broadcast_to(meta_tokens.astype(x_prompt.dtype)[None], (b_p, N_META, D_MODEL))
    xp = jnp.concatenate([meta, x_prompt], axis=1)
    xs = x_sample
    pos_p = jnp.arange(N_META + s_p)
    pos_s = PAST_LEN + jnp.arange(x_sample.shape[1])
    ckv_p, kr_p, cv_p, st_p = [], [], [], []
    ckv_s, kr_s, cv_s, st_s = [], [], [], []
    for layer in range(DEPTH):
        lw = (norm_mix[layer], w_in[layer], q_norm[layer], w_uq[layer], kv_norm[layer], w_uk[layer],
              w_uv[layer], conv_w[layer], a_log[layer], dt_bias[layer], gdn_norm[layer], w_out[layer],
              norm_ffn[layer], w_group[layer], b_group[layer], w_router[layer], b_router[layer],
              w_gate[layer], w_up[layer], w_down[layer])
        xp, c1, k1, v1, s1 = block_forward(xp, pos_p, mla_prompt, gdn_prompt, *lw)
        mla_core = functools.partial(mla_sample, cache_kv_latent=cache_kv_latent[layer],
                                     cache_k_rope=cache_k_rope[layer], page_table=page_table)
        gdn_core = functools.partial(gdn_sample, conv_state=state_conv[layer], rec_state=state_gdn[layer])
        xs, c2, k2, v2, s2 = block_forward(xs, pos_s, mla_core, gdn_core, *lw)
        ckv_p.append(c1); kr_p.append(k1); cv_p.append(v1); st_p.append(s1)
        ckv_s.append(c2); kr_s.append(k2); cv_s.append(v2); st_s.append(s2)
    y_prompt = rms_norm(xp, norm_final)[:, N_META:]
    y_sample = rms_norm(xs, norm_final)
    return (y_prompt, y_sample, jnp.stack(ckv_p), jnp.stack(kr_p), jnp.stack(ckv_s), jnp.stack(kr_s),
            jnp.stack(cv_p), jnp.stack(cv_s), jnp.stack(st_p), jnp.stack(st_s))
```

```python
import functools

import jax
import jax.numpy as jnp
from jax import lax
from jax.experimental import pallas as pl
from jax.experimental.pallas import tpu as pltpu

F32 = jnp.float32
BF16 = jnp.bfloat16
HIGHEST = lax.Precision.HIGHEST

D_MODEL = 1024
N_META = 16
RMS_EPS = 1e-6
MLA_HEADS = 16
Q_LORA = 384
KV_LORA = 256
NOPE_DIM = 64
ROPE_DIM = 32
V_DIM = 64
ROPE_THETA = 10000.0
MLA_SCALE = (NOPE_DIM + ROPE_DIM) ** -0.5
PAGE_SIZE = 128
GDN_HEADS = 8
GDN_DK = 128
GDN_DV = 128
GDN_KEY = GDN_HEADS * GDN_DK
GDN_QKV = 3 * GDN_KEY
CONV_W = 4
CHUNK = 64
N_GROUPS = 4
EXPERTS_PER_GROUP = 8
N_EXPERTS = 32
TOP_K = 2
D_EXPERT = 256
MOE_BLOCK = 128

_OFF_KV = Q_LORA
_OFF_QKV = _OFF_KV + KV_LORA + ROPE_DIM
_OFF_Z = _OFF_QKV + GDN_QKV
_OFF_B = _OFF_Z + GDN_KEY
_OFF_A = _OFF_B + GDN_HEADS
_OFF_GM = _OFF_A + GDN_HEADS
_OFF_GG = _OFF_GM + D_MODEL
P_QKV = 0
P_Z = 3072
P_GM = 4096
P_GG = 5120
P_KVC = 6144
P_SMALL = 6400
P_QD = 6528
P_TOTAL = 6912
SM_B = 0
SM_A = 8
SM_KR = 64

LANES = 128
VMEM_LIMIT = 56 * 1024 * 1024
ATTN_TQ = (512, 256, 128)


def _pick(n, candidates):
    for c in candidates:
        if n % c == 0:
            return c
    raise ValueError(f"no tile for {n} in {candidates}")


def _dot(a, b):
    return jnp.dot(a, b, preferred_element_type=F32)


def _dot_nt(a, b):
    return lax.dot_general(a, b, (((1,), (1,)), ((), ())), preferred_element_type=F32)


def _dot_tn(a, b):
    return lax.dot_general(a, b, (((0,), (0,)), ((), ())), preferred_element_type=F32)


def _sigmoid(x):
    return 1.0 / (1.0 + jnp.exp(-x))


def _silu(x):
    return x * _sigmoid(x)


def _softplus(x):
    return jnp.maximum(x, 0.0) + jnp.log1p(jnp.exp(-jnp.abs(x)))


def _rms(x, w):
    return x * lax.rsqrt(jnp.mean(x * x, axis=-1, keepdims=True) + RMS_EPS) * w


def _inproj_kernel(x_ref, nw_ref, w_ref, o_ref, hn_ref):
    @pl.when(pl.program_id(1) == 0)
    def _():
        hn_ref[...] = _rms(x_ref[...], nw_ref[...]).astype(BF16)

    o_ref[...] = _dot(hn_ref[...], w_ref[...])


def _inproj(x_all, norm_w, w_packed):
    r = x_all.shape[0]
    tm = _pick(r, (1280, 640, 512, 256, 128))
    tn = 768
    return pl.pallas_call(
        _inproj_kernel,
        out_shape=jax.ShapeDtypeStruct((r, P_TOTAL), F32),
        grid=(r // tm, P_TOTAL // tn),
        in_specs=[
            pl.BlockSpec((tm, D_MODEL), lambda i, j: (i, 0)),
            pl.BlockSpec((1, D_MODEL), lambda i, j: (0, 0)),
            pl.BlockSpec((D_MODEL, tn), lambda i, j: (0, j)),
        ],
        out_specs=pl.BlockSpec((tm, tn), lambda i, j: (i, j)),
        scratch_shapes=[pltpu.VMEM((tm, D_MODEL), BF16)],
        compiler_params=pltpu.CompilerParams(
            dimension_semantics=("parallel", "arbitrary"), vmem_limit_bytes=VMEM_LIMIT),
        name="inproj",
    )(x_all, norm_w, w_packed)


def _mla_prep_kernel(qd_ref, kvc_ref, sm_ref, c_ref, s_ref, qn_ref, kvn_ref, wq_ref, wqs_ref, wk_ref, wv_ref,
                     q_out, k_out, v_out, ckv_out, kr_out):
    cos = c_ref[...]
    sin = s_ref[...]
    qn = _rms(qd_ref[...], qn_ref[...]).astype(BF16)
    q = _dot(qn, wq_ref[...])
    qs = _dot(qn, wqs_ref[...])
    for h in range(MLA_HEADS):
        sl = slice(h * LANES, (h + 1) * LANES)
        q_out[h] = ((q[:, sl] * cos + qs[:, sl] * sin) * MLA_SCALE).astype(BF16)
    ckv = _rms(kvc_ref[...], kvn_ref[...])
    ckv_out[...] = ckv
    cb = ckv.astype(BF16)
    sm = sm_ref[...]
    lane = lax.broadcasted_iota(jnp.int32, sm.shape, 1)
    cos_k = jnp.where((lane >= SM_KR) & (lane < SM_KR + ROPE_DIM), cos, 0.0)
    krot = sm * cos_k + pltpu.roll(sm, LANES - ROPE_DIM, 1) * sin
    kr_out[...] = krot
    kk = _dot(cb, wk_ref[...])
    vv = _dot(cb, wv_ref[...])
    for h in range(MLA_HEADS):
        sl = slice(h * LANES, (h + 1) * LANES)
        k_out[h] = (kk[:, sl] + krot).astype(BF16)
        v_out[h] = vv[:, sl].astype(BF16)


def _mla_prep(proj, cos_t, sin_t, q_norm, kv_norm, wq, wqs, wk, wv):
    r = proj.shape[0]
    tm = _pick(r, (640, 512, 256, 128))
    hw = MLA_HEADS * LANES
    full = lambda shape: pl.BlockSpec(shape, lambda i: (0,) * len(shape))
    head_out = pl.BlockSpec((MLA_HEADS, tm, LANES), lambda i: (0, i, 0))
    return pl.pallas_call(
        _mla_prep_kernel,
        out_shape=(
            jax.ShapeDtypeStruct((MLA_HEADS, r, LANES), BF16),
            jax.ShapeDtypeStruct((MLA_HEADS, r, LANES), BF16),
            jax.ShapeDtypeStruct((MLA_HEADS, r, LANES), BF16),
            jax.ShapeDtypeStruct((r, KV_LORA), F32),
            jax.ShapeDtypeStruct((r, LANES), F32),
        ),
        grid=(r // tm,),
        in_specs=[
            pl.BlockSpec((tm, Q_LORA), lambda i: (i, P_QD // Q_LORA)),
            pl.BlockSpec((tm, KV_LORA), lambda i: (i, P_KVC // KV_LORA)),
            pl.BlockSpec((tm, LANES), lambda i: (i, P_SMALL // LANES)),
            pl.BlockSpec((tm, LANES), lambda i: (i, 0)),
            pl.BlockSpec((tm, LANES), lambda i: (i, 0)),
            full((1, Q_LORA)), full((1, KV_LORA)),
            full((Q_LORA, hw)), full((Q_LORA, hw)), full((KV_LORA, hw)), full((KV_LORA, hw)),
        ],
        out_specs=(head_out, head_out, head_out,
                   pl.BlockSpec((tm, KV_LORA), lambda i: (i, 0)),
                   pl.BlockSpec((tm, LANES), lambda i: (i, 0))),
        compiler_params=pltpu.CompilerParams(dimension_semantics=("parallel",), vmem_limit_bytes=VMEM_LIMIT),
        name="mla_prep",
    )(proj, proj, proj, cos_t, sin_t, q_norm, kv_norm, wq, wqs, wk, wv)


def _attn_prompt_kernel(q_ref, k_ref, v_ref, km_ref, vm_ref, o_ref, *, tq):
    qi = pl.program_id(2)
    out = None
    for hh in range(2):
        q = q_ref[hh]
        s0 = _dot_nt(q, km_ref[hh])
        m = jnp.max(s0, axis=1, keepdims=True)
        p0 = jnp.exp(s0 - m)
        l = jnp.sum(p0, axis=1, keepdims=True)
        acc = _dot(p0.astype(BF16), vm_ref[hh])

        def step(kb, vb, carry, mask):
            m, l, acc = carry
            s = _dot_nt(q, kb)
            if mask is not None:
                s = jnp.where(mask, s, -1e30)
            m_new = jnp.maximum(m, jnp.max(s, axis=1, keepdims=True))
            a = jnp.exp(m - m_new)
            p = jnp.exp(s - m_new)
            return m_new, a * l + jnp.sum(p, axis=1, keepdims=True), a * acc + _dot(p.astype(BF16), vb)

        def body(j, carry):
            off = pl.multiple_of(j * tq, tq)
            return step(k_ref[hh, pl.ds(off, tq), :], v_ref[hh, pl.ds(off, tq), :], carry, None)

        m, l, acc = lax.fori_loop(0, qi, body, (m, l, acc))
        off = pl.multiple_of(qi * tq, tq)
        row = lax.broadcasted_iota(jnp.int32, (tq, tq), 0)
        col = lax.broadcasted_iota(jnp.int32, (tq, tq), 1)
        m, l, acc = step(k_ref[hh, pl.ds(off, tq), :], v_ref[hh, pl.ds(off, tq), :], (m, l, acc), col <= row)
        o = acc / l
        out = o if out is None else out + o
    o_ref[...] = out.astype(o_ref.dtype)


def _attn_prompt(q, k, v, batch, seq, meta_row):
    tq = _pick(seq, ATTN_TQ)
    nq = seq // tq
    kern = functools.partial(_attn_prompt_kernel, tq=tq)
    return pl.pallas_call(
        kern,
        out_shape=jax.ShapeDtypeStruct((batch * seq, D_MODEL), BF16),
        grid=(batch, MLA_HEADS // 2, nq),
        in_specs=[
            pl.BlockSpec((2, tq, LANES), lambda b, p, i: (p, b * nq + i, 0)),
            pl.BlockSpec((2, seq, LANES), lambda b, p, i: (p, b, 0)),
            pl.BlockSpec((2, seq, LANES), lambda b, p, i: (p, b, 0)),
            pl.BlockSpec((2, N_META, LANES), lambda b, p, i: (p, meta_row // N_META, 0)),
            pl.BlockSpec((2, N_META, LANES), lambda b, p, i: (p, meta_row // N_META, 0)),
        ],
        out_specs=pl.BlockSpec((tq, LANES), lambda b, p, i: (b * nq + i, p)),
        compiler_params=pltpu.CompilerParams(
            dimension_semantics=("parallel", "parallel", "arbitrary"), vmem_limit_bytes=VMEM_LIMIT),
        name="attn_prompt",
    )(q, k, v, k, v)


def _gate_lanes(sm, alog_ref, dtb_ref):
    g = -jnp.exp(alog_ref[...]) * _softplus(sm + dtb_ref[...])
    beta = _sigmoid(sm)
    return g, beta


def _qkv_heads(xc):
    xf = _silu(xc)
    qs, ks, vs = [], [], []
    for h in range(GDN_HEADS):
        q = xf[:, h * GDN_DK:(h + 1) * GDN_DK]
        k = xf[:, GDN_KEY + h * GDN_DK:GDN_KEY + (h + 1) * GDN_DK]
        qs.append(q * lax.rsqrt(jnp.sum(q * q, axis=-1, keepdims=True) + RMS_EPS) * (GDN_DK ** -0.5))
        ks.append(k * lax.rsqrt(jnp.sum(k * k, axis=-1, keepdims=True) + RMS_EPS))
        vs.append(xf[:, 2 * GDN_KEY + h * GDN_DV:2 * GDN_KEY + (h + 1) * GDN_DV])
    return qs, ks, vs


def _unit_lower_inverse(a, c):
    row = lax.broadcasted_iota(jnp.int32, (c, c), 0)
    col = lax.broadcasted_iota(jnp.int32, (c, c), 1)
    n = -a
    t = jnp.where(row == col, 1.0, 0.0) + n
    p = n
    span = 2
    while span < c:
        p = jnp.dot(p, p, precision=HIGHEST, preferred_element_type=F32)
        t = t + jnp.dot(p, t, precision=HIGHEST, preferred_element_type=F32)
        span *= 2
    return t


def _gdn_chunk(xs, sm, conv_ref, alog_ref, dtb_ref, states, c):
    xc = xs[0] * conv_ref[0:1, :]
    for j in range(1, CONV_W):
        xc = xc + xs[j] * conv_ref[j:j + 1, :]
    qs, ks, vs = _qkv_heads(xc)
    g, beta = _gate_lanes(sm, alog_ref, dtb_ref)
    row = lax.broadcasted_iota(jnp.int32, (c, c), 0)
    col = lax.broadcasted_iota(jnp.int32, (c, c), 1)
    causal = col <= row
    strict = col < row
    gcum = jnp.dot(jnp.where(causal, 1.0, 0.0), g, precision=HIGHEST, preferred_element_type=F32)
    gcum_t = lax.dot_general(g, jnp.where(col >= row, 1.0, 0.0), (((0,), (0,)), ((), ())),
                             precision=HIGHEST, preferred_element_type=F32)
    outs, new_states = [], []
    for h in range(GDN_HEADS):
        q, k, v, s = qs[h], ks[h], vs[h], states[h]
        gc = gcum[:, SM_A + h:SM_A + h + 1]
        gr = gcum_t[SM_A + h:SM_A + h + 1, :]
        bc = beta[:, SM_B + h:SM_B + h + 1]
        decay = jnp.where(causal, jnp.exp(jnp.where(causal, gc - gr, 0.0)), 0.0)
        kb = k * bc
        kbf = k.astype(BF16)
        a = jnp.where(strict, _dot_nt(kb.astype(BF16), kbf) * decay, 0.0)
        t = _unit_lower_inverse(a, c)
        eg = jnp.exp(gc)
        rhs = jnp.concatenate([v * bc, kb * eg], axis=1)
        sol = _dot(t.astype(BF16), rhs.astype(BF16))
        u, w = sol[:, :GDN_DV], sol[:, GDN_DV:]
        sb = s.astype(BF16)
        v_new = u - _dot(w.astype(BF16), sb)
        attn = jnp.where(causal, _dot_nt(q.astype(BF16), kbf) * decay, 0.0)
        vnb = v_new.astype(BF16)
        outs.append(_dot((q * eg).astype(BF16), sb) + _dot(attn.astype(BF16), vnb))
        g_last = gc[c - 1:c, :]
        kd = k * jnp.exp(g_last - gc)
        new_states.append(s * jnp.exp(g_last) + _dot_tn(kd.astype(BF16), vnb))
    return outs, new_states


def _gdn_meta_kernel(x_ref, sm_ref, conv_ref, alog_ref, dtb_ref, s_out, xe_ref):
    c = N_META
    xe_ref[0:8, :] = jnp.zeros((8, GDN_QKV), F32)
    xe_ref[8:8 + c, :] = x_ref[...]
    xs = [xe_ref[pl.ds(8 - (CONV_W - 1) + j, c), :] for j in range(CONV_W)]
    zero = jnp.zeros((GDN_DK, GDN_DV), F32)
    _, st = _gdn_chunk(xs, sm_ref[...], conv_ref, alog_ref, dtb_ref, [zero] * GDN_HEADS, c)
    for h in range(GDN_HEADS):
        s_out[h] = st[h]


def _gdn_meta(proj, conv_w, alog_v, dtb_v, meta_row):
    full = lambda shape: pl.BlockSpec(shape, lambda i: (0,) * len(shape))
    return pl.pallas_call(
        _gdn_meta_kernel,
        out_shape=jax.ShapeDtypeStruct((GDN_HEADS, GDN_DK, GDN_DV), F32),
        grid=(1,),
        in_specs=[
            pl.BlockSpec((N_META, GDN_QKV), lambda i: (meta_row // N_META, 0)),
            pl.BlockSpec((N_META, LANES), lambda i: (meta_row // N_META, P_SMALL // LANES)),
            full((CONV_W, GDN_QKV)), full((1, LANES)), full((1, LANES)),
        ],
        out_specs=full((GDN_HEADS, GDN_DK, GDN_DV)),
        scratch_shapes=[pltpu.VMEM((8 + N_META, GDN_QKV), F32)],
        compiler_params=pltpu.CompilerParams(vmem_limit_bytes=VMEM_LIMIT),
        name="gdn_meta",
    )(proj, proj, conv_w, alog_v, dtb_v)


def _gdn_prompt_kernel(x_ref, sm_ref, xm_ref, s0_ref, conv_ref, alog_ref, dtb_ref, o_ref, s_out, xe_ref, st_ref):
    c = CHUNK
    ci = pl.program_id(1)

    @pl.when(ci == 0)
    def _():
        xe_ref[0:8, :] = xm_ref[N_META - 8:N_META, :]
        st_ref[...] = s0_ref[...]

    xe_ref[8:8 + c, :] = x_ref[...]
    xs = [xe_ref[pl.ds(8 - (CONV_W - 1) + j, c), :] for j in range(CONV_W)]
    outs, st = _gdn_chunk(xs, sm_ref[...], conv_ref, alog_ref, dtb_ref,
                          [st_ref[h] for h in range(GDN_HEADS)], c)
    for h in range(GDN_HEADS):
        st_ref[h] = st[h]
        o_ref[:, h * GDN_DV:(h + 1) * GDN_DV] = outs[h]
    xe_ref[0:8, :] = xe_ref[c:c + 8, :]

    @pl.when(ci == pl.num_programs(1) - 1)
    def _():
        s_out[0] = st_ref[...]


def _gdn_prompt(proj, s_meta, conv_w, alog_v, dtb_v, batch, seq, meta_row):
    nc = seq // CHUNK
    full = lambda shape: pl.BlockSpec(shape, lambda b, c: (0,) * len(shape))
    return pl.pallas_call(
        _gdn_prompt_kernel,
        out_shape=(jax.ShapeDtypeStruct((batch * seq, D_MODEL), F32),
                   jax.ShapeDtypeStruct((batch, GDN_HEADS, GDN_DK, GDN_DV), F32)),
        grid=(batch, nc),
        in_specs=[
            pl.BlockSpec((CHUNK, GDN_QKV), lambda b, c: (b * nc + c, 0)),
            pl.BlockSpec((CHUNK, LANES), lambda b, c: (b * nc + c, P_SMALL // LANES)),
            pl.BlockSpec((N_META, GDN_QKV), lambda b, c: (meta_row // N_META, 0)),
            full((GDN_HEADS, GDN_DK, GDN_DV)),
            full((CONV_W, GDN_QKV)), full((1, LANES)), full((1, LANES)),
        ],
        out_specs=(pl.BlockSpec((CHUNK, D_MODEL), lambda b, c: (b * nc + c, 0)),
                   pl.BlockSpec((1, GDN_HEADS, GDN_DK, GDN_DV), lambda b, c: (b, 0, 0, 0))),
        scratch_shapes=[pltpu.VMEM((8 + CHUNK, GDN_QKV), F32), pltpu.VMEM((GDN_HEADS, GDN_DK, GDN_DV), F32)],
        compiler_params=pltpu.CompilerParams(
            dimension_semantics=("parallel", "arbitrary"), vmem_limit_bytes=VMEM_LIMIT),
        name="gdn_prompt",
    )(proj, proj, proj, s_meta, conv_w, alog_v, dtb_v)


def _gdn_sample_kernel(x_ref, sm_ref, cs_ref, st_ref, conv_ref, alog_ref, dtb_ref, o_ref, s_out, *, nb):
    xc = x_ref[...] * conv_ref[CONV_W - 1:CONV_W, :]
    for j in range(CONV_W - 1):
        xc = xc + cs_ref[:, j, :] * conv_ref[j:j + 1, :]
    qs, ks, vs = _qkv_heads(xc)
    g, beta = _gate_lanes(sm_ref[...], alog_ref, dtb_ref)
    eg = jnp.exp(g)
    for h in range(GDN_HEADS):
        q_t = qs[h].T
        k_t = ks[h].T
        for b in range(nb):
            kcol = k_t[:, b:b + 1]
            s1 = st_ref[b, h] * eg[b:b + 1, SM_A + h:SM_A + h + 1]
            r = jnp.sum(s1 * kcol, axis=0, keepdims=True)
            delta = (vs[h][b:b + 1, :] - r) * beta[b:b + 1, SM_B + h:SM_B + h + 1]
            s2 = s1 + kcol * delta
            s_out[b, h] = s2
            o_ref[b:b + 1, h * GDN_DV:(h + 1) * GDN_DV] = jnp.sum(s2 * q_t[:, b:b + 1], axis=0, keepdims=True)


def _gdn_sample(proj, state_conv, state_gdn, conv_w, alog_v, dtb_v, row0):
    ns = state_gdn.shape[0]
    nb = 8
    full = lambda shape: pl.BlockSpec(shape, lambda i: (0,) * len(shape))
    kern = functools.partial(_gdn_sample_kernel, nb=nb)
    return pl.pallas_call(
        kern,
        out_shape=(jax.ShapeDtypeStruct((ns, D_MODEL), F32),
                   jax.ShapeDtypeStruct(state_gdn.shape, F32)),
        grid=(ns // nb,),
        in_specs=[
            pl.BlockSpec((nb, GDN_QKV), lambda i: (row0 // nb + i, 0)),
            pl.BlockSpec((nb, LANES), lambda i: (row0 // nb + i, P_SMALL // LANES)),
            pl.BlockSpec((nb, CONV_W - 1, GDN_QKV), lambda i: (i, 0, 0)),
            pl.BlockSpec((nb, GDN_HEADS, GDN_DK, GDN_DV), lambda i: (i, 0, 0, 0)),
            full((CONV_W, GDN_QKV)), full((1, LANES)), full((1, LANES)),
        ],
        out_specs=(pl.BlockSpec((nb, D_MODEL), lambda i: (i, 0)),
                   pl.BlockSpec((nb, GDN_HEADS, GDN_DK, GDN_DV), lambda i: (i, 0, 0, 0))),
        compiler_params=pltpu.CompilerParams(dimension_semantics=("parallel",), vmem_limit_bytes=VMEM_LIMIT),
        name="gdn_sample",
    )(proj, proj, state_conv, state_gdn, conv_w, alog_v, dtb_v)


def _q_absorb_kernel(q_ref, wukt_ref, ql_out, qr_out):
    for h in range(MLA_HEADS):
        q = q_ref[h]
        ql_out[h] = _dot(q, wukt_ref[h]).astype(BF16)
        qr_out[h] = q[:, NOPE_DIM:NOPE_DIM + ROPE_DIM]


def _q_absorb(q, wukt, ns, row0):
    return pl.pallas_call(
        _q_absorb_kernel,
        out_shape=(jax.ShapeDtypeStruct((MLA_HEADS, ns, KV_LORA), BF16),
                   jax.ShapeDtypeStruct((MLA_HEADS, ns, ROPE_DIM), BF16)),
        grid=(1,),
        in_specs=[pl.BlockSpec((MLA_HEADS, ns, LANES), lambda i: (0, row0 // ns, 0)),
                  pl.BlockSpec((MLA_HEADS, LANES, KV_LORA), lambda i: (0, 0, 0))],
        out_specs=(pl.BlockSpec((MLA_HEADS, ns, KV_LORA), lambda i: (0, 0, 0)),
                   pl.BlockSpec((MLA_HEADS, ns, ROPE_DIM), lambda i: (0, 0, 0))),
        compiler_params=pltpu.CompilerParams(vmem_limit_bytes=VMEM_LIMIT),
        name="q_absorb",
    )(q, wukt)


def _mla_sample_kernel(pt_ref, ql_ref, qr_ref, cn_ref, krn_ref, cc_hbm, cr_hbm, o_ref,
                       cbuf, rbuf, sem, m_ref, l_ref, acc_ref, *, ppc):
    nch = pl.num_programs(1)
    ch = pl.program_id(1)
    t = pl.program_id(0) * nch + ch
    total = pl.num_programs(0) * nch
    slot = t % 2

    def page_copies(step, slot_, i):
        page = pt_ref[step * ppc + i]
        return (pltpu.make_async_copy(cc_hbm.at[page], cbuf.at[slot_, i], sem.at[slot_, 0]),
                pltpu.make_async_copy(cr_hbm.at[page], rbuf.at[slot_, i], sem.at[slot_, 1]))

    def start_all(step, slot_):
        for i in range(ppc):
            cc, cr = page_copies(step, slot_, i)
            cc.start()
            cr.start()

    @pl.when(t == 0)
    def _():
        start_all(0, 0)

    @pl.when(t + 1 < total)
    def _():
        start_all(t + 1, 1 - slot)

    for i in range(ppc):
        cc, cr = page_copies(t, slot, i)
        cc.wait()
        cr.wait()

    @pl.when(ch == 0)
    def _():
        m_ref[...] = jnp.full(m_ref.shape, -1e30, F32)
        l_ref[...] = jnp.zeros(l_ref.shape, F32)
        acc_ref[...] = jnp.zeros(acc_ref.shape, F32)

    ql = ql_ref[0]
    qr = qr_ref[0]
    c = cbuf[slot].reshape(ppc * PAGE_SIZE, KV_LORA).astype(BF16)
    kr = rbuf[slot].reshape(ppc * PAGE_SIZE, ROPE_DIM).astype(BF16)
    s = _dot_nt(ql, c) + _dot_nt(qr, kr)
    m_old = m_ref[...]
    m_new = jnp.maximum(m_old, jnp.max(s, axis=1, keepdims=True))
    a = jnp.exp(m_old - m_new)
    p = jnp.exp(s - m_new)
    l_new = a * l_ref[...] + jnp.sum(p, axis=1, keepdims=True)
    acc_new = a * acc_ref[...] + _dot(p.astype(BF16), c)
    m_ref[...] = m_new
    l_ref[...] = l_new
    acc_ref[...] = acc_new

    @pl.when(ch == nch - 1)
    def _():
        cn = cn_ref[0]
        krn = krn_ref[0][:, SM_KR:SM_KR + ROPE_DIM]
        s_n = (jnp.sum(ql.astype(F32) * cn, axis=1, keepdims=True)
               + jnp.sum(qr.astype(F32) * krn, axis=1, keepdims=True))
        m2 = jnp.maximum(m_new, s_n)
        a2 = jnp.exp(m_new - m2)
        p_n = jnp.exp(s_n - m2)
        o_ref[0] = (a2 * acc_new + p_n * cn) / (a2 * l_new + p_n)


def _mla_sample(page_table, ql, qr, c_new, kr_new, cache_c, cache_r):
    ns, n_pages = page_table.shape
    ppc = _pick(n_pages, (32, 16, 8, 4, 2, 1))
    kern = functools.partial(_mla_sample_kernel, ppc=ppc)
    grid_spec = pltpu.PrefetchScalarGridSpec(
        num_scalar_prefetch=1,
        grid=(ns, n_pages // ppc),
        in_specs=[
            pl.BlockSpec((1, MLA_HEADS, KV_LORA), lambda b, c, pt: (b, 0, 0)),
            pl.BlockSpec((1, MLA_HEADS, ROPE_DIM), lambda b, c, pt: (b, 0, 0)),
            pl.BlockSpec((1, 1, KV_LORA), lambda b, c, pt: (b, 0, 0)),
            pl.BlockSpec((1, 1, LANES), lambda b, c, pt: (b, 0, 0)),
            pl.BlockSpec(memory_space=pl.ANY),
            pl.BlockSpec(memory_space=pl.ANY),
        ],
        out_specs=pl.BlockSpec((1, MLA_HEADS, KV_LORA), lambda b, c, pt: (b, 0, 0)),
        scratch_shapes=[
            pltpu.VMEM((2, ppc, PAGE_SIZE, KV_LORA), F32),
            pltpu.VMEM((2, ppc, PAGE_SIZE, ROPE_DIM), F32),
            pltpu.SemaphoreType.DMA((2, 2)),
            pltpu.VMEM((MLA_HEADS, 1), F32),
            pltpu.VMEM((MLA_HEADS, 1), F32),
            pltpu.VMEM((MLA_HEADS, KV_LORA), F32),
        ],
    )
    return pl.pallas_call(
        kern,
        out_shape=jax.ShapeDtypeStruct((ns, MLA_HEADS, KV_LORA), F32),
        grid_spec=grid_spec,
        compiler_params=pltpu.CompilerParams(
            dimension_semantics=("arbitrary", "arbitrary"), vmem_limit_bytes=VMEM_LIMIT),
        name="mla_sample",
    )(page_table.reshape(-1), ql, qr, c_new, kr_new, cache_c, cache_r)


def _o_proj_sample_kernel(ol_ref, wv_ref, o_ref):
    for p in range(MLA_HEADS // 2):
        acc = None
        for h in (2 * p, 2 * p + 1):
            part = _dot(ol_ref[h].astype(BF16), wv_ref[:, h * LANES:(h + 1) * LANES])
            acc = part if acc is None else acc + part
        o_ref[:, p * LANES:(p + 1) * LANES] = acc.astype(o_ref.dtype)


def _o_proj_sample(o_lat_t, wv):
    ns = o_lat_t.shape[1]
    return pl.pallas_call(
        _o_proj_sample_kernel,
        out_shape=jax.ShapeDtypeStruct((ns, D_MODEL), BF16),
        grid=(1,),
        in_specs=[pl.BlockSpec((MLA_HEADS, ns, KV_LORA), lambda i: (0, 0, 0)),
                  pl.BlockSpec((KV_LORA, MLA_HEADS * LANES), lambda i: (0, 0))],
        out_specs=pl.BlockSpec((ns, D_MODEL), lambda i: (0, 0)),
        compiler_params=pltpu.CompilerParams(vmem_limit_bytes=VMEM_LIMIT),
        name="o_proj_sample",
    )(o_lat_t, wv)


def _outproj_kernel(x_ref, om_ref, og_ref, z_ref, gm_ref, gg_ref, wo_ref, gn_ref, nf_ref, wr_ref, br_ref,
                    xmid_out, hf_out, route_out):
    og = og_ref[...]
    parts = []
    for h in range(GDN_HEADS):
        oh = og[:, h * GDN_DV:(h + 1) * GDN_DV]
        parts.append(oh * lax.rsqrt(jnp.mean(oh * oh, axis=-1, keepdims=True) + RMS_EPS))
    o_gdn = jnp.concatenate(parts, axis=1) * gn_ref[...] * _silu(z_ref[...])
    merged = _sigmoid(gm_ref[...]) * om_ref[...].astype(F32) + _sigmoid(gg_ref[...]) * o_gdn
    x_mid = x_ref[...] + _dot(merged.astype(BF16), wo_ref[...])
    xmid_out[...] = x_mid
    hf = _rms(x_mid, nf_ref[...]).astype(BF16)
    bits = lax.bitcast_convert_type(hf.astype(F32), jnp.uint32)
    half = D_MODEL // 2
    hf_out[...] = bits[:, half:] | (bits[:, :half] >> 16)

    logits = _dot(hf, wr_ref[...]) + br_ref[...]
    lane = lax.broadcasted_iota(jnp.int32, logits.shape, 1)
    neg = -jnp.inf
    big = 4 * LANES
    is_g = lane < N_GROUPS
    lg = jnp.where(is_g, logits, neg)
    mg = jnp.max(lg, axis=1, keepdims=True)
    grp = jnp.min(jnp.where(lg == mg, lane, big), axis=1, keepdims=True)
    gate_g = 1.0 / jnp.sum(jnp.where(is_g, jnp.exp(logits - mg), 0.0), axis=1, keepdims=True)
    e_lane = lane - N_GROUPS
    in_grp = (e_lane >= 0) & (e_lane < N_EXPERTS) & ((e_lane >> 3) == grp)
    le = jnp.where(in_grp, logits, neg)
    v1 = jnp.max(le, axis=1, keepdims=True)
    i1 = jnp.min(jnp.where(le == v1, lane, big), axis=1, keepdims=True)
    le2 = jnp.where(lane == i1, neg, le)
    v2 = jnp.max(le2, axis=1, keepdims=True)
    i2 = jnp.min(jnp.where(le2 == v2, lane, big), axis=1, keepdims=True)
    e = jnp.exp(v2 - v1)
    w1 = gate_g / (1.0 + e)
    w2 = gate_g * e / (1.0 + e)
    route = jnp.where(lane == 0, (i1 - N_GROUPS).astype(F32),
                      jnp.where(lane == 1, (i2 - N_GROUPS).astype(F32),
                                jnp.where(lane == 2, w1, jnp.where(lane == 3, w2, 0.0))))
    route_out[...] = route


def _row_tile(n, limit):
    t = limit - limit % 16
    while t >= 16:
        if n % t == 0:
            return t
        t -= 16
    raise ValueError(f"no row tile for {n}")


def _outproj(x_all, o_mla, o_gdn, proj, w_out, gn_t, norm_ffn, w_r, b_r, row0):
    n = o_mla.shape[0]
    tm = _pick(n, (512, 256, 128, 64, 32, 16))
    assert row0 % tm == 0
    r0 = row0 // tm
    full = lambda shape: pl.BlockSpec(shape, lambda i: (0,) * len(shape))
    row = lambda w: pl.BlockSpec((tm, w), lambda i: (i, 0))
    shared = lambda w, j=0: pl.BlockSpec((tm, w), lambda i, j=j: (r0 + i, j))
    return pl.pallas_call(
        _outproj_kernel,
        out_shape=(jax.ShapeDtypeStruct((n, D_MODEL), F32),
                   jax.ShapeDtypeStruct((n, D_MODEL // 2), jnp.uint32),
                   jax.ShapeDtypeStruct((n, LANES), F32)),
        grid=(n // tm,),
        in_specs=[shared(D_MODEL), row(D_MODEL), row(D_MODEL),
                  shared(D_MODEL, P_Z // D_MODEL), shared(D_MODEL, P_GM // D_MODEL), shared(D_MODEL, P_GG // D_MODEL),
                  full((D_MODEL, D_MODEL)), full((1, D_MODEL)), full((1, D_MODEL)),
                  full((D_MODEL, LANES)), full((1, LANES))],
        out_specs=(row(D_MODEL), row(D_MODEL // 2), row(LANES)),
        compiler_params=pltpu.CompilerParams(dimension_semantics=("parallel",), vmem_limit_bytes=VMEM_LIMIT),
        name="outproj_route",
    )(x_all, o_mla, o_gdn, proj, proj, proj, w_out, gn_t, norm_ffn, w_r, b_r)


def _moe_plan(eid, wgt, nsp):
    nt = eid.shape[0]
    ts = nt // nsp
    n_asg = ts * TOP_K
    nb = (n_asg + N_EXPERTS * (MOE_BLOCK - 1) + MOE_BLOCK - 1) // MOE_BLOCK
    e = eid.reshape(nsp, n_asg)
    w = wgt.reshape(nsp, n_asg)
    tok = jnp.repeat(jnp.arange(ts, dtype=jnp.int32), TOP_K)
    onehot = (e[..., None] == jnp.arange(N_EXPERTS, dtype=jnp.int32)).astype(jnp.int32)
    cs = jnp.cumsum(onehot, axis=1)
    rank = jnp.take_along_axis(cs, e[..., None], axis=2)[..., 0] - 1
    counts = cs[:, -1, :]
    padded = (counts + MOE_BLOCK - 1) // MOE_BLOCK * MOE_BLOCK
    pad_end = jnp.cumsum(padded, axis=1)
    dest = jnp.take_along_axis(pad_end - padded, e, axis=1) + rank
    split = jnp.arange(nsp, dtype=jnp.int32)[:, None]
    slot_tok = jnp.zeros((nsp, nb * MOE_BLOCK), jnp.int32).at[split, dest].set(jnp.broadcast_to(tok, dest.shape))
    slot_w = jnp.zeros((nsp, nb * MOE_BLOCK), F32).at[split, dest].set(w)
    starts = jnp.arange(nb, dtype=jnp.int32) * MOE_BLOCK
    blk_e = jnp.minimum(jnp.sum(pad_end[:, None, :] <= starts[None, :, None], axis=2), N_EXPERTS - 1)
    nused = pad_end[:, -1] // MOE_BLOCK
    return (blk_e.astype(jnp.int32).reshape(-1), nused.astype(jnp.int32), slot_tok.reshape(-1),
            slot_w.reshape(-1), ts, nb)


def _split_pieces(k, ts, n_prompt, n_sample):
    lo, hi = k * ts, (k + 1) * ts
    pieces = []
    if lo < n_prompt:
        pieces.append((0, lo, 0, min(hi, n_prompt) - lo))
    if hi > n_prompt:
        start = max(lo, n_prompt)
        pieces.append((1, start - n_prompt, start - lo, hi - start))
    assert hi <= n_prompt + n_sample
    return pieces


def _moe_kernel(blk_ref, nused_ref, tok_ref, w_ref, hf_p, hf_q, xmid_p, xmid_q, wgu_ref, wd_ref, nfin_ref,
                y_p, y_q, hf_s, acc, xb, yb, sem, *, ts, nb, rc, nsp, n_prompt, n_sample):
    s = pl.program_id(0)
    j = pl.program_id(1)
    half = D_MODEL // 2

    def split_copies(k, load):
        cps = []
        for src, r0, l0, n in _split_pieces(k, ts, n_prompt, n_sample):
            if load:
                cps.append(pltpu.make_async_copy((hf_p, hf_q)[src].at[pl.ds(r0, n)], hf_s.at[pl.ds(l0, n)],
                                                 sem.at[0, src]))
                cps.append(pltpu.make_async_copy((xmid_p, xmid_q)[src].at[pl.ds(r0, n)], acc.at[pl.ds(l0, n)],
                                                 sem.at[1, src]))
            else:
                cps.append(pltpu.make_async_copy(acc.at[pl.ds(l0, n)], (y_p, y_q)[src].at[pl.ds(r0, n)],
                                                 sem.at[2, src]))
        return cps

    def run_copies(load):
        for k in range(nsp):
            @pl.when(s == k)
            def _(k=k):
                cps = split_copies(k, load)
                for cp in cps:
                    cp.start()
                for cp in cps:
                    cp.wait()

    @pl.when(j == 0)
    def _():
        run_copies(True)

    @pl.when(j < nused_ref[s])
    def _():
        base = (s * nb + j) * MOE_BLOCK
        for r in range(MOE_BLOCK):
            xb[r:r + 1, :] = hf_s[pl.ds(tok_ref[base + r], 1), :]
        bits = xb[...]
        lo = lax.bitcast_convert_type(bits << 16, F32).astype(BF16)
        hi = lax.bitcast_convert_type(bits & jnp.uint32(0xFFFF0000), F32).astype(BF16)
        gu = _dot(lo, wgu_ref[0, :half, :]) + _dot(hi, wgu_ref[0, half:, :])
        hmid = _silu(gu[:, :D_EXPERT]) * gu[:, D_EXPERT:]
        yb[...] = _dot(hmid.astype(BF16), wd_ref[0])
        for r in range(MOE_BLOCK):
            tok = tok_ref[base + r]
            acc[pl.ds(tok, 1), :] = acc[pl.ds(tok, 1), :] + w_ref[base + r] * yb[r:r + 1, :]

    @pl.when(j == nb - 1)
    def _():
        def body(i, carry):
            rows = pl.ds(pl.multiple_of(i * rc, 8), rc)
            acc[rows, :] = _rms(acc[rows, :], nfin_ref[...])
            return carry

        lax.fori_loop(0, ts // rc, body, 0)
        run_copies(False)


def _moe(plan, hf_p, hf_q, xmid_p, xmid_q, wgu, wd, norm_final):
    blk_e, nused, slot_tok, slot_w, ts, nb = plan
    n_prompt, n_sample = xmid_p.shape[0], xmid_q.shape[0]
    nsp = (n_prompt + n_sample) // ts
    rc = _row_tile(ts, 512) if ts % 16 == 0 else 8
    kern = functools.partial(_moe_kernel, ts=ts, nb=nb, rc=rc, nsp=nsp, n_prompt=n_prompt, n_sample=n_sample)
    hbm = pl.BlockSpec(memory_space=pl.ANY)
    grid_spec = pltpu.PrefetchScalarGridSpec(
        num_scalar_prefetch=4,
        grid=(nsp, nb),
        in_specs=[
            hbm, hbm, hbm, hbm,
            pl.BlockSpec((1, D_MODEL, 2 * D_EXPERT), lambda s, j, be, nu, st, sw: (be[s * nb + j], 0, 0)),
            pl.BlockSpec((1, D_EXPERT, D_MODEL), lambda s, j, be, nu, st, sw: (be[s * nb + j], 0, 0)),
            pl.BlockSpec((1, D_MODEL), lambda s, j, be, nu, st, sw: (0, 0)),
        ],
        out_specs=(hbm, hbm),
        scratch_shapes=[
            pltpu.VMEM((ts, D_MODEL // 2), jnp.uint32),
            pltpu.VMEM((ts, D_MODEL), F32),
            pltpu.VMEM((MOE_BLOCK, D_MODEL // 2), jnp.uint32),
            pltpu.VMEM((MOE_BLOCK, D_MODEL), F32),
            pltpu.SemaphoreType.DMA((3, 2)),
        ],
    )
    return pl.pallas_call(
        kern,
        out_shape=(jax.ShapeDtypeStruct((n_prompt, D_MODEL), F32), jax.ShapeDtypeStruct((n_sample, D_MODEL), F32)),
        grid_spec=grid_spec,
        compiler_params=pltpu.CompilerParams(
            dimension_semantics=("arbitrary", "arbitrary"), vmem_limit_bytes=VMEM_LIMIT),
        name="moe",
    )(blk_e, nused, slot_tok, slot_w, hf_p, hf_q, xmid_p, xmid_q, wgu, wd, norm_final)


def _pack_w_in(w):
    kr = w[:, _OFF_KV + KV_LORA:_OFF_QKV]
    kr_sw = jnp.concatenate([kr[:, ROPE_DIM // 2:], kr[:, :ROPE_DIM // 2]], axis=1)
    small = jnp.concatenate([w[:, _OFF_B:_OFF_A], w[:, _OFF_A:_OFF_GM],
                             jnp.zeros((D_MODEL, SM_KR - 2 * GDN_HEADS), w.dtype), kr, kr_sw], axis=1)
    packed = jnp.concatenate([w[:, _OFF_QKV:_OFF_Z], w[:, _OFF_Z:_OFF_B], w[:, _OFF_GM:_OFF_GG], w[:, _OFF_GG:],
                              w[:, _OFF_KV:_OFF_KV + KV_LORA], small, w[:, :Q_LORA]], axis=1)
    return packed.astype(BF16)


def _pack_mla_weights(w_uq, w_uk, w_uv):
    zq = jnp.zeros((Q_LORA, MLA_HEADS, LANES - NOPE_DIM - ROPE_DIM), w_uq.dtype)
    wq = jnp.concatenate([w_uq, zq], axis=2).reshape(Q_LORA, MLA_HEADS * LANES)
    rope = w_uq[:, :, NOPE_DIM:]
    rope_sw = jnp.concatenate([rope[..., ROPE_DIM // 2:], rope[..., :ROPE_DIM // 2]], axis=2)
    wqs = jnp.concatenate([jnp.zeros((Q_LORA, MLA_HEADS, NOPE_DIM), w_uq.dtype), rope_sw, zq], axis=2)
    wqs = wqs.reshape(Q_LORA, MLA_HEADS * LANES)
    wk = jnp.concatenate([w_uk, jnp.zeros((KV_LORA, MLA_HEADS, LANES - NOPE_DIM), w_uk.dtype)], axis=2)
    wk = wk.reshape(KV_LORA, MLA_HEADS * LANES)
    zv = jnp.zeros((KV_LORA, MLA_HEADS // 2, V_DIM), w_uv.dtype)
    wv = jnp.stack([jnp.concatenate([w_uv[:, 0::2], zv], axis=2),
                    jnp.concatenate([zv, w_uv[:, 1::2]], axis=2)], axis=2)
    wv = wv.reshape(KV_LORA, MLA_HEADS * LANES)
    return wq.astype(BF16), wqs.astype(BF16), wk.astype(BF16), wv.astype(BF16)


def _rope_tables(pos):
    inv_freq = ROPE_THETA ** (-jnp.arange(0, ROPE_DIM, 2, dtype=F32) / ROPE_DIM)
    ang = pos.astype(F32)[:, None] * inv_freq[None, :]
    cos, sin = jnp.cos(ang), jnp.sin(ang)
    n = pos.shape[0]
    cos_t = jnp.concatenate([jnp.ones((n, NOPE_DIM), F32), cos, cos, jnp.zeros((n, ROPE_DIM), F32)], axis=1)
    sin_t = jnp.concatenate([jnp.zeros((n, NOPE_DIM), F32), -sin, sin, jnp.zeros((n, ROPE_DIM), F32)], axis=1)
    return cos_t, sin_t


def _head_lanes(v):
    return jnp.zeros((1, LANES), F32).at[0, SM_A:SM_A + GDN_HEADS].set(v.astype(F32))


def _moe_splits(nt):
    for nsp in (3, 4, 2, 1):
        if nt % (nsp * 8) == 0:
            return nsp
    return 1


def kernel(x_prompt, x_sample, cache_kv_latent, cache_k_rope, page_table, state_conv, state_gdn, meta_tokens,
           norm_mix, w_in, q_norm, w_uq, kv_norm, w_uk, w_uv, conv_w, a_log, dt_bias, gdn_norm, w_out, norm_ffn,
           w_group, b_group, w_router, b_router, w_gate, w_up, w_down, norm_final):
    batch, seq, _ = x_prompt.shape
    ns, dec_seq, _ = x_sample.shape
    assert dec_seq == 1 and w_in.shape[0] == 1 and seq % CHUNK == 0
    n_pages = page_table.shape[1]
    n_prompt = batch * seq
    nt = n_prompt + ns
    meta_row = nt
    n_rows = -(-(nt + N_META) // LANES) * LANES
    assert n_prompt % ns == 0 and nt % N_META == 0 and ns % 8 == 0

    x_all = jnp.concatenate([x_prompt.reshape(n_prompt, D_MODEL), x_sample.reshape(ns, D_MODEL),
                             meta_tokens.astype(x_prompt.dtype),
                             jnp.zeros((n_rows - nt - N_META, D_MODEL), x_prompt.dtype)], axis=0)
    pos = jnp.concatenate([jnp.tile(N_META + jnp.arange(seq), batch), jnp.full((ns,), n_pages * PAGE_SIZE),
                           jnp.arange(N_META), jnp.zeros((n_rows - nt - N_META,), jnp.int32)])
    cos_t, sin_t = _rope_tables(pos)
    w_packed = _pack_w_in(w_in[0])
    wq, wqs, wk, wv = _pack_mla_weights(w_uq[0], w_uk[0], w_uv[0])
    wukt = jnp.concatenate([jnp.transpose(w_uk[0], (1, 2, 0)),
                            jnp.zeros((MLA_HEADS, LANES - NOPE_DIM, KV_LORA), w_uk.dtype)], axis=1).astype(BF16)
    alog_v, dtb_v = _head_lanes(a_log[0]), _head_lanes(dt_bias[0])
    cw = conv_w[0].astype(F32)
    gn_t = jnp.tile(gdn_norm[0].astype(F32), GDN_HEADS)[None]
    w_r = jnp.concatenate([w_group[0], w_router[0],
                           jnp.zeros((D_MODEL, LANES - N_GROUPS - N_EXPERTS), w_group.dtype)], axis=1).astype(BF16)
    b_r = jnp.concatenate([b_group[0], b_router[0], jnp.zeros((LANES - N_GROUPS - N_EXPERTS,), b_group.dtype)])[None]
    wgu = jnp.concatenate([w_gate[0], w_up[0]], axis=2).astype(BF16)
    wd = w_down[0].astype(BF16)

    proj = _inproj(x_all, norm_mix[0][None].astype(F32), w_packed)
    q, k, v, ckv, krot = _mla_prep(proj, cos_t, sin_t, q_norm[0][None].astype(F32), kv_norm[0][None].astype(F32),
                                   wq, wqs, wk, wv)

    o_mla_p = _attn_prompt(q, k, v, batch, seq, meta_row)
    ql, qr = _q_absorb(q, wukt, ns, n_prompt)
    o_lat = _mla_sample(page_table, jnp.transpose(ql, (1, 0, 2)), jnp.transpose(qr, (1, 0, 2)),
                        ckv[n_prompt:nt].reshape(ns, 1, KV_LORA), krot[n_prompt:nt].reshape(ns, 1, LANES),
                        cache_kv_latent[0], cache_k_rope[0])
    o_mla_s = _o_proj_sample(jnp.transpose(o_lat, (1, 0, 2)), wv)

    s_meta = _gdn_meta(proj, cw, alog_v, dtb_v, meta_row)
    o_gdn_p, gdn_p = _gdn_prompt(proj, s_meta, cw, alog_v, dtb_v, batch, seq, meta_row)
    o_gdn_s, gdn_s = _gdn_sample(proj, state_conv[0], state_gdn[0], cw, alog_v, dtb_v, n_prompt)

    tail = (proj, w_out[0].astype(BF16), gn_t, norm_ffn[0][None].astype(F32), w_r, b_r.astype(F32))
    xmid_p, hf_p, route_p = _outproj(x_all, o_mla_p, o_gdn_p, *tail, 0)
    xmid_s, hf_s, route_s = _outproj(x_all, o_mla_s, o_gdn_s, *tail, n_prompt)
    route = jnp.concatenate([route_p[:, :2 * TOP_K], route_s[:, :2 * TOP_K]], axis=0)
    plan = _moe_plan(route[:, :TOP_K].astype(jnp.int32), route[:, TOP_K:], _moe_splits(nt))
    y_p, y_s = _moe(plan, hf_p, hf_s, xmid_p, xmid_s, wgu, wd, norm_final[None].astype(F32))

    def with_meta(rows, width):
        meta = jnp.broadcast_to(rows[meta_row:meta_row + N_META][None], (batch, N_META, width))
        return jnp.concatenate([meta, rows[:n_prompt].reshape(batch, seq, width)], axis=1)[None]

    k_rope = krot[:, SM_KR:SM_KR + ROPE_DIM]
    qkv_p = proj[:n_prompt, :GDN_QKV].reshape(batch, seq, GDN_QKV)
    conv_s = jnp.concatenate([state_conv[0][:, 1:].astype(F32), proj[n_prompt:nt, None, :GDN_QKV]], axis=1)
    return (y_p.reshape(batch, seq, D_MODEL), y_s.reshape(ns, 1, D_MODEL),
            with_meta(ckv, KV_LORA), with_meta(k_rope, ROPE_DIM),
            ckv[n_prompt:nt].reshape(1, ns, 1, KV_LORA), k_rope[n_prompt:nt].reshape(1, ns, 1, ROPE_DIM),
            qkv_p[:, seq - (CONV_W - 1):][None], conv_s[None], gdn_p[None], gdn_s[None])
```

```python
import functools

import jax
import jax.numpy as jnp
from jax import lax
from jax.experimental import pallas as pl
from jax.experimental.pallas import tpu as pltpu

F32 = jnp.float32
BF16 = jnp.bfloat16
HIGHEST = lax.Precision.HIGHEST

D_MODEL = 1024
N_META = 16
RMS_EPS = 1e-6
MLA_HEADS = 16
Q_LORA = 384
KV_LORA = 256
NOPE_DIM = 64
ROPE_DIM = 32
V_DIM = 64
ROPE_THETA = 10000.0
MLA_SCALE = (NOPE_DIM + ROPE_DIM) ** -0.5
PAGE_SIZE = 128
GDN_HEADS = 8
GDN_DK = 128
GDN_DV = 128
GDN_KEY = GDN_HEADS * GDN_DK
GDN_QKV = 3 * GDN_KEY
CONV_W = 4
CHUNK = 64
N_GROUPS = 4
EXPERTS_PER_GROUP = 8
N_EXPERTS = 32
TOP_K = 2
D_EXPERT = 256
MOE_BLOCK = 128

_OFF_KV = Q_LORA
_OFF_QKV = _OFF_KV + KV_LORA + ROPE_DIM
_OFF_Z = _OFF_QKV + GDN_QKV
_OFF_B = _OFF_Z + GDN_KEY
_OFF_A = _OFF_B + GDN_HEADS
_OFF_GM = _OFF_A + GDN_HEADS
_OFF_GG = _OFF_GM + D_MODEL
P_QKV = 0
P_Z = 3072
P_GM = 4096
P_GG = 5120
P_KVC = 6144
P_SMALL = 6400
P_QD = 6528
P_TOTAL = 6912
SM_B = 0
SM_A = 8
SM_KR = 64

LANES = 128
VMEM_LIMIT = 56 * 1024 * 1024
ATTN_TQ = (512, 256, 128)


def _pick(n, candidates):
    for c in candidates:
        if n % c == 0:
            return c
    raise ValueError(f"no tile for {n} in {candidates}")


def _dot(a, b):
    return jnp.dot(a, b, preferred_element_type=F32)


def _dot_nt(a, b):
    return lax.dot_general(a, b, (((1,), (1,)), ((), ())), preferred_element_type=F32)


def _dot_tn(a, b):
    return lax.dot_general(a, b, (((0,), (0,)), ((), ())), preferred_element_type=F32)


def _sigmoid(x):
    return 1.0 / (1.0 + jnp.exp(-x))


def _silu(x):
    return x * _sigmoid(x)


def _softplus(x):
    return jnp.maximum(x, 0.0) + jnp.log1p(jnp.exp(-jnp.abs(x)))


def _rms(x, w):
    return x * lax.rsqrt(jnp.mean(x * x, axis=-1, keepdims=True) + RMS_EPS) * w


def _inproj_kernel(x_ref, nw_ref, w_ref, o_ref, hn_ref):
    @pl.when(pl.program_id(1) == 0)
    def _():
        hn_ref[...] = _rms(x_ref[...], nw_ref[...]).astype(BF16)

    o_ref[...] = _dot(hn_ref[...], w_ref[...])


def _inproj(x_all, norm_w, w_packed):
    r = x_all.shape[0]
    tm = _pick(r, (1280, 640, 512, 256, 128))
    tn = 768
    return pl.pallas_call(
        _inproj_kernel,
        out_shape=jax.ShapeDtypeStruct((r, P_TOTAL), F32),
        grid=(r // tm, P_TOTAL // tn),
        in_specs=[
            pl.BlockSpec((tm, D_MODEL), lambda i, j: (i, 0)),
            pl.BlockSpec((1, D_MODEL), lambda i, j: (0, 0)),
            pl.BlockSpec((D_MODEL, tn), lambda i, j: (0, j)),
        ],
        out_specs=pl.BlockSpec((tm, tn), lambda i, j: (i, j)),
        scratch_shapes=[pltpu.VMEM((tm, D_MODEL), BF16)],
        compiler_params=pltpu.CompilerParams(
            dimension_semantics=("parallel", "arbitrary"), vmem_limit_bytes=VMEM_LIMIT),
        name="inproj",
    )(x_all, norm_w, w_packed)


def _mla_prep_kernel(qd_ref, kvc_ref, sm_ref, c_ref, s_ref, qn_ref, kvn_ref, wq_ref, wqs_ref, wk_ref, wv_ref,
                     q_out, k_out, v_out, ckv_out, kr_out):
    cos = c_ref[...]
    sin = s_ref[...]
    qn = _rms(qd_ref[...], qn_ref[...]).astype(BF16)
    q = _dot(qn, wq_ref[...])
    qs = _dot(qn, wqs_ref[...])
    for h in range(MLA_HEADS):
        sl = slice(h * LANES, (h + 1) * LANES)
        q_out[h] = ((q[:, sl] * cos + qs[:, sl] * sin) * MLA_SCALE).astype(BF16)
    ckv = _rms(kvc_ref[...], kvn_ref[...])
    ckv_out[...] = ckv
    cb = ckv.astype(BF16)
    sm = sm_ref[...]
    lane = lax.broadcasted_iota(jnp.int32, sm.shape, 1)
    cos_k = jnp.where((lane >= SM_KR) & (lane < SM_KR + ROPE_DIM), cos, 0.0)
    krot = sm * cos_k + pltpu.roll(sm, LANES - ROPE_DIM, 1) * sin
    kr_out[...] = krot
    kk = _dot(cb, wk_ref[...])
    vv = _dot(cb, wv_ref[...])
    for h in range(MLA_HEADS):
        sl = slice(h * LANES, (h + 1) * LANES)
        k_out[h] = (kk[:, sl] + krot).astype(BF16)
        v_out[h] = vv[:, sl].astype(BF16)


def _mla_prep(proj, cos_t, sin_t, q_norm, kv_norm, wq, wqs, wk, wv):
    r = proj.shape[0]
    tm = _pick(r, (640, 512, 256, 128))
    hw = MLA_HEADS * LANES
    full = lambda shape: pl.BlockSpec(shape, lambda i: (0,) * len(shape))
    head_out = pl.BlockSpec((MLA_HEADS, tm, LANES), lambda i: (0, i, 0))
    return pl.pallas_call(
        _mla_prep_kernel,
        out_shape=(
            jax.ShapeDtypeStruct((MLA_HEADS, r, LANES), BF16),
            jax.ShapeDtypeStruct((MLA_HEADS, r, LANES), BF16),
            jax.ShapeDtypeStruct((MLA_HEADS, r, LANES), BF16),
            jax.ShapeDtypeStruct((r, KV_LORA), F32),
            jax.ShapeDtypeStruct((r, LANES), F32),
        ),
        grid=(r // tm,),
        in_specs=[
            pl.BlockSpec((tm, Q_LORA), lambda i: (i, P_QD // Q_LORA)),
            pl.BlockSpec((tm, KV_LORA), lambda i: (i, P_KVC // KV_LORA)),
            pl.BlockSpec((tm, LANES), lambda i: (i, P_SMALL // LANES)),
            pl.BlockSpec((tm, LANES), lambda i: (i, 0)),
            pl.BlockSpec((tm, LANES), lambda i: (i, 0)),
            full((1, Q_LORA)), full((1, KV_LORA)),
            full((Q_LORA, hw)), full((Q_LORA, hw)), full((KV_LORA, hw)), full((KV_LORA, hw)),
        ],
        out_specs=(head_out, head_out, head_out,
                   pl.BlockSpec((tm, KV_LORA), lambda i: (i, 0)),
                   pl.BlockSpec((tm, LANES), lambda i: (i, 0))),
        compiler_params=pltpu.CompilerParams(dimension_semantics=("parallel",), vmem_limit_bytes=VMEM_LIMIT),
        name="mla_prep",
    )(proj, proj, proj, cos_t, sin_t, q_norm, kv_norm, wq, wqs, wk, wv)


def _attn_prompt_kernel(q_ref, k_ref, v_ref, km_ref, vm_ref, o_ref, *, tq):
    qi = pl.program_id(2)
    out = None
    for hh in range(2):
        q = q_ref[hh]
        s0 = _dot_nt(q, km_ref[hh])
        m = jnp.max(s0, axis=1, keepdims=True)
        p0 = jnp.exp(s0 - m)
        l = jnp.sum(p0, axis=1, keepdims=True)
        acc = _dot(p0.astype(BF16), vm_ref[hh])

        def step(kb, vb, carry, mask):
            m, l, acc = carry
            s = _dot_nt(q, kb)
            if mask is not None:
                s = jnp.where(mask, s, -1e30)
            m_new = jnp.maximum(m, jnp.max(s, axis=1, keepdims=True))
            a = jnp.exp(m - m_new)
            p = jnp.exp(s - m_new)
            return m_new, a * l + jnp.sum(p, axis=1, keepdims=True), a * acc + _dot(p.astype(BF16), vb)

        def body(j, carry):
            off = pl.multiple_of(j * tq, tq)
            return step(k_ref[hh, pl.ds(off, tq), :], v_ref[hh, pl.ds(off, tq), :], carry, None)

        m, l, acc = lax.fori_loop(0, qi, body, (m, l, acc))
        off = pl.multiple_of(qi * tq, tq)
        row = lax.broadcasted_iota(jnp.int32, (tq, tq), 0)
        col = lax.broadcasted_iota(jnp.int32, (tq, tq), 1)
        m, l, acc = step(k_ref[hh, pl.ds(off, tq), :], v_ref[hh, pl.ds(off, tq), :], (m, l, acc), col <= row)
        o = acc / l
        out = o if out is None else out + o
    o_ref[...] = out.astype(o_ref.dtype)


def _attn_prompt(q, k, v, batch, seq, meta_row):
    tq = _pick(seq, ATTN_TQ)
    nq = seq // tq
    kern = functools.partial(_attn_prompt_kernel, tq=tq)
    return pl.pallas_call(
        kern,
        out_shape=jax.ShapeDtypeStruct((batch * seq, D_MODEL), BF16),
        grid=(batch, MLA_HEADS // 2, nq),
        in_specs=[
            pl.BlockSpec((2, tq, LANES), lambda b, p, i: (p, b * nq + i, 0)),
            pl.BlockSpec((2, seq, LANES), lambda b, p, i: (p, b, 0)),
            pl.BlockSpec((2, seq, LANES), lambda b, p, i: (p, b, 0)),
            pl.BlockSpec((2, N_META, LANES), lambda b, p, i: (p, meta_row // N_META, 0)),
            pl.BlockSpec((2, N_META, LANES), lambda b, p, i: (p, meta_row // N_META, 0)),
        ],
        out_specs=pl.BlockSpec((tq, LANES), lambda b, p, i: (b * nq + i, p)),
        compiler_params=pltpu.CompilerParams(
            dimension_semantics=("parallel", "parallel", "arbitrary"), vmem_limit_bytes=VMEM_LIMIT),
        name="attn_prompt",
    )(q, k, v, k, v)


def _gate_lanes(sm, alog_ref, dtb_ref):
    g = -jnp.exp(alog_ref[...]) * _softplus(sm + dtb_ref[...])
    beta = _sigmoid(sm)
    return g, beta


def _qkv_heads(xc):
    xf = _silu(xc)
    qs, ks, vs = [], [], []
    for h in range(GDN_HEADS):
        q = xf[:, h * GDN_DK:(h + 1) * GDN_DK]
        k = xf[:, GDN_KEY + h * GDN_DK:GDN_KEY + (h + 1) * GDN_DK]
        qs.append(q * lax.rsqrt(jnp.sum(q * q, axis=-1, keepdims=True) + RMS_EPS) * (GDN_DK ** -0.5))
        ks.append(k * lax.rsqrt(jnp.sum(k * k, axis=-1, keepdims=True) + RMS_EPS))
        vs.append(xf[:, 2 * GDN_KEY + h * GDN_DV:2 * GDN_KEY + (h + 1) * GDN_DV])
    return qs, ks, vs


def _split_bf16(x):
    hi = x.astype(BF16)
    return hi, (x - hi.astype(F32)).astype(BF16)


def _dot_split(a, b):
    return _dot(a[0], b[0]) + (_dot(a[0], b[1]) + _dot(a[1], b[0]))


def _unit_lower_inverses(mats, c):
    row = lax.broadcasted_iota(jnp.int32, (c, c), 0)
    col = lax.broadcasted_iota(jnp.int32, (c, c), 1)
    eye = jnp.where(row == col, 1.0, 0.0)
    ps = [-a for a in mats]
    ts = [eye + p for p in ps]
    span = 2
    while span < c:
        psp = [_split_bf16(p) for p in ps]
        ps = [_dot_split(p, p) for p in psp]
        psp = [_split_bf16(p) for p in ps]
        ts = [t + _dot_split(p, _split_bf16(t)) for p, t in zip(psp, ts)]
        span *= 2
    return ts


def _gdn_chunk_terms(xs, sm, conv_ref, alog_ref, dtb_ref, c):
    heads = range(GDN_HEADS)
    xc = xs[0] * conv_ref[0:1, :]
    for j in range(1, CONV_W):
        xc = xc + xs[j] * conv_ref[j:j + 1, :]
    qs, ks, vs = _qkv_heads(xc)
    g, beta = _gate_lanes(sm, alog_ref, dtb_ref)
    row = lax.broadcasted_iota(jnp.int32, (c, c), 0)
    col = lax.broadcasted_iota(jnp.int32, (c, c), 1)
    causal = col <= row
    strict = col < row
    gcum = jnp.dot(jnp.where(causal, 1.0, 0.0), g, precision=HIGHEST, preferred_element_type=F32)
    gcum_t = lax.dot_general(g, jnp.where(col >= row, 1.0, 0.0), (((0,), (0,)), ((), ())),
                             precision=HIGHEST, preferred_element_type=F32)
    gc = [gcum[:, SM_A + h:SM_A + h + 1] for h in heads]
    gr = [gcum_t[SM_A + h:SM_A + h + 1, :] for h in heads]
    bc = [beta[:, SM_B + h:SM_B + h + 1] for h in heads]
    decay = [jnp.where(causal, jnp.exp(jnp.where(causal, gc[h] - gr[h], 0.0)), 0.0) for h in heads]
    kb = [ks[h] * bc[h] for h in heads]
    kbf = [ks[h].astype(BF16) for h in heads]
    a = [jnp.where(strict, _dot_nt(kb[h].astype(BF16), kbf[h]) * decay[h], 0.0) for h in heads]
    t = _unit_lower_inverses(a, c)
    eg = [jnp.exp(gc[h]) for h in heads]
    sol = [_dot(t[h].astype(BF16), jnp.concatenate([vs[h] * bc[h], kb[h] * eg[h]], axis=1).astype(BF16))
           for h in heads]
    u = [sol[h][:, :GDN_DV] for h in heads]
    w = [sol[h][:, GDN_DV:] for h in heads]
    attn = [jnp.where(causal, _dot_nt(qs[h].astype(BF16), kbf[h]) * decay[h], 0.0) for h in heads]
    qg = [qs[h] * eg[h] for h in heads]
    g_last = gcum[c - 1:c, :]
    kd = [ks[h] * jnp.exp(g_last[:, SM_A + h:SM_A + h + 1] - gc[h]) for h in heads]
    return u, w, qg, kd, attn, jnp.exp(g_last)


def _gdn_meta_kernel(x_ref, sm_ref, conv_ref, alog_ref, dtb_ref, s_out, xe_ref):
    c = N_META
    xe_ref[0:8, :] = jnp.zeros((8, GDN_QKV), F32)
    xe_ref[8:8 + c, :] = x_ref[...]
    xs = [xe_ref[pl.ds(8 - (CONV_W - 1) + j, c), :] for j in range(CONV_W)]
    u, _, _, kd, _, _ = _gdn_chunk_terms(xs, sm_ref[...], conv_ref, alog_ref, dtb_ref, c)
    for h in range(GDN_HEADS):
        s_out[h] = _dot_tn(kd[h].astype(BF16), u[h].astype(BF16))


def _gdn_meta(proj, conv_w, alog_v, dtb_v, meta_row):
    full = lambda shape: pl.BlockSpec(shape, lambda i: (0,) * len(shape))
    return pl.pallas_call(
        _gdn_meta_kernel,
        out_shape=jax.ShapeDtypeStruct((GDN_HEADS, GDN_DK, GDN_DV), F32),
        grid=(1,),
        in_specs=[
            pl.BlockSpec((N_META, GDN_QKV), lambda i: (meta_row // N_META, 0)),
            pl.BlockSpec((N_META, LANES), lambda i: (meta_row // N_META, P_SMALL // LANES)),
            full((CONV_W, GDN_QKV)), full((1, LANES)), full((1, LANES)),
        ],
        out_specs=full((GDN_HEADS, GDN_DK, GDN_DV)),
        scratch_shapes=[pltpu.VMEM((8 + N_META, GDN_QKV), F32)],
        compiler_params=pltpu.CompilerParams(vmem_limit_bytes=VMEM_LIMIT),
        name="gdn_meta",
    )(proj, proj, conv_w, alog_v, dtb_v)


def _gdn_terms_kernel(x_ref, hist_ref, sm_ref, conv_ref, alog_ref, dtb_ref,
                      u_out, w_out, qg_out, kd_out, attn_out, dec_out, xe_ref):
    c = CHUNK
    xe_ref[0:8, :] = hist_ref[...]
    xe_ref[8:8 + c, :] = x_ref[...]
    xs = [xe_ref[pl.ds(8 - (CONV_W - 1) + j, c), :] for j in range(CONV_W)]
    u, w, qg, kd, attn, dec = _gdn_chunk_terms(xs, sm_ref[...], conv_ref, alog_ref, dtb_ref, c)
    for h in range(GDN_HEADS):
        sl = slice(h * GDN_DV, (h + 1) * GDN_DV)
        u_out[:, sl] = u[h]
        w_out[:, sl] = w[h].astype(BF16)
        qg_out[:, sl] = qg[h].astype(BF16)
        kd_out[:, sl] = kd[h].astype(BF16)
        attn_out[h] = attn[h].astype(BF16)
    dec_out[0] = dec


def _gdn_terms(proj, conv_w, alog_v, dtb_v, batch, seq, meta_row):
    nc = seq // CHUNK
    n = batch * seq
    full = lambda shape: pl.BlockSpec(shape, lambda i: (0,) * len(shape))
    rows = lambda: pl.BlockSpec((CHUNK, D_MODEL), lambda i: (i, 0))

    def hist_index(i):
        return (jnp.where(i % nc == 0, (meta_row + N_META) // 8, i * (CHUNK // 8)) - 1, 0)

    return pl.pallas_call(
        _gdn_terms_kernel,
        out_shape=(jax.ShapeDtypeStruct((n, D_MODEL), F32),
                   jax.ShapeDtypeStruct((n, D_MODEL), BF16),
                   jax.ShapeDtypeStruct((n, D_MODEL), BF16),
                   jax.ShapeDtypeStruct((n, D_MODEL), BF16),
                   jax.ShapeDtypeStruct((GDN_HEADS, n, CHUNK), BF16),
                   jax.ShapeDtypeStruct((batch * nc, 1, LANES), F32)),
        grid=(batch * nc,),
        in_specs=[
            pl.BlockSpec((CHUNK, GDN_QKV), lambda i: (i, 0)),
            pl.BlockSpec((8, GDN_QKV), hist_index),
            pl.BlockSpec((CHUNK, LANES), lambda i: (i, P_SMALL // LANES)),
            full((CONV_W, GDN_QKV)), full((1, LANES)), full((1, LANES)),
        ],
        out_specs=(rows(), rows(), rows(), rows(),
                   pl.BlockSpec((GDN_HEADS, CHUNK, CHUNK), lambda i: (0, i, 0)),
                   pl.BlockSpec((1, 1, LANES), lambda i: (i, 0, 0))),
        scratch_shapes=[pltpu.VMEM((8 + CHUNK, GDN_QKV), F32)],
        compiler_params=pltpu.CompilerParams(dimension_semantics=("parallel",), vmem_limit_bytes=VMEM_LIMIT),
        name="gdn_terms",
    )(proj, proj, proj, conv_w, alog_v, dtb_v)


def _gdn_scan_kernel(u_ref, w_ref, qg_ref, kd_ref, attn_ref, dec_ref, s0_ref, o_ref, s_out, st_ref, *, cpg):
    c = CHUNK
    heads = range(GDN_HEADS)

    @pl.when(pl.program_id(1) == 0)
    def _():
        st_ref[...] = s0_ref[...]

    def chunk(ci, carry):
        rows = pl.ds(pl.multiple_of(ci * c, c), c)
        dec = dec_ref[ci]
        sl = [slice(h * GDN_DV, (h + 1) * GDN_DV) for h in heads]
        s_old = [st_ref[h] for h in heads]
        sb = [s.astype(BF16) for s in s_old]
        lhs = [jnp.concatenate([w_ref[rows, sl[h]], qg_ref[rows, sl[h]]], axis=0) for h in heads]
        r = [_dot(lhs[h], sb[h]) for h in heads]
        vnb = [(u_ref[rows, sl[h]] - r[h][:c]).astype(BF16) for h in heads]
        out = [r[h][c:] + _dot(attn_ref[h, rows, :], vnb[h]) for h in heads]
        upd = [_dot_tn(kd_ref[rows, sl[h]], vnb[h]) for h in heads]
        for h in heads:
            o_ref[rows, sl[h]] = out[h]
            st_ref[h] = s_old[h] * dec[:, SM_A + h:SM_A + h + 1] + upd[h]
        return carry

    lax.fori_loop(0, cpg, chunk, 0)

    @pl.when(pl.program_id(1) == pl.num_programs(1) - 1)
    def _():
        s_out[0] = st_ref[...]


def _gdn_scan(u, w, qg, kd, attn, dec, s_meta, batch, seq):
    nc = seq // CHUNK
    cpg = _pick(nc, (8, 4, 2, 1))
    ng = nc // cpg
    rows = lambda: pl.BlockSpec((cpg * CHUNK, D_MODEL), lambda b, g: (b * ng + g, 0))
    kern = functools.partial(_gdn_scan_kernel, cpg=cpg)
    return pl.pallas_call(
        kern,
        out_shape=(jax.ShapeDtypeStruct((batch * seq, D_MODEL), F32),
                   jax.ShapeDtypeStruct((batch, GDN_HEADS, GDN_DK, GDN_DV), F32)),
        grid=(batch, ng),
        in_specs=[rows(), rows(), rows(), rows(),
                  pl.BlockSpec((GDN_HEADS, cpg * CHUNK, CHUNK), lambda b, g: (0, b * ng + g, 0)),
                  pl.BlockSpec((cpg, 1, LANES), lambda b, g: (b * ng + g, 0, 0)),
                  pl.BlockSpec((GDN_HEADS, GDN_DK, GDN_DV), lambda b, g: (0, 0, 0))],
        out_specs=(rows(),
                   pl.BlockSpec((1, GDN_HEADS, GDN_DK, GDN_DV), lambda b, g: (b, 0, 0, 0))),
        scratch_shapes=[pltpu.VMEM((GDN_HEADS, GDN_DK, GDN_DV), F32)],
        compiler_params=pltpu.CompilerParams(
            dimension_semantics=("parallel", "arbitrary"), vmem_limit_bytes=VMEM_LIMIT),
        name="gdn_scan",
    )(u, w, qg, kd, attn, dec, s_meta)


def _gdn_sample_kernel(x_ref, sm_ref, cs_ref, st_ref, conv_ref, alog_ref, dtb_ref, o_ref, s_out, *, nb):
    xc = x_ref[...] * conv_ref[CONV_W - 1:CONV_W, :]
    for j in range(CONV_W - 1):
        xc = xc + cs_ref[:, j, :] * conv_ref[j:j + 1, :]
    qs, ks, vs = _qkv_heads(xc)
    g, beta = _gate_lanes(sm_ref[...], alog_ref, dtb_ref)
    eg = jnp.exp(g)
    for h in range(GDN_HEADS):
        q_t = qs[h].T
        k_t = ks[h].T
        for b in range(nb):
            kcol = k_t[:, b:b + 1]
            s1 = st_ref[b, h] * eg[b:b + 1, SM_A + h:SM_A + h + 1]
            r = jnp.sum(s1 * kcol, axis=0, keepdims=True)
            delta = (vs[h][b:b + 1, :] - r) * beta[b:b + 1, SM_B + h:SM_B + h + 1]
            s2 = s1 + kcol * delta
            s_out[b, h] = s2
            o_ref[b:b + 1, h * GDN_DV:(h + 1) * GDN_DV] = jnp.sum(s2 * q_t[:, b:b + 1], axis=0, keepdims=True)


def _gdn_sample(proj, state_conv, state_gdn, conv_w, alog_v, dtb_v, row0):
    ns = state_gdn.shape[0]
    nb = 8
    full = lambda shape: pl.BlockSpec(shape, lambda i: (0,) * len(shape))
    kern = functools.partial(_gdn_sample_kernel, nb=nb)
    return pl.pallas_call(
        kern,
        out_shape=(jax.ShapeDtypeStruct((ns, D_MODEL), F32),
                   jax.ShapeDtypeStruct(state_gdn.shape, F32)),
        grid=(ns // nb,),
        in_specs=[
            pl.BlockSpec((nb, GDN_QKV), lambda i: (row0 // nb + i, 0)),
            pl.BlockSpec((nb, LANES), lambda i: (row0 // nb + i, P_SMALL // LANES)),
            pl.BlockSpec((nb, CONV_W - 1, GDN_QKV), lambda i: (i, 0, 0)),
            pl.BlockSpec((nb, GDN_HEADS, GDN_DK, GDN_DV), lambda i: (i, 0, 0, 0)),
            full((CONV_W, GDN_QKV)), full((1, LANES)), full((1, LANES)),
        ],
        out_specs=(pl.BlockSpec((nb, D_MODEL), lambda i: (i, 0)),
                   pl.BlockSpec((nb, GDN_HEADS, GDN_DK, GDN_DV), lambda i: (i, 0, 0, 0))),
        compiler_params=pltpu.CompilerParams(dimension_semantics=("parallel",), vmem_limit_bytes=VMEM_LIMIT),
        name="gdn_sample",
    )(proj, proj, state_conv, state_gdn, conv_w, alog_v, dtb_v)


def _q_absorb_kernel(q_ref, wukt_ref, ql_out, qr_out):
    for h in range(MLA_HEADS):
        q = q_ref[h]
        ql_out[h] = _dot(q, wukt_ref[h]).astype(BF16)
        qr_out[h] = q[:, NOPE_DIM:NOPE_DIM + ROPE_DIM]


def _q_absorb(q, wukt, ns, row0):
    return pl.pallas_call(
        _q_absorb_kernel,
        out_shape=(jax.ShapeDtypeStruct((MLA_HEADS, ns, KV_LORA), BF16),
                   jax.ShapeDtypeStruct((MLA_HEADS, ns, ROPE_DIM), BF16)),
        grid=(1,),
        in_specs=[pl.BlockSpec((MLA_HEADS, ns, LANES), lambda i: (0, row0 // ns, 0)),
                  pl.BlockSpec((MLA_HEADS, LANES, KV_LORA), lambda i: (0, 0, 0))],
        out_specs=(pl.BlockSpec((MLA_HEADS, ns, KV_LORA), lambda i: (0, 0, 0)),
                   pl.BlockSpec((MLA_HEADS, ns, ROPE_DIM), lambda i: (0, 0, 0))),
        compiler_params=pltpu.CompilerParams(vmem_limit_bytes=VMEM_LIMIT),
        name="q_absorb",
    )(q, wukt)


def _mla_sample_kernel(pt_ref, ql_ref, qr_ref, cn_ref, krn_ref, cc_hbm, cr_hbm, o_ref,
                       cbuf, rbuf, sem, m_ref, l_ref, acc_ref, *, ppc):
    nch = pl.num_programs(1)
    ch = pl.program_id(1)
    t = pl.program_id(0) * nch + ch
    total = pl.num_programs(0) * nch
    slot = t % 2

    def page_copies(step, slot_, i):
        page = pt_ref[step * ppc + i]
        return (pltpu.make_async_copy(cc_hbm.at[page], cbuf.at[slot_, i], sem.at[slot_, 0]),
                pltpu.make_async_copy(cr_hbm.at[page], rbuf.at[slot_, :, pl.ds(i * PAGE_SIZE, PAGE_SIZE)],
                                      sem.at[slot_, 1]))

    def start_all(step, slot_):
        for i in range(ppc):
            cc, cr = page_copies(step, slot_, i)
            cc.start()
            cr.start()

    @pl.when(t == 0)
    def _():
        start_all(0, 0)

    @pl.when(t + 1 < total)
    def _():
        start_all(t + 1, 1 - slot)

    for i in range(ppc):
        cc, cr = page_copies(t, slot, i)
        cc.wait()
        cr.wait()

    @pl.when(ch == 0)
    def _():
        m_ref[...] = jnp.full(m_ref.shape, -1e30, F32)
        l_ref[...] = jnp.zeros(l_ref.shape, F32)
        acc_ref[...] = jnp.zeros(acc_ref.shape, F32)

    ql = ql_ref[0]
    qr = qr_ref[0]
    c = cbuf[slot].reshape(ppc * PAGE_SIZE, KV_LORA).astype(BF16)
    s = _dot_nt(ql, c) + _dot(qr, rbuf[slot].astype(BF16))
    m_old = m_ref[...]
    m_new = jnp.maximum(m_old, jnp.max(s, axis=1, keepdims=True))
    a = jnp.exp(m_old - m_new)
    p = jnp.exp(s - m_new)
    l_new = a * l_ref[...] + jnp.sum(p, axis=1, keepdims=True)
    acc_new = a * acc_ref[...] + _dot(p.astype(BF16), c)
    m_ref[...] = m_new
    l_ref[...] = l_new
    acc_ref[...] = acc_new

    @pl.when(ch == nch - 1)
    def _():
        cn = cn_ref[0]
        krn = krn_ref[0][:, SM_KR:SM_KR + ROPE_DIM]
        s_n = (jnp.sum(ql.astype(F32) * cn, axis=1, keepdims=True)
               + jnp.sum(qr.astype(F32) * krn, axis=1, keepdims=True))
        m2 = jnp.maximum(m_new, s_n)
        a2 = jnp.exp(m_new - m2)
        p_n = jnp.exp(s_n - m2)
        o_ref[0] = (a2 * acc_new + p_n * cn) / (a2 * l_new + p_n)


def _mla_sample(page_table, ql, qr, c_new, kr_new, cache_c, cache_r):
    ns, n_pages = page_table.shape
    ppc = _pick(n_pages, (32, 16, 8, 4, 2, 1))
    kern = functools.partial(_mla_sample_kernel, ppc=ppc)
    grid_spec = pltpu.PrefetchScalarGridSpec(
        num_scalar_prefetch=1,
        grid=(ns, n_pages // ppc),
        in_specs=[
            pl.BlockSpec((1, MLA_HEADS, KV_LORA), lambda b, c, pt: (b, 0, 0)),
            pl.BlockSpec((1, MLA_HEADS, ROPE_DIM), lambda b, c, pt: (b, 0, 0)),
            pl.BlockSpec((1, 1, KV_LORA), lambda b, c, pt: (b, 0, 0)),
            pl.BlockSpec((1, 1, LANES), lambda b, c, pt: (b, 0, 0)),
            pl.BlockSpec(memory_space=pl.ANY),
            pl.BlockSpec(memory_space=pl.ANY),
        ],
        out_specs=pl.BlockSpec((1, MLA_HEADS, KV_LORA), lambda b, c, pt: (b, 0, 0)),
        scratch_shapes=[
            pltpu.VMEM((2, ppc, PAGE_SIZE, KV_LORA), F32),
            pltpu.VMEM((2, ROPE_DIM, ppc * PAGE_SIZE), F32),
            pltpu.SemaphoreType.DMA((2, 2)),
            pltpu.VMEM((MLA_HEADS, 1), F32),
            pltpu.VMEM((MLA_HEADS, 1), F32),
            pltpu.VMEM((MLA_HEADS, KV_LORA), F32),
        ],
    )
    return pl.pallas_call(
        kern,
        out_shape=jax.ShapeDtypeStruct((ns, MLA_HEADS, KV_LORA), F32),
        grid_spec=grid_spec,
        compiler_params=pltpu.CompilerParams(
            dimension_semantics=("arbitrary", "arbitrary"), vmem_limit_bytes=VMEM_LIMIT),
        name="mla_sample",
    )(page_table.reshape(-1), ql, qr, c_new, kr_new, cache_c, cache_r)


def _o_proj_sample_kernel(ol_ref, wv_ref, o_ref):
    for p in range(MLA_HEADS // 2):
        acc = None
        for h in (2 * p, 2 * p + 1):
            part = _dot(ol_ref[h].astype(BF16), wv_ref[:, h * LANES:(h + 1) * LANES])
            acc = part if acc is None else acc + part
        o_ref[:, p * LANES:(p + 1) * LANES] = acc.astype(o_ref.dtype)


def _o_proj_sample(o_lat_t, wv):
    ns = o_lat_t.shape[1]
    return pl.pallas_call(
        _o_proj_sample_kernel,
        out_shape=jax.ShapeDtypeStruct((ns, D_MODEL), BF16),
        grid=(1,),
        in_specs=[pl.BlockSpec((MLA_HEADS, ns, KV_LORA), lambda i: (0, 0, 0)),
                  pl.BlockSpec((KV_LORA, MLA_HEADS * LANES), lambda i: (0, 0))],
        out_specs=pl.BlockSpec((ns, D_MODEL), lambda i: (0, 0)),
        compiler_params=pltpu.CompilerParams(vmem_limit_bytes=VMEM_LIMIT),
        name="o_proj_sample",
    )(o_lat_t, wv)


def _outproj_kernel(x_ref, om_ref, og_ref, z_ref, gm_ref, gg_ref, wo_ref, gn_ref, nf_ref, wr_ref, br_ref,
                    xmid_out, hf_out, route_out):
    og = og_ref[...]
    parts = []
    for h in range(GDN_HEADS):
        oh = og[:, h * GDN_DV:(h + 1) * GDN_DV]
        parts.append(oh * lax.rsqrt(jnp.mean(oh * oh, axis=-1, keepdims=True) + RMS_EPS))
    o_gdn = jnp.concatenate(parts, axis=1) * gn_ref[...] * _silu(z_ref[...])
    merged = _sigmoid(gm_ref[...]) * om_ref[...].astype(F32) + _sigmoid(gg_ref[...]) * o_gdn
    x_mid = x_ref[...] + _dot(merged.astype(BF16), wo_ref[...])
    xmid_out[...] = x_mid
    hf = _rms(x_mid, nf_ref[...]).astype(BF16)
    bits = lax.bitcast_convert_type(hf.astype(F32), jnp.uint32)
    half = D_MODEL // 2
    hf_out[...] = bits[:, half:] | (bits[:, :half] >> 16)

    logits = _dot(hf, wr_ref[...]) + br_ref[...]
    lane = lax.broadcasted_iota(jnp.int32, logits.shape, 1)
    neg = -jnp.inf
    big = 4 * LANES
    is_g = lane < N_GROUPS
    lg = jnp.where(is_g, logits, neg)
    mg = jnp.max(lg, axis=1, keepdims=True)
    grp = jnp.min(jnp.where(lg == mg, lane, big), axis=1, keepdims=True)
    gate_g = 1.0 / jnp.sum(jnp.where(is_g, jnp.exp(logits - mg), 0.0), axis=1, keepdims=True)
    e_lane = lane - N_GROUPS
    in_grp = (e_lane >= 0) & (e_lane < N_EXPERTS) & ((e_lane >> 3) == grp)
    le = jnp.where(in_grp, logits, neg)
    v1 = jnp.max(le, axis=1, keepdims=True)
    i1 = jnp.min(jnp.where(le == v1, lane, big), axis=1, keepdims=True)
    le2 = jnp.where(lane == i1, neg, le)
    v2 = jnp.max(le2, axis=1, keepdims=True)
    i2 = jnp.min(jnp.where(le2 == v2, lane, big), axis=1, keepdims=True)
    e = jnp.exp(v2 - v1)
    w1 = gate_g / (1.0 + e)
    w2 = gate_g * e / (1.0 + e)
    route = jnp.where(lane == 0, (i1 - N_GROUPS).astype(F32),
                      jnp.where(lane == 1, (i2 - N_GROUPS).astype(F32),
                                jnp.where(lane == 2, w1, jnp.where(lane == 3, w2, 0.0))))
    route_out[...] = route


def _row_tile(n, limit):
    t = limit - limit % 16
    while t >= 16:
        if n % t == 0:
            return t
        t -= 16
    raise ValueError(f"no row tile for {n}")


def _outproj(x_all, o_mla, o_gdn, proj, w_out, gn_t, norm_ffn, w_r, b_r, row0):
    n = o_mla.shape[0]
    tm = _pick(n, (512, 256, 128, 64, 32, 16))
    assert row0 % tm == 0
    r0 = row0 // tm
    full = lambda shape: pl.BlockSpec(shape, lambda i: (0,) * len(shape))
    row = lambda w: pl.BlockSpec((tm, w), lambda i: (i, 0))
    shared = lambda w, j=0: pl.BlockSpec((tm, w), lambda i, j=j: (r0 + i, j))
    return pl.pallas_call(
        _outproj_kernel,
        out_shape=(jax.ShapeDtypeStruct((n, D_MODEL), F32),
                   jax.ShapeDtypeStruct((n, D_MODEL // 2), jnp.uint32),
                   jax.ShapeDtypeStruct((n, LANES), F32)),
        grid=(n // tm,),
        in_specs=[shared(D_MODEL), row(D_MODEL), row(D_MODEL),
                  shared(D_MODEL, P_Z // D_MODEL), shared(D_MODEL, P_GM // D_MODEL), shared(D_MODEL, P_GG // D_MODEL),
                  full((D_MODEL, D_MODEL)), full((1, D_MODEL)), full((1, D_MODEL)),
                  full((D_MODEL, LANES)), full((1, LANES))],
        out_specs=(row(D_MODEL), row(D_MODEL // 2), row(LANES)),
        compiler_params=pltpu.CompilerParams(dimension_semantics=("parallel",), vmem_limit_bytes=VMEM_LIMIT),
        name="outproj_route",
    )(x_all, o_mla, o_gdn, proj, proj, proj, w_out, gn_t, norm_ffn, w_r, b_r)


def _moe_plan(eid, wgt, nsp):
    nt = eid.shape[0]
    ts = nt // nsp
    n_asg = ts * TOP_K
    nb = (n_asg + N_EXPERTS * (MOE_BLOCK - 1) + MOE_BLOCK - 1) // MOE_BLOCK
    e = eid.reshape(nsp, n_asg)
    bs = _pick(n_asg, (128, 64, 32, 16, 8, 4, 2, 1))
    onehot = (e[..., None] == jnp.arange(N_EXPERTS, dtype=jnp.int32)).astype(F32)
    blocks = onehot.reshape(nsp, n_asg // bs, bs, N_EXPERTS)
    within = jnp.einsum('ij,sbjk->sbik', jnp.tril(jnp.ones((bs, bs), F32)), blocks)
    totals = within[:, :, -1, :]
    offs = jnp.cumsum(totals, axis=1) - totals
    cs = (within + offs[:, :, None, :]).reshape(nsp, n_asg, N_EXPERTS)
    rank = jnp.sum(cs * onehot, axis=2) - 1.0
    counts = (offs[:, -1, :] + totals[:, -1, :]).astype(jnp.int32)
    padded = (counts + MOE_BLOCK - 1) // MOE_BLOCK * MOE_BLOCK
    pad_end = jnp.cumsum(padded, axis=1)
    pad_start = (pad_end - padded).astype(F32)
    dest = (jnp.sum(onehot * pad_start[:, None, :], axis=2) + rank).astype(jnp.int32)
    starts = jnp.arange(nb, dtype=jnp.int32) * MOE_BLOCK
    blk_e = jnp.minimum(jnp.sum(pad_end[:, None, :] <= starts[None, :, None], axis=2), N_EXPERTS - 1)
    nused = pad_end[:, -1] // MOE_BLOCK
    return (blk_e.astype(jnp.int32).reshape(-1), nused.astype(jnp.int32), dest.reshape(-1),
            wgt.reshape(-1), ts, nb)


def _split_pieces(k, ts, n_prompt, n_sample):
    lo, hi = k * ts, (k + 1) * ts
    pieces = []
    if lo < n_prompt:
        pieces.append((0, lo, 0, min(hi, n_prompt) - lo))
    if hi > n_prompt:
        start = max(lo, n_prompt)
        pieces.append((1, start - n_prompt, start - lo, hi - start))
    assert hi <= n_prompt + n_sample
    return pieces


def _moe_kernel(blk_ref, nused_ref, dest_ref, wgt_ref, hf_p, hf_q, xmid_p, xmid_q, wgu_ref, wd_ref, nfin_ref,
                y_p, y_q, hf_s, acc, xb, yb, tok_ref, w_ref, sem, *, ts, nb, rc, nsp, n_prompt, n_sample):
    s = pl.program_id(0)
    j = pl.program_id(1)
    half = D_MODEL // 2
    n_asg = ts * TOP_K

    @pl.when(j == 0)
    def _():
        def clear(i, carry):
            tok_ref[i] = 0
            w_ref[i] = 0.0
            return carry

        def fill(i, carry):
            d = dest_ref[s * n_asg + i]
            tok_ref[d] = i // TOP_K
            w_ref[d] = wgt_ref[s * n_asg + i]
            return carry

        lax.fori_loop(0, nb * MOE_BLOCK, clear, 0, unroll=8)
        lax.fori_loop(0, n_asg, fill, 0, unroll=8)

    def split_copies(k, load):
        cps = []
        for src, r0, l0, n in _split_pieces(k, ts, n_prompt, n_sample):
            if load:
                cps.append(pltpu.make_async_copy((hf_p, hf_q)[src].at[pl.ds(r0, n)], hf_s.at[pl.ds(l0, n)],
                                                 sem.at[0, src]))
                cps.append(pltpu.make_async_copy((xmid_p, xmid_q)[src].at[pl.ds(r0, n)], acc.at[pl.ds(l0, n)],
                                                 sem.at[1, src]))
            else:
                cps.append(pltpu.make_async_copy(acc.at[pl.ds(l0, n)], (y_p, y_q)[src].at[pl.ds(r0, n)],
                                                 sem.at[2, src]))
        return cps

    def run_copies(load):
        for k in range(nsp):
            @pl.when(s == k)
            def _(k=k):
                cps = split_copies(k, load)
                for cp in cps:
                    cp.start()
                for cp in cps:
                    cp.wait()

    @pl.when(j == 0)
    def _():
        run_copies(True)

    @pl.when(j < nused_ref[s])
    def _():
        base = j * MOE_BLOCK
        for r in range(MOE_BLOCK):
            xb[r:r + 1, :] = hf_s[pl.ds(tok_ref[base + r], 1), :]
        bits = xb[...]
        lo = lax.bitcast_convert_type(bits << 16, F32).astype(BF16)
        hi = lax.bitcast_convert_type(bits & jnp.uint32(0xFFFF0000), F32).astype(BF16)
        gu = _dot(lo, wgu_ref[0, :half, :]) + _dot(hi, wgu_ref[0, half:, :])
        hmid = _silu(gu[:, :D_EXPERT]) * gu[:, D_EXPERT:]
        yb[...] = _dot(hmid.astype(BF16), wd_ref[0])
        for r in range(MOE_BLOCK):
            tok = tok_ref[base + r]
            acc[pl.ds(tok, 1), :] = acc[pl.ds(tok, 1), :] + w_ref[base + r] * yb[r:r + 1, :]

    @pl.when(j == nb - 1)
    def _():
        def body(i, carry):
            rows = pl.ds(pl.multiple_of(i * rc, 8), rc)
            acc[rows, :] = _rms(acc[rows, :], nfin_ref[...])
            return carry

        lax.fori_loop(0, ts // rc, body, 0)
        run_copies(False)


def _moe(plan, hf_p, hf_q, xmid_p, xmid_q, wgu, wd, norm_final):
    blk_e, nused, dest, wgt, ts, nb = plan
    n_prompt, n_sample = xmid_p.shape[0], xmid_q.shape[0]
    nsp = (n_prompt + n_sample) // ts
    rc = _row_tile(ts, 512) if ts % 16 == 0 else 8
    kern = functools.partial(_moe_kernel, ts=ts, nb=nb, rc=rc, nsp=nsp, n_prompt=n_prompt, n_sample=n_sample)
    hbm = pl.BlockSpec(memory_space=pl.ANY)
    grid_spec = pltpu.PrefetchScalarGridSpec(
        num_scalar_prefetch=4,
        grid=(nsp, nb),
        in_specs=[
            hbm, hbm, hbm, hbm,
            pl.BlockSpec((1, D_MODEL, 2 * D_EXPERT), lambda s, j, be, nu, st, sw: (be[s * nb + j], 0, 0)),
            pl.BlockSpec((1, D_EXPERT, D_MODEL), lambda s, j, be, nu, st, sw: (be[s * nb + j], 0, 0)),
            pl.BlockSpec((1, D_MODEL), lambda s, j, be, nu, st, sw: (0, 0)),
        ],
        out_specs=(hbm, hbm),
        scratch_shapes=[
            pltpu.VMEM((ts, D_MODEL // 2), jnp.uint32),
            pltpu.VMEM((ts, D_MODEL), F32),
            pltpu.VMEM((MOE_BLOCK, D_MODEL // 2), jnp.uint32),
            pltpu.VMEM((MOE_BLOCK, D_MODEL), F32),
            pltpu.SMEM((nb * MOE_BLOCK,), jnp.int32),
            pltpu.SMEM((nb * MOE_BLOCK,), F32),
            pltpu.SemaphoreType.DMA((3, 2)),
        ],
    )
    return pl.pallas_call(
        kern,
        out_shape=(jax.ShapeDtypeStruct((n_prompt, D_MODEL), F32), jax.ShapeDtypeStruct((n_sample, D_MODEL), F32)),
        grid_spec=grid_spec,
        compiler_params=pltpu.CompilerParams(
            dimension_semantics=("arbitrary", "arbitrary"), vmem_limit_bytes=VMEM_LIMIT),
        name="moe",
    )(blk_e, nused, dest, wgt, hf_p, hf_q, xmid_p, xmid_q, wgu, wd, norm_final)


def _pack_w_in(w):
    kr = w[:, _OFF_KV + KV_LORA:_OFF_QKV]
    kr_sw = jnp.concatenate([kr[:, ROPE_DIM // 2:], kr[:, :ROPE_DIM // 2]], axis=1)
    small = jnp.concatenate([w[:, _OFF_B:_OFF_A], w[:, _OFF_A:_OFF_GM],
                             jnp.zeros((D_MODEL, SM_KR - 2 * GDN_HEADS), w.dtype), kr, kr_sw], axis=1)
    packed = jnp.concatenate([w[:, _OFF_QKV:_OFF_Z], w[:, _OFF_Z:_OFF_B], w[:, _OFF_GM:_OFF_GG], w[:, _OFF_GG:],
                              w[:, _OFF_KV:_OFF_KV + KV_LORA], small, w[:, :Q_LORA]], axis=1)
    return packed.astype(BF16)


def _pack_mla_weights(w_uq, w_uk, w_uv):
    zq = jnp.zeros((Q_LORA, MLA_HEADS, LANES - NOPE_DIM - ROPE_DIM), w_uq.dtype)
    wq = jnp.concatenate([w_uq, zq], axis=2).reshape(Q_LORA, MLA_HEADS * LANES)
    rope = w_uq[:, :, NOPE_DIM:]
    rope_sw = jnp.concatenate([rope[..., ROPE_DIM // 2:], rope[..., :ROPE_DIM // 2]], axis=2)
    wqs = jnp.concatenate([jnp.zeros((Q_LORA, MLA_HEADS, NOPE_DIM), w_uq.dtype), rope_sw, zq], axis=2)
    wqs = wqs.reshape(Q_LORA, MLA_HEADS * LANES)
    wk = jnp.concatenate([w_uk, jnp.zeros((KV_LORA, MLA_HEADS, LANES - NOPE_DIM), w_uk.dtype)], axis=2)
    wk = wk.reshape(KV_LORA, MLA_HEADS * LANES)
    zv = jnp.zeros((KV_LORA, MLA_HEADS // 2, V_DIM), w_uv.dtype)
    wv = jnp.stack([jnp.concatenate([w_uv[:, 0::2], zv], axis=2),
                    jnp.concatenate([zv, w_uv[:, 1::2]], axis=2)], axis=2)
    wv = wv.reshape(KV_LORA, MLA_HEADS * LANES)
    return wq.astype(BF16), wqs.astype(BF16), wk.astype(BF16), wv.astype(BF16)


def _rope_tables(pos):
    inv_freq = ROPE_THETA ** (-jnp.arange(0, ROPE_DIM, 2, dtype=F32) / ROPE_DIM)
    ang = pos.astype(F32)[:, None] * inv_freq[None, :]
    cos, sin = jnp.cos(ang), jnp.sin(ang)
    n = pos.shape[0]
    cos_t = jnp.concatenate([jnp.ones((n, NOPE_DIM), F32), cos, cos, jnp.zeros((n, ROPE_DIM), F32)], axis=1)
    sin_t = jnp.concatenate([jnp.zeros((n, NOPE_DIM), F32), -sin, sin, jnp.zeros((n, ROPE_DIM), F32)], axis=1)
    return cos_t, sin_t


def _head_lanes(v):
    return jnp.zeros((1, LANES), F32).at[0, SM_A:SM_A + GDN_HEADS].set(v.astype(F32))


def _moe_splits(nt):
    for nsp in (3, 4, 2, 1):
        if nt % (nsp * 8) == 0:
            return nsp
    return 1


def kernel(x_prompt, x_sample, cache_kv_latent, cache_k_rope, page_table, state_conv, state_gdn, meta_tokens,
           norm_mix, w_in, q_norm, w_uq, kv_norm, w_uk, w_uv, conv_w, a_log, dt_bias, gdn_norm, w_out, norm_ffn,
           w_group, b_group, w_router, b_router, w_gate, w_up, w_down, norm_final):
    batch, seq, _ = x_prompt.shape
    ns, dec_seq, _ = x_sample.shape
    assert dec_seq == 1 and w_in.shape[0] == 1 and seq % CHUNK == 0
    n_pages = page_table.shape[1]
    n_prompt = batch * seq
    nt = n_prompt + ns
    meta_row = nt
    n_rows = -(-(nt + N_META) // LANES) * LANES
    assert n_prompt % ns == 0 and nt % N_META == 0 and ns % 8 == 0

    x_all = jnp.concatenate([x_prompt.reshape(n_prompt, D_MODEL), x_sample.reshape(ns, D_MODEL),
                             meta_tokens.astype(x_prompt.dtype),
                             jnp.zeros((n_rows - nt - N_META, D_MODEL), x_prompt.dtype)], axis=0)
    pos = jnp.concatenate([jnp.tile(N_META + jnp.arange(seq), batch), jnp.full((ns,), n_pages * PAGE_SIZE),
                           jnp.arange(N_META), jnp.zeros((n_rows - nt - N_META,), jnp.int32)])
    cos_t, sin_t = _rope_tables(pos)
    w_packed = _pack_w_in(w_in[0])
    wq, wqs, wk, wv = _pack_mla_weights(w_uq[0], w_uk[0], w_uv[0])
    wukt = jnp.concatenate([jnp.transpose(w_uk[0], (1, 2, 0)),
                            jnp.zeros((MLA_HEADS, LANES - NOPE_DIM, KV_LORA), w_uk.dtype)], axis=1).astype(BF16)
    alog_v, dtb_v = _head_lanes(a_log[0]), _head_lanes(dt_bias[0])
    cw = conv_w[0].astype(F32)
    gn_t = jnp.tile(gdn_norm[0].astype(F32), GDN_HEADS)[None]
    w_r = jnp.concatenate([w_group[0], w_router[0],
                           jnp.zeros((D_MODEL, LANES - N_GROUPS - N_EXPERTS), w_group.dtype)], axis=1).astype(BF16)
    b_r = jnp.concatenate([b_group[0], b_router[0], jnp.zeros((LANES - N_GROUPS - N_EXPERTS,), b_group.dtype)])[None]
    wgu = jnp.concatenate([w_gate[0], w_up[0]], axis=2).astype(BF16)
    wd = w_down[0].astype(BF16)

    proj = _inproj(x_all, norm_mix[0][None].astype(F32), w_packed)
    q, k, v, ckv, krot = _mla_prep(proj, cos_t, sin_t, q_norm[0][None].astype(F32), kv_norm[0][None].astype(F32),
                                   wq, wqs, wk, wv)

    o_mla_p = _attn_prompt(q, k, v, batch, seq, meta_row)
    ql, qr = _q_absorb(q, wukt, ns, n_prompt)
    o_lat = _mla_sample(page_table, jnp.transpose(ql, (1, 0, 2)), jnp.transpose(qr, (1, 0, 2)),
                        ckv[n_prompt:nt].reshape(ns, 1, KV_LORA), krot[n_prompt:nt].reshape(ns, 1, LANES),
                        cache_kv_latent[0], jnp.swapaxes(cache_k_rope[0], 1, 2))
    o_mla_s = _o_proj_sample(jnp.transpose(o_lat, (1, 0, 2)), wv)

    s_meta = _gdn_meta(proj, cw, alog_v, dtb_v, meta_row)
    terms = _gdn_terms(proj, cw, alog_v, dtb_v, batch, seq, meta_row)
    o_gdn_p, gdn_p = _gdn_scan(*terms, s_meta, batch, seq)
    o_gdn_s, gdn_s = _gdn_sample(proj, state_conv[0], state_gdn[0], cw, alog_v, dtb_v, n_prompt)

    tail = (proj, w_out[0].astype(BF16), gn_t, norm_ffn[0][None].astype(F32), w_r, b_r.astype(F32))
    xmid_p, hf_p, route_p = _outproj(x_all, o_mla_p, o_gdn_p, *tail, 0)
    xmid_s, hf_s, route_s = _outproj(x_all, o_mla_s, o_gdn_s, *tail, n_prompt)
    route = jnp.concatenate([route_p[:, :2 * TOP_K], route_s[:, :2 * TOP_K]], axis=0)
    plan = _moe_plan(route[:, :TOP_K].astype(jnp.int32), route[:, TOP_K:], _moe_splits(nt))
    y_p, y_s = _moe(plan, hf_p, hf_s, xmid_p, xmid_s, wgu, wd, norm_final[None].astype(F32))

    def with_meta(rows, width):
        meta = jnp.broadcast_to(rows[meta_row:meta_row + N_META][None], (batch, N_META, width))
        return jnp.concatenate([meta, rows[:n_prompt].reshape(batch, seq, width)], axis=1)[None]

    k_rope = krot[:, SM_KR:SM_KR + ROPE_DIM]
    conv_p = jnp.stack([proj[(b + 1) * seq - (CONV_W - 1):(b + 1) * seq, :GDN_QKV] for b in range(batch)])
    conv_s = jnp.concatenate([state_conv[0][:, 1:].astype(F32), proj[n_prompt:nt, None, :GDN_QKV]], axis=1)
    return (y_p.reshape(batch, seq, D_MODEL), y_s.reshape(ns, 1, D_MODEL),
            with_meta(ckv, KV_LORA), with_meta(k_rope, ROPE_DIM),
            ckv[n_prompt:nt].reshape(1, ns, 1, KV_LORA), k_rope[n_prompt:nt].reshape(1, ns, 1, ROPE_DIM),
            conv_p[None], conv_s[None], gdn_p[None], gdn_s[None])
```

```python
import functools

import jax
import jax.numpy as jnp
from jax import lax
from jax.experimental import pallas as pl
from jax.experimental.pallas import tpu as pltpu

F32 = jnp.float32
BF16 = jnp.bfloat16
HIGHEST = lax.Precision.HIGHEST

D_MODEL = 1024
N_META = 16
RMS_EPS = 1e-6
MLA_HEADS = 16
Q_LORA = 384
KV_LORA = 256
NOPE_DIM = 64
ROPE_DIM = 32
V_DIM = 64
ROPE_THETA = 10000.0
MLA_SCALE = (NOPE_DIM + ROPE_DIM) ** -0.5
PAGE_SIZE = 128
GDN_HEADS = 8
GDN_DK = 128
GDN_DV = 128
GDN_KEY = GDN_HEADS * GDN_DK
GDN_QKV = 3 * GDN_KEY
CONV_W = 4
CHUNK = 64
N_GROUPS = 4
EXPERTS_PER_GROUP = 8
N_EXPERTS = 32
TOP_K = 2
D_EXPERT = 256
MOE_BLOCK = 128

_OFF_KV = Q_LORA
_OFF_QKV = _OFF_KV + KV_LORA + ROPE_DIM
_OFF_Z = _OFF_QKV + GDN_QKV
_OFF_B = _OFF_Z + GDN_KEY
_OFF_A = _OFF_B + GDN_HEADS
_OFF_GM = _OFF_A + GDN_HEADS
_OFF_GG = _OFF_GM + D_MODEL
P_QKV = 0
P_Z = 3072
P_GM = 4096
P_GG = 5120
P_KVC = 6144
P_SMALL = 6400
P_QD = 6528
P_TOTAL = 6912
SM_B = 0
SM_A = 8
SM_KR = 64

LANES = 128
VMEM_LIMIT = 56 * 1024 * 1024
ATTN_TQ = (512, 256, 128)


def _pick(n, candidates):
    for c in candidates:
        if n % c == 0:
            return c
    raise ValueError(f"no tile for {n} in {candidates}")


def _dot(a, b):
    return jnp.dot(a, b, preferred_element_type=F32)


def _dot_nt(a, b):
    return lax.dot_general(a, b, (((1,), (1,)), ((), ())), preferred_element_type=F32)


def _dot_tn(a, b):
    return lax.dot_general(a, b, (((0,), (0,)), ((), ())), preferred_element_type=F32)


def _sigmoid(x):
    return 1.0 / (1.0 + jnp.exp(-x))


def _silu(x):
    return x * _sigmoid(x)


def _softplus(x):
    return jnp.maximum(x, 0.0) + jnp.log1p(jnp.exp(-jnp.abs(x)))


def _rms(x, w):
    return x * lax.rsqrt(jnp.mean(x * x, axis=-1, keepdims=True) + RMS_EPS) * w


def _inproj_kernel(x_ref, nw_ref, w_ref, o_ref, hn_ref):
    @pl.when(pl.program_id(1) == 0)
    def _():
        hn_ref[...] = _rms(x_ref[...], nw_ref[...]).astype(BF16)

    o_ref[...] = _dot(hn_ref[...], w_ref[...])


def _inproj(x_all, norm_w, w_packed):
    r = x_all.shape[0]
    tm = _pick(r, (1280, 640, 512, 256, 128))
    tn = 768
    return pl.pallas_call(
        _inproj_kernel,
        out_shape=jax.ShapeDtypeStruct((r, P_TOTAL), F32),
        grid=(r // tm, P_TOTAL // tn),
        in_specs=[
            pl.BlockSpec((tm, D_MODEL), lambda i, j: (i, 0)),
            pl.BlockSpec((1, D_MODEL), lambda i, j: (0, 0)),
            pl.BlockSpec((D_MODEL, tn), lambda i, j: (0, j)),
        ],
        out_specs=pl.BlockSpec((tm, tn), lambda i, j: (i, j)),
        scratch_shapes=[pltpu.VMEM((tm, D_MODEL), BF16)],
        compiler_params=pltpu.CompilerParams(
            dimension_semantics=("parallel", "arbitrary"), vmem_limit_bytes=VMEM_LIMIT),
        name="inproj",
    )(x_all, norm_w, w_packed)


def _mla_prep_kernel(qd_ref, kvc_ref, sm_ref, c_ref, s_ref, qn_ref, kvn_ref, wq_ref, wqs_ref, wk_ref, wv_ref,
                     q_out, k_out, v_out, ckv_out, kr_out):
    cos = c_ref[...]
    sin = s_ref[...]
    qn = _rms(qd_ref[...], qn_ref[...]).astype(BF16)
    q = _dot(qn, wq_ref[...])
    qs = _dot(qn, wqs_ref[...])
    for h in range(MLA_HEADS):
        sl = slice(h * LANES, (h + 1) * LANES)
        q_out[h] = ((q[:, sl] * cos + qs[:, sl] * sin) * MLA_SCALE).astype(BF16)
    ckv = _rms(kvc_ref[...], kvn_ref[...])
    ckv_out[...] = ckv
    cb = ckv.astype(BF16)
    sm = sm_ref[...]
    lane = lax.broadcasted_iota(jnp.int32, sm.shape, 1)
    cos_k = jnp.where((lane >= SM_KR) & (lane < SM_KR + ROPE_DIM), cos, 0.0)
    krot = sm * cos_k + pltpu.roll(sm, LANES - ROPE_DIM, 1) * sin
    kr_out[...] = krot
    kk = _dot(cb, wk_ref[...])
    vv = _dot(cb, wv_ref[...])
    for h in range(MLA_HEADS):
        sl = slice(h * LANES, (h + 1) * LANES)
        k_out[h] = (kk[:, sl] + krot).astype(BF16)
        v_out[h] = vv[:, sl].astype(BF16)


def _mla_prep(proj, cos_t, sin_t, q_norm, kv_norm, wq, wqs, wk, wv):
    r = proj.shape[0]
    tm = _pick(r, (640, 512, 256, 128))
    hw = MLA_HEADS * LANES
    full = lambda shape: pl.BlockSpec(shape, lambda i: (0,) * len(shape))
    head_out = pl.BlockSpec((MLA_HEADS, tm, LANES), lambda i: (0, i, 0))
    return pl.pallas_call(
        _mla_prep_kernel,
        out_shape=(
            jax.ShapeDtypeStruct((MLA_HEADS, r, LANES), BF16),
            jax.ShapeDtypeStruct((MLA_HEADS, r, LANES), BF16),
            jax.ShapeDtypeStruct((MLA_HEADS, r, LANES), BF16),
            jax.ShapeDtypeStruct((r, KV_LORA), F32),
            jax.ShapeDtypeStruct((r, LANES), F32),
        ),
        grid=(r // tm,),
        in_specs=[
            pl.BlockSpec((tm, Q_LORA), lambda i: (i, P_QD // Q_LORA)),
            pl.BlockSpec((tm, KV_LORA), lambda i: (i, P_KVC // KV_LORA)),
            pl.BlockSpec((tm, LANES), lambda i: (i, P_SMALL // LANES)),
            pl.BlockSpec((tm, LANES), lambda i: (i, 0)),
            pl.BlockSpec((tm, LANES), lambda i: (i, 0)),
            full((1, Q_LORA)), full((1, KV_LORA)),
            full((Q_LORA, hw)), full((Q_LORA, hw)), full((KV_LORA, hw)), full((KV_LORA, hw)),
        ],
        out_specs=(head_out, head_out, head_out,
                   pl.BlockSpec((tm, KV_LORA), lambda i: (i, 0)),
                   pl.BlockSpec((tm, LANES), lambda i: (i, 0))),
        compiler_params=pltpu.CompilerParams(dimension_semantics=("parallel",), vmem_limit_bytes=VMEM_LIMIT),
        name="mla_prep",
    )(proj, proj, proj, cos_t, sin_t, q_norm, kv_norm, wq, wqs, wk, wv)


def _attn_prompt_kernel(q_ref, k_ref, v_ref, km_ref, vm_ref, o_ref, *, tq):
    qi = pl.program_id(2)
    hs = (0, 1)
    half = tq // 2
    q = [q_ref[h] for h in hs]

    def update(qs, rows, width, carry, mask):
        ms, ls, accs = carry
        s = [_dot_nt(qs[h], k_ref[h, pl.ds(rows, width), :]) for h in hs]
        if mask is not None:
            s = [jnp.where(mask, x, -1e30) for x in s]
        m_new = [jnp.maximum(ms[h], jnp.max(s[h], axis=1, keepdims=True)) for h in hs]
        a = [jnp.exp(ms[h] - m_new[h]) for h in hs]
        p = [jnp.exp(s[h] - m_new[h]) for h in hs]
        l_new = [a[h] * ls[h] + jnp.sum(p[h], axis=1, keepdims=True) for h in hs]
        acc_new = [a[h] * accs[h] + _dot(p[h].astype(BF16), v_ref[h, pl.ds(rows, width), :]) for h in hs]
        return m_new, l_new, acc_new

    s0 = [_dot_nt(q[h], km_ref[h]) for h in hs]
    m = [jnp.max(s0[h], axis=1, keepdims=True) for h in hs]
    p0 = [jnp.exp(s0[h] - m[h]) for h in hs]
    l = [jnp.sum(p0[h], axis=1, keepdims=True) for h in hs]
    acc = [_dot(p0[h].astype(BF16), vm_ref[h]) for h in hs]

    def body(j, carry):
        return update(q, pl.multiple_of(j * tq, tq), tq, carry, None)

    m, l, acc = lax.fori_loop(0, qi, body, (m, l, acc))

    off = pl.multiple_of(qi * tq, tq)
    row = lax.broadcasted_iota(jnp.int32, (tq, half), 0)
    col = lax.broadcasted_iota(jnp.int32, (tq, half), 1)
    m, l, acc = update(q, off, half, (m, l, acc), col <= row)
    bot = lambda xs: [x[half:] for x in xs]
    mb, lb, accb = update(bot(q), pl.multiple_of(off + half, half), half, (bot(m), bot(l), bot(acc)),
                          (col <= row)[:half])
    o = [jnp.concatenate([acc[h][:half] / l[h][:half], accb[h] / lb[h]], axis=0) for h in hs]
    o_ref[...] = (o[0] + o[1]).astype(o_ref.dtype)


def _attn_prompt(q, k, v, batch, seq, meta_row):
    tq = _pick(seq, ATTN_TQ)
    nq = seq // tq
    kern = functools.partial(_attn_prompt_kernel, tq=tq)
    return pl.pallas_call(
        kern,
        out_shape=jax.ShapeDtypeStruct((batch * seq, D_MODEL), BF16),
        grid=(batch, MLA_HEADS // 2, nq),
        in_specs=[
            pl.BlockSpec((2, tq, LANES), lambda b, p, i: (p, b * nq + i, 0)),
            pl.BlockSpec((2, seq, LANES), lambda b, p, i: (p, b, 0)),
            pl.BlockSpec((2, seq, LANES), lambda b, p, i: (p, b, 0)),
            pl.BlockSpec((2, N_META, LANES), lambda b, p, i: (p, meta_row // N_META, 0)),
            pl.BlockSpec((2, N_META, LANES), lambda b, p, i: (p, meta_row // N_META, 0)),
        ],
        out_specs=pl.BlockSpec((tq, LANES), lambda b, p, i: (b * nq + i, p)),
        compiler_params=pltpu.CompilerParams(
            dimension_semantics=("parallel", "parallel", "arbitrary"), vmem_limit_bytes=VMEM_LIMIT),
        name="attn_prompt",
    )(q, k, v, k, v)


def _gate_lanes(sm, alog_ref, dtb_ref):
    g = -jnp.exp(alog_ref[...]) * _softplus(sm + dtb_ref[...])
    beta = _sigmoid(sm)
    return g, beta


def _qkv_heads(xc):
    xf = _silu(xc)
    qs, ks, vs = [], [], []
    for h in range(GDN_HEADS):
        q = xf[:, h * GDN_DK:(h + 1) * GDN_DK]
        k = xf[:, GDN_KEY + h * GDN_DK:GDN_KEY + (h + 1) * GDN_DK]
        qs.append(q * lax.rsqrt(jnp.sum(q * q, axis=-1, keepdims=True) + RMS_EPS) * (GDN_DK ** -0.5))
        ks.append(k * lax.rsqrt(jnp.sum(k * k, axis=-1, keepdims=True) + RMS_EPS))
        vs.append(xf[:, 2 * GDN_KEY + h * GDN_DV:2 * GDN_KEY + (h + 1) * GDN_DV])
    return qs, ks, vs


def _split_bf16(x):
    hi = x.astype(BF16)
    return hi, (x - hi.astype(F32)).astype(BF16)


def _dot_split(a, b):
    return _dot(a[0], b[0]) + (_dot(a[0], b[1]) + _dot(a[1], b[0]))


def _unit_lower_inverses(mats, c):
    row = lax.broadcasted_iota(jnp.int32, (c, c), 0)
    col = lax.broadcasted_iota(jnp.int32, (c, c), 1)
    eye = jnp.where(row == col, 1.0, 0.0)
    ps = [-a for a in mats]
    ts = [eye + p for p in ps]
    span = 2
    while span < c:
        psp = [_split_bf16(p) for p in ps]
        ps = [_dot_split(p, p) for p in psp]
        psp = [_split_bf16(p) for p in ps]
        ts = [t + _dot_split(p, _split_bf16(t)) for p, t in zip(psp, ts)]
        span *= 2
    return ts


def _gdn_chunk_terms(xs, sm, conv_ref, alog_ref, dtb_ref, c):
    heads = range(GDN_HEADS)
    xc = xs[0] * conv_ref[0:1, :]
    for j in range(1, CONV_W):
        xc = xc + xs[j] * conv_ref[j:j + 1, :]
    qs, ks, vs = _qkv_heads(xc)
    g, beta = _gate_lanes(sm, alog_ref, dtb_ref)
    row = lax.broadcasted_iota(jnp.int32, (c, c), 0)
    col = lax.broadcasted_iota(jnp.int32, (c, c), 1)
    causal = col <= row
    strict = col < row
    gcum = jnp.dot(jnp.where(causal, 1.0, 0.0), g, precision=HIGHEST, preferred_element_type=F32)
    gcum_t = lax.dot_general(g, jnp.where(col >= row, 1.0, 0.0), (((0,), (0,)), ((), ())),
                             precision=HIGHEST, preferred_element_type=F32)
    gc = [gcum[:, SM_A + h:SM_A + h + 1] for h in heads]
    gr = [gcum_t[SM_A + h:SM_A + h + 1, :] for h in heads]
    bc = [beta[:, SM_B + h:SM_B + h + 1] for h in heads]
    decay = [jnp.where(causal, jnp.exp(jnp.where(causal, gc[h] - gr[h], 0.0)), 0.0) for h in heads]
    kb = [ks[h] * bc[h] for h in heads]
    kbf = [ks[h].astype(BF16) for h in heads]
    a = [jnp.where(strict, _dot_nt(kb[h].astype(BF16), kbf[h]) * decay[h], 0.0) for h in heads]
    t = _unit_lower_inverses(a, c)
    eg = [jnp.exp(gc[h]) for h in heads]
    sol = [_dot(t[h].astype(BF16), jnp.concatenate([vs[h] * bc[h], kb[h] * eg[h]], axis=1).astype(BF16))
           for h in heads]
    u = [sol[h][:, :GDN_DV] for h in heads]
    w = [sol[h][:, GDN_DV:] for h in heads]
    attn = [jnp.where(causal, _dot_nt(qs[h].astype(BF16), kbf[h]) * decay[h], 0.0) for h in heads]
    qg = [qs[h] * eg[h] for h in heads]
    g_last = gcum[c - 1:c, :]
    kd = [ks[h] * jnp.exp(g_last[:, SM_A + h:SM_A + h + 1] - gc[h]) for h in heads]
    return u, w, qg, kd, attn, jnp.exp(g_last)


def _gdn_meta_kernel(x_ref, sm_ref, conv_ref, alog_ref, dtb_ref, s_out, xe_ref):
    c = N_META
    xe_ref[0:8, :] = jnp.zeros((8, GDN_QKV), F32)
    xe_ref[8:8 + c, :] = x_ref[...]
    xs = [xe_ref[pl.ds(8 - (CONV_W - 1) + j, c), :] for j in range(CONV_W)]
    u, _, _, kd, _, _ = _gdn_chunk_terms(xs, sm_ref[...], conv_ref, alog_ref, dtb_ref, c)
    for h in range(GDN_HEADS):
        s_out[h] = _dot_tn(kd[h].astype(BF16), u[h].astype(BF16))


def _gdn_meta(proj, conv_w, alog_v, dtb_v, meta_row):
    full = lambda shape: pl.BlockSpec(shape, lambda i: (0,) * len(shape))
    return pl.pallas_call(
        _gdn_meta_kernel,
        out_shape=jax.ShapeDtypeStruct((GDN_HEADS, GDN_DK, GDN_DV), F32),
        grid=(1,),
        in_specs=[
            pl.BlockSpec((N_META, GDN_QKV), lambda i: (meta_row // N_META, 0)),
            pl.BlockSpec((N_META, LANES), lambda i: (meta_row // N_META, P_SMALL // LANES)),
            full((CONV_W, GDN_QKV)), full((1, LANES)), full((1, LANES)),
        ],
        out_specs=full((GDN_HEADS, GDN_DK, GDN_DV)),
        scratch_shapes=[pltpu.VMEM((8 + N_META, GDN_QKV), F32)],
        compiler_params=pltpu.CompilerParams(vmem_limit_bytes=VMEM_LIMIT),
        name="gdn_meta",
    )(proj, proj, conv_w, alog_v, dtb_v)


def _gdn_terms_kernel(x_ref, hist_ref, sm_ref, conv_ref, alog_ref, dtb_ref,
                      u_out, w_out, qg_out, kd_out, attn_out, dec_out, xe_ref):
    c = CHUNK
    xe_ref[0:8, :] = hist_ref[...]
    xe_ref[8:8 + c, :] = x_ref[...]
    xs = [xe_ref[pl.ds(8 - (CONV_W - 1) + j, c), :] for j in range(CONV_W)]
    u, w, qg, kd, attn, dec = _gdn_chunk_terms(xs, sm_ref[...], conv_ref, alog_ref, dtb_ref, c)
    for h in range(GDN_HEADS):
        sl = slice(h * GDN_DV, (h + 1) * GDN_DV)
        u_out[:, sl] = u[h]
        w_out[:, sl] = w[h].astype(BF16)
        qg_out[:, sl] = qg[h].astype(BF16)
        kd_out[:, sl] = kd[h].astype(BF16)
        attn_out[h] = attn[h].astype(BF16)
    dec_out[0] = dec


def _gdn_terms(proj, conv_w, alog_v, dtb_v, batch, seq, meta_row):
    nc = seq // CHUNK
    n = batch * seq
    full = lambda shape: pl.BlockSpec(shape, lambda i: (0,) * len(shape))
    rows = lambda: pl.BlockSpec((CHUNK, D_MODEL), lambda i: (i, 0))

    def hist_index(i):
        return (jnp.where(i % nc == 0, (meta_row + N_META) // 8, i * (CHUNK // 8)) - 1, 0)

    return pl.pallas_call(
        _gdn_terms_kernel,
        out_shape=(jax.ShapeDtypeStruct((n, D_MODEL), F32),
                   jax.ShapeDtypeStruct((n, D_MODEL), BF16),
                   jax.ShapeDtypeStruct((n, D_MODEL), BF16),
                   jax.ShapeDtypeStruct((n, D_MODEL), BF16),
                   jax.ShapeDtypeStruct((GDN_HEADS, n, CHUNK), BF16),
                   jax.ShapeDtypeStruct((batch * nc, 1, LANES), F32)),
        grid=(batch * nc,),
        in_specs=[
            pl.BlockSpec((CHUNK, GDN_QKV), lambda i: (i, 0)),
            pl.BlockSpec((8, GDN_QKV), hist_index),
            pl.BlockSpec((CHUNK, LANES), lambda i: (i, P_SMALL // LANES)),
            full((CONV_W, GDN_QKV)), full((1, LANES)), full((1, LANES)),
        ],
        out_specs=(rows(), rows(), rows(), rows(),
                   pl.BlockSpec((GDN_HEADS, CHUNK, CHUNK), lambda i: (0, i, 0)),
                   pl.BlockSpec((1, 1, LANES), lambda i: (i, 0, 0))),
        scratch_shapes=[pltpu.VMEM((8 + CHUNK, GDN_QKV), F32)],
        compiler_params=pltpu.CompilerParams(dimension_semantics=("parallel",), vmem_limit_bytes=VMEM_LIMIT),
        name="gdn_terms",
    )(proj, proj, proj, conv_w, alog_v, dtb_v)


def _gdn_scan_kernel(u_ref, w_ref, qg_ref, kd_ref, attn_ref, dec_ref, s0_ref, o_ref, s_out, st_ref, *, cpg):
    c = CHUNK
    heads = range(GDN_HEADS)

    @pl.when(pl.program_id(1) == 0)
    def _():
        st_ref[...] = s0_ref[...]

    def chunk(ci, carry):
        rows = pl.ds(pl.multiple_of(ci * c, c), c)
        dec = dec_ref[ci]
        sl = [slice(h * GDN_DV, (h + 1) * GDN_DV) for h in heads]
        s_old = [st_ref[h] for h in heads]
        sb = [s.astype(BF16) for s in s_old]
        lhs = [jnp.concatenate([w_ref[rows, sl[h]], qg_ref[rows, sl[h]]], axis=0) for h in heads]
        r = [_dot(lhs[h], sb[h]) for h in heads]
        vnb = [(u_ref[rows, sl[h]] - r[h][:c]).astype(BF16) for h in heads]
        out = [r[h][c:] + _dot(attn_ref[h, rows, :], vnb[h]) for h in heads]
        upd = [_dot_tn(kd_ref[rows, sl[h]], vnb[h]) for h in heads]
        for h in heads:
            o_ref[rows, sl[h]] = out[h]
            st_ref[h] = s_old[h] * dec[:, SM_A + h:SM_A + h + 1] + upd[h]
        return carry

    lax.fori_loop(0, cpg, chunk, 0)

    @pl.when(pl.program_id(1) == pl.num_programs(1) - 1)
    def _():
        s_out[0] = st_ref[...]


def _gdn_scan(u, w, qg, kd, attn, dec, s_meta, batch, seq):
    nc = seq // CHUNK
    cpg = _pick(nc, (8, 4, 2, 1))
    ng = nc // cpg
    rows = lambda: pl.BlockSpec((cpg * CHUNK, D_MODEL), lambda b, g: (b * ng + g, 0))
    kern = functools.partial(_gdn_scan_kernel, cpg=cpg)
    return pl.pallas_call(
        kern,
        out_shape=(jax.ShapeDtypeStruct((batch * seq, D_MODEL), F32),
                   jax.ShapeDtypeStruct((batch, GDN_HEADS, GDN_DK, GDN_DV), F32)),
        grid=(batch, ng),
        in_specs=[rows(), rows(), rows(), rows(),
                  pl.BlockSpec((GDN_HEADS, cpg * CHUNK, CHUNK), lambda b, g: (0, b * ng + g, 0)),
                  pl.BlockSpec((cpg, 1, LANES), lambda b, g: (b * ng + g, 0, 0)),
                  pl.BlockSpec((GDN_HEADS, GDN_DK, GDN_DV), lambda b, g: (0, 0, 0))],
        out_specs=(rows(),
                   pl.BlockSpec((1, GDN_HEADS, GDN_DK, GDN_DV), lambda b, g: (b, 0, 0, 0))),
        scratch_shapes=[pltpu.VMEM((GDN_HEADS, GDN_DK, GDN_DV), F32)],
        compiler_params=pltpu.CompilerParams(
            dimension_semantics=("parallel", "arbitrary"), vmem_limit_bytes=VMEM_LIMIT),
        name="gdn_scan",
    )(u, w, qg, kd, attn, dec, s_meta)


def _gdn_sample_kernel(x_ref, sm_ref, cs_ref, st_ref, conv_ref, alog_ref, dtb_ref, o_ref, s_out, *, nb):
    xc = x_ref[...] * conv_ref[CONV_W - 1:CONV_W, :]
    for j in range(CONV_W - 1):
        xc = xc + cs_ref[:, j, :] * conv_ref[j:j + 1, :]
    qs, ks, vs = _qkv_heads(xc)
    g, beta = _gate_lanes(sm_ref[...], alog_ref, dtb_ref)
    eg = jnp.exp(g)
    for h in range(GDN_HEADS):
        q_t = qs[h].T
        k_t = ks[h].T
        for b in range(nb):
            kcol = k_t[:, b:b + 1]
            s1 = st_ref[b, h] * eg[b:b + 1, SM_A + h:SM_A + h + 1]
            r = jnp.sum(s1 * kcol, axis=0, keepdims=True)
            delta = (vs[h][b:b + 1, :] - r) * beta[b:b + 1, SM_B + h:SM_B + h + 1]
            s2 = s1 + kcol * delta
            s_out[b, h] = s2
            o_ref[b:b + 1, h * GDN_DV:(h + 1) * GDN_DV] = jnp.sum(s2 * q_t[:, b:b + 1], axis=0, keepdims=True)


def _gdn_sample(proj, state_conv, state_gdn, conv_w, alog_v, dtb_v, row0):
    ns = state_gdn.shape[0]
    nb = 8
    full = lambda shape: pl.BlockSpec(shape, lambda i: (0,) * len(shape))
    kern = functools.partial(_gdn_sample_kernel, nb=nb)
    return pl.pallas_call(
        kern,
        out_shape=(jax.ShapeDtypeStruct((ns, D_MODEL), F32),
                   jax.ShapeDtypeStruct(state_gdn.shape, F32)),
        grid=(ns // nb,),
        in_specs=[
            pl.BlockSpec((nb, GDN_QKV), lambda i: (row0 // nb + i, 0)),
            pl.BlockSpec((nb, LANES), lambda i: (row0 // nb + i, P_SMALL // LANES)),
            pl.BlockSpec((nb, CONV_W - 1, GDN_QKV), lambda i: (i, 0, 0)),
            pl.BlockSpec((nb, GDN_HEADS, GDN_DK, GDN_DV), lambda i: (i, 0, 0, 0)),
            full((CONV_W, GDN_QKV)), full((1, LANES)), full((1, LANES)),
        ],
        out_specs=(pl.BlockSpec((nb, D_MODEL), lambda i: (i, 0)),
                   pl.BlockSpec((nb, GDN_HEADS, GDN_DK, GDN_DV), lambda i: (i, 0, 0, 0))),
        compiler_params=pltpu.CompilerParams(dimension_semantics=("parallel",), vmem_limit_bytes=VMEM_LIMIT),
        name="gdn_sample",
    )(proj, proj, state_conv, state_gdn, conv_w, alog_v, dtb_v)


def _q_absorb_kernel(q_ref, wukt_ref, ql_out, qr_out):
    for h in range(MLA_HEADS):
        q = q_ref[h]
        ql_out[h] = _dot(q, wukt_ref[h]).astype(BF16)
        qr_out[h] = q[:, NOPE_DIM:NOPE_DIM + ROPE_DIM]


def _q_absorb(q, wukt, ns, row0):
    return pl.pallas_call(
        _q_absorb_kernel,
        out_shape=(jax.ShapeDtypeStruct((MLA_HEADS, ns, KV_LORA), BF16),
                   jax.ShapeDtypeStruct((MLA_HEADS, ns, ROPE_DIM), BF16)),
        grid=(1,),
        in_specs=[pl.BlockSpec((MLA_HEADS, ns, LANES), lambda i: (0, row0 // ns, 0)),
                  pl.BlockSpec((MLA_HEADS, LANES, KV_LORA), lambda i: (0, 0, 0))],
        out_specs=(pl.BlockSpec((MLA_HEADS, ns, KV_LORA), lambda i: (0, 0, 0)),
                   pl.BlockSpec((MLA_HEADS, ns, ROPE_DIM), lambda i: (0, 0, 0))),
        compiler_params=pltpu.CompilerParams(vmem_limit_bytes=VMEM_LIMIT),
        name="q_absorb",
    )(q, wukt)


MLA_SAMPLE_BUFFERS = 4


def _mla_sample_kernel(pt_ref, ql_ref, qr_ref, cn_ref, krn_ref, cc_hbm, cr_hbm, o_ref, *scratch, ppc, nsub):
    nbuf = MLA_SAMPLE_BUFFERS
    cbufs, rbufs, sem = scratch[:nbuf], scratch[nbuf:2 * nbuf], scratch[2 * nbuf]
    b = pl.program_id(0)
    n_seq = pl.num_programs(0)
    n_pages = nbuf * ppc

    def page_copies(seq, k, i):
        page = pt_ref[seq * n_pages + k * ppc + i]
        return (pltpu.make_async_copy(cc_hbm.at[page], cbufs[k].at[i], sem.at[k, 0]),
                pltpu.make_async_copy(cr_hbm.at[page], rbufs[k].at[:, pl.ds(i * PAGE_SIZE, PAGE_SIZE)],
                                      sem.at[k, 1]))

    def start_chunk(seq, k):
        for i in range(ppc):
            cc, cr = page_copies(seq, k, i)
            cc.start()
            cr.start()

    def wait_chunk(seq, k):
        for i in range(ppc):
            cc, cr = page_copies(seq, k, i)
            cc.wait()
            cr.wait()

    @pl.when(b == 0)
    def _():
        start_chunk(0, 0)
        start_chunk(0, 1)

    nxt = jnp.minimum(b + 1, n_seq - 1)
    ql = ql_ref[0]
    qr = qr_ref[0]
    pps = ppc // nsub
    m = jnp.full((MLA_HEADS, 1), -1e30, F32)
    l = jnp.zeros((MLA_HEADS, 1), F32)
    acc = jnp.zeros((MLA_HEADS, KV_LORA), F32)
    for k in range(nbuf):
        wait_chunk(b, k)
        if k + 2 < nbuf:
            start_chunk(b, k + 2)
        else:
            start_chunk(nxt, k + 2 - nbuf)
        ms, ls, accs = [m], [l], [acc]
        for sub in range(nsub):
            c = cbufs[k][sub * pps:(sub + 1) * pps].reshape(pps * PAGE_SIZE, KV_LORA).astype(BF16)
            kr = rbufs[k][:, sub * pps * PAGE_SIZE:(sub + 1) * pps * PAGE_SIZE].astype(BF16)
            s = _dot_nt(ql, c) + _dot(qr, kr)
            m_i = jnp.max(s, axis=1, keepdims=True)
            p = jnp.exp(s - m_i)
            ms.append(m_i)
            ls.append(jnp.sum(p, axis=1, keepdims=True))
            accs.append(_dot(p.astype(BF16), c))
        m = functools.reduce(jnp.maximum, ms)
        scale = [jnp.exp(m_i - m) for m_i in ms]
        l = sum(a * l_i for a, l_i in zip(scale, ls))
        acc = sum(a * acc_i for a, acc_i in zip(scale, accs))

    cn = cn_ref[0]
    krn = krn_ref[0][:, SM_KR:SM_KR + ROPE_DIM]
    s_n = (jnp.sum(ql.astype(F32) * cn, axis=1, keepdims=True)
           + jnp.sum(qr.astype(F32) * krn, axis=1, keepdims=True))
    m2 = jnp.maximum(m, s_n)
    a2 = jnp.exp(m - m2)
    p_n = jnp.exp(s_n - m2)
    o_ref[0] = (a2 * acc + p_n * cn) / (a2 * l + p_n)

    @pl.when(b == n_seq - 1)
    def _():
        wait_chunk(nxt, 0)
        wait_chunk(nxt, 1)


def _mla_sample(page_table, ql, qr, c_new, kr_new, cache_c, cache_r):
    ns, n_pages = page_table.shape
    nbuf = MLA_SAMPLE_BUFFERS
    assert n_pages % nbuf == 0
    ppc = n_pages // nbuf
    kern = functools.partial(_mla_sample_kernel, ppc=ppc, nsub=_pick(ppc, (2, 1)))
    grid_spec = pltpu.PrefetchScalarGridSpec(
        num_scalar_prefetch=1,
        grid=(ns,),
        in_specs=[
            pl.BlockSpec((1, MLA_HEADS, KV_LORA), lambda b, pt: (b, 0, 0)),
            pl.BlockSpec((1, MLA_HEADS, ROPE_DIM), lambda b, pt: (b, 0, 0)),
            pl.BlockSpec((1, 1, KV_LORA), lambda b, pt: (b, 0, 0)),
            pl.BlockSpec((1, 1, LANES), lambda b, pt: (b, 0, 0)),
            pl.BlockSpec(memory_space=pl.ANY),
            pl.BlockSpec(memory_space=pl.ANY),
        ],
        out_specs=pl.BlockSpec((1, MLA_HEADS, KV_LORA), lambda b, pt: (b, 0, 0)),
        scratch_shapes=([pltpu.VMEM((ppc, PAGE_SIZE, KV_LORA), F32)] * nbuf
                        + [pltpu.VMEM((ROPE_DIM, ppc * PAGE_SIZE), F32)] * nbuf
                        + [pltpu.SemaphoreType.DMA((nbuf, 2))]),
    )
    return pl.pallas_call(
        kern,
        out_shape=jax.ShapeDtypeStruct((ns, MLA_HEADS, KV_LORA), F32),
        grid_spec=grid_spec,
        compiler_params=pltpu.CompilerParams(dimension_semantics=("arbitrary",), vmem_limit_bytes=VMEM_LIMIT),
        name="mla_sample",
    )(page_table.reshape(-1), ql, qr, c_new, kr_new, cache_c, cache_r)


def _o_proj_sample_kernel(ol_ref, wv_ref, o_ref):
    for p in range(MLA_HEADS // 2):
        acc = None
        for h in (2 * p, 2 * p + 1):
            part = _dot(ol_ref[h].astype(BF16), wv_ref[:, h * LANES:(h + 1) * LANES])
            acc = part if acc is None else acc + part
        o_ref[:, p * LANES:(p + 1) * LANES] = acc.astype(o_ref.dtype)


def _o_proj_sample(o_lat_t, wv):
    ns = o_lat_t.shape[1]
    return pl.pallas_call(
        _o_proj_sample_kernel,
        out_shape=jax.ShapeDtypeStruct((ns, D_MODEL), BF16),
        grid=(1,),
        in_specs=[pl.BlockSpec((MLA_HEADS, ns, KV_LORA), lambda i: (0, 0, 0)),
                  pl.BlockSpec((KV_LORA, MLA_HEADS * LANES), lambda i: (0, 0))],
        out_specs=pl.BlockSpec((ns, D_MODEL), lambda i: (0, 0)),
        compiler_params=pltpu.CompilerParams(vmem_limit_bytes=VMEM_LIMIT),
        name="o_proj_sample",
    )(o_lat_t, wv)


def _outproj_kernel(x_ref, om_ref, og_ref, z_ref, gm_ref, gg_ref, wo_ref, gn_ref, nf_ref, wr_ref, br_ref,
                    xmid_out, hf_out, route_out):
    og = og_ref[...]
    parts = []
    for h in range(GDN_HEADS):
        oh = og[:, h * GDN_DV:(h + 1) * GDN_DV]
        parts.append(oh * lax.rsqrt(jnp.mean(oh * oh, axis=-1, keepdims=True) + RMS_EPS))
    o_gdn = jnp.concatenate(parts, axis=1) * gn_ref[...] * _silu(z_ref[...])
    merged = _sigmoid(gm_ref[...]) * om_ref[...].astype(F32) + _sigmoid(gg_ref[...]) * o_gdn
    x_mid = x_ref[...] + _dot(merged.astype(BF16), wo_ref[...])
    xmid_out[...] = x_mid
    hf = _rms(x_mid, nf_ref[...]).astype(BF16)
    bits = lax.bitcast_convert_type(hf.astype(F32), jnp.uint32)
    half = D_MODEL // 2
    hf_out[...] = bits[:, half:] | (bits[:, :half] >> 16)

    logits = _dot(hf, wr_ref[...]) + br_ref[...]
    lane = lax.broadcasted_iota(jnp.int32, logits.shape, 1)
    neg = -jnp.inf
    big = 4 * LANES
    is_g = lane < N_GROUPS
    lg = jnp.where(is_g, logits, neg)
    mg = jnp.max(lg, axis=1, keepdims=True)
    grp = jnp.min(jnp.where(lg == mg, lane, big), axis=1, keepdims=True)
    gate_g = 1.0 / jnp.sum(jnp.where(is_g, jnp.exp(logits - mg), 0.0), axis=1, keepdims=True)
    e_lane = lane - N_GROUPS
    in_grp = (e_lane >= 0) & (e_lane < N_EXPERTS) & ((e_lane >> 3) == grp)
    le = jnp.where(in_grp, logits, neg)
    v1 = jnp.max(le, axis=1, keepdims=True)
    i1 = jnp.min(jnp.where(le == v1, lane, big), axis=1, keepdims=True)
    le2 = jnp.where(lane == i1, neg, le)
    v2 = jnp.max(le2, axis=1, keepdims=True)
    i2 = jnp.min(jnp.where(le2 == v2, lane, big), axis=1, keepdims=True)
    e = jnp.exp(v2 - v1)
    w1 = gate_g / (1.0 + e)
    w2 = gate_g * e / (1.0 + e)
    route = jnp.where(lane == 0, (i1 - N_GROUPS).astype(F32),
                      jnp.where(lane == 1, (i2 - N_GROUPS).astype(F32),
                                jnp.where(lane == 2, w1, jnp.where(lane == 3, w2, 0.0))))
    route_out[...] = route


def _row_tile(n, limit):
    t = limit - limit % 16
    while t >= 16:
        if n % t == 0:
            return t
        t -= 16
    raise ValueError(f"no row tile for {n}")


def _outproj(x_all, o_mla, o_gdn, proj, w_out, gn_t, norm_ffn, w_r, b_r, row0):
    n = o_mla.shape[0]
    tm = _pick(n, (512, 256, 128, 64, 32, 16))
    assert row0 % tm == 0
    r0 = row0 // tm
    full = lambda shape: pl.BlockSpec(shape, lambda i: (0,) * len(shape))
    row = lambda w: pl.BlockSpec((tm, w), lambda i: (i, 0))
    shared = lambda w, j=0: pl.BlockSpec((tm, w), lambda i, j=j: (r0 + i, j))
    return pl.pallas_call(
        _outproj_kernel,
        out_shape=(jax.ShapeDtypeStruct((n, D_MODEL), F32),
                   jax.ShapeDtypeStruct((n, D_MODEL // 2), jnp.uint32),
                   jax.ShapeDtypeStruct((n, LANES), F32)),
        grid=(n // tm,),
        in_specs=[shared(D_MODEL), row(D_MODEL), row(D_MODEL),
                  shared(D_MODEL, P_Z // D_MODEL), shared(D_MODEL, P_GM // D_MODEL), shared(D_MODEL, P_GG // D_MODEL),
                  full((D_MODEL, D_MODEL)), full((1, D_MODEL)), full((1, D_MODEL)),
                  full((D_MODEL, LANES)), full((1, LANES))],
        out_specs=(row(D_MODEL), row(D_MODEL // 2), row(LANES)),
        compiler_params=pltpu.CompilerParams(dimension_semantics=("parallel",), vmem_limit_bytes=VMEM_LIMIT),
        name="outproj_route",
    )(x_all, o_mla, o_gdn, proj, proj, proj, w_out, gn_t, norm_ffn, w_r, b_r)


def _moe_plan(eid, wgt, nsp):
    nt = eid.shape[0]
    ts = nt // nsp
    n_asg = ts * TOP_K
    nb = (n_asg + N_EXPERTS * (MOE_BLOCK - 1) + MOE_BLOCK - 1) // MOE_BLOCK
    e = eid.reshape(nsp, n_asg)
    bs = _pick(n_asg, (128, 64, 32, 16, 8, 4, 2, 1))
    onehot = (e[..., None] == jnp.arange(N_EXPERTS, dtype=jnp.int32)).astype(F32)
    blocks = onehot.reshape(nsp, n_asg // bs, bs, N_EXPERTS)
    within = jnp.einsum('ij,sbjk->sbik', jnp.tril(jnp.ones((bs, bs), F32)), blocks)
    totals = within[:, :, -1, :]
    offs = jnp.cumsum(totals, axis=1) - totals
    cs = (within + offs[:, :, None, :]).reshape(nsp, n_asg, N_EXPERTS)
    rank = jnp.sum(cs * onehot, axis=2) - 1.0
    counts = (offs[:, -1, :] + totals[:, -1, :]).astype(jnp.int32)
    padded = (counts + MOE_BLOCK - 1) // MOE_BLOCK * MOE_BLOCK
    pad_end = jnp.cumsum(padded, axis=1)
    pad_start = (pad_end - padded).astype(F32)
    dest = (jnp.sum(onehot * pad_start[:, None, :], axis=2) + rank).astype(jnp.int32)
    starts = jnp.arange(nb, dtype=jnp.int32) * MOE_BLOCK
    blk_e = jnp.minimum(jnp.sum(pad_end[:, None, :] <= starts[None, :, None], axis=2), N_EXPERTS - 1)
    nused = pad_end[:, -1] // MOE_BLOCK
    scalars = (blk_e.astype(jnp.int32).reshape(-1), nused.astype(jnp.int32), counts.reshape(-1),
               (pad_end - padded).reshape(-1), dest.reshape(-1), wgt.reshape(-1))
    return scalars, ts, nb


def _split_pieces(k, ts, n_prompt, n_sample):
    lo, hi = k * ts, (k + 1) * ts
    pieces = []
    if lo < n_prompt:
        pieces.append((0, lo, 0, min(hi, n_prompt) - lo))
    if hi > n_prompt:
        start = max(lo, n_prompt)
        pieces.append((1, start - n_prompt, start - lo, hi - start))
    assert hi <= n_prompt + n_sample
    return pieces


def _moe_kernel(blk_ref, nused_ref, cnt_ref, pstart_ref, dest_ref, wgt_ref, hf_p, hf_q, xmid_p, xmid_q,
                wgu_ref, wd_ref, nfin_ref, y_p, y_q, hf_s, acc, xb, yb, tok_ref, w_ref, sem,
                *, ts, nb, rc, nsp, n_prompt, n_sample):
    s = pl.program_id(0)
    j = pl.program_id(1)
    half = D_MODEL // 2
    n_asg = ts * TOP_K

    def build_slot_tables():
        def fill(i, carry):
            d = dest_ref[s * n_asg + i]
            tok_ref[d] = i // TOP_K
            w_ref[d] = wgt_ref[s * n_asg + i]
            return carry

        def clear_tail(e, carry):
            cnt = cnt_ref[s * N_EXPERTS + e]
            first = pstart_ref[s * N_EXPERTS + e] + cnt

            def clear(i, c):
                tok_ref[first + i] = 0
                w_ref[first + i] = 0.0
                return c

            lax.fori_loop(0, (-cnt) & (MOE_BLOCK - 1), clear, 0)
            return carry

        lax.fori_loop(0, n_asg, fill, 0, unroll=8)
        lax.fori_loop(0, N_EXPERTS, clear_tail, 0)

    def split_copies(k, load):
        cps = []
        for src, r0, l0, n in _split_pieces(k, ts, n_prompt, n_sample):
            if load:
                cps.append(pltpu.make_async_copy((hf_p, hf_q)[src].at[pl.ds(r0, n)], hf_s.at[pl.ds(l0, n)],
                                                 sem.at[0, src]))
                cps.append(pltpu.make_async_copy((xmid_p, xmid_q)[src].at[pl.ds(r0, n)], acc.at[pl.ds(l0, n)],
                                                 sem.at[1, src]))
            else:
                cps.append(pltpu.make_async_copy(acc.at[pl.ds(l0, n)], (y_p, y_q)[src].at[pl.ds(r0, n)],
                                                 sem.at[2, src]))
        return cps

    def run_copies(load, between=None):
        for k in range(nsp):
            @pl.when(s == k)
            def _(k=k):
                cps = split_copies(k, load)
                for cp in cps:
                    cp.start()
                if between is not None:
                    between()
                for cp in cps:
                    cp.wait()

    @pl.when(j == 0)
    def _():
        run_copies(True, build_slot_tables)

    @pl.when(j < nused_ref[s])
    def _():
        base = j * MOE_BLOCK
        for r in range(MOE_BLOCK):
            xb[r:r + 1, :] = hf_s[pl.ds(tok_ref[base + r], 1), :]
        bits = xb[...]
        lo = lax.bitcast_convert_type(bits << 16, F32).astype(BF16)
        hi = lax.bitcast_convert_type(bits & jnp.uint32(0xFFFF0000), F32).astype(BF16)
        gu = _dot(lo, wgu_ref[0, :half, :]) + _dot(hi, wgu_ref[0, half:, :])
        hmid = _silu(gu[:, :D_EXPERT]) * gu[:, D_EXPERT:]
        yb[...] = _dot(hmid.astype(BF16), wd_ref[0])
        for r in range(MOE_BLOCK):
            tok = tok_ref[base + r]
            acc[pl.ds(tok, 1), :] = acc[pl.ds(tok, 1), :] + w_ref[base + r] * yb[r:r + 1, :]

    @pl.when(j == nb - 1)
    def _():
        def body(i, carry):
            rows = pl.ds(pl.multiple_of(i * rc, 8), rc)
            acc[rows, :] = _rms(acc[rows, :], nfin_ref[...])
            return carry

        lax.fori_loop(0, ts // rc, body, 0)
        run_copies(False)


def _moe(plan, hf_p, hf_q, xmid_p, xmid_q, wgu, wd, norm_final):
    scalars, ts, nb = plan
    n_prompt, n_sample = xmid_p.shape[0], xmid_q.shape[0]
    nsp = (n_prompt + n_sample) // ts
    rc = _row_tile(ts, 512) if ts % 16 == 0 else 8
    kern = functools.partial(_moe_kernel, ts=ts, nb=nb, rc=rc, nsp=nsp, n_prompt=n_prompt, n_sample=n_sample)
    hbm = pl.BlockSpec(memory_space=pl.ANY)
    grid_spec = pltpu.PrefetchScalarGridSpec(
        num_scalar_prefetch=len(scalars),
        grid=(nsp, nb),
        in_specs=[
            hbm, hbm, hbm, hbm,
            pl.BlockSpec((1, D_MODEL, 2 * D_EXPERT), lambda s, j, be, *_: (be[s * nb + j], 0, 0)),
            pl.BlockSpec((1, D_EXPERT, D_MODEL), lambda s, j, be, *_: (be[s * nb + j], 0, 0)),
            pl.BlockSpec((1, D_MODEL), lambda s, j, *_: (0, 0)),
        ],
        out_specs=(hbm, hbm),
        scratch_shapes=[
            pltpu.VMEM((ts, D_MODEL // 2), jnp.uint32),
            pltpu.VMEM((ts, D_MODEL), F32),
            pltpu.VMEM((MOE_BLOCK, D_MODEL // 2), jnp.uint32),
            pltpu.VMEM((MOE_BLOCK, D_MODEL), F32),
            pltpu.SMEM((nb * MOE_BLOCK,), jnp.int32),
            pltpu.SMEM((nb * MOE_BLOCK,), F32),
            pltpu.SemaphoreType.DMA((3, 2)),
        ],
    )
    return pl.pallas_call(
        kern,
        out_shape=(jax.ShapeDtypeStruct((n_prompt, D_MODEL), F32), jax.ShapeDtypeStruct((n_sample, D_MODEL), F32)),
        grid_spec=grid_spec,
        compiler_params=pltpu.CompilerParams(
            dimension_semantics=("arbitrary", "arbitrary"), vmem_limit_bytes=VMEM_LIMIT),
        name="moe",
    )(*scalars, hf_p, hf_q, xmid_p, xmid_q, wgu, wd, norm_final)


def _pack_w_in(w):
    kr = w[:, _OFF_KV + KV_LORA:_OFF_QKV]
    kr_sw = jnp.concatenate([kr[:, ROPE_DIM // 2:], kr[:, :ROPE_DIM // 2]], axis=1)
    small = jnp.concatenate([w[:, _OFF_B:_OFF_A], w[:, _OFF_A:_OFF_GM],
                             jnp.zeros((D_MODEL, SM_KR - 2 * GDN_HEADS), w.dtype), kr, kr_sw], axis=1)
    packed = jnp.concatenate([w[:, _OFF_QKV:_OFF_Z], w[:, _OFF_Z:_OFF_B], w[:, _OFF_GM:_OFF_GG], w[:, _OFF_GG:],
                              w[:, _OFF_KV:_OFF_KV + KV_LORA], small, w[:, :Q_LORA]], axis=1)
    return packed.astype(BF16)


def _pack_mla_weights(w_uq, w_uk, w_uv):
    zq = jnp.zeros((Q_LORA, MLA_HEADS, LANES - NOPE_DIM - ROPE_DIM), w_uq.dtype)
    wq = jnp.concatenate([w_uq, zq], axis=2).reshape(Q_LORA, MLA_HEADS * LANES)
    rope = w_uq[:, :, NOPE_DIM:]
    rope_sw = jnp.concatenate([rope[..., ROPE_DIM // 2:], rope[..., :ROPE_DIM // 2]], axis=2)
    wqs = jnp.concatenate([jnp.zeros((Q_LORA, MLA_HEADS, NOPE_DIM), w_uq.dtype), rope_sw, zq], axis=2)
    wqs = wqs.reshape(Q_LORA, MLA_HEADS * LANES)
    wk = jnp.concatenate([w_uk, jnp.zeros((KV_LORA, MLA_HEADS, LANES - NOPE_DIM), w_uk.dtype)], axis=2)
    wk = wk.reshape(KV_LORA, MLA_HEADS * LANES)
    zv = jnp.zeros((KV_LORA, MLA_HEADS // 2, V_DIM), w_uv.dtype)
    wv = jnp.stack([jnp.concatenate([w_uv[:, 0::2], zv], axis=2),
                    jnp.concatenate([zv, w_uv[:, 1::2]], axis=2)], axis=2)
    wv = wv.reshape(KV_LORA, MLA_HEADS * LANES)
    return wq.astype(BF16), wqs.astype(BF16), wk.astype(BF16), wv.astype(BF16)


def _rope_tables(pos):
    inv_freq = ROPE_THETA ** (-jnp.arange(0, ROPE_DIM, 2, dtype=F32) / ROPE_DIM)
    ang = pos.astype(F32)[:, None] * inv_freq[None, :]
    cos, sin = jnp.cos(ang), jnp.sin(ang)
    n = pos.shape[0]
    cos_t = jnp.concatenate([jnp.ones((n, NOPE_DIM), F32), cos, cos, jnp.zeros((n, ROPE_DIM), F32)], axis=1)
    sin_t = jnp.concatenate([jnp.zeros((n, NOPE_DIM), F32), -sin, sin, jnp.zeros((n, ROPE_DIM), F32)], axis=1)
    return cos_t, sin_t


def _head_lanes(v):
    return jnp.zeros((1, LANES), F32).at[0, SM_A:SM_A + GDN_HEADS].set(v.astype(F32))


def _moe_splits(nt):
    for nsp in (3, 4, 2, 1):
        if nt % (nsp * 8) == 0:
            return nsp
    return 1


def kernel(x_prompt, x_sample, cache_kv_latent, cache_k_rope, page_table, state_conv, state_gdn, meta_tokens,
           norm_mix, w_in, q_norm, w_uq, kv_norm, w_uk, w_uv, conv_w, a_log, dt_bias, gdn_norm, w_out, norm_ffn,
           w_group, b_group, w_router, b_router, w_gate, w_up, w_down, norm_final):
    batch, seq, _ = x_prompt.shape
    ns, dec_seq, _ = x_sample.shape
    assert dec_seq == 1 and w_in.shape[0] == 1 and seq % CHUNK == 0
    n_pages = page_table.shape[1]
    n_prompt = batch * seq
    nt = n_prompt + ns
    meta_row = nt
    n_rows = -(-(nt + N_META) // LANES) * LANES
    assert n_prompt % ns == 0 and nt % N_META == 0 and ns % 8 == 0

    x_all = jnp.concatenate([x_prompt.reshape(n_prompt, D_MODEL), x_sample.reshape(ns, D_MODEL),
                             meta_tokens.astype(x_prompt.dtype),
                             jnp.zeros((n_rows - nt - N_META, D_MODEL), x_prompt.dtype)], axis=0)
    pos = jnp.concatenate([jnp.tile(N_META + jnp.arange(seq), batch), jnp.full((ns,), n_pages * PAGE_SIZE),
                           jnp.arange(N_META), jnp.zeros((n_rows - nt - N_META,), jnp.int32)])
    cos_t, sin_t = _rope_tables(pos)
    w_packed = _pack_w_in(w_in[0])
    wq, wqs, wk, wv = _pack_mla_weights(w_uq[0], w_uk[0], w_uv[0])
    wukt = jnp.concatenate([jnp.transpose(w_uk[0], (1, 2, 0)),
                            jnp.zeros((MLA_HEADS, LANES - NOPE_DIM, KV_LORA), w_uk.dtype)], axis=1).astype(BF16)
    alog_v, dtb_v = _head_lanes(a_log[0]), _head_lanes(dt_bias[0])
    cw = conv_w[0].astype(F32)
    gn_t = jnp.tile(gdn_norm[0].astype(F32), GDN_HEADS)[None]
    w_r = jnp.concatenate([w_group[0], w_router[0],
                           jnp.zeros((D_MODEL, LANES - N_GROUPS - N_EXPERTS), w_group.dtype)], axis=1).astype(BF16)
    b_r = jnp.concatenate([b_group[0], b_router[0], jnp.zeros((LANES - N_GROUPS - N_EXPERTS,), b_group.dtype)])[None]
    wgu = jnp.concatenate([w_gate[0], w_up[0]], axis=2).astype(BF16)
    wd = w_down[0].astype(BF16)

    proj = _inproj(x_all, norm_mix[0][None].astype(F32), w_packed)
    q, k, v, ckv, krot = _mla_prep(proj, cos_t, sin_t, q_norm[0][None].astype(F32), kv_norm[0][None].astype(F32),
                                   wq, wqs, wk, wv)

    o_mla_p = _attn_prompt(q, k, v, batch, seq, meta_row)
    ql, qr = _q_absorb(q, wukt, ns, n_prompt)
    o_lat = _mla_sample(page_table, jnp.transpose(ql, (1, 0, 2)), jnp.transpose(qr, (1, 0, 2)),
                        ckv[n_prompt:nt].reshape(ns, 1, KV_LORA), krot[n_prompt:nt].reshape(ns, 1, LANES),
                        cache_kv_latent[0], jnp.swapaxes(cache_k_rope[0], 1, 2))
    o_mla_s = _o_proj_sample(jnp.transpose(o_lat, (1, 0, 2)), wv)

    s_meta = _gdn_meta(proj, cw, alog_v, dtb_v, meta_row)
    terms = _gdn_terms(proj, cw, alog_v, dtb_v, batch, seq, meta_row)
    o_gdn_p, gdn_p = _gdn_scan(*terms, s_meta, batch, seq)
    o_gdn_s, gdn_s = _gdn_sample(proj, state_conv[0], state_gdn[0], cw, alog_v, dtb_v, n_prompt)

    tail = (proj, w_out[0].astype(BF16), gn_t, norm_ffn[0][None].astype(F32), w_r, b_r.astype(F32))
    xmid_p, hf_p, route_p = _outproj(x_all, o_mla_p, o_gdn_p, *tail, 0)
    xmid_s, hf_s, route_s = _outproj(x_all, o_mla_s, o_gdn_s, *tail, n_prompt)
    route = jnp.concatenate([route_p[:, :2 * TOP_K], route_s[:, :2 * TOP_K]], axis=0)
    plan = _moe_plan(route[:, :TOP_K].astype(jnp.int32), route[:, TOP_K:], _moe_splits(nt))
    y_p, y_s = _moe(plan, hf_p, hf_s, xmid_p, xmid_s, wgu, wd, norm_final[None].astype(F32))

    def with_meta(rows, width):
        meta = jnp.broadcast_to(rows[meta_row:meta_row + N_META][None], (batch, N_META, width))
        return jnp.concatenate([meta, rows[:n_prompt].reshape(batch, seq, width)], axis=1)[None]

    k_rope = krot[:, SM_KR:SM_KR + ROPE_DIM]
    conv_p = jnp.stack([proj[(b + 1) * seq - (CONV_W - 1):(b + 1) * seq, :GDN_QKV] for b in range(batch)])
    conv_s = jnp.concatenate([state_conv[0][:, 1:].astype(F32), proj[n_prompt:nt, None, :GDN_QKV]], axis=1)
    return (y_p.reshape(batch, seq, D_MODEL), y_s.reshape(ns, 1, D_MODEL),
            with_meta(ckv, KV_LORA), with_meta(k_rope, ROPE_DIM),
            ckv[n_prompt:nt].reshape(1, ns, 1, KV_LORA), k_rope[n_prompt:nt].reshape(1, ns, 1, ROPE_DIM),
            conv_p[None], conv_s[None], gdn_p[None], gdn_s[None])
```

```python
import functools

import jax
import jax.numpy as jnp
from jax import lax
from jax.experimental import pallas as pl
from jax.experimental.pallas import tpu as pltpu

F32 = jnp.float32
BF16 = jnp.bfloat16
HIGHEST = lax.Precision.HIGHEST

D_MODEL = 1024
N_META = 16
RMS_EPS = 1e-6
MLA_HEADS = 16
Q_LORA = 384
KV_LORA = 256
NOPE_DIM = 64
ROPE_DIM = 32
V_DIM = 64
ROPE_THETA = 10000.0
MLA_SCALE = (NOPE_DIM + ROPE_DIM) ** -0.5
PAGE_SIZE = 128
GDN_HEADS = 8
GDN_DK = 128
GDN_DV = 128
GDN_KEY = GDN_HEADS * GDN_DK
GDN_QKV = 3 * GDN_KEY
CONV_W = 4
CHUNK = 64
N_GROUPS = 4
EXPERTS_PER_GROUP = 8
N_EXPERTS = 32
TOP_K = 2
D_EXPERT = 256
MOE_BLOCK = 128

_OFF_KV = Q_LORA
_OFF_QKV = _OFF_KV + KV_LORA + ROPE_DIM
_OFF_Z = _OFF_QKV + GDN_QKV
_OFF_B = _OFF_Z + GDN_KEY
_OFF_A = _OFF_B + GDN_HEADS
_OFF_GM = _OFF_A + GDN_HEADS
_OFF_GG = _OFF_GM + D_MODEL
P_QKV = 0
P_Z = 3072
P_GM = 4096
P_GG = 5120
P_KVC = 6144
P_SMALL = 6400
P_QD = 6528
P_TOTAL = 6912
SM_B = 0
SM_A = 8
SM_KR = 64

LANES = 128
VMEM_LIMIT = 56 * 1024 * 1024
ATTN_TQ = (512, 256, 128)


def _pick(n, candidates):
    for c in candidates:
        if n % c == 0:
            return c
    raise ValueError(f"no tile for {n} in {candidates}")


def _dot(a, b):
    return jnp.dot(a, b, preferred_element_type=F32)


def _dot_nt(a, b):
    return lax.dot_general(a, b, (((1,), (1,)), ((), ())), preferred_element_type=F32)


def _dot_tn(a, b):
    return lax.dot_general(a, b, (((0,), (0,)), ((), ())), preferred_element_type=F32)


def _sigmoid(x):
    return 1.0 / (1.0 + jnp.exp(-x))


def _silu(x):
    return x * _sigmoid(x)


def _softplus(x):
    return jnp.maximum(x, 0.0) + jnp.log1p(jnp.exp(-jnp.abs(x)))


def _rms(x, w):
    return x * lax.rsqrt(jnp.mean(x * x, axis=-1, keepdims=True) + RMS_EPS) * w


def _inproj_kernel(x_ref, nw_ref, w_ref, o_ref, hn_ref):
    @pl.when(pl.program_id(1) == 0)
    def _():
        hn_ref[...] = _rms(x_ref[...], nw_ref[...]).astype(BF16)

    o_ref[...] = _dot(hn_ref[...], w_ref[...])


def _inproj(x_all, norm_w, w_packed):
    r = x_all.shape[0]
    tm = _pick(r, (1280, 640, 512, 256, 128))
    tn = 768
    return pl.pallas_call(
        _inproj_kernel,
        out_shape=jax.ShapeDtypeStruct((r, P_TOTAL), F32),
        grid=(r // tm, P_TOTAL // tn),
        in_specs=[
            pl.BlockSpec((tm, D_MODEL), lambda i, j: (i, 0)),
            pl.BlockSpec((1, D_MODEL), lambda i, j: (0, 0)),
            pl.BlockSpec((D_MODEL, tn), lambda i, j: (0, j)),
        ],
        out_specs=pl.BlockSpec((tm, tn), lambda i, j: (i, j)),
        scratch_shapes=[pltpu.VMEM((tm, D_MODEL), BF16)],
        compiler_params=pltpu.CompilerParams(
            dimension_semantics=("parallel", "arbitrary"), vmem_limit_bytes=VMEM_LIMIT),
        name="inproj",
    )(x_all, norm_w, w_packed)


def _mla_prep_kernel(qd_ref, kvc_ref, sm_ref, c_ref, s_ref, qn_ref, kvn_ref, wq_ref, wqs_ref, wk_ref, wv_ref,
                     q_out, k_out, v_out, ckv_out, kr_out):
    cos = c_ref[...]
    sin = s_ref[...]
    qn = _rms(qd_ref[...], qn_ref[...]).astype(BF16)
    q = _dot(qn, wq_ref[...])
    qs = _dot(qn, wqs_ref[...])
    for h in range(MLA_HEADS):
        sl = slice(h * LANES, (h + 1) * LANES)
        q_out[h] = ((q[:, sl] * cos + qs[:, sl] * sin) * MLA_SCALE).astype(BF16)
    ckv = _rms(kvc_ref[...], kvn_ref[...])
    ckv_out[...] = ckv
    cb = ckv.astype(BF16)
    sm = sm_ref[...]
    lane = lax.broadcasted_iota(jnp.int32, sm.shape, 1)
    cos_k = jnp.where((lane >= SM_KR) & (lane < SM_KR + ROPE_DIM), cos, 0.0)
    krot = sm * cos_k + pltpu.roll(sm, LANES - ROPE_DIM, 1) * sin
    kr_out[...] = krot
    kk = _dot(cb, wk_ref[...])
    vv = _dot(cb, wv_ref[...])
    for h in range(MLA_HEADS):
        sl = slice(h * LANES, (h + 1) * LANES)
        k_out[h] = (kk[:, sl] + krot).astype(BF16)
        v_out[h] = vv[:, sl].astype(BF16)


def _mla_prep(proj, cos_t, sin_t, q_norm, kv_norm, wq, wqs, wk, wv):
    r = proj.shape[0]
    tm = _pick(r, (640, 512, 256, 128))
    hw = MLA_HEADS * LANES
    full = lambda shape: pl.BlockSpec(shape, lambda i: (0,) * len(shape))
    head_out = pl.BlockSpec((MLA_HEADS, tm, LANES), lambda i: (0, i, 0))
    return pl.pallas_call(
        _mla_prep_kernel,
        out_shape=(
            jax.ShapeDtypeStruct((MLA_HEADS, r, LANES), BF16),
            jax.ShapeDtypeStruct((MLA_HEADS, r, LANES), BF16),
            jax.ShapeDtypeStruct((MLA_HEADS, r, LANES), BF16),
            jax.ShapeDtypeStruct((r, KV_LORA), F32),
            jax.ShapeDtypeStruct((r, LANES), F32),
        ),
        grid=(r // tm,),
        in_specs=[
            pl.BlockSpec((tm, Q_LORA), lambda i: (i, P_QD // Q_LORA)),
            pl.BlockSpec((tm, KV_LORA), lambda i: (i, P_KVC // KV_LORA)),
            pl.BlockSpec((tm, LANES), lambda i: (i, P_SMALL // LANES)),
            pl.BlockSpec((tm, LANES), lambda i: (i, 0)),
            pl.BlockSpec((tm, LANES), lambda i: (i, 0)),
            full((1, Q_LORA)), full((1, KV_LORA)),
            full((Q_LORA, hw)), full((Q_LORA, hw)), full((KV_LORA, hw)), full((KV_LORA, hw)),
        ],
        out_specs=(head_out, head_out, head_out,
                   pl.BlockSpec((tm, KV_LORA), lambda i: (i, 0)),
                   pl.BlockSpec((tm, LANES), lambda i: (i, 0))),
        compiler_params=pltpu.CompilerParams(dimension_semantics=("parallel",), vmem_limit_bytes=VMEM_LIMIT),
        name="mla_prep",
    )(proj, proj, proj, cos_t, sin_t, q_norm, kv_norm, wq, wqs, wk, wv)


def _attn_prompt_kernel(q_ref, k_ref, v_ref, km_ref, vm_ref, o_ref, *, tq):
    qi = pl.program_id(2)
    hs = (0, 1)
    half = tq // 2
    q = [q_ref[h] for h in hs]

    def update(qs, rows, width, carry, mask):
        ms, ls, accs = carry
        s = [_dot_nt(qs[h], k_ref[h, pl.ds(rows, width), :]) for h in hs]
        if mask is not None:
            s = [jnp.where(mask, x, -1e30) for x in s]
        m_new = [jnp.maximum(ms[h], jnp.max(s[h], axis=1, keepdims=True)) for h in hs]
        a = [jnp.exp(ms[h] - m_new[h]) for h in hs]
        p = [jnp.exp(s[h] - m_new[h]) for h in hs]
        l_new = [a[h] * ls[h] + jnp.sum(p[h], axis=1, keepdims=True) for h in hs]
        acc_new = [a[h] * accs[h] + _dot(p[h].astype(BF16), v_ref[h, pl.ds(rows, width), :]) for h in hs]
        return m_new, l_new, acc_new

    s0 = [_dot_nt(q[h], km_ref[h]) for h in hs]
    m = [jnp.max(s0[h], axis=1, keepdims=True) for h in hs]
    p0 = [jnp.exp(s0[h] - m[h]) for h in hs]
    l = [jnp.sum(p0[h], axis=1, keepdims=True) for h in hs]
    acc = [_dot(p0[h].astype(BF16), vm_ref[h]) for h in hs]

    def body(j, carry):
        return update(q, pl.multiple_of(j * tq, tq), tq, carry, None)

    m, l, acc = lax.fori_loop(0, qi, body, (m, l, acc))

    off = pl.multiple_of(qi * tq, tq)
    row = lax.broadcasted_iota(jnp.int32, (tq, half), 0)
    col = lax.broadcasted_iota(jnp.int32, (tq, half), 1)
    m, l, acc = update(q, off, half, (m, l, acc), col <= row)
    bot = lambda xs: [x[half:] for x in xs]
    mb, lb, accb = update(bot(q), pl.multiple_of(off + half, half), half, (bot(m), bot(l), bot(acc)),
                          (col <= row)[:half])
    o = [jnp.concatenate([acc[h][:half] / l[h][:half], accb[h] / lb[h]], axis=0) for h in hs]
    o_ref[...] = (o[0] + o[1]).astype(o_ref.dtype)


def _attn_prompt(q, k, v, batch, seq, meta_row):
    tq = _pick(seq, ATTN_TQ)
    nq = seq // tq
    kern = functools.partial(_attn_prompt_kernel, tq=tq)
    return pl.pallas_call(
        kern,
        out_shape=jax.ShapeDtypeStruct((batch * seq, D_MODEL), BF16),
        grid=(batch, MLA_HEADS // 2, nq),
        in_specs=[
            pl.BlockSpec((2, tq, LANES), lambda b, p, i: (p, b * nq + i, 0)),
            pl.BlockSpec((2, seq, LANES), lambda b, p, i: (p, b, 0)),
            pl.BlockSpec((2, seq, LANES), lambda b, p, i: (p, b, 0)),
            pl.BlockSpec((2, N_META, LANES), lambda b, p, i: (p, meta_row // N_META, 0)),
            pl.BlockSpec((2, N_META, LANES), lambda b, p, i: (p, meta_row // N_META, 0)),
        ],
        out_specs=pl.BlockSpec((tq, LANES), lambda b, p, i: (b * nq + i, p)),
        compiler_params=pltpu.CompilerParams(
            dimension_semantics=("parallel", "parallel", "arbitrary"), vmem_limit_bytes=VMEM_LIMIT),
        name="attn_prompt",
    )(q, k, v, k, v)


def _gate_lanes(sm, alog_ref, dtb_ref):
    g = -jnp.exp(alog_ref[...]) * _softplus(sm + dtb_ref[...])
    beta = _sigmoid(sm)
    return g, beta


def _qkv_heads(xc):
    xf = _silu(xc)
    qs, ks, vs = [], [], []
    for h in range(GDN_HEADS):
        q = xf[:, h * GDN_DK:(h + 1) * GDN_DK]
        k = xf[:, GDN_KEY + h * GDN_DK:GDN_KEY + (h + 1) * GDN_DK]
        qs.append(q * lax.rsqrt(jnp.sum(q * q, axis=-1, keepdims=True) + RMS_EPS) * (GDN_DK ** -0.5))
        ks.append(k * lax.rsqrt(jnp.sum(k * k, axis=-1, keepdims=True) + RMS_EPS))
        vs.append(xf[:, 2 * GDN_KEY + h * GDN_DV:2 * GDN_KEY + (h + 1) * GDN_DV])
    return qs, ks, vs


def _split_bf16(x):
    hi = x.astype(BF16)
    return hi, (x - hi.astype(F32)).astype(BF16)


def _dot_split(a, b):
    return _dot(a[0], b[0]) + (_dot(a[0], b[1]) + _dot(a[1], b[0]))


def _unit_lower_inverses(mats, c):
    row = lax.broadcasted_iota(jnp.int32, (c, c), 0)
    col = lax.broadcasted_iota(jnp.int32, (c, c), 1)
    eye = jnp.where(row == col, 1.0, 0.0)
    ps = [-a for a in mats]
    ts = [eye + p for p in ps]
    span = 2
    while span < c:
        psp = [_split_bf16(p) for p in ps]
        ps = [_dot_split(p, p) for p in psp]
        psp = [_split_bf16(p) for p in ps]
        ts = [t + _dot_split(p, _split_bf16(t)) for p, t in zip(psp, ts)]
        span *= 2
    return ts


def _gdn_chunk_terms(xs, sm, conv_ref, alog_ref, dtb_ref, c):
    heads = range(GDN_HEADS)
    xc = xs[0] * conv_ref[0:1, :]
    for j in range(1, CONV_W):
        xc = xc + xs[j] * conv_ref[j:j + 1, :]
    qs, ks, vs = _qkv_heads(xc)
    g, beta = _gate_lanes(sm, alog_ref, dtb_ref)
    row = lax.broadcasted_iota(jnp.int32, (c, c), 0)
    col = lax.broadcasted_iota(jnp.int32, (c, c), 1)
    causal = col <= row
    strict = col < row
    gcum = jnp.dot(jnp.where(causal, 1.0, 0.0), g, precision=HIGHEST, preferred_element_type=F32)
    gcum_t = lax.dot_general(g, jnp.where(col >= row, 1.0, 0.0), (((0,), (0,)), ((), ())),
                             precision=HIGHEST, preferred_element_type=F32)
    gc = [gcum[:, SM_A + h:SM_A + h + 1] for h in heads]
    gr = [gcum_t[SM_A + h:SM_A + h + 1, :] for h in heads]
    bc = [beta[:, SM_B + h:SM_B + h + 1] for h in heads]
    decay = [jnp.where(causal, jnp.exp(jnp.where(causal, gc[h] - gr[h], 0.0)), 0.0) for h in heads]
    kb = [ks[h] * bc[h] for h in heads]
    kbf = [ks[h].astype(BF16) for h in heads]
    a = [jnp.where(strict, _dot_nt(kb[h].astype(BF16), kbf[h]) * decay[h], 0.0) for h in heads]
    t = _unit_lower_inverses(a, c)
    eg = [jnp.exp(gc[h]) for h in heads]
    sol = [_dot(t[h].astype(BF16), jnp.concatenate([vs[h] * bc[h], kb[h] * eg[h]], axis=1).astype(BF16))
           for h in heads]
    u = [sol[h][:, :GDN_DV] for h in heads]
    w = [sol[h][:, GDN_DV:] for h in heads]
    attn = [jnp.where(causal, _dot_nt(qs[h].astype(BF16), kbf[h]) * decay[h], 0.0) for h in heads]
    qg = [qs[h] * eg[h] for h in heads]
    g_last = gcum[c - 1:c, :]
    kd = [ks[h] * jnp.exp(g_last[:, SM_A + h:SM_A + h + 1] - gc[h]) for h in heads]
    return u, w, qg, kd, attn, jnp.exp(g_last)


def _gdn_meta_kernel(x_ref, sm_ref, conv_ref, alog_ref, dtb_ref, s_out, xe_ref):
    c = N_META
    xe_ref[0:8, :] = jnp.zeros((8, GDN_QKV), F32)
    xe_ref[8:8 + c, :] = x_ref[...]
    xs = [xe_ref[pl.ds(8 - (CONV_W - 1) + j, c), :] for j in range(CONV_W)]
    u, _, _, kd, _, _ = _gdn_chunk_terms(xs, sm_ref[...], conv_ref, alog_ref, dtb_ref, c)
    for h in range(GDN_HEADS):
        s_out[h] = _dot_tn(kd[h].astype(BF16), u[h].astype(BF16))


def _gdn_meta(proj, conv_w, alog_v, dtb_v, meta_row):
    full = lambda shape: pl.BlockSpec(shape, lambda i: (0,) * len(shape))
    return pl.pallas_call(
        _gdn_meta_kernel,
        out_shape=jax.ShapeDtypeStruct((GDN_HEADS, GDN_DK, GDN_DV), F32),
        grid=(1,),
        in_specs=[
            pl.BlockSpec((N_META, GDN_QKV), lambda i: (meta_row // N_META, 0)),
            pl.BlockSpec((N_META, LANES), lambda i: (meta_row // N_META, P_SMALL // LANES)),
            full((CONV_W, GDN_QKV)), full((1, LANES)), full((1, LANES)),
        ],
        out_specs=full((GDN_HEADS, GDN_DK, GDN_DV)),
        scratch_shapes=[pltpu.VMEM((8 + N_META, GDN_QKV), F32)],
        compiler_params=pltpu.CompilerParams(vmem_limit_bytes=VMEM_LIMIT),
        name="gdn_meta",
    )(proj, proj, conv_w, alog_v, dtb_v)


def _gdn_terms_kernel(x_ref, hist_ref, sm_ref, conv_ref, alog_ref, dtb_ref,
                      u_out, w_out, qg_out, kd_out, attn_out, dec_out, xe_ref):
    c = CHUNK
    xe_ref[0:8, :] = hist_ref[...]
    xe_ref[8:8 + c, :] = x_ref[...]
    xs = [xe_ref[pl.ds(8 - (CONV_W - 1) + j, c), :] for j in range(CONV_W)]
    u, w, qg, kd, attn, dec = _gdn_chunk_terms(xs, sm_ref[...], conv_ref, alog_ref, dtb_ref, c)
    for h in range(GDN_HEADS):
        sl = slice(h * GDN_DV, (h + 1) * GDN_DV)
        u_out[:, sl] = u[h]
        w_out[:, sl] = w[h].astype(BF16)
        qg_out[:, sl] = qg[h].astype(BF16)
        kd_out[:, sl] = kd[h].astype(BF16)
        attn_out[h] = attn[h].astype(BF16)
    dec_out[0] = dec


def _gdn_terms(proj, conv_w, alog_v, dtb_v, batch, seq, meta_row):
    nc = seq // CHUNK
    n = batch * seq
    full = lambda shape: pl.BlockSpec(shape, lambda i: (0,) * len(shape))
    rows = lambda: pl.BlockSpec((CHUNK, D_MODEL), lambda i: (i, 0))

    def hist_index(i):
        return (jnp.where(i % nc == 0, (meta_row + N_META) // 8, i * (CHUNK // 8)) - 1, 0)

    return pl.pallas_call(
        _gdn_terms_kernel,
        out_shape=(jax.ShapeDtypeStruct((n, D_MODEL), F32),
                   jax.ShapeDtypeStruct((n, D_MODEL), BF16),
                   jax.ShapeDtypeStruct((n, D_MODEL), BF16),
                   jax.ShapeDtypeStruct((n, D_MODEL), BF16),
                   jax.ShapeDtypeStruct((GDN_HEADS, n, CHUNK), BF16),
                   jax.ShapeDtypeStruct((batch * nc, 1, LANES), F32)),
        grid=(batch * nc,),
        in_specs=[
            pl.BlockSpec((CHUNK, GDN_QKV), lambda i: (i, 0)),
            pl.BlockSpec((8, GDN_QKV), hist_index),
            pl.BlockSpec((CHUNK, LANES), lambda i: (i, P_SMALL // LANES)),
            full((CONV_W, GDN_QKV)), full((1, LANES)), full((1, LANES)),
        ],
        out_specs=(rows(), rows(), rows(), rows(),
                   pl.BlockSpec((GDN_HEADS, CHUNK, CHUNK), lambda i: (0, i, 0)),
                   pl.BlockSpec((1, 1, LANES), lambda i: (i, 0, 0))),
        scratch_shapes=[pltpu.VMEM((8 + CHUNK, GDN_QKV), F32)],
        compiler_params=pltpu.CompilerParams(dimension_semantics=("parallel",), vmem_limit_bytes=VMEM_LIMIT),
        name="gdn_terms",
    )(proj, proj, proj, conv_w, alog_v, dtb_v)


def _gdn_scan_kernel(u_ref, w_ref, qg_ref, kd_ref, attn_ref, dec_ref, s0_ref, o_ref, s_out, st_ref, *, cpg):
    c = CHUNK
    heads = range(GDN_HEADS)

    @pl.when(pl.program_id(1) == 0)
    def _():
        st_ref[...] = s0_ref[...]

    def chunk(ci, carry):
        rows = pl.ds(pl.multiple_of(ci * c, c), c)
        dec = dec_ref[ci]
        sl = [slice(h * GDN_DV, (h + 1) * GDN_DV) for h in heads]
        s_old = [st_ref[h] for h in heads]
        sb = [s.astype(BF16) for s in s_old]
        lhs = [jnp.concatenate([w_ref[rows, sl[h]], qg_ref[rows, sl[h]]], axis=0) for h in heads]
        r = [_dot(lhs[h], sb[h]) for h in heads]
        vnb = [(u_ref[rows, sl[h]] - r[h][:c]).astype(BF16) for h in heads]
        out = [r[h][c:] + _dot(attn_ref[h, rows, :], vnb[h]) for h in heads]
        upd = [_dot_tn(kd_ref[rows, sl[h]], vnb[h]) for h in heads]
        for h in heads:
            o_ref[rows, sl[h]] = out[h]
            st_ref[h] = s_old[h] * dec[:, SM_A + h:SM_A + h + 1] + upd[h]
        return carry

    lax.fori_loop(0, cpg, chunk, 0)

    @pl.when(pl.program_id(1) == pl.num_programs(1) - 1)
    def _():
        s_out[0] = st_ref[...]


def _gdn_scan(u, w, qg, kd, attn, dec, s_meta, batch, seq):
    nc = seq // CHUNK
    cpg = _pick(nc, (8, 4, 2, 1))
    ng = nc // cpg
    rows = lambda: pl.BlockSpec((cpg * CHUNK, D_MODEL), lambda b, g: (b * ng + g, 0))
    kern = functools.partial(_gdn_scan_kernel, cpg=cpg)
    return pl.pallas_call(
        kern,
        out_shape=(jax.ShapeDtypeStruct((batch * seq, D_MODEL), F32),
                   jax.ShapeDtypeStruct((batch, GDN_HEADS, GDN_DK, GDN_DV), F32)),
        grid=(batch, ng),
        in_specs=[rows(), rows(), rows(), rows(),
                  pl.BlockSpec((GDN_HEADS, cpg * CHUNK, CHUNK), lambda b, g: (0, b * ng + g, 0)),
                  pl.BlockSpec((cpg, 1, LANES), lambda b, g: (b * ng + g, 0, 0)),
                  pl.BlockSpec((GDN_HEADS, GDN_DK, GDN_DV), lambda b, g: (0, 0, 0))],
        out_specs=(rows(),
                   pl.BlockSpec((1, GDN_HEADS, GDN_DK, GDN_DV), lambda b, g: (b, 0, 0, 0))),
        scratch_shapes=[pltpu.VMEM((GDN_HEADS, GDN_DK, GDN_DV), F32)],
        compiler_params=pltpu.CompilerParams(
            dimension_semantics=("parallel", "arbitrary"), vmem_limit_bytes=VMEM_LIMIT),
        name="gdn_scan",
    )(u, w, qg, kd, attn, dec, s_meta)


def _gdn_sample_kernel(x_ref, sm_ref, cs_ref, st_ref, conv_ref, alog_ref, dtb_ref, o_ref, s_out, *, nb):
    xc = x_ref[...] * conv_ref[CONV_W - 1:CONV_W, :]
    for j in range(CONV_W - 1):
        xc = xc + cs_ref[:, j, :] * conv_ref[j:j + 1, :]
    qs, ks, vs = _qkv_heads(xc)
    g, beta = _gate_lanes(sm_ref[...], alog_ref, dtb_ref)
    eg = jnp.exp(g)
    for h in range(GDN_HEADS):
        q_t = qs[h].T
        k_t = ks[h].T
        for b in range(nb):
            kcol = k_t[:, b:b + 1]
            s1 = st_ref[b, h] * eg[b:b + 1, SM_A + h:SM_A + h + 1]
            r = jnp.sum(s1 * kcol, axis=0, keepdims=True)
            delta = (vs[h][b:b + 1, :] - r) * beta[b:b + 1, SM_B + h:SM_B + h + 1]
            s2 = s1 + kcol * delta
            s_out[b, h] = s2
            o_ref[b:b + 1, h * GDN_DV:(h + 1) * GDN_DV] = jnp.sum(s2 * q_t[:, b:b + 1], axis=0, keepdims=True)


def _gdn_sample(proj, state_conv, state_gdn, conv_w, alog_v, dtb_v, row0):
    ns = state_gdn.shape[0]
    nb = 8
    full = lambda shape: pl.BlockSpec(shape, lambda i: (0,) * len(shape))
    kern = functools.partial(_gdn_sample_kernel, nb=nb)
    return pl.pallas_call(
        kern,
        out_shape=(jax.ShapeDtypeStruct((ns, D_MODEL), F32),
                   jax.ShapeDtypeStruct(state_gdn.shape, F32)),
        grid=(ns // nb,),
        in_specs=[
            pl.BlockSpec((nb, GDN_QKV), lambda i: (row0 // nb + i, 0)),
            pl.BlockSpec((nb, LANES), lambda i: (row0 // nb + i, P_SMALL // LANES)),
            pl.BlockSpec((nb, CONV_W - 1, GDN_QKV), lambda i: (i, 0, 0)),
            pl.BlockSpec((nb, GDN_HEADS, GDN_DK, GDN_DV), lambda i: (i, 0, 0, 0)),
            full((CONV_W, GDN_QKV)), full((1, LANES)), full((1, LANES)),
        ],
        out_specs=(pl.BlockSpec((nb, D_MODEL), lambda i: (i, 0)),
                   pl.BlockSpec((nb, GDN_HEADS, GDN_DK, GDN_DV), lambda i: (i, 0, 0, 0))),
        compiler_params=pltpu.CompilerParams(dimension_semantics=("parallel",), vmem_limit_bytes=VMEM_LIMIT),
        name="gdn_sample",
    )(proj, proj, state_conv, state_gdn, conv_w, alog_v, dtb_v)


def _q_absorb_kernel(q_ref, wukt_ref, ql_out, qr_out):
    for h in range(MLA_HEADS):
        q = q_ref[h]
        ql_out[h] = _dot(q, wukt_ref[h]).astype(BF16)
        qr_out[h] = q[:, NOPE_DIM:NOPE_DIM + ROPE_DIM]


def _q_absorb(q, wukt, ns, row0):
    return pl.pallas_call(
        _q_absorb_kernel,
        out_shape=(jax.ShapeDtypeStruct((MLA_HEADS, ns, KV_LORA), BF16),
                   jax.ShapeDtypeStruct((MLA_HEADS, ns, ROPE_DIM), BF16)),
        grid=(1,),
        in_specs=[pl.BlockSpec((MLA_HEADS, ns, LANES), lambda i: (0, row0 // ns, 0)),
                  pl.BlockSpec((MLA_HEADS, LANES, KV_LORA), lambda i: (0, 0, 0))],
        out_specs=(pl.BlockSpec((MLA_HEADS, ns, KV_LORA), lambda i: (0, 0, 0)),
                   pl.BlockSpec((MLA_HEADS, ns, ROPE_DIM), lambda i: (0, 0, 0))),
        compiler_params=pltpu.CompilerParams(vmem_limit_bytes=VMEM_LIMIT),
        name="q_absorb",
    )(q, wukt)


def _mla_sample_kernel(pt_ref, ql_ref, qr_ref, cn_ref, krn_ref, cc_hbm, cr_hbm, o_ref, cbuf, rbuf, sem,
                       *, n_pages, nsub):
    b = pl.program_id(0)
    slot = b % 2

    def page_copies(seq, slot_, i):
        page = pt_ref[seq * n_pages + i]
        return (pltpu.make_async_copy(cc_hbm.at[page], cbuf.at[slot_, i], sem.at[slot_, 0]),
                pltpu.make_async_copy(cr_hbm.at[page], rbuf.at[slot_, :, pl.ds(i * PAGE_SIZE, PAGE_SIZE)],
                                      sem.at[slot_, 1]))

    def start_pages(seq, slot_):
        for i in range(n_pages):
            cc, cr = page_copies(seq, slot_, i)
            cc.start()
            cr.start()

    @pl.when(b == 0)
    def _():
        start_pages(0, 0)

    @pl.when(b + 1 < pl.num_programs(0))
    def _():
        start_pages(b + 1, 1 - slot)

    for i in range(n_pages):
        cc, cr = page_copies(b, slot, i)
        cc.wait()
        cr.wait()

    ql = ql_ref[0]
    qr = qr_ref[0]
    pps = n_pages // nsub
    subs = range(nsub)
    c = [cbuf[slot, i * pps:(i + 1) * pps].reshape(pps * PAGE_SIZE, KV_LORA).astype(BF16) for i in subs]
    s = [_dot_nt(ql, c[i]) + _dot(qr, rbuf[slot, :, i * pps * PAGE_SIZE:(i + 1) * pps * PAGE_SIZE].astype(BF16))
         for i in subs]
    ms = [jnp.max(s[i], axis=1, keepdims=True) for i in subs]
    p = [jnp.exp(s[i] - ms[i]) for i in subs]
    ls = [jnp.sum(p[i], axis=1, keepdims=True) for i in subs]
    accs = [_dot(p[i].astype(BF16), c[i]) for i in subs]

    cn = cn_ref[0]
    krn = krn_ref[0][:, SM_KR:SM_KR + ROPE_DIM]
    ms.append(jnp.sum(ql.astype(F32) * cn, axis=1, keepdims=True)
              + jnp.sum(qr.astype(F32) * krn, axis=1, keepdims=True))
    ls.append(jnp.ones_like(ms[-1]))
    accs.append(jnp.broadcast_to(cn, (MLA_HEADS, KV_LORA)))
    m = functools.reduce(jnp.maximum, ms)
    scale = [jnp.exp(m_i - m) for m_i in ms]
    l = sum(a * l_i for a, l_i in zip(scale, ls))
    acc = sum(a * acc_i for a, acc_i in zip(scale, accs))
    o_ref[0] = acc / l


def _mla_sample(page_table, ql, qr, c_new, kr_new, cache_c, cache_r):
    ns, n_pages = page_table.shape
    kern = functools.partial(_mla_sample_kernel, n_pages=n_pages, nsub=_pick(n_pages, (8, 4, 2, 1)))
    grid_spec = pltpu.PrefetchScalarGridSpec(
        num_scalar_prefetch=1,
        grid=(ns,),
        in_specs=[
            pl.BlockSpec((1, MLA_HEADS, KV_LORA), lambda b, pt: (b, 0, 0)),
            pl.BlockSpec((1, MLA_HEADS, ROPE_DIM), lambda b, pt: (b, 0, 0)),
            pl.BlockSpec((1, 1, KV_LORA), lambda b, pt: (b, 0, 0)),
            pl.BlockSpec((1, 1, LANES), lambda b, pt: (b, 0, 0)),
            pl.BlockSpec(memory_space=pl.ANY),
            pl.BlockSpec(memory_space=pl.ANY),
        ],
        out_specs=pl.BlockSpec((1, MLA_HEADS, KV_LORA), lambda b, pt: (b, 0, 0)),
        scratch_shapes=[pltpu.VMEM((2, n_pages, PAGE_SIZE, KV_LORA), F32),
                        pltpu.VMEM((2, ROPE_DIM, n_pages * PAGE_SIZE), F32),
                        pltpu.SemaphoreType.DMA((2, 2))],
    )
    return pl.pallas_call(
        kern,
        out_shape=jax.ShapeDtypeStruct((ns, MLA_HEADS, KV_LORA), F32),
        grid_spec=grid_spec,
        compiler_params=pltpu.CompilerParams(dimension_semantics=("arbitrary",), vmem_limit_bytes=VMEM_LIMIT),
        name="mla_sample",
    )(page_table.reshape(-1), ql, qr, c_new, kr_new, cache_c, cache_r)


def _o_proj_sample_kernel(ol_ref, wv_ref, o_ref):
    for p in range(MLA_HEADS // 2):
        acc = None
        for h in (2 * p, 2 * p + 1):
            part = _dot(ol_ref[h].astype(BF16), wv_ref[:, h * LANES:(h + 1) * LANES])
            acc = part if acc is None else acc + part
        o_ref[:, p * LANES:(p + 1) * LANES] = acc.astype(o_ref.dtype)


def _o_proj_sample(o_lat_t, wv):
    ns = o_lat_t.shape[1]
    return pl.pallas_call(
        _o_proj_sample_kernel,
        out_shape=jax.ShapeDtypeStruct((ns, D_MODEL), BF16),
        grid=(1,),
        in_specs=[pl.BlockSpec((MLA_HEADS, ns, KV_LORA), lambda i: (0, 0, 0)),
                  pl.BlockSpec((KV_LORA, MLA_HEADS * LANES), lambda i: (0, 0))],
        out_specs=pl.BlockSpec((ns, D_MODEL), lambda i: (0, 0)),
        compiler_params=pltpu.CompilerParams(vmem_limit_bytes=VMEM_LIMIT),
        name="o_proj_sample",
    )(o_lat_t, wv)


def _outproj_kernel(x_ref, om_ref, og_ref, z_ref, gm_ref, gg_ref, wo_ref, gn_ref, nf_ref, wr_ref, br_ref,
                    xmid_out, hf_out, route_out):
    og = og_ref[...]
    parts = []
    for h in range(GDN_HEADS):
        oh = og[:, h * GDN_DV:(h + 1) * GDN_DV]
        parts.append(oh * lax.rsqrt(jnp.mean(oh * oh, axis=-1, keepdims=True) + RMS_EPS))
    o_gdn = jnp.concatenate(parts, axis=1) * gn_ref[...] * _silu(z_ref[...])
    merged = _sigmoid(gm_ref[...]) * om_ref[...].astype(F32) + _sigmoid(gg_ref[...]) * o_gdn
    x_mid = x_ref[...] + _dot(merged.astype(BF16), wo_ref[...])
    xmid_out[...] = x_mid
    hf = _rms(x_mid, nf_ref[...]).astype(BF16)
    bits = lax.bitcast_convert_type(hf.astype(F32), jnp.uint32)
    half = D_MODEL // 2
    hf_out[...] = bits[:, half:] | (bits[:, :half] >> 16)

    logits = _dot(hf, wr_ref[...]) + br_ref[...]
    lane = lax.broadcasted_iota(jnp.int32, logits.shape, 1)
    neg = -jnp.inf
    big = 4 * LANES
    is_g = lane < N_GROUPS
    lg = jnp.where(is_g, logits, neg)
    mg = jnp.max(lg, axis=1, keepdims=True)
    grp = jnp.min(jnp.where(lg == mg, lane, big), axis=1, keepdims=True)
    gate_g = 1.0 / jnp.sum(jnp.where(is_g, jnp.exp(logits - mg), 0.0), axis=1, keepdims=True)
    e_lane = lane - N_GROUPS
    in_grp = (e_lane >= 0) & (e_lane < N_EXPERTS) & ((e_lane >> 3) == grp)
    le = jnp.where(in_grp, logits, neg)
    v1 = jnp.max(le, axis=1, keepdims=True)
    i1 = jnp.min(jnp.where(le == v1, lane, big), axis=1, keepdims=True)
    le2 = jnp.where(lane == i1, neg, le)
    v2 = jnp.max(le2, axis=1, keepdims=True)
    i2 = jnp.min(jnp.where(le2 == v2, lane, big), axis=1, keepdims=True)
    e = jnp.exp(v2 - v1)
    w1 = gate_g / (1.0 + e)
    w2 = gate_g * e / (1.0 + e)
    route = jnp.where(lane == 0, (i1 - N_GROUPS).astype(F32),
                      jnp.where(lane == 1, (i2 - N_GROUPS).astype(F32),
                                jnp.where(lane == 2, w1, jnp.where(lane == 3, w2, 0.0))))
    route_out[...] = route


def _row_tile(n, limit):
    t = limit - limit % 16
    while t >= 16:
        if n % t == 0:
            return t
        t -= 16
    raise ValueError(f"no row tile for {n}")


def _outproj(x_all, o_mla, o_gdn, proj, w_out, gn_t, norm_ffn, w_r, b_r, row0):
    n = o_mla.shape[0]
    tm = _pick(n, (512, 256, 128, 64, 32, 16))
    assert row0 % tm == 0
    r0 = row0 // tm
    full = lambda shape: pl.BlockSpec(shape, lambda i: (0,) * len(shape))
    row = lambda w: pl.BlockSpec((tm, w), lambda i: (i, 0))
    shared = lambda w, j=0: pl.BlockSpec((tm, w), lambda i, j=j: (r0 + i, j))
    return pl.pallas_call(
        _outproj_kernel,
        out_shape=(jax.ShapeDtypeStruct((n, D_MODEL), F32),
                   jax.ShapeDtypeStruct((n, D_MODEL // 2), jnp.uint32),
                   jax.ShapeDtypeStruct((n, LANES), F32)),
        grid=(n // tm,),
        in_specs=[shared(D_MODEL), row(D_MODEL), row(D_MODEL),
                  shared(D_MODEL, P_Z // D_MODEL), shared(D_MODEL, P_GM // D_MODEL), shared(D_MODEL, P_GG // D_MODEL),
                  full((D_MODEL, D_MODEL)), full((1, D_MODEL)), full((1, D_MODEL)),
                  full((D_MODEL, LANES)), full((1, LANES))],
        out_specs=(row(D_MODEL), row(D_MODEL // 2), row(LANES)),
        compiler_params=pltpu.CompilerParams(dimension_semantics=("parallel",), vmem_limit_bytes=VMEM_LIMIT),
        name="outproj_route",
    )(x_all, o_mla, o_gdn, proj, proj, proj, w_out, gn_t, norm_ffn, w_r, b_r)


def _moe_plan(eid, wgt, nsp):
    nt = eid.shape[0]
    ts = nt // nsp
    n_asg = ts * TOP_K
    nb = (n_asg + N_EXPERTS * (MOE_BLOCK - 1) + MOE_BLOCK - 1) // MOE_BLOCK
    e = eid.reshape(nsp, n_asg)
    ids = jnp.broadcast_to(jnp.arange(n_asg, dtype=jnp.int32), e.shape)
    _, ids_s, w_s = lax.sort((e, ids, wgt.reshape(nsp, n_asg)), dimension=1, num_keys=1, is_stable=True)
    rows_s = (ids_s % TOP_K) * (ts + 8) + ids_s // TOP_K
    counts = jnp.sum((e[..., None] == jnp.arange(N_EXPERTS, dtype=jnp.int32)).astype(jnp.int32), axis=1)
    run_start = jnp.cumsum(counts, axis=1) - counts
    padded = (counts + MOE_BLOCK - 1) // MOE_BLOCK * MOE_BLOCK
    pad_end = jnp.cumsum(padded, axis=1)
    starts = jnp.arange(nb, dtype=jnp.int32) * MOE_BLOCK
    blk_e = jnp.minimum(jnp.sum(pad_end[:, None, :] <= starts[None, :, None], axis=2), N_EXPERTS - 1)
    of_blk = lambda a: jnp.take_along_axis(a, blk_e, axis=1)
    rank0 = starts[None, :] - of_blk(pad_end - padded)
    blk_p0 = of_blk(run_start) + rank0
    blk_nv = jnp.clip(of_blk(counts) - rank0, 0, MOE_BLOCK)
    nused = pad_end[:, -1] // MOE_BLOCK
    tail = lambda v, dt: jnp.full((nsp, MOE_BLOCK), v, dt)
    rows_t = jnp.concatenate([rows_s, tail(ts, jnp.int32)], axis=1)
    w_t = jnp.concatenate([w_s, tail(0.0, F32)], axis=1)
    as_i32 = lambda a: a.astype(jnp.int32).reshape(-1)
    scalars = (as_i32(blk_e), as_i32(nused), as_i32(blk_p0), as_i32(blk_nv), as_i32(rows_t), w_t.reshape(-1))
    return scalars, ts, nb


def _split_pieces(k, ts, n_prompt, n_sample):
    lo, hi = k * ts, (k + 1) * ts
    pieces = []
    if lo < n_prompt:
        pieces.append((0, lo, 0, min(hi, n_prompt) - lo))
    if hi > n_prompt:
        start = max(lo, n_prompt)
        pieces.append((1, start - n_prompt, start - lo, hi - start))
    assert hi <= n_prompt + n_sample
    return pieces


def _moe_kernel(blk_ref, nused_ref, p0_ref, nv_ref, rows_ref, w_ref, hf_p, hf_q, xmid_p, xmid_q,
                wg_ref, wu_ref, wd_ref, nfin_ref, y_p, y_q, hf_s, comb, acc, xb, yb, sem,
                *, ts, nb, rc, nsp, n_prompt, n_sample):
    s = pl.program_id(0)
    j = pl.program_id(1)
    half = D_MODEL // 2
    stride = ts + 8

    def zero_spare_rows():
        hf_s[ts:ts + 8, :] = jnp.zeros((8, half), jnp.uint32)

    def split_copies(k, load):
        cps = []
        for src, r0, l0, n in _split_pieces(k, ts, n_prompt, n_sample):
            if load:
                cps.append(pltpu.make_async_copy((hf_p, hf_q)[src].at[pl.ds(r0, n)], hf_s.at[pl.ds(l0, n)],
                                                 sem.at[0, src]))
                cps.append(pltpu.make_async_copy((xmid_p, xmid_q)[src].at[pl.ds(r0, n)], acc.at[pl.ds(l0, n)],
                                                 sem.at[1, src]))
            else:
                cps.append(pltpu.make_async_copy(acc.at[pl.ds(l0, n)], (y_p, y_q)[src].at[pl.ds(r0, n)],
                                                 sem.at[2, src]))
        return cps

    def run_copies(load, between=None):
        for k in range(nsp):
            @pl.when(s == k)
            def _(k=k):
                cps = split_copies(k, load)
                for cp in cps:
                    cp.start()
                if between is not None:
                    between()
                for cp in cps:
                    cp.wait()

    @pl.when(j == 0)
    def _():
        run_copies(True, zero_spare_rows)

    @pl.when(j < nused_ref[s])
    def _():
        p0 = s * (ts * TOP_K + MOE_BLOCK) + p0_ref[s * nb + j]
        nv = nv_ref[s * nb + j]
        comb_rows = [jnp.where(r < nv, rows_ref[p0 + r], ts) for r in range(MOE_BLOCK)]
        for r in range(MOE_BLOCK):
            row = comb_rows[r]
            for k in range(1, TOP_K):
                row = jnp.where(row >= k * stride, row - stride, row)
            xb[r:r + 1, :] = hf_s[pl.ds(row, 1), :]
        bits = xb[...]
        lo = lax.bitcast_convert_type(bits << 16, F32).astype(BF16)
        hi = lax.bitcast_convert_type(bits & jnp.uint32(0xFFFF0000), F32).astype(BF16)
        g = _dot(lo, wg_ref[0, :half, :]) + _dot(hi, wg_ref[0, half:, :])
        u = _dot(lo, wu_ref[0, :half, :]) + _dot(hi, wu_ref[0, half:, :])
        yb[...] = _dot((_silu(g) * u).astype(BF16), wd_ref[0])
        for r in range(MOE_BLOCK):
            comb[pl.ds(comb_rows[r], 1), :] = jnp.where(r < nv, w_ref[p0 + r], 0.0) * yb[r:r + 1, :]

    @pl.when(j == nb - 1)
    def _():
        def body(i, carry):
            r0 = pl.multiple_of(i * rc, 8)
            rows = pl.ds(r0, rc)
            moe = comb[rows, :]
            for k in range(1, TOP_K):
                moe = moe + comb[pl.ds(k * stride + r0, rc), :]
            acc[rows, :] = _rms(acc[rows, :] + moe, nfin_ref[...])
            return carry

        lax.fori_loop(0, ts // rc, body, 0)
        run_copies(False)


def _moe(plan, hf_p, hf_q, xmid_p, xmid_q, wg, wu, wd, norm_final):
    scalars, ts, nb = plan
    n_prompt, n_sample = xmid_p.shape[0], xmid_q.shape[0]
    nsp = (n_prompt + n_sample) // ts
    rc = _row_tile(ts, 256) if ts % 16 == 0 else 8
    kern = functools.partial(_moe_kernel, ts=ts, nb=nb, rc=rc, nsp=nsp, n_prompt=n_prompt, n_sample=n_sample)
    hbm = pl.BlockSpec(memory_space=pl.ANY)
    expert = lambda shape: pl.BlockSpec((1,) + shape, lambda s, j, be, *_: (be[s * nb + j], 0, 0))
    grid_spec = pltpu.PrefetchScalarGridSpec(
        num_scalar_prefetch=len(scalars),
        grid=(nsp, nb),
        in_specs=[
            hbm, hbm, hbm, hbm,
            expert((D_MODEL, D_EXPERT)), expert((D_MODEL, D_EXPERT)), expert((D_EXPERT, D_MODEL)),
            pl.BlockSpec((1, D_MODEL), lambda s, j, *_: (0, 0)),
        ],
        out_specs=(hbm, hbm),
        scratch_shapes=[
            pltpu.VMEM((ts + 8, D_MODEL // 2), jnp.uint32),
            pltpu.VMEM((TOP_K * (ts + 8), D_MODEL), F32),
            pltpu.VMEM((ts, D_MODEL), F32),
            pltpu.VMEM((MOE_BLOCK, D_MODEL // 2), jnp.uint32),
            pltpu.VMEM((MOE_BLOCK, D_MODEL), F32),
            pltpu.SemaphoreType.DMA((3, 2)),
        ],
    )
    return pl.pallas_call(
        kern,
        out_shape=(jax.ShapeDtypeStruct((n_prompt, D_MODEL), F32), jax.ShapeDtypeStruct((n_sample, D_MODEL), F32)),
        grid_spec=grid_spec,
        compiler_params=pltpu.CompilerParams(
            dimension_semantics=("arbitrary", "arbitrary"), vmem_limit_bytes=VMEM_LIMIT),
        name="moe",
    )(*scalars, hf_p, hf_q, xmid_p, xmid_q, wg, wu, wd, norm_final)


def _pack_w_in(w):
    kr = w[:, _OFF_KV + KV_LORA:_OFF_QKV]
    kr_sw = jnp.concatenate([kr[:, ROPE_DIM // 2:], kr[:, :ROPE_DIM // 2]], axis=1)
    small = jnp.concatenate([w[:, _OFF_B:_OFF_A], w[:, _OFF_A:_OFF_GM],
                             jnp.zeros((D_MODEL, SM_KR - 2 * GDN_HEADS), w.dtype), kr, kr_sw], axis=1)
    packed = jnp.concatenate([w[:, _OFF_QKV:_OFF_Z], w[:, _OFF_Z:_OFF_B], w[:, _OFF_GM:_OFF_GG], w[:, _OFF_GG:],
                              w[:, _OFF_KV:_OFF_KV + KV_LORA], small, w[:, :Q_LORA]], axis=1)
    return packed.astype(BF16)


def _pack_mla_weights(w_uq, w_uk, w_uv):
    zq = jnp.zeros((Q_LORA, MLA_HEADS, LANES - NOPE_DIM - ROPE_DIM), w_uq.dtype)
    wq = jnp.concatenate([w_uq, zq], axis=2).reshape(Q_LORA, MLA_HEADS * LANES)
    rope = w_uq[:, :, NOPE_DIM:]
    rope_sw = jnp.concatenate([rope[..., ROPE_DIM // 2:], rope[..., :ROPE_DIM // 2]], axis=2)
    wqs = jnp.concatenate([jnp.zeros((Q_LORA, MLA_HEADS, NOPE_DIM), w_uq.dtype), rope_sw, zq], axis=2)
    wqs = wqs.reshape(Q_LORA, MLA_HEADS * LANES)
    wk = jnp.concatenate([w_uk, jnp.zeros((KV_LORA, MLA_HEADS, LANES - NOPE_DIM), w_uk.dtype)], axis=2)
    wk = wk.reshape(KV_LORA, MLA_HEADS * LANES)
    zv = jnp.zeros((KV_LORA, MLA_HEADS // 2, V_DIM), w_uv.dtype)
    wv = jnp.stack([jnp.concatenate([w_uv[:, 0::2], zv], axis=2),
                    jnp.concatenate([zv, w_uv[:, 1::2]], axis=2)], axis=2)
    wv = wv.reshape(KV_LORA, MLA_HEADS * LANES)
    return wq.astype(BF16), wqs.astype(BF16), wk.astype(BF16), wv.astype(BF16)


def _rope_tables(pos):
    inv_freq = ROPE_THETA ** (-jnp.arange(0, ROPE_DIM, 2, dtype=F32) / ROPE_DIM)
    ang = pos.astype(F32)[:, None] * inv_freq[None, :]
    cos, sin = jnp.cos(ang), jnp.sin(ang)
    n = pos.shape[0]
    cos_t = jnp.concatenate([jnp.ones((n, NOPE_DIM), F32), cos, cos, jnp.zeros((n, ROPE_DIM), F32)], axis=1)
    sin_t = jnp.concatenate([jnp.zeros((n, NOPE_DIM), F32), -sin, sin, jnp.zeros((n, ROPE_DIM), F32)], axis=1)
    return cos_t, sin_t


def _head_lanes(v):
    return jnp.zeros((1, LANES), F32).at[0, SM_A:SM_A + GDN_HEADS].set(v.astype(F32))


def _moe_splits(nt):
    for nsp in (6, 3, 4, 2, 1):
        if nt % (nsp * 8) == 0:
            return nsp
    return 1


def kernel(x_prompt, x_sample, cache_kv_latent, cache_k_rope, page_table, state_conv, state_gdn, meta_tokens,
           norm_mix, w_in, q_norm, w_uq, kv_norm, w_uk, w_uv, conv_w, a_log, dt_bias, gdn_norm, w_out, norm_ffn,
           w_group, b_group, w_router, b_router, w_gate, w_up, w_down, norm_final):
    batch, seq, _ = x_prompt.shape
    ns, dec_seq, _ = x_sample.shape
    assert dec_seq == 1 and w_in.shape[0] == 1 and seq % CHUNK == 0
    n_pages = page_table.shape[1]
    n_prompt = batch * seq
    nt = n_prompt + ns
    meta_row = nt
    n_rows = -(-(nt + N_META) // LANES) * LANES
    assert n_prompt % ns == 0 and nt % N_META == 0 and ns % 8 == 0

    x_all = jnp.concatenate([x_prompt.reshape(n_prompt, D_MODEL), x_sample.reshape(ns, D_MODEL),
                             meta_tokens.astype(x_prompt.dtype),
                             jnp.zeros((n_rows - nt - N_META, D_MODEL), x_prompt.dtype)], axis=0)
    pos = jnp.concatenate([jnp.tile(N_META + jnp.arange(seq), batch), jnp.full((ns,), n_pages * PAGE_SIZE),
                           jnp.arange(N_META), jnp.zeros((n_rows - nt - N_META,), jnp.int32)])
    cos_t, sin_t = _rope_tables(pos)
    w_packed = _pack_w_in(w_in[0])
    wq, wqs, wk, wv = _pack_mla_weights(w_uq[0], w_uk[0], w_uv[0])
    wukt = jnp.concatenate([jnp.transpose(w_uk[0], (1, 2, 0)),
                            jnp.zeros((MLA_HEADS, LANES - NOPE_DIM, KV_LORA), w_uk.dtype)], axis=1).astype(BF16)
    alog_v, dtb_v = _head_lanes(a_log[0]), _head_lanes(dt_bias[0])
    cw = conv_w[0].astype(F32)
    gn_t = jnp.tile(gdn_norm[0].astype(F32), GDN_HEADS)[None]
    w_r = jnp.concatenate([w_group[0], w_router[0],
                           jnp.zeros((D_MODEL, LANES - N_GROUPS - N_EXPERTS), w_group.dtype)], axis=1).astype(BF16)
    b_r = jnp.concatenate([b_group[0], b_router[0], jnp.zeros((LANES - N_GROUPS - N_EXPERTS,), b_group.dtype)])[None]
    wg, wu, wd = w_gate[0].astype(BF16), w_up[0].astype(BF16), w_down[0].astype(BF16)

    proj = _inproj(x_all, norm_mix[0][None].astype(F32), w_packed)
    q, k, v, ckv, krot = _mla_prep(proj, cos_t, sin_t, q_norm[0][None].astype(F32), kv_norm[0][None].astype(F32),
                                   wq, wqs, wk, wv)

    o_mla_p = _attn_prompt(q, k, v, batch, seq, meta_row)
    ql, qr = _q_absorb(q, wukt, ns, n_prompt)
    o_lat = _mla_sample(page_table, jnp.transpose(ql, (1, 0, 2)), jnp.transpose(qr, (1, 0, 2)),
                        ckv[n_prompt:nt].reshape(ns, 1, KV_LORA), krot[n_prompt:nt].reshape(ns, 1, LANES),
                        cache_kv_latent[0], jnp.swapaxes(cache_k_rope[0], 1, 2))
    o_mla_s = _o_proj_sample(jnp.transpose(o_lat, (1, 0, 2)), wv)

    s_meta = _gdn_meta(proj, cw, alog_v, dtb_v, meta_row)
    terms = _gdn_terms(proj, cw, alog_v, dtb_v, batch, seq, meta_row)
    o_gdn_p, gdn_p = _gdn_scan(*terms, s_meta, batch, seq)
    o_gdn_s, gdn_s = _gdn_sample(proj, state_conv[0], state_gdn[0], cw, alog_v, dtb_v, n_prompt)

    tail = (proj, w_out[0].astype(BF16), gn_t, norm_ffn[0][None].astype(F32), w_r, b_r.astype(F32))
    xmid_p, hf_p, route_p = _outproj(x_all, o_mla_p, o_gdn_p, *tail, 0)
    xmid_s, hf_s, route_s = _outproj(x_all, o_mla_s, o_gdn_s, *tail, n_prompt)
    route = jnp.concatenate([route_p[:, :2 * TOP_K], route_s[:, :2 * TOP_K]], axis=0)
    plan = _moe_plan(route[:, :TOP_K].astype(jnp.int32), route[:, TOP_K:], _moe_splits(nt))
    y_p, y_s = _moe(plan, hf_p, hf_s, xmid_p, xmid_s, wg, wu, wd, norm_final[None].astype(F32))

    def with_meta(rows, width):
        meta = jnp.broadcast_to(rows[meta_row:meta_row + N_META][None], (batch, N_META, width))
        return jnp.concatenate([meta, rows[:n_prompt].reshape(batch, seq, width)], axis=1)[None]

    k_rope = krot[:, SM_KR:SM_KR + ROPE_DIM]
    conv_p = jnp.stack([proj[(b + 1) * seq - (CONV_W - 1):(b + 1) * seq, :GDN_QKV] for b in range(batch)])
    conv_s = jnp.concatenate([state_conv[0][:, 1:].astype(F32), proj[n_prompt:nt, None, :GDN_QKV]], axis=1)
    return (y_p.reshape(batch, seq, D_MODEL), y_s.reshape(ns, 1, D_MODEL),
            with_meta(ckv, KV_LORA), with_meta(k_rope, ROPE_DIM),
            ckv[n_prompt:nt].reshape(1, ns, 1, KV_LORA), k_rope[n_prompt:nt].reshape(1, ns, 1, ROPE_DIM),
            conv_p[None], conv_s[None], gdn_p[None], gdn_s[None])
```

```python
import functools

import jax
import jax.numpy as jnp
from jax import lax
from jax.experimental import pallas as pl
from jax.experimental.pallas import tpu as pltpu

F32 = jnp.float32
BF16 = jnp.bfloat16
HIGHEST = lax.Precision.HIGHEST

D_MODEL = 1024
N_META = 16
RMS_EPS = 1e-6
MLA_HEADS = 16
Q_LORA = 384
KV_LORA = 256
NOPE_DIM = 64
ROPE_DIM = 32
V_DIM = 64
ROPE_THETA = 10000.0
MLA_SCALE = (NOPE_DIM + ROPE_DIM) ** -0.5
PAGE_SIZE = 128
GDN_HEADS = 8
GDN_DK = 128
GDN_DV = 128
GDN_KEY = GDN_HEADS * GDN_DK
GDN_QKV = 3 * GDN_KEY
CONV_W = 4
CHUNK = 64
N_GROUPS = 4
EXPERTS_PER_GROUP = 8
N_EXPERTS = 32
TOP_K = 2
D_EXPERT = 256
MOE_BLOCK = 128

_OFF_KV = Q_LORA
_OFF_QKV = _OFF_KV + KV_LORA + ROPE_DIM
_OFF_Z = _OFF_QKV + GDN_QKV
_OFF_B = _OFF_Z + GDN_KEY
_OFF_A = _OFF_B + GDN_HEADS
_OFF_GM = _OFF_A + GDN_HEADS
_OFF_GG = _OFF_GM + D_MODEL
P_QKV = 0
P_Z = 3072
P_GM = 4096
P_GG = 5120
P_KVC = 6144
P_SMALL = 6400
P_QD = 6528
P_TOTAL = 6912
SM_B = 0
SM_A = 8
SM_KR = 64

LANES = 128
VMEM_LIMIT = 56 * 1024 * 1024
ATTN_TQ = (512, 256, 128)


def _pick(n, candidates):
    for c in candidates:
        if n % c == 0:
            return c
    raise ValueError(f"no tile for {n} in {candidates}")


def _dot(a, b):
    return jnp.dot(a, b, preferred_element_type=F32)


def _dot_nt(a, b):
    return lax.dot_general(a, b, (((1,), (1,)), ((), ())), preferred_element_type=F32)


def _dot_tn(a, b):
    return lax.dot_general(a, b, (((0,), (0,)), ((), ())), preferred_element_type=F32)


def _sigmoid(x):
    return 1.0 / (1.0 + jnp.exp(-x))


def _silu(x):
    return x * _sigmoid(x)


def _softplus(x):
    return jnp.maximum(x, 0.0) + jnp.log1p(jnp.exp(-jnp.abs(x)))


def _rms(x, w):
    return x * lax.rsqrt(jnp.mean(x * x, axis=-1, keepdims=True) + RMS_EPS) * w


def _inproj_kernel(x_ref, nw_ref, w_ref, o_ref, hn_ref):
    @pl.when(pl.program_id(1) == 0)
    def _():
        hn_ref[...] = _rms(x_ref[...], nw_ref[...]).astype(BF16)

    o_ref[...] = _dot(hn_ref[...], w_ref[...])


def _inproj(x_all, norm_w, w_packed):
    r = x_all.shape[0]
    tm = _pick(r, (1280, 640, 512, 256, 128))
    tn = 768
    return pl.pallas_call(
        _inproj_kernel,
        out_shape=jax.ShapeDtypeStruct((r, P_TOTAL), F32),
        grid=(r // tm, P_TOTAL // tn),
        in_specs=[
            pl.BlockSpec((tm, D_MODEL), lambda i, j: (i, 0)),
            pl.BlockSpec((1, D_MODEL), lambda i, j: (0, 0)),
            pl.BlockSpec((D_MODEL, tn), lambda i, j: (0, j)),
        ],
        out_specs=pl.BlockSpec((tm, tn), lambda i, j: (i, j)),
        scratch_shapes=[pltpu.VMEM((tm, D_MODEL), BF16)],
        compiler_params=pltpu.CompilerParams(
            dimension_semantics=("parallel", "arbitrary"), vmem_limit_bytes=VMEM_LIMIT),
        name="inproj",
    )(x_all, norm_w, w_packed)


def _mla_prep_kernel(qd_ref, kvc_ref, sm_ref, c_ref, s_ref, qn_ref, kvn_ref, wq_ref, wqs_ref, wk_ref, wv_ref,
                     q_out, k_out, v_out, ckv_out, kr_out):
    cos = c_ref[...]
    sin = s_ref[...]
    qn = _rms(qd_ref[...], qn_ref[...]).astype(BF16)
    q = _dot(qn, wq_ref[...])
    qs = _dot(qn, wqs_ref[...])
    for h in range(MLA_HEADS):
        sl = slice(h * LANES, (h + 1) * LANES)
        q_out[h] = ((q[:, sl] * cos + qs[:, sl] * sin) * MLA_SCALE).astype(BF16)
    ckv = _rms(kvc_ref[...], kvn_ref[...])
    ckv_out[...] = ckv
    cb = ckv.astype(BF16)
    sm = sm_ref[...]
    lane = lax.broadcasted_iota(jnp.int32, sm.shape, 1)
    cos_k = jnp.where((lane >= SM_KR) & (lane < SM_KR + ROPE_DIM), cos, 0.0)
    krot = sm * cos_k + pltpu.roll(sm, LANES - ROPE_DIM, 1) * sin
    kr_out[...] = krot
    kk = _dot(cb, wk_ref[...])
    vv = _dot(cb, wv_ref[...])
    for h in range(MLA_HEADS):
        sl = slice(h * LANES, (h + 1) * LANES)
        k_out[h] = (kk[:, sl] + krot).astype(BF16)
        v_out[h] = vv[:, sl].astype(BF16)


def _mla_prep(proj, cos_t, sin_t, q_norm, kv_norm, wq, wqs, wk, wv):
    r = proj.shape[0]
    tm = _pick(r, (640, 512, 256, 128))
    hw = MLA_HEADS * LANES
    full = lambda shape: pl.BlockSpec(shape, lambda i: (0,) * len(shape))
    head_out = pl.BlockSpec((MLA_HEADS, tm, LANES), lambda i: (0, i, 0))
    return pl.pallas_call(
        _mla_prep_kernel,
        out_shape=(
            jax.ShapeDtypeStruct((MLA_HEADS, r, LANES), BF16),
            jax.ShapeDtypeStruct((MLA_HEADS, r, LANES), BF16),
            jax.ShapeDtypeStruct((MLA_HEADS, r, LANES), BF16),
            jax.ShapeDtypeStruct((r, KV_LORA), F32),
            jax.ShapeDtypeStruct((r, LANES), F32),
        ),
        grid=(r // tm,),
        in_specs=[
            pl.BlockSpec((tm, Q_LORA), lambda i: (i, P_QD // Q_LORA)),
            pl.BlockSpec((tm, KV_LORA), lambda i: (i, P_KVC // KV_LORA)),
            pl.BlockSpec((tm, LANES), lambda i: (i, P_SMALL // LANES)),
            pl.BlockSpec((tm, LANES), lambda i: (i, 0)),
            pl.BlockSpec((tm, LANES), lambda i: (i, 0)),
            full((1, Q_LORA)), full((1, KV_LORA)),
            full((Q_LORA, hw)), full((Q_LORA, hw)), full((KV_LORA, hw)), full((KV_LORA, hw)),
        ],
        out_specs=(head_out, head_out, head_out,
                   pl.BlockSpec((tm, KV_LORA), lambda i: (i, 0)),
                   pl.BlockSpec((tm, LANES), lambda i: (i, 0))),
        compiler_params=pltpu.CompilerParams(dimension_semantics=("parallel",), vmem_limit_bytes=VMEM_LIMIT),
        name="mla_prep",
    )(proj, proj, proj, cos_t, sin_t, q_norm, kv_norm, wq, wqs, wk, wv)


def _attn_prompt_kernel(q_ref, k_ref, v_ref, km_ref, vm_ref, o_ref, *, tq):
    qi = pl.program_id(2)
    hs = (0, 1)
    half = tq // 2
    q = [q_ref[h] for h in hs]

    def update(qs, rows, width, carry, mask):
        ms, ls, accs = carry
        s = [_dot_nt(qs[h], k_ref[h, pl.ds(rows, width), :]) for h in hs]
        if mask is not None:
            s = [jnp.where(mask, x, -1e30) for x in s]
        m_new = [jnp.maximum(ms[h], jnp.max(s[h], axis=1, keepdims=True)) for h in hs]
        a = [jnp.exp(ms[h] - m_new[h]) for h in hs]
        p = [jnp.exp(s[h] - m_new[h]) for h in hs]
        l_new = [a[h] * ls[h] + jnp.sum(p[h], axis=1, keepdims=True) for h in hs]
        acc_new = [a[h] * accs[h] + _dot(p[h].astype(BF16), v_ref[h, pl.ds(rows, width), :]) for h in hs]
        return m_new, l_new, acc_new

    s0 = [_dot_nt(q[h], km_ref[h]) for h in hs]
    m = [jnp.max(s0[h], axis=1, keepdims=True) for h in hs]
    p0 = [jnp.exp(s0[h] - m[h]) for h in hs]
    l = [jnp.sum(p0[h], axis=1, keepdims=True) for h in hs]
    acc = [_dot(p0[h].astype(BF16), vm_ref[h]) for h in hs]

    def body(j, carry):
        return update(q, pl.multiple_of(j * tq, tq), tq, carry, None)

    m, l, acc = lax.fori_loop(0, qi, body, (m, l, acc))

    off = pl.multiple_of(qi * tq, tq)
    row = lax.broadcasted_iota(jnp.int32, (tq, half), 0)
    col = lax.broadcasted_iota(jnp.int32, (tq, half), 1)
    m, l, acc = update(q, off, half, (m, l, acc), col <= row)
    bot = lambda xs: [x[half:] for x in xs]
    mb, lb, accb = update(bot(q), pl.multiple_of(off + half, half), half, (bot(m), bot(l), bot(acc)),
                          (col <= row)[:half])
    o = [jnp.concatenate([acc[h][:half] / l[h][:half], accb[h] / lb[h]], axis=0) for h in hs]
    o_ref[...] = (o[0] + o[1]).astype(o_ref.dtype)


def _attn_prompt(q, k, v, batch, seq, meta_row):
    tq = _pick(seq, ATTN_TQ)
    nq = seq // tq
    kern = functools.partial(_attn_prompt_kernel, tq=tq)
    return pl.pallas_call(
        kern,
        out_shape=jax.ShapeDtypeStruct((batch * seq, D_MODEL), BF16),
        grid=(batch, MLA_HEADS // 2, nq),
        in_specs=[
            pl.BlockSpec((2, tq, LANES), lambda b, p, i: (p, b * nq + i, 0)),
            pl.BlockSpec((2, seq, LANES), lambda b, p, i: (p, b, 0)),
            pl.BlockSpec((2, seq, LANES), lambda b, p, i: (p, b, 0)),
            pl.BlockSpec((2, N_META, LANES), lambda b, p, i: (p, meta_row // N_META, 0)),
            pl.BlockSpec((2, N_META, LANES), lambda b, p, i: (p, meta_row // N_META, 0)),
        ],
        out_specs=pl.BlockSpec((tq, LANES), lambda b, p, i: (b * nq + i, p)),
        compiler_params=pltpu.CompilerParams(
            dimension_semantics=("parallel", "parallel", "arbitrary"), vmem_limit_bytes=VMEM_LIMIT),
        name="attn_prompt",
    )(q, k, v, k, v)


def _gate_lanes(sm, alog_ref, dtb_ref):
    g = -jnp.exp(alog_ref[...]) * _softplus(sm + dtb_ref[...])
    beta = _sigmoid(sm)
    return g, beta


def _qkv_heads(xc):
    xf = _silu(xc)
    qs, ks, vs = [], [], []
    for h in range(GDN_HEADS):
        q = xf[:, h * GDN_DK:(h + 1) * GDN_DK]
        k = xf[:, GDN_KEY + h * GDN_DK:GDN_KEY + (h + 1) * GDN_DK]
        qs.append(q * lax.rsqrt(jnp.sum(q * q, axis=-1, keepdims=True) + RMS_EPS) * (GDN_DK ** -0.5))
        ks.append(k * lax.rsqrt(jnp.sum(k * k, axis=-1, keepdims=True) + RMS_EPS))
        vs.append(xf[:, 2 * GDN_KEY + h * GDN_DV:2 * GDN_KEY + (h + 1) * GDN_DV])
    return qs, ks, vs


def _split_bf16(x):
    hi = x.astype(BF16)
    return hi, (x - hi.astype(F32)).astype(BF16)


def _dot_split(a, b):
    return _dot(a[0], b[0]) + (_dot(a[0], b[1]) + _dot(a[1], b[0]))


def _unit_lower_inverses(mats, c):
    row = lax.broadcasted_iota(jnp.int32, (c, c), 0)
    col = lax.broadcasted_iota(jnp.int32, (c, c), 1)
    eye = jnp.where(row == col, 1.0, 0.0)
    ps = [-a for a in mats]
    ts = [eye + p for p in ps]
    span = 2
    while span < c:
        psp = [_split_bf16(p) for p in ps]
        ps = [_dot_split(p, p) for p in psp]
        psp = [_split_bf16(p) for p in ps]
        ts = [t + _dot_split(p, _split_bf16(t)) for p, t in zip(psp, ts)]
        span *= 2
    return ts


def _gdn_chunk_terms(xs, sm, conv_ref, alog_ref, dtb_ref, c):
    heads = range(GDN_HEADS)
    xc = xs[0] * conv_ref[0:1, :]
    for j in range(1, CONV_W):
        xc = xc + xs[j] * conv_ref[j:j + 1, :]
    qs, ks, vs = _qkv_heads(xc)
    g, beta = _gate_lanes(sm, alog_ref, dtb_ref)
    row = lax.broadcasted_iota(jnp.int32, (c, c), 0)
    col = lax.broadcasted_iota(jnp.int32, (c, c), 1)
    causal = col <= row
    strict = col < row
    gcum = jnp.dot(jnp.where(causal, 1.0, 0.0), g, precision=HIGHEST, preferred_element_type=F32)
    gcum_t = lax.dot_general(g, jnp.where(col >= row, 1.0, 0.0), (((0,), (0,)), ((), ())),
                             precision=HIGHEST, preferred_element_type=F32)
    gc = [gcum[:, SM_A + h:SM_A + h + 1] for h in heads]
    gr = [gcum_t[SM_A + h:SM_A + h + 1, :] for h in heads]
    bc = [beta[:, SM_B + h:SM_B + h + 1] for h in heads]
    decay = [jnp.where(causal, jnp.exp(jnp.where(causal, gc[h] - gr[h], 0.0)), 0.0) for h in heads]
    kb = [ks[h] * bc[h] for h in heads]
    kbf = [ks[h].astype(BF16) for h in heads]
    a = [jnp.where(strict, _dot_nt(kb[h].astype(BF16), kbf[h]) * decay[h], 0.0) for h in heads]
    t = _unit_lower_inverses(a, c)
    eg = [jnp.exp(gc[h]) for h in heads]
    sol = [_dot(t[h].astype(BF16), jnp.concatenate([vs[h] * bc[h], kb[h] * eg[h]], axis=1).astype(BF16))
           for h in heads]
    u = [sol[h][:, :GDN_DV] for h in heads]
    w = [sol[h][:, GDN_DV:] for h in heads]
    attn = [jnp.where(causal, _dot_nt(qs[h].astype(BF16), kbf[h]) * decay[h], 0.0) for h in heads]
    qg = [qs[h] * eg[h] for h in heads]
    g_last = gcum[c - 1:c, :]
    kd = [ks[h] * jnp.exp(g_last[:, SM_A + h:SM_A + h + 1] - gc[h]) for h in heads]
    return u, w, qg, kd, attn, jnp.exp(g_last)


def _conv_shifts(hist, x):
    xe = jnp.concatenate([hist, x], axis=0)
    return [pltpu.roll(xe, d, 0)[8:] for d in range(CONV_W - 1, 0, -1)] + [x]


def _gdn_meta_kernel(x_ref, sm_ref, conv_ref, alog_ref, dtb_ref, s_out):
    xs = _conv_shifts(jnp.zeros((8, GDN_QKV), F32), x_ref[...])
    u, _, _, kd, _, _ = _gdn_chunk_terms(xs, sm_ref[...], conv_ref, alog_ref, dtb_ref, N_META)
    for h in range(GDN_HEADS):
        s_out[h] = _dot_tn(kd[h].astype(BF16), u[h].astype(BF16))


def _gdn_meta(proj, conv_w, alog_v, dtb_v, meta_row):
    full = lambda shape: pl.BlockSpec(shape, lambda i: (0,) * len(shape))
    return pl.pallas_call(
        _gdn_meta_kernel,
        out_shape=jax.ShapeDtypeStruct((GDN_HEADS, GDN_DK, GDN_DV), F32),
        grid=(1,),
        in_specs=[
            pl.BlockSpec((N_META, GDN_QKV), lambda i: (meta_row // N_META, 0)),
            pl.BlockSpec((N_META, LANES), lambda i: (meta_row // N_META, P_SMALL // LANES)),
            full((CONV_W, GDN_QKV)), full((1, LANES)), full((1, LANES)),
        ],
        out_specs=full((GDN_HEADS, GDN_DK, GDN_DV)),
        compiler_params=pltpu.CompilerParams(vmem_limit_bytes=VMEM_LIMIT),
        name="gdn_meta",
    )(proj, proj, conv_w, alog_v, dtb_v)


def _gdn_terms_kernel(x_ref, hist_ref, sm_ref, conv_ref, alog_ref, dtb_ref,
                      u_out, w_out, qg_out, kd_out, attn_out, dec_out, *, cps):
    c = CHUNK
    hist = hist_ref[...]
    for cc in range(cps):
        rows = slice(cc * c, (cc + 1) * c)
        x = x_ref[rows, :]
        u, w, qg, kd, attn, dec = _gdn_chunk_terms(_conv_shifts(hist, x), sm_ref[rows, :], conv_ref, alog_ref,
                                                   dtb_ref, c)
        for h in range(GDN_HEADS):
            sl = slice(h * GDN_DV, (h + 1) * GDN_DV)
            u_out[rows, sl] = u[h]
            w_out[rows, sl] = w[h].astype(BF16)
            qg_out[rows, sl] = qg[h].astype(BF16)
            kd_out[rows, sl] = kd[h].astype(BF16)
            attn_out[h, rows, :] = attn[h].astype(BF16)
        dec_out[cc] = dec
        hist = x[c - 8:]


def _gdn_terms(proj, conv_w, alog_v, dtb_v, batch, seq, meta_row):
    nc = seq // CHUNK
    n = batch * seq
    cps = _pick(nc, (2, 1))
    step = cps * CHUNK
    spb = nc // cps
    full = lambda shape: pl.BlockSpec(shape, lambda i: (0,) * len(shape))
    rows = lambda: pl.BlockSpec((step, D_MODEL), lambda i: (i, 0))

    def hist_index(i):
        return (jnp.where(i % spb == 0, (meta_row + N_META) // 8, i * (step // 8)) - 1, 0)

    return pl.pallas_call(
        functools.partial(_gdn_terms_kernel, cps=cps),
        out_shape=(jax.ShapeDtypeStruct((n, D_MODEL), F32),
                   jax.ShapeDtypeStruct((n, D_MODEL), BF16),
                   jax.ShapeDtypeStruct((n, D_MODEL), BF16),
                   jax.ShapeDtypeStruct((n, D_MODEL), BF16),
                   jax.ShapeDtypeStruct((GDN_HEADS, n, CHUNK), BF16),
                   jax.ShapeDtypeStruct((batch * nc, 1, LANES), F32)),
        grid=(batch * spb,),
        in_specs=[
            pl.BlockSpec((step, GDN_QKV), lambda i: (i, 0)),
            pl.BlockSpec((8, GDN_QKV), hist_index),
            pl.BlockSpec((step, LANES), lambda i: (i, P_SMALL // LANES)),
            full((CONV_W, GDN_QKV)), full((1, LANES)), full((1, LANES)),
        ],
        out_specs=(rows(), rows(), rows(), rows(),
                   pl.BlockSpec((GDN_HEADS, step, CHUNK), lambda i: (0, i, 0)),
                   pl.BlockSpec((cps, 1, LANES), lambda i: (i, 0, 0))),
        compiler_params=pltpu.CompilerParams(dimension_semantics=("parallel",), vmem_limit_bytes=VMEM_LIMIT),
        name="gdn_terms",
    )(proj, proj, proj, conv_w, alog_v, dtb_v)


def _gdn_scan_kernel(u_ref, w_ref, qg_ref, kd_ref, attn_ref, dec_ref, s0_ref, o_ref, s_out, st_ref, *, cpg):
    c = CHUNK
    heads = range(GDN_HEADS)

    @pl.when(pl.program_id(1) == 0)
    def _():
        st_ref[...] = s0_ref[...]

    def chunk(ci, carry):
        rows = pl.ds(pl.multiple_of(ci * c, c), c)
        dec = dec_ref[ci]
        sl = [slice(h * GDN_DV, (h + 1) * GDN_DV) for h in heads]
        s_old = [st_ref[h] for h in heads]
        sb = [s.astype(BF16) for s in s_old]
        lhs = [jnp.concatenate([w_ref[rows, sl[h]], qg_ref[rows, sl[h]]], axis=0) for h in heads]
        r = [_dot(lhs[h], sb[h]) for h in heads]
        vnb = [(u_ref[rows, sl[h]] - r[h][:c]).astype(BF16) for h in heads]
        out = [r[h][c:] + _dot(attn_ref[h, rows, :], vnb[h]) for h in heads]
        upd = [_dot_tn(kd_ref[rows, sl[h]], vnb[h]) for h in heads]
        for h in heads:
            o_ref[rows, sl[h]] = out[h]
            st_ref[h] = s_old[h] * dec[:, SM_A + h:SM_A + h + 1] + upd[h]
        return carry

    lax.fori_loop(0, cpg, chunk, 0)

    @pl.when(pl.program_id(1) == pl.num_programs(1) - 1)
    def _():
        s_out[0] = st_ref[...]


def _gdn_scan(u, w, qg, kd, attn, dec, s_meta, batch, seq):
    nc = seq // CHUNK
    cpg = _pick(nc, (8, 4, 2, 1))
    ng = nc // cpg
    rows = lambda: pl.BlockSpec((cpg * CHUNK, D_MODEL), lambda b, g: (b * ng + g, 0))
    kern = functools.partial(_gdn_scan_kernel, cpg=cpg)
    return pl.pallas_call(
        kern,
        out_shape=(jax.ShapeDtypeStruct((batch * seq, D_MODEL), F32),
                   jax.ShapeDtypeStruct((batch, GDN_HEADS, GDN_DK, GDN_DV), F32)),
        grid=(batch, ng),
        in_specs=[rows(), rows(), rows(), rows(),
                  pl.BlockSpec((GDN_HEADS, cpg * CHUNK, CHUNK), lambda b, g: (0, b * ng + g, 0)),
                  pl.BlockSpec((cpg, 1, LANES), lambda b, g: (b * ng + g, 0, 0)),
                  pl.BlockSpec((GDN_HEADS, GDN_DK, GDN_DV), lambda b, g: (0, 0, 0))],
        out_specs=(rows(),
                   pl.BlockSpec((1, GDN_HEADS, GDN_DK, GDN_DV), lambda b, g: (b, 0, 0, 0))),
        scratch_shapes=[pltpu.VMEM((GDN_HEADS, GDN_DK, GDN_DV), F32)],
        compiler_params=pltpu.CompilerParams(
            dimension_semantics=("parallel", "arbitrary"), vmem_limit_bytes=VMEM_LIMIT),
        name="gdn_scan",
    )(u, w, qg, kd, attn, dec, s_meta)


def _gdn_sample_kernel(x_ref, sm_ref, cs_ref, st_ref, conv_ref, alog_ref, dtb_ref, o_ref, s_out, *, nb):
    xc = x_ref[...] * conv_ref[CONV_W - 1:CONV_W, :]
    for j in range(CONV_W - 1):
        xc = xc + cs_ref[:, j, :] * conv_ref[j:j + 1, :]
    qs, ks, vs = _qkv_heads(xc)
    g, beta = _gate_lanes(sm_ref[...], alog_ref, dtb_ref)
    eg = jnp.exp(g)
    for h in range(GDN_HEADS):
        q_t = qs[h].T
        k_t = ks[h].T
        for b in range(nb):
            kcol = k_t[:, b:b + 1]
            s1 = st_ref[b, h] * eg[b:b + 1, SM_A + h:SM_A + h + 1]
            r = jnp.sum(s1 * kcol, axis=0, keepdims=True)
            delta = (vs[h][b:b + 1, :] - r) * beta[b:b + 1, SM_B + h:SM_B + h + 1]
            s2 = s1 + kcol * delta
            s_out[b, h] = s2
            o_ref[b:b + 1, h * GDN_DV:(h + 1) * GDN_DV] = jnp.sum(s2 * q_t[:, b:b + 1], axis=0, keepdims=True)


def _gdn_sample(proj, state_conv, state_gdn, conv_w, alog_v, dtb_v, row0):
    ns = state_gdn.shape[0]
    nb = 8
    full = lambda shape: pl.BlockSpec(shape, lambda i: (0,) * len(shape))
    kern = functools.partial(_gdn_sample_kernel, nb=nb)
    return pl.pallas_call(
        kern,
        out_shape=(jax.ShapeDtypeStruct((ns, D_MODEL), F32),
                   jax.ShapeDtypeStruct(state_gdn.shape, F32)),
        grid=(ns // nb,),
        in_specs=[
            pl.BlockSpec((nb, GDN_QKV), lambda i: (row0 // nb + i, 0)),
            pl.BlockSpec((nb, LANES), lambda i: (row0 // nb + i, P_SMALL // LANES)),
            pl.BlockSpec((nb, CONV_W - 1, GDN_QKV), lambda i: (i, 0, 0)),
            pl.BlockSpec((nb, GDN_HEADS, GDN_DK, GDN_DV), lambda i: (i, 0, 0, 0)),
            full((CONV_W, GDN_QKV)), full((1, LANES)), full((1, LANES)),
        ],
        out_specs=(pl.BlockSpec((nb, D_MODEL), lambda i: (i, 0)),
                   pl.BlockSpec((nb, GDN_HEADS, GDN_DK, GDN_DV), lambda i: (i, 0, 0, 0))),
        compiler_params=pltpu.CompilerParams(dimension_semantics=("parallel",), vmem_limit_bytes=VMEM_LIMIT),
        name="gdn_sample",
    )(proj, proj, state_conv, state_gdn, conv_w, alog_v, dtb_v)


def _q_absorb_kernel(q_ref, wukt_ref, ql_out, qr_out):
    for h in range(MLA_HEADS):
        q = q_ref[h]
        ql_out[h] = _dot(q, wukt_ref[h]).astype(BF16)
        qr_out[h] = q[:, NOPE_DIM:NOPE_DIM + ROPE_DIM]


def _q_absorb(q, wukt, ns, row0):
    return pl.pallas_call(
        _q_absorb_kernel,
        out_shape=(jax.ShapeDtypeStruct((MLA_HEADS, ns, KV_LORA), BF16),
                   jax.ShapeDtypeStruct((MLA_HEADS, ns, ROPE_DIM), BF16)),
        grid=(1,),
        in_specs=[pl.BlockSpec((MLA_HEADS, ns, LANES), lambda i: (0, row0 // ns, 0)),
                  pl.BlockSpec((MLA_HEADS, LANES, KV_LORA), lambda i: (0, 0, 0))],
        out_specs=(pl.BlockSpec((MLA_HEADS, ns, KV_LORA), lambda i: (0, 0, 0)),
                   pl.BlockSpec((MLA_HEADS, ns, ROPE_DIM), lambda i: (0, 0, 0))),
        compiler_params=pltpu.CompilerParams(vmem_limit_bytes=VMEM_LIMIT),
        name="q_absorb",
    )(q, wukt)


def _mla_sample_kernel(pt_ref, ql_ref, qr_ref, cn_ref, krn_ref, cc_hbm, cr_hbm, o_ref, cbuf, rbuf, sem,
                       *, n_pages, nsub):
    b = pl.program_id(0)
    slot = b % 2

    def page_copies(seq, slot_, i):
        page = pt_ref[seq * n_pages + i]
        return (pltpu.make_async_copy(cc_hbm.at[page], cbuf.at[slot_, i], sem.at[slot_, 0]),
                pltpu.make_async_copy(cr_hbm.at[page], rbuf.at[slot_, :, pl.ds(i * PAGE_SIZE, PAGE_SIZE)],
                                      sem.at[slot_, 1]))

    def start_pages(seq, slot_):
        for i in range(n_pages):
            cc, cr = page_copies(seq, slot_, i)
            cc.start()
            cr.start()

    @pl.when(b == 0)
    def _():
        start_pages(0, 0)

    @pl.when(b + 1 < pl.num_programs(0))
    def _():
        start_pages(b + 1, 1 - slot)

    for i in range(n_pages):
        cc, cr = page_copies(b, slot, i)
        cc.wait()
        cr.wait()

    ql = ql_ref[0]
    qr = qr_ref[0]
    pps = n_pages // nsub
    subs = range(nsub)
    c = [cbuf[slot, i * pps:(i + 1) * pps].reshape(pps * PAGE_SIZE, KV_LORA).astype(BF16) for i in subs]
    s = [_dot_nt(ql, c[i]) + _dot(qr, rbuf[slot, :, i * pps * PAGE_SIZE:(i + 1) * pps * PAGE_SIZE].astype(BF16))
         for i in subs]
    ms = [jnp.max(s[i], axis=1, keepdims=True) for i in subs]
    p = [jnp.exp(s[i] - ms[i]) for i in subs]
    ls = [jnp.sum(p[i], axis=1, keepdims=True) for i in subs]
    accs = [_dot(p[i].astype(BF16), c[i]) for i in subs]

    cn = cn_ref[0]
    krn = krn_ref[0][:, SM_KR:SM_KR + ROPE_DIM]
    ms.append(jnp.sum(ql.astype(F32) * cn, axis=1, keepdims=True)
              + jnp.sum(qr.astype(F32) * krn, axis=1, keepdims=True))
    ls.append(jnp.ones_like(ms[-1]))
    accs.append(jnp.broadcast_to(cn, (MLA_HEADS, KV_LORA)))
    m = functools.reduce(jnp.maximum, ms)
    scale = [jnp.exp(m_i - m) for m_i in ms]
    l = sum(a * l_i for a, l_i in zip(scale, ls))
    acc = sum(a * acc_i for a, acc_i in zip(scale, accs))
    o_ref[0] = acc / l


def _mla_sample(page_table, ql, qr, c_new, kr_new, cache_c, cache_r):
    ns, n_pages = page_table.shape
    kern = functools.partial(_mla_sample_kernel, n_pages=n_pages, nsub=_pick(n_pages, (8, 4, 2, 1)))
    grid_spec = pltpu.PrefetchScalarGridSpec(
        num_scalar_prefetch=1,
        grid=(ns,),
        in_specs=[
            pl.BlockSpec((1, MLA_HEADS, KV_LORA), lambda b, pt: (b, 0, 0)),
            pl.BlockSpec((1, MLA_HEADS, ROPE_DIM), lambda b, pt: (b, 0, 0)),
            pl.BlockSpec((1, 1, KV_LORA), lambda b, pt: (b, 0, 0)),
            pl.BlockSpec((1, 1, LANES), lambda b, pt: (b, 0, 0)),
            pl.BlockSpec(memory_space=pl.ANY),
            pl.BlockSpec(memory_space=pl.ANY),
        ],
        out_specs=pl.BlockSpec((1, MLA_HEADS, KV_LORA), lambda b, pt: (b, 0, 0)),
        scratch_shapes=[pltpu.VMEM((2, n_pages, PAGE_SIZE, KV_LORA), F32),
                        pltpu.VMEM((2, ROPE_DIM, n_pages * PAGE_SIZE), F32),
                        pltpu.SemaphoreType.DMA((2, 2))],
    )
    return pl.pallas_call(
        kern,
        out_shape=jax.ShapeDtypeStruct((ns, MLA_HEADS, KV_LORA), F32),
        grid_spec=grid_spec,
        compiler_params=pltpu.CompilerParams(dimension_semantics=("arbitrary",), vmem_limit_bytes=VMEM_LIMIT),
        name="mla_sample",
    )(page_table.reshape(-1), ql, qr, c_new, kr_new, cache_c, cache_r)


def _o_proj_sample_kernel(ol_ref, wv_ref, o_ref):
    for p in range(MLA_HEADS // 2):
        acc = None
        for h in (2 * p, 2 * p + 1):
            part = _dot(ol_ref[h].astype(BF16), wv_ref[:, h * LANES:(h + 1) * LANES])
            acc = part if acc is None else acc + part
        o_ref[:, p * LANES:(p + 1) * LANES] = acc.astype(o_ref.dtype)


def _o_proj_sample(o_lat_t, wv):
    ns = o_lat_t.shape[1]
    return pl.pallas_call(
        _o_proj_sample_kernel,
        out_shape=jax.ShapeDtypeStruct((ns, D_MODEL), BF16),
        grid=(1,),
        in_specs=[pl.BlockSpec((MLA_HEADS, ns, KV_LORA), lambda i: (0, 0, 0)),
                  pl.BlockSpec((KV_LORA, MLA_HEADS * LANES), lambda i: (0, 0))],
        out_specs=pl.BlockSpec((ns, D_MODEL), lambda i: (0, 0)),
        compiler_params=pltpu.CompilerParams(vmem_limit_bytes=VMEM_LIMIT),
        name="o_proj_sample",
    )(o_lat_t, wv)


def _outproj_kernel(x_ref, om_ref, og_ref, z_ref, gm_ref, gg_ref, wo_ref, gn_ref, nf_ref, wr_ref, br_ref,
                    xmid_out, hf_out, route_out):
    og = og_ref[...]
    parts = []
    for h in range(GDN_HEADS):
        oh = og[:, h * GDN_DV:(h + 1) * GDN_DV]
        parts.append(oh * lax.rsqrt(jnp.mean(oh * oh, axis=-1, keepdims=True) + RMS_EPS))
    o_gdn = jnp.concatenate(parts, axis=1) * gn_ref[...] * _silu(z_ref[...])
    merged = _sigmoid(gm_ref[...]) * om_ref[...].astype(F32) + _sigmoid(gg_ref[...]) * o_gdn
    x_mid = x_ref[...] + _dot(merged.astype(BF16), wo_ref[...])
    xmid_out[...] = x_mid
    hf = _rms(x_mid, nf_ref[...]).astype(BF16)
    bits = lax.bitcast_convert_type(hf.astype(F32), jnp.uint32)
    half = D_MODEL // 2
    hf_out[...] = bits[:, half:] | (bits[:, :half] >> 16)

    logits = _dot(hf, wr_ref[...]) + br_ref[...]
    lane = lax.broadcasted_iota(jnp.int32, logits.shape, 1)
    neg = -jnp.inf
    big = 4 * LANES
    is_g = lane < N_GROUPS
    lg = jnp.where(is_g, logits, neg)
    mg = jnp.max(lg, axis=1, keepdims=True)
    grp = jnp.min(jnp.where(lg == mg, lane, big), axis=1, keepdims=True)
    gate_g = 1.0 / jnp.sum(jnp.where(is_g, jnp.exp(logits - mg), 0.0), axis=1, keepdims=True)
    e_lane = lane - N_GROUPS
    in_grp = (e_lane >= 0) & (e_lane < N_EXPERTS) & ((e_lane >> 3) == grp)
    le = jnp.where(in_grp, logits, neg)
    v1 = jnp.max(le, axis=1, keepdims=True)
    i1 = jnp.min(jnp.where(le == v1, lane, big), axis=1, keepdims=True)
    le2 = jnp.where(lane == i1, neg, le)
    v2 = jnp.max(le2, axis=1, keepdims=True)
    i2 = jnp.min(jnp.where(le2 == v2, lane, big), axis=1, keepdims=True)
    e = jnp.exp(v2 - v1)
    w1 = gate_g / (1.0 + e)
    w2 = gate_g * e / (1.0 + e)
    route = jnp.where(lane == 0, (i1 - N_GROUPS).astype(F32),
                      jnp.where(lane == 1, (i2 - N_GROUPS).astype(F32),
                                jnp.where(lane == 2, w1, jnp.where(lane == 3, w2, 0.0))))
    route_out[...] = route


def _row_tile(n, limit):
    t = limit - limit % 16
    while t >= 16:
        if n % t == 0:
            return t
        t -= 16
    raise ValueError(f"no row tile for {n}")


def _outproj(x_all, o_mla, o_gdn, proj, w_out, gn_t, norm_ffn, w_r, b_r, row0):
    n = o_mla.shape[0]
    tm = _pick(n, (512, 256, 128, 64, 32, 16))
    assert row0 % tm == 0
    r0 = row0 // tm
    full = lambda shape: pl.BlockSpec(shape, lambda i: (0,) * len(shape))
    row = lambda w: pl.BlockSpec((tm, w), lambda i: (i, 0))
    shared = lambda w, j=0: pl.BlockSpec((tm, w), lambda i, j=j: (r0 + i, j))
    return pl.pallas_call(
        _outproj_kernel,
        out_shape=(jax.ShapeDtypeStruct((n, D_MODEL), F32),
                   jax.ShapeDtypeStruct((n, D_MODEL // 2), jnp.uint32),
                   jax.ShapeDtypeStruct((n, LANES), F32)),
        grid=(n // tm,),
        in_specs=[shared(D_MODEL), row(D_MODEL), row(D_MODEL),
                  shared(D_MODEL, P_Z // D_MODEL), shared(D_MODEL, P_GM // D_MODEL), shared(D_MODEL, P_GG // D_MODEL),
                  full((D_MODEL, D_MODEL)), full((1, D_MODEL)), full((1, D_MODEL)),
                  full((D_MODEL, LANES)), full((1, LANES))],
        out_specs=(row(D_MODEL), row(D_MODEL // 2), row(LANES)),
        compiler_params=pltpu.CompilerParams(dimension_semantics=("parallel",), vmem_limit_bytes=VMEM_LIMIT),
        name="outproj_route",
    )(x_all, o_mla, o_gdn, proj, proj, proj, w_out, gn_t, norm_ffn, w_r, b_r)


def _moe_plan(eid, wgt, nsp):
    nt = eid.shape[0]
    ts = nt // nsp
    n_asg = ts * TOP_K
    nb = (n_asg + N_EXPERTS * (MOE_BLOCK - 1) + MOE_BLOCK - 1) // MOE_BLOCK
    e = eid.reshape(nsp, n_asg)
    ids = jnp.broadcast_to(jnp.arange(n_asg, dtype=jnp.int32), e.shape)
    _, ids_s, w_s = lax.sort((e, ids, wgt.reshape(nsp, n_asg)), dimension=1, num_keys=1, is_stable=True)
    rows_s = (ids_s % TOP_K) * (ts + 8) + ids_s // TOP_K
    counts = jnp.sum((e[..., None] == jnp.arange(N_EXPERTS, dtype=jnp.int32)).astype(jnp.int32), axis=1)
    run_start = jnp.cumsum(counts, axis=1) - counts
    padded = (counts + MOE_BLOCK - 1) // MOE_BLOCK * MOE_BLOCK
    pad_end = jnp.cumsum(padded, axis=1)
    starts = jnp.arange(nb, dtype=jnp.int32) * MOE_BLOCK
    blk_e = jnp.minimum(jnp.sum(pad_end[:, None, :] <= starts[None, :, None], axis=2), N_EXPERTS - 1)
    of_blk = lambda a: jnp.take_along_axis(a, blk_e, axis=1)
    rank0 = starts[None, :] - of_blk(pad_end - padded)
    blk_p0 = of_blk(run_start) + rank0
    nused = pad_end[:, -1] // MOE_BLOCK
    tail = lambda v, dt: jnp.full((nsp, MOE_BLOCK), v, dt)
    tok_t = jnp.concatenate([ids_s // TOP_K, tail(ts, jnp.int32)], axis=1)
    rows_t = jnp.concatenate([rows_s, tail(ts, jnp.int32)], axis=1)
    w_t = jnp.concatenate([w_s, tail(0.0, F32)], axis=1)
    as_i32 = lambda a: a.astype(jnp.int32).reshape(-1)
    scalars = (as_i32(blk_e), as_i32(nused), as_i32(blk_p0), as_i32(tok_t), as_i32(rows_t), w_t.reshape(-1))
    return scalars, ts, nb


def _split_pieces(k, ts, n_prompt, n_sample):
    lo, hi = k * ts, (k + 1) * ts
    pieces = []
    if lo < n_prompt:
        pieces.append((0, lo, 0, min(hi, n_prompt) - lo))
    if hi > n_prompt:
        start = max(lo, n_prompt)
        pieces.append((1, start - n_prompt, start - lo, hi - start))
    assert hi <= n_prompt + n_sample
    return pieces


def _moe_kernel(blk_ref, nused_ref, p0_ref, tok_ref, rows_ref, w_ref, hf_p, hf_q, xmid_p, xmid_q,
                wg_ref, wu_ref, wd_ref, nfin_ref, y_p, y_q, hf_s, comb, acc, xb, yb, sem,
                *, ts, nb, rc, nsp, n_prompt, n_sample):
    s = pl.program_id(0)
    j = pl.program_id(1)
    half = D_MODEL // 2
    stride = ts + 8

    def zero_spare_rows():
        hf_s[ts:ts + 8, :] = jnp.zeros((8, half), jnp.uint32)

    def split_copies(k, load):
        cps = []
        for src, r0, l0, n in _split_pieces(k, ts, n_prompt, n_sample):
            if load:
                cps.append(pltpu.make_async_copy((hf_p, hf_q)[src].at[pl.ds(r0, n)], hf_s.at[pl.ds(l0, n)],
                                                 sem.at[0, src]))
                cps.append(pltpu.make_async_copy((xmid_p, xmid_q)[src].at[pl.ds(r0, n)], acc.at[pl.ds(l0, n)],
                                                 sem.at[1, src]))
            else:
                cps.append(pltpu.make_async_copy(acc.at[pl.ds(l0, n)], (y_p, y_q)[src].at[pl.ds(r0, n)],
                                                 sem.at[2, src]))
        return cps

    def run_copies(load, between=None):
        for k in range(nsp):
            @pl.when(s == k)
            def _(k=k):
                cps = split_copies(k, load)
                for cp in cps:
                    cp.start()
                if between is not None:
                    between()
                for cp in cps:
                    cp.wait()

    @pl.when(j == 0)
    def _():
        run_copies(True, zero_spare_rows)

    @pl.when(j < nused_ref[s])
    def _():
        p0 = s * (ts * TOP_K + MOE_BLOCK) + p0_ref[s * nb + j]
        for r in range(MOE_BLOCK):
            xb[r:r + 1, :] = hf_s[pl.ds(tok_ref[p0 + r], 1), :]
        bits = xb[...]
        lo = lax.bitcast_convert_type(bits << 16, F32).astype(BF16)
        hi = lax.bitcast_convert_type(bits & jnp.uint32(0xFFFF0000), F32).astype(BF16)
        g = _dot(lo, wg_ref[0, :half, :]) + _dot(hi, wg_ref[0, half:, :])
        u = _dot(lo, wu_ref[0, :half, :]) + _dot(hi, wu_ref[0, half:, :])
        yb[...] = _dot((_silu(g) * u).astype(BF16), wd_ref[0])
        for r in range(MOE_BLOCK):
            comb[pl.ds(rows_ref[p0 + r], 1), :] = w_ref[p0 + r] * yb[r:r + 1, :]

    @pl.when(j == nb - 1)
    def _():
        def body(i, carry):
            r0 = pl.multiple_of(i * rc, 8)
            rows = pl.ds(r0, rc)
            moe = comb[rows, :]
            for k in range(1, TOP_K):
                moe = moe + comb[pl.ds(k * stride + r0, rc), :]
            acc[rows, :] = _rms(acc[rows, :] + moe, nfin_ref[...])
            return carry

        lax.fori_loop(0, ts // rc, body, 0)
        run_copies(False)


def _moe(plan, hf_p, hf_q, xmid_p, xmid_q, wg, wu, wd, norm_final):
    scalars, ts, nb = plan
    n_prompt, n_sample = xmid_p.shape[0], xmid_q.shape[0]
    nsp = (n_prompt + n_sample) // ts
    rc = _row_tile(ts, 256) if ts % 16 == 0 else 8
    kern = functools.partial(_moe_kernel, ts=ts, nb=nb, rc=rc, nsp=nsp, n_prompt=n_prompt, n_sample=n_sample)
    hbm = pl.BlockSpec(memory_space=pl.ANY)
    expert = lambda shape: pl.BlockSpec((1,) + shape, lambda s, j, be, *_: (be[s * nb + j], 0, 0))
    grid_spec = pltpu.PrefetchScalarGridSpec(
        num_scalar_prefetch=len(scalars),
        grid=(nsp, nb),
        in_specs=[
            hbm, hbm, hbm, hbm,
            expert((D_MODEL, D_EXPERT)), expert((D_MODEL, D_EXPERT)), expert((D_EXPERT, D_MODEL)),
            pl.BlockSpec((1, D_MODEL), lambda s, j, *_: (0, 0)),
        ],
        out_specs=(hbm, hbm),
        scratch_shapes=[
            pltpu.VMEM((ts + 8, D_MODEL // 2), jnp.uint32),
            pltpu.VMEM((TOP_K * (ts + 8), D_MODEL), F32),
            pltpu.VMEM((ts, D_MODEL), F32),
            pltpu.VMEM((MOE_BLOCK, D_MODEL // 2), jnp.uint32),
            pltpu.VMEM((MOE_BLOCK, D_MODEL), F32),
            pltpu.SemaphoreType.DMA((3, 2)),
        ],
    )
    return pl.pallas_call(
        kern,
        out_shape=(jax.ShapeDtypeStruct((n_prompt, D_MODEL), F32), jax.ShapeDtypeStruct((n_sample, D_MODEL), F32)),
        grid_spec=grid_spec,
        compiler_params=pltpu.CompilerParams(
            dimension_semantics=("arbitrary", "arbitrary"), vmem_limit_bytes=VMEM_LIMIT),
        name="moe",
    )(*scalars, hf_p, hf_q, xmid_p, xmid_q, wg, wu, wd, norm_final)


def _pack_w_in(w):
    kr = w[:, _OFF_KV + KV_LORA:_OFF_QKV]
    kr_sw = jnp.concatenate([kr[:, ROPE_DIM // 2:], kr[:, :ROPE_DIM // 2]], axis=1)
    small = jnp.concatenate([w[:, _OFF_B:_OFF_A], w[:, _OFF_A:_OFF_GM],
                             jnp.zeros((D_MODEL, SM_KR - 2 * GDN_HEADS), w.dtype), kr, kr_sw], axis=1)
    packed = jnp.concatenate([w[:, _OFF_QKV:_OFF_Z], w[:, _OFF_Z:_OFF_B], w[:, _OFF_GM:_OFF_GG], w[:, _OFF_GG:],
                              w[:, _OFF_KV:_OFF_KV + KV_LORA], small, w[:, :Q_LORA]], axis=1)
    return packed.astype(BF16)


def _pack_mla_weights(w_uq, w_uk, w_uv):
    zq = jnp.zeros((Q_LORA, MLA_HEADS, LANES - NOPE_DIM - ROPE_DIM), w_uq.dtype)
    wq = jnp.concatenate([w_uq, zq], axis=2).reshape(Q_LORA, MLA_HEADS * LANES)
    rope = w_uq[:, :, NOPE_DIM:]
    rope_sw = jnp.concatenate([rope[..., ROPE_DIM // 2:], rope[..., :ROPE_DIM // 2]], axis=2)
    wqs = jnp.concatenate([jnp.zeros((Q_LORA, MLA_HEADS, NOPE_DIM), w_uq.dtype), rope_sw, zq], axis=2)
    wqs = wqs.reshape(Q_LORA, MLA_HEADS * LANES)
    wk = jnp.concatenate([w_uk, jnp.zeros((KV_LORA, MLA_HEADS, LANES - NOPE_DIM), w_uk.dtype)], axis=2)
    wk = wk.reshape(KV_LORA, MLA_HEADS * LANES)
    zv = jnp.zeros((KV_LORA, MLA_HEADS // 2, V_DIM), w_uv.dtype)
    wv = jnp.stack([jnp.concatenate([w_uv[:, 0::2], zv], axis=2),
                    jnp.concatenate([zv, w_uv[:, 1::2]], axis=2)], axis=2)
    wv = wv.reshape(KV_LORA, MLA_HEADS * LANES)
    return wq.astype(BF16), wqs.astype(BF16), wk.astype(BF16), wv.astype(BF16)


def _rope_tables(pos):
    inv_freq = ROPE_THETA ** (-jnp.arange(0, ROPE_DIM, 2, dtype=F32) / ROPE_DIM)
    ang = pos.astype(F32)[:, None] * inv_freq[None, :]
    cos, sin = jnp.cos(ang), jnp.sin(ang)
    n = pos.shape[0]
    cos_t = jnp.concatenate([jnp.ones((n, NOPE_DIM), F32), cos, cos, jnp.zeros((n, ROPE_DIM), F32)], axis=1)
    sin_t = jnp.concatenate([jnp.zeros((n, NOPE_DIM), F32), -sin, sin, jnp.zeros((n, ROPE_DIM), F32)], axis=1)
    return cos_t, sin_t


def _head_lanes(v):
    return jnp.zeros((1, LANES), F32).at[0, SM_A:SM_A + GDN_HEADS].set(v.astype(F32))


def _moe_splits(nt):
    for nsp in (6, 3, 4, 2, 1):
        if nt % (nsp * 8) == 0:
            return nsp
    return 1


def kernel(x_prompt, x_sample, cache_kv_latent, cache_k_rope, page_table, state_conv, state_gdn, meta_tokens,
           norm_mix, w_in, q_norm, w_uq, kv_norm, w_uk, w_uv, conv_w, a_log, dt_bias, gdn_norm, w_out, norm_ffn,
           w_group, b_group, w_router, b_router, w_gate, w_up, w_down, norm_final):
    batch, seq, _ = x_prompt.shape
    ns, dec_seq, _ = x_sample.shape
    assert dec_seq == 1 and w_in.shape[0] == 1 and seq % CHUNK == 0
    n_pages = page_table.shape[1]
    n_prompt = batch * seq
    nt = n_prompt + ns
    meta_row = nt
    n_rows = -(-(nt + N_META) // LANES) * LANES
    assert n_prompt % ns == 0 and nt % N_META == 0 and ns % 8 == 0

    x_all = jnp.concatenate([x_prompt.reshape(n_prompt, D_MODEL), x_sample.reshape(ns, D_MODEL),
                             meta_tokens.astype(x_prompt.dtype),
                             jnp.zeros((n_rows - nt - N_META, D_MODEL), x_prompt.dtype)], axis=0)
    pos = jnp.concatenate([jnp.tile(N_META + jnp.arange(seq), batch), jnp.full((ns,), n_pages * PAGE_SIZE),
                           jnp.arange(N_META), jnp.zeros((n_rows - nt - N_META,), jnp.int32)])
    cos_t, sin_t = _rope_tables(pos)
    w_packed = _pack_w_in(w_in[0])
    wq, wqs, wk, wv = _pack_mla_weights(w_uq[0], w_uk[0], w_uv[0])
    wukt = jnp.concatenate([jnp.transpose(w_uk[0], (1, 2, 0)),
                            jnp.zeros((MLA_HEADS, LANES - NOPE_DIM, KV_LORA), w_uk.dtype)], axis=1).astype(BF16)
    alog_v, dtb_v = _head_lanes(a_log[0]), _head_lanes(dt_bias[0])
    cw = conv_w[0].astype(F32)
    gn_t = jnp.tile(gdn_norm[0].astype(F32), GDN_HEADS)[None]
    w_r = jnp.concatenate([w_group[0], w_router[0],
                           jnp.zeros((D_MODEL, LANES - N_GROUPS - N_EXPERTS), w_group.dtype)], axis=1).astype(BF16)
    b_r = jnp.concatenate([b_group[0], b_router[0], jnp.zeros((LANES - N_GROUPS - N_EXPERTS,), b_group.dtype)])[None]
    wg, wu, wd = w_gate[0].astype(BF16), w_up[0].astype(BF16), w_down[0].astype(BF16)

    proj = _inproj(x_all, norm_mix[0][None].astype(F32), w_packed)
    q, k, v, ckv, krot = _mla_prep(proj, cos_t, sin_t, q_norm[0][None].astype(F32), kv_norm[0][None].astype(F32),
                                   wq, wqs, wk, wv)

    o_mla_p = _attn_prompt(q, k, v, batch, seq, meta_row)
    ql, qr = _q_absorb(q, wukt, ns, n_prompt)
    o_lat = _mla_sample(page_table, jnp.transpose(ql, (1, 0, 2)), jnp.transpose(qr, (1, 0, 2)),
                        ckv[n_prompt:nt].reshape(ns, 1, KV_LORA), krot[n_prompt:nt].reshape(ns, 1, LANES),
                        cache_kv_latent[0], jnp.swapaxes(cache_k_rope[0], 1, 2))
    o_mla_s = _o_proj_sample(jnp.transpose(o_lat, (1, 0, 2)), wv)

    s_meta = _gdn_meta(proj, cw, alog_v, dtb_v, meta_row)
    terms = _gdn_terms(proj, cw, alog_v, dtb_v, batch, seq, meta_row)
    o_gdn_p, gdn_p = _gdn_scan(*terms, s_meta, batch, seq)
    o_gdn_s, gdn_s = _gdn_sample(proj, state_conv[0], state_gdn[0], cw, alog_v, dtb_v, n_prompt)

    tail = (proj, w_out[0].astype(BF16), gn_t, norm_ffn[0][None].astype(F32), w_r, b_r.astype(F32))
    xmid_p, hf_p, route_p = _outproj(x_all, o_mla_p, o_gdn_p, *tail, 0)
    xmid_s, hf_s, route_s = _outproj(x_all, o_mla_s, o_gdn_s, *tail, n_prompt)
    route = jnp.concatenate([route_p[:, :2 * TOP_K], route_s[:, :2 * TOP_K]], axis=0)
    plan = _moe_plan(route[:, :TOP_K].astype(jnp.int32), route[:, TOP_K:], _moe_splits(nt))
    y_p, y_s = _moe(plan, hf_p, hf_s, xmid_p, xmid_s, wg, wu, wd, norm_final[None].astype(F32))

    def with_meta(rows, width):
        meta = jnp.broadcast_to(rows[meta_row:meta_row + N_META][None], (batch, N_META, width))
        return jnp.concatenate([meta, rows[:n_prompt].reshape(batch, seq, width)], axis=1)[None]

    k_rope = krot[:, SM_KR:SM_KR + ROPE_DIM]
    conv_p = jnp.stack([proj[(b + 1) * seq - (CONV_W - 1):(b + 1) * seq, :GDN_QKV] for b in range(batch)])
    conv_s = jnp.concatenate([state_conv[0][:, 1:].astype(F32), proj[n_prompt:nt, None, :GDN_QKV]], axis=1)
    return (y_p.reshape(batch, seq, D_MODEL), y_s.reshape(ns, 1, D_MODEL),
            with_meta(ckv, KV_LORA), with_meta(k_rope, ROPE_DIM),
            ckv[n_prompt:nt].reshape(1, ns, 1, KV_LORA), k_rope[n_prompt:nt].reshape(1, ns, 1, ROPE_DIM),
            conv_p[None], conv_s[None], gdn_p[None], gdn_s[None])
```

```python
import functools

import jax
import jax.numpy as jnp
from jax import lax
from jax.experimental import pallas as pl
from jax.experimental.pallas import tpu as pltpu

F32 = jnp.float32
BF16 = jnp.bfloat16
HIGHEST = lax.Precision.HIGHEST

D_MODEL = 1024
N_META = 16
RMS_EPS = 1e-6
MLA_HEADS = 16
Q_LORA = 384
KV_LORA = 256
NOPE_DIM = 64
ROPE_DIM = 32
V_DIM = 64
ROPE_THETA = 10000.0
MLA_SCALE = (NOPE_DIM + ROPE_DIM) ** -0.5
PAGE_SIZE = 128
GDN_HEADS = 8
GDN_DK = 128
GDN_DV = 128
GDN_KEY = GDN_HEADS * GDN_DK
GDN_QKV = 3 * GDN_KEY
CONV_W = 4
CHUNK = 64
N_GROUPS = 4
EXPERTS_PER_GROUP = 8
N_EXPERTS = 32
TOP_K = 2
D_EXPERT = 256
MOE_BLOCK = 128

_OFF_KV = Q_LORA
_OFF_QKV = _OFF_KV + KV_LORA + ROPE_DIM
_OFF_Z = _OFF_QKV + GDN_QKV
_OFF_B = _OFF_Z + GDN_KEY
_OFF_A = _OFF_B + GDN_HEADS
_OFF_GM = _OFF_A + GDN_HEADS
_OFF_GG = _OFF_GM + D_MODEL
P_QKV = 0
P_Z = 3072
P_GM = 4096
P_GG = 5120
P_KVC = 6144
P_SMALL = 6400
P_QD = 6528
P_TOTAL = 6912
SM_B = 0
SM_A = 8
SM_KR = 64

LANES = 128
VMEM_LIMIT = 56 * 1024 * 1024
ATTN_TQ = (512, 256, 128)


def _pick(n, candidates):
    for c in candidates:
        if n % c == 0:
            return c
    raise ValueError(f"no tile for {n} in {candidates}")


def _dot(a, b):
    return jnp.dot(a, b, preferred_element_type=F32)


def _dot_nt(a, b):
    return lax.dot_general(a, b, (((1,), (1,)), ((), ())), preferred_element_type=F32)


def _dot_tn(a, b):
    return lax.dot_general(a, b, (((0,), (0,)), ((), ())), preferred_element_type=F32)


def _sigmoid(x):
    return 1.0 / (1.0 + jnp.exp(-x))


def _silu(x):
    return x * _sigmoid(x)


def _softplus(x):
    return jnp.maximum(x, 0.0) + jnp.log1p(jnp.exp(-jnp.abs(x)))


def _rms(x, w):
    return x * lax.rsqrt(jnp.mean(x * x, axis=-1, keepdims=True) + RMS_EPS) * w


def _inproj_kernel(x_ref, nw_ref, w_ref, o_ref, hn_ref):
    @pl.when(pl.program_id(1) == 0)
    def _():
        hn_ref[...] = _rms(x_ref[...], nw_ref[...]).astype(BF16)

    o_ref[...] = _dot(hn_ref[...], w_ref[...])


def _inproj(x_all, norm_w, w_packed):
    r = x_all.shape[0]
    tm = _pick(r, (1536, 1280, 768, 640, 512, 256, 128))
    tn = 768
    return pl.pallas_call(
        _inproj_kernel,
        out_shape=jax.ShapeDtypeStruct((r, P_TOTAL), F32),
        grid=(r // tm, P_TOTAL // tn),
        in_specs=[
            pl.BlockSpec((tm, D_MODEL), lambda i, j: (i, 0)),
            pl.BlockSpec((1, D_MODEL), lambda i, j: (0, 0)),
            pl.BlockSpec((D_MODEL, tn), lambda i, j: (0, j)),
        ],
        out_specs=pl.BlockSpec((tm, tn), lambda i, j: (i, j)),
        scratch_shapes=[pltpu.VMEM((tm, D_MODEL), BF16)],
        compiler_params=pltpu.CompilerParams(
            dimension_semantics=("parallel", "arbitrary"), vmem_limit_bytes=VMEM_LIMIT),
        name="inproj",
    )(x_all, norm_w, w_packed)


LOG2E = 1.4426950408889634


def _mla_prep_kernel(qd_ref, kvc_ref, sm_ref, c_ref, s_ref, qn_ref, kvn_ref, wqt_ref, wqst_ref, wk_ref, wvt_ref,
                     qt_out, k_out, vt_out, ckv_out, kr_out):
    cos = c_ref[...]
    sin = s_ref[...]
    cos_t, sin_t = cos.T, sin.T
    qn_t = _rms(qd_ref[...], qn_ref[...]).T.astype(BF16)
    qt = _dot(wqt_ref[...], qn_t)
    qst = _dot(wqst_ref[...], qn_t)
    for h in range(MLA_HEADS):
        sl = slice(h * LANES, (h + 1) * LANES)
        qt_out[h] = ((qt[sl] * cos_t + qst[sl] * sin_t) * (MLA_SCALE * LOG2E)).astype(BF16)
    ckv = _rms(kvc_ref[...], kvn_ref[...])
    ckv_out[...] = ckv
    sm = sm_ref[...]
    lane = lax.broadcasted_iota(jnp.int32, sm.shape, 1)
    cos_k = jnp.where((lane >= SM_KR) & (lane < SM_KR + ROPE_DIM), cos, 0.0)
    krot = sm * cos_k + pltpu.roll(sm, LANES - ROPE_DIM, 1) * sin
    kr_out[...] = krot
    kk = _dot(ckv.astype(BF16), wk_ref[...])
    vvt = _dot(wvt_ref[...], ckv.T.astype(BF16))
    for h in range(MLA_HEADS):
        sl = slice(h * LANES, (h + 1) * LANES)
        k_out[h] = (kk[:, sl] + krot).astype(BF16)
        vt_out[h, 0] = vvt[sl].astype(BF16)


def _mla_prep(proj, cos_t, sin_t, q_norm, kv_norm, wqt, wqst, wk, wvt, tm):
    r = proj.shape[0]
    hw = MLA_HEADS * LANES
    full = lambda shape: pl.BlockSpec(shape, lambda i: (0,) * len(shape))
    return pl.pallas_call(
        _mla_prep_kernel,
        out_shape=(
            jax.ShapeDtypeStruct((MLA_HEADS, LANES, r), BF16),
            jax.ShapeDtypeStruct((MLA_HEADS, r, LANES), BF16),
            jax.ShapeDtypeStruct((MLA_HEADS, r // tm, LANES, tm), BF16),
            jax.ShapeDtypeStruct((r, KV_LORA), F32),
            jax.ShapeDtypeStruct((r, LANES), F32),
        ),
        grid=(r // tm,),
        in_specs=[
            pl.BlockSpec((tm, Q_LORA), lambda i: (i, P_QD // Q_LORA)),
            pl.BlockSpec((tm, KV_LORA), lambda i: (i, P_KVC // KV_LORA)),
            pl.BlockSpec((tm, LANES), lambda i: (i, P_SMALL // LANES)),
            pl.BlockSpec((tm, LANES), lambda i: (i, 0)),
            pl.BlockSpec((tm, LANES), lambda i: (i, 0)),
            full((1, Q_LORA)), full((1, KV_LORA)),
            full((hw, Q_LORA)), full((hw, Q_LORA)), full((KV_LORA, hw)), full((hw, KV_LORA)),
        ],
        out_specs=(pl.BlockSpec((MLA_HEADS, LANES, tm), lambda i: (0, 0, i)),
                   pl.BlockSpec((MLA_HEADS, tm, LANES), lambda i: (0, i, 0)),
                   pl.BlockSpec((MLA_HEADS, 1, LANES, tm), lambda i: (0, i, 0, 0)),
                   pl.BlockSpec((tm, KV_LORA), lambda i: (i, 0)),
                   pl.BlockSpec((tm, LANES), lambda i: (i, 0))),
        compiler_params=pltpu.CompilerParams(dimension_semantics=("parallel",), vmem_limit_bytes=VMEM_LIMIT),
        name="mla_prep",
    )(proj, proj, proj, cos_t, sin_t, q_norm, kv_norm, wqt, wqst, wk, wvt)


def _attn_prompt_kernel(qt_ref, k_ref, vt_ref, km_ref, vmt_ref, o_ref, *, tq, meta_lane):
    qi = pl.program_id(2)
    hs = (0, 1)
    half = tq // 2
    qt = [qt_ref[h] for h in hs]

    def update(qts, k_blk, vt_blk, carry, mask):
        ms, ls, accs = carry
        s = [_dot(k_blk[h], qts[h]) for h in hs]
        if mask is not None:
            s = [jnp.where(mask, x, -1e30) for x in s]
        m_new = [jnp.maximum(ms[h], jnp.max(s[h], axis=0, keepdims=True)) for h in hs]
        a = [jnp.exp2(ms[h] - m_new[h]) for h in hs]
        p = [jnp.exp2(s[h] - m_new[h]) for h in hs]
        l_new = [a[h] * ls[h] + jnp.sum(p[h], axis=0, keepdims=True) for h in hs]
        acc_new = [a[h] * accs[h] + _dot(vt_blk[h], p[h].astype(BF16)) for h in hs]
        return m_new, l_new, acc_new

    s0 = [_dot(km_ref[h], qt[h]) for h in hs]
    m = [jnp.max(s0[h], axis=0, keepdims=True) for h in hs]
    p0 = [jnp.exp2(s0[h] - m[h]) for h in hs]
    l = [jnp.sum(p0[h], axis=0, keepdims=True) for h in hs]
    acc = [_dot(vmt_ref[h, 0][:, meta_lane:meta_lane + N_META], p0[h].astype(BF16)) for h in hs]

    def body(j, carry):
        rows = pl.ds(pl.multiple_of(j * tq, tq), tq)
        return update(qt, [k_ref[h, rows, :] for h in hs], [vt_ref[h, j] for h in hs], carry, None)

    m, l, acc = lax.fori_loop(0, qi, body, (m, l, acc))

    off = pl.multiple_of(qi * tq, tq)
    key = lax.broadcasted_iota(jnp.int32, (half, tq), 0)
    qry = lax.broadcasted_iota(jnp.int32, (half, tq), 1)
    vt_d = [vt_ref[h, qi] for h in hs]
    m, l, acc = update(qt, [k_ref[h, pl.ds(off, half), :] for h in hs], [v[:, :half] for v in vt_d],
                       (m, l, acc), key <= qry)
    late = lambda xs: [x[:, half:] for x in xs]
    mb, lb, accb = update(late(qt), [k_ref[h, pl.ds(pl.multiple_of(off + half, half), half), :] for h in hs],
                          late(vt_d), (late(m), late(l), late(acc)), (key <= qry)[:, :half])
    ot = [jnp.concatenate([acc[h][:, :half] / l[h][:, :half], accb[h] / lb[h]], axis=1) for h in hs]
    o_ref[...] = (ot[0] + ot[1]).T.astype(o_ref.dtype)


def _attn_prompt(qt, k, vt, batch, seq, meta_row, tq):
    nq = seq // tq
    kern = functools.partial(_attn_prompt_kernel, tq=tq, meta_lane=meta_row % tq)
    return pl.pallas_call(
        kern,
        out_shape=jax.ShapeDtypeStruct((batch * seq, D_MODEL), BF16),
        grid=(batch, MLA_HEADS // 2, nq),
        in_specs=[
            pl.BlockSpec((2, LANES, tq), lambda b, p, i: (p, 0, b * nq + i)),
            pl.BlockSpec((2, seq, LANES), lambda b, p, i: (p, b, 0)),
            pl.BlockSpec((2, nq, LANES, tq), lambda b, p, i: (p, b, 0, 0)),
            pl.BlockSpec((2, N_META, LANES), lambda b, p, i: (p, meta_row // N_META, 0)),
            pl.BlockSpec((2, 1, LANES, tq), lambda b, p, i: (p, meta_row // tq, 0, 0)),
        ],
        out_specs=pl.BlockSpec((tq, LANES), lambda b, p, i: (b * nq + i, p)),
        compiler_params=pltpu.CompilerParams(
            dimension_semantics=("parallel", "parallel", "arbitrary"), vmem_limit_bytes=VMEM_LIMIT),
        name="attn_prompt",
    )(qt, k, vt, k, vt)


def _gate_lanes(sm, alog_ref, dtb_ref):
    g = -jnp.exp(alog_ref[...]) * _softplus(sm + dtb_ref[...])
    beta = _sigmoid(sm)
    return g, beta


def _qkv_heads(xc):
    xf = _silu(xc)
    qs, ks, vs = [], [], []
    for h in range(GDN_HEADS):
        q = xf[:, h * GDN_DK:(h + 1) * GDN_DK]
        k = xf[:, GDN_KEY + h * GDN_DK:GDN_KEY + (h + 1) * GDN_DK]
        qs.append(q * lax.rsqrt(jnp.sum(q * q, axis=-1, keepdims=True) + RMS_EPS) * (GDN_DK ** -0.5))
        ks.append(k * lax.rsqrt(jnp.sum(k * k, axis=-1, keepdims=True) + RMS_EPS))
        vs.append(xf[:, 2 * GDN_KEY + h * GDN_DV:2 * GDN_KEY + (h + 1) * GDN_DV])
    return qs, ks, vs


def _split_bf16(x):
    hi = x.astype(BF16)
    return hi, (x - hi.astype(F32)).astype(BF16)


def _dot_split(a, b):
    return _dot(a[0], b[0]) + (_dot(a[0], b[1]) + _dot(a[1], b[0]))


def _unit_lower_inverses(mats, c):
    row = lax.broadcasted_iota(jnp.int32, (c, c), 0)
    col = lax.broadcasted_iota(jnp.int32, (c, c), 1)
    eye = jnp.where(row == col, 1.0, 0.0)
    ps = [-a for a in mats]
    ts = [eye + p for p in ps]
    span = 2
    while span < c:
        psp = [_split_bf16(p) for p in ps]
        ps = [_dot_split(p, p) for p in psp]
        psp = [_split_bf16(p) for p in ps]
        ts = [t + _dot_split(p, _split_bf16(t)) for p, t in zip(psp, ts)]
        span *= 2
    return ts


def _gdn_chunk_terms(xs, sm, conv_ref, alog_ref, dtb_ref, c):
    heads = range(GDN_HEADS)
    xc = xs[0] * conv_ref[0:1, :]
    for j in range(1, CONV_W):
        xc = xc + xs[j] * conv_ref[j:j + 1, :]
    qs, ks, vs = _qkv_heads(xc)
    g, beta = _gate_lanes(sm, alog_ref, dtb_ref)
    row = lax.broadcasted_iota(jnp.int32, (c, c), 0)
    col = lax.broadcasted_iota(jnp.int32, (c, c), 1)
    causal = col <= row
    strict = col < row
    gcum = jnp.dot(jnp.where(causal, 1.0, 0.0), g, precision=HIGHEST, preferred_element_type=F32)
    gcum_t = lax.dot_general(g, jnp.where(col >= row, 1.0, 0.0), (((0,), (0,)), ((), ())),
                             precision=HIGHEST, preferred_element_type=F32)
    gc = [gcum[:, SM_A + h:SM_A + h + 1] for h in heads]
    gr = [gcum_t[SM_A + h:SM_A + h + 1, :] for h in heads]
    bc = [beta[:, SM_B + h:SM_B + h + 1] for h in heads]
    decay = [jnp.where(causal, jnp.exp(jnp.where(causal, gc[h] - gr[h], 0.0)), 0.0) for h in heads]
    kb = [ks[h] * bc[h] for h in heads]
    kbf = [ks[h].astype(BF16) for h in heads]
    a = [jnp.where(strict, _dot_nt(kb[h].astype(BF16), kbf[h]) * decay[h], 0.0) for h in heads]
    t = _unit_lower_inverses(a, c)
    eg = [jnp.exp(gc[h]) for h in heads]
    sol = [_dot(t[h].astype(BF16), jnp.concatenate([vs[h] * bc[h], kb[h] * eg[h]], axis=1).astype(BF16))
           for h in heads]
    u = [sol[h][:, :GDN_DV] for h in heads]
    w = [sol[h][:, GDN_DV:] for h in heads]
    attn = [jnp.where(causal, _dot_nt(qs[h].astype(BF16), kbf[h]) * decay[h], 0.0) for h in heads]
    qg = [qs[h] * eg[h] for h in heads]
    g_last = gcum[c - 1:c, :]
    kd = [ks[h] * jnp.exp(g_last[:, SM_A + h:SM_A + h + 1] - gc[h]) for h in heads]
    return u, w, qg, kd, attn, jnp.exp(g_last)


def _conv_shifts(hist, x):
    xe = jnp.concatenate([hist, x], axis=0)
    return [pltpu.roll(xe, d, 0)[8:] for d in range(CONV_W - 1, 0, -1)] + [x]


def _gdn_meta_kernel(x_ref, sm_ref, conv_ref, alog_ref, dtb_ref, s_out):
    xs = _conv_shifts(jnp.zeros((8, GDN_QKV), F32), x_ref[...])
    u, _, _, kd, _, _ = _gdn_chunk_terms(xs, sm_ref[...], conv_ref, alog_ref, dtb_ref, N_META)
    for h in range(GDN_HEADS):
        s_out[h] = _dot_tn(kd[h].astype(BF16), u[h].astype(BF16))


def _gdn_meta(proj, conv_w, alog_v, dtb_v, meta_row):
    full = lambda shape: pl.BlockSpec(shape, lambda i: (0,) * len(shape))
    return pl.pallas_call(
        _gdn_meta_kernel,
        out_shape=jax.ShapeDtypeStruct((GDN_HEADS, GDN_DK, GDN_DV), F32),
        grid=(1,),
        in_specs=[
            pl.BlockSpec((N_META, GDN_QKV), lambda i: (meta_row // N_META, 0)),
            pl.BlockSpec((N_META, LANES), lambda i: (meta_row // N_META, P_SMALL // LANES)),
            full((CONV_W, GDN_QKV)), full((1, LANES)), full((1, LANES)),
        ],
        out_specs=full((GDN_HEADS, GDN_DK, GDN_DV)),
        compiler_params=pltpu.CompilerParams(vmem_limit_bytes=VMEM_LIMIT),
        name="gdn_meta",
    )(proj, proj, conv_w, alog_v, dtb_v)


def _gdn_terms_kernel(x_ref, hist_ref, sm_ref, conv_ref, alog_ref, dtb_ref,
                      u_out, w_out, qg_out, kd_out, attn_out, dec_out, *, cps):
    c = CHUNK
    hist = hist_ref[...]
    for cc in range(cps):
        rows = slice(cc * c, (cc + 1) * c)
        x = x_ref[rows, :]
        u, w, qg, kd, attn, dec = _gdn_chunk_terms(_conv_shifts(hist, x), sm_ref[rows, :], conv_ref, alog_ref,
                                                   dtb_ref, c)
        for h in range(GDN_HEADS):
            sl = slice(h * GDN_DV, (h + 1) * GDN_DV)
            u_out[rows, sl] = u[h]
            w_out[rows, sl] = w[h].astype(BF16)
            qg_out[rows, sl] = qg[h].astype(BF16)
            kd_out[rows, sl] = kd[h].astype(BF16)
            attn_out[h, rows, :] = attn[h].astype(BF16)
        dec_out[cc] = dec
        hist = x[c - 8:]


def _gdn_terms(proj, conv_w, alog_v, dtb_v, batch, seq, meta_row):
    nc = seq // CHUNK
    n = batch * seq
    cps = _pick(nc, (2, 1))
    step = cps * CHUNK
    spb = nc // cps
    full = lambda shape: pl.BlockSpec(shape, lambda i: (0,) * len(shape))
    rows = lambda: pl.BlockSpec((step, D_MODEL), lambda i: (i, 0))

    def hist_index(i):
        return (jnp.where(i % spb == 0, (meta_row + N_META) // 8, i * (step // 8)) - 1, 0)

    return pl.pallas_call(
        functools.partial(_gdn_terms_kernel, cps=cps),
        out_shape=(jax.ShapeDtypeStruct((n, D_MODEL), F32),
                   jax.ShapeDtypeStruct((n, D_MODEL), BF16),
                   jax.ShapeDtypeStruct((n, D_MODEL), BF16),
                   jax.ShapeDtypeStruct((n, D_MODEL), BF16),
                   jax.ShapeDtypeStruct((GDN_HEADS, n, CHUNK), BF16),
                   jax.ShapeDtypeStruct((batch * nc, 1, LANES), F32)),
        grid=(batch * spb,),
        in_specs=[
            pl.BlockSpec((step, GDN_QKV), lambda i: (i, 0)),
            pl.BlockSpec((8, GDN_QKV), hist_index),
            pl.BlockSpec((step, LANES), lambda i: (i, P_SMALL // LANES)),
            full((CONV_W, GDN_QKV)), full((1, LANES)), full((1, LANES)),
        ],
        out_specs=(rows(), rows(), rows(), rows(),
                   pl.BlockSpec((GDN_HEADS, step, CHUNK), lambda i: (0, i, 0)),
                   pl.BlockSpec((cps, 1, LANES), lambda i: (i, 0, 0))),
        compiler_params=pltpu.CompilerParams(dimension_semantics=("parallel",), vmem_limit_bytes=VMEM_LIMIT),
        name="gdn_terms",
    )(proj, proj, proj, conv_w, alog_v, dtb_v)


def _gdn_scan_kernel(u_ref, w_ref, qg_ref, kd_ref, attn_ref, dec_ref, s0_ref, o_ref, s_out, st_ref, *, cpg):
    c = CHUNK
    heads = range(GDN_HEADS)

    @pl.when(pl.program_id(1) == 0)
    def _():
        st_ref[...] = s0_ref[...]

    def chunk(ci, carry):
        rows = pl.ds(pl.multiple_of(ci * c, c), c)
        dec = dec_ref[ci]
        sl = [slice(h * GDN_DV, (h + 1) * GDN_DV) for h in heads]
        s_old = [st_ref[h] for h in heads]
        sb = [s.astype(BF16) for s in s_old]
        lhs = [jnp.concatenate([w_ref[rows, sl[h]], qg_ref[rows, sl[h]]], axis=0) for h in heads]
        r = [_dot(lhs[h], sb[h]) for h in heads]
        vnb = [(u_ref[rows, sl[h]] - r[h][:c]).astype(BF16) for h in heads]
        out = [r[h][c:] + _dot(attn_ref[h, rows, :], vnb[h]) for h in heads]
        upd = [_dot_tn(kd_ref[rows, sl[h]], vnb[h]) for h in heads]
        for h in heads:
            o_ref[rows, sl[h]] = out[h]
            st_ref[h] = s_old[h] * dec[:, SM_A + h:SM_A + h + 1] + upd[h]
        return carry

    lax.fori_loop(0, cpg, chunk, 0)

    @pl.when(pl.program_id(1) == pl.num_programs(1) - 1)
    def _():
        s_out[0] = st_ref[...]


def _gdn_scan(u, w, qg, kd, attn, dec, s_meta, batch, seq):
    nc = seq // CHUNK
    cpg = _pick(nc, (8, 4, 2, 1))
    ng = nc // cpg
    rows = lambda: pl.BlockSpec((cpg * CHUNK, D_MODEL), lambda b, g: (b * ng + g, 0))
    kern = functools.partial(_gdn_scan_kernel, cpg=cpg)
    return pl.pallas_call(
        kern,
        out_shape=(jax.ShapeDtypeStruct((batch * seq, D_MODEL), F32),
                   jax.ShapeDtypeStruct((batch, GDN_HEADS, GDN_DK, GDN_DV), F32)),
        grid=(batch, ng),
        in_specs=[rows(), rows(), rows(), rows(),
                  pl.BlockSpec((GDN_HEADS, cpg * CHUNK, CHUNK), lambda b, g: (0, b * ng + g, 0)),
                  pl.BlockSpec((cpg, 1, LANES), lambda b, g: (b * ng + g, 0, 0)),
                  pl.BlockSpec((GDN_HEADS, GDN_DK, GDN_DV), lambda b, g: (0, 0, 0))],
        out_specs=(rows(),
                   pl.BlockSpec((1, GDN_HEADS, GDN_DK, GDN_DV), lambda b, g: (b, 0, 0, 0))),
        scratch_shapes=[pltpu.VMEM((GDN_HEADS, GDN_DK, GDN_DV), F32)],
        compiler_params=pltpu.CompilerParams(
            dimension_semantics=("parallel", "arbitrary"), vmem_limit_bytes=VMEM_LIMIT),
        name="gdn_scan",
    )(u, w, qg, kd, attn, dec, s_meta)


def _gdn_sample_kernel(x_ref, sm_ref, cs_ref, st_ref, conv_ref, alog_ref, dtb_ref, o_ref, s_out, *, nb):
    xc = x_ref[...] * conv_ref[CONV_W - 1:CONV_W, :]
    for j in range(CONV_W - 1):
        xc = xc + cs_ref[:, j, :] * conv_ref[j:j + 1, :]
    qs, ks, vs = _qkv_heads(xc)
    g, beta = _gate_lanes(sm_ref[...], alog_ref, dtb_ref)
    eg = jnp.exp(g)
    for h in range(GDN_HEADS):
        q_t = qs[h].T
        k_t = ks[h].T
        for b in range(nb):
            kcol = k_t[:, b:b + 1]
            s1 = st_ref[b, h] * eg[b:b + 1, SM_A + h:SM_A + h + 1]
            r = jnp.sum(s1 * kcol, axis=0, keepdims=True)
            delta = (vs[h][b:b + 1, :] - r) * beta[b:b + 1, SM_B + h:SM_B + h + 1]
            s2 = s1 + kcol * delta
            s_out[b, h] = s2
            o_ref[b:b + 1, h * GDN_DV:(h + 1) * GDN_DV] = jnp.sum(s2 * q_t[:, b:b + 1], axis=0, keepdims=True)


def _gdn_sample(proj, state_conv, state_gdn, conv_w, alog_v, dtb_v, row0):
    ns = state_gdn.shape[0]
    nb = 8
    full = lambda shape: pl.BlockSpec(shape, lambda i: (0,) * len(shape))
    kern = functools.partial(_gdn_sample_kernel, nb=nb)
    return pl.pallas_call(
        kern,
        out_shape=(jax.ShapeDtypeStruct((ns, D_MODEL), F32),
                   jax.ShapeDtypeStruct(state_gdn.shape, F32)),
        grid=(ns // nb,),
        in_specs=[
            pl.BlockSpec((nb, GDN_QKV), lambda i: (row0 // nb + i, 0)),
            pl.BlockSpec((nb, LANES), lambda i: (row0 // nb + i, P_SMALL // LANES)),
            pl.BlockSpec((nb, CONV_W - 1, GDN_QKV), lambda i: (i, 0, 0)),
            pl.BlockSpec((nb, GDN_HEADS, GDN_DK, GDN_DV), lambda i: (i, 0, 0, 0)),
            full((CONV_W, GDN_QKV)), full((1, LANES)), full((1, LANES)),
        ],
        out_specs=(pl.BlockSpec((nb, D_MODEL), lambda i: (i, 0)),
                   pl.BlockSpec((nb, GDN_HEADS, GDN_DK, GDN_DV), lambda i: (i, 0, 0, 0))),
        compiler_params=pltpu.CompilerParams(dimension_semantics=("parallel",), vmem_limit_bytes=VMEM_LIMIT),
        name="gdn_sample",
    )(proj, proj, state_conv, state_gdn, conv_w, alog_v, dtb_v)


def _q_absorb_kernel(qt_ref, wukt_ref, ql_out, qrt_out):
    for h in range(MLA_HEADS):
        qt = qt_ref[h]
        ql_out[h] = _dot_tn(qt, wukt_ref[h]).astype(BF16)
        qrt_out[h] = qt[NOPE_DIM:NOPE_DIM + ROPE_DIM]


def _q_absorb(qt, wukt, ns, row0):
    return pl.pallas_call(
        _q_absorb_kernel,
        out_shape=(jax.ShapeDtypeStruct((MLA_HEADS, ns, KV_LORA), BF16),
                   jax.ShapeDtypeStruct((MLA_HEADS, ROPE_DIM, ns), BF16)),
        grid=(1,),
        in_specs=[pl.BlockSpec((MLA_HEADS, LANES, ns), lambda i: (0, 0, row0 // ns)),
                  pl.BlockSpec((MLA_HEADS, LANES, KV_LORA), lambda i: (0, 0, 0))],
        out_specs=(pl.BlockSpec((MLA_HEADS, ns, KV_LORA), lambda i: (0, 0, 0)),
                   pl.BlockSpec((MLA_HEADS, ROPE_DIM, ns), lambda i: (0, 0, 0))),
        compiler_params=pltpu.CompilerParams(vmem_limit_bytes=VMEM_LIMIT),
        name="q_absorb",
    )(qt, wukt)


def _mla_sample_kernel(pt_ref, ql_ref, qr_ref, cn_ref, krn_ref, cc_hbm, cr_hbm, o_ref, cbuf, rbuf, sem,
                       *, n_pages, nsub):
    b = pl.program_id(0)
    slot = b % 2

    def page_copies(seq, slot_, i):
        page = pt_ref[seq * n_pages + i]
        return (pltpu.make_async_copy(cc_hbm.at[page], cbuf.at[slot_, i], sem.at[slot_, 0]),
                pltpu.make_async_copy(cr_hbm.at[page], rbuf.at[slot_, :, pl.ds(i * PAGE_SIZE, PAGE_SIZE)],
                                      sem.at[slot_, 1]))

    def start_pages(seq, slot_):
        for i in range(n_pages):
            cc, cr = page_copies(seq, slot_, i)
            cc.start()
            cr.start()

    @pl.when(b == 0)
    def _():
        start_pages(0, 0)

    @pl.when(b + 1 < pl.num_programs(0))
    def _():
        start_pages(b + 1, 1 - slot)

    for i in range(n_pages):
        cc, cr = page_copies(b, slot, i)
        cc.wait()
        cr.wait()

    ql = ql_ref[0]
    qr = qr_ref[0]
    pps = n_pages // nsub
    subs = range(nsub)
    c = [cbuf[slot, i * pps:(i + 1) * pps].reshape(pps * PAGE_SIZE, KV_LORA).astype(BF16) for i in subs]
    s = [_dot_nt(ql, c[i]) + _dot(qr, rbuf[slot, :, i * pps * PAGE_SIZE:(i + 1) * pps * PAGE_SIZE].astype(BF16))
         for i in subs]
    ms = [jnp.max(s[i], axis=1, keepdims=True) for i in subs]
    p = [jnp.exp2(s[i] - ms[i]) for i in subs]
    ls = [jnp.sum(p[i], axis=1, keepdims=True) for i in subs]
    accs = [_dot(p[i].astype(BF16), c[i]) for i in subs]

    cn = cn_ref[0]
    krn = krn_ref[0][:, SM_KR:SM_KR + ROPE_DIM]
    ms.append(jnp.sum(ql.astype(F32) * cn, axis=1, keepdims=True)
              + jnp.sum(qr.astype(F32) * krn, axis=1, keepdims=True))
    ls.append(jnp.ones_like(ms[-1]))
    accs.append(jnp.broadcast_to(cn, (MLA_HEADS, KV_LORA)))
    m = functools.reduce(jnp.maximum, ms)
    scale = [jnp.exp2(m_i - m) for m_i in ms]
    l = sum(a * l_i for a, l_i in zip(scale, ls))
    acc = sum(a * acc_i for a, acc_i in zip(scale, accs))
    o_ref[0] = acc / l


def _mla_sample(page_table, ql, qr, c_new, kr_new, cache_c, cache_r):
    ns, n_pages = page_table.shape
    kern = functools.partial(_mla_sample_kernel, n_pages=n_pages, nsub=_pick(n_pages, (8, 4, 2, 1)))
    grid_spec = pltpu.PrefetchScalarGridSpec(
        num_scalar_prefetch=1,
        grid=(ns,),
        in_specs=[
            pl.BlockSpec((1, MLA_HEADS, KV_LORA), lambda b, pt: (b, 0, 0)),
            pl.BlockSpec((1, MLA_HEADS, ROPE_DIM), lambda b, pt: (b, 0, 0)),
            pl.BlockSpec((1, 1, KV_LORA), lambda b, pt: (b, 0, 0)),
            pl.BlockSpec((1, 1, LANES), lambda b, pt: (b, 0, 0)),
            pl.BlockSpec(memory_space=pl.ANY),
            pl.BlockSpec(memory_space=pl.ANY),
        ],
        out_specs=pl.BlockSpec((1, MLA_HEADS, KV_LORA), lambda b, pt: (b, 0, 0)),
        scratch_shapes=[pltpu.VMEM((2, n_pages, PAGE_SIZE, KV_LORA), F32),
                        pltpu.VMEM((2, ROPE_DIM, n_pages * PAGE_SIZE), F32),
                        pltpu.SemaphoreType.DMA((2, 2))],
    )
    return pl.pallas_call(
        kern,
        out_shape=jax.ShapeDtypeStruct((ns, MLA_HEADS, KV_LORA), F32),
        grid_spec=grid_spec,
        compiler_params=pltpu.CompilerParams(dimension_semantics=("arbitrary",), vmem_limit_bytes=VMEM_LIMIT),
        name="mla_sample",
    )(page_table.reshape(-1), ql, qr, c_new, kr_new, cache_c, cache_r)


def _o_proj_sample_kernel(ol_ref, wv_ref, o_ref):
    for p in range(MLA_HEADS // 2):
        acc = None
        for h in (2 * p, 2 * p + 1):
            part = _dot(ol_ref[h].astype(BF16), wv_ref[:, h * LANES:(h + 1) * LANES])
            acc = part if acc is None else acc + part
        o_ref[:, p * LANES:(p + 1) * LANES] = acc.astype(o_ref.dtype)


def _o_proj_sample(o_lat_t, wv):
    ns = o_lat_t.shape[1]
    return pl.pallas_call(
        _o_proj_sample_kernel,
        out_shape=jax.ShapeDtypeStruct((ns, D_MODEL), BF16),
        grid=(1,),
        in_specs=[pl.BlockSpec((MLA_HEADS, ns, KV_LORA), lambda i: (0, 0, 0)),
                  pl.BlockSpec((KV_LORA, MLA_HEADS * LANES), lambda i: (0, 0))],
        out_specs=pl.BlockSpec((ns, D_MODEL), lambda i: (0, 0)),
        compiler_params=pltpu.CompilerParams(vmem_limit_bytes=VMEM_LIMIT),
        name="o_proj_sample",
    )(o_lat_t, wv)


def _outproj_kernel(x_ref, om_ref, og_ref, z_ref, gm_ref, gg_ref, wo_ref, gn_ref, nf_ref, wr_ref, br_ref,
                    xmid_out, hf_out, route_out):
    og = og_ref[...]
    parts = []
    for h in range(GDN_HEADS):
        oh = og[:, h * GDN_DV:(h + 1) * GDN_DV]
        parts.append(oh * lax.rsqrt(jnp.mean(oh * oh, axis=-1, keepdims=True) + RMS_EPS))
    o_gdn = jnp.concatenate(parts, axis=1) * gn_ref[...] * _silu(z_ref[...])
    merged = _sigmoid(gm_ref[...]) * om_ref[...].astype(F32) + _sigmoid(gg_ref[...]) * o_gdn
    x_mid = x_ref[...] + _dot(merged.astype(BF16), wo_ref[...])
    xmid_out[...] = x_mid
    hf = _rms(x_mid, nf_ref[...]).astype(BF16)
    bits = lax.bitcast_convert_type(hf.astype(F32), jnp.uint32)
    half = D_MODEL // 2
    hf_out[...] = bits[:, half:] | (bits[:, :half] >> 16)

    logits = _dot(hf, wr_ref[...]) + br_ref[...]
    lane = lax.broadcasted_iota(jnp.int32, logits.shape, 1)
    neg = -jnp.inf
    big = 4 * LANES
    is_g = lane < N_GROUPS
    lg = jnp.where(is_g, logits, neg)
    mg = jnp.max(lg, axis=1, keepdims=True)
    grp = jnp.min(jnp.where(lg == mg, lane, big), axis=1, keepdims=True)
    gate_g = 1.0 / jnp.sum(jnp.where(is_g, jnp.exp(logits - mg), 0.0), axis=1, keepdims=True)
    e_lane = lane - N_GROUPS
    in_grp = (e_lane >= 0) & (e_lane < N_EXPERTS) & ((e_lane >> 3) == grp)
    le = jnp.where(in_grp, logits, neg)
    v1 = jnp.max(le, axis=1, keepdims=True)
    i1 = jnp.min(jnp.where(le == v1, lane, big), axis=1, keepdims=True)
    le2 = jnp.where(lane == i1, neg, le)
    v2 = jnp.max(le2, axis=1, keepdims=True)
    i2 = jnp.min(jnp.where(le2 == v2, lane, big), axis=1, keepdims=True)
    e = jnp.exp(v2 - v1)
    w1 = gate_g / (1.0 + e)
    w2 = gate_g * e / (1.0 + e)
    route = jnp.where(lane == 0, (i1 - N_GROUPS).astype(F32),
                      jnp.where(lane == 1, (i2 - N_GROUPS).astype(F32),
                                jnp.where(lane == 2, w1, jnp.where(lane == 3, w2, 0.0))))
    route_out[...] = route


def _row_tile(n, limit):
    t = limit - limit % 16
    while t >= 16:
        if n % t == 0:
            return t
        t -= 16
    raise ValueError(f"no row tile for {n}")


def _outproj(x_all, o_mla, o_gdn, proj, w_out, gn_t, norm_ffn, w_r, b_r, row0):
    n = o_mla.shape[0]
    tm = _pick(n, (512, 256, 128, 64, 32, 16))
    assert row0 % tm == 0
    r0 = row0 // tm
    full = lambda shape: pl.BlockSpec(shape, lambda i: (0,) * len(shape))
    row = lambda w: pl.BlockSpec((tm, w), lambda i: (i, 0))
    shared = lambda w, j=0: pl.BlockSpec((tm, w), lambda i, j=j: (r0 + i, j))
    return pl.pallas_call(
        _outproj_kernel,
        out_shape=(jax.ShapeDtypeStruct((n, D_MODEL), F32),
                   jax.ShapeDtypeStruct((n, D_MODEL // 2), jnp.uint32),
                   jax.ShapeDtypeStruct((n, LANES), F32)),
        grid=(n // tm,),
        in_specs=[shared(D_MODEL), row(D_MODEL), row(D_MODEL),
                  shared(D_MODEL, P_Z // D_MODEL), shared(D_MODEL, P_GM // D_MODEL), shared(D_MODEL, P_GG // D_MODEL),
                  full((D_MODEL, D_MODEL)), full((1, D_MODEL)), full((1, D_MODEL)),
                  full((D_MODEL, LANES)), full((1, LANES))],
        out_specs=(row(D_MODEL), row(D_MODEL // 2), row(LANES)),
        compiler_params=pltpu.CompilerParams(dimension_semantics=("parallel",), vmem_limit_bytes=VMEM_LIMIT),
        name="outproj_route",
    )(x_all, o_mla, o_gdn, proj, proj, proj, w_out, gn_t, norm_ffn, w_r, b_r)


def _moe_plan(eid, wgt, nsp):
    nt = eid.shape[0]
    ts = nt // nsp
    n_asg = ts * TOP_K
    nb = (n_asg + N_EXPERTS * (MOE_BLOCK - 1) + MOE_BLOCK - 1) // MOE_BLOCK
    e = eid.reshape(nsp, n_asg)
    ids = jnp.broadcast_to(jnp.arange(n_asg, dtype=jnp.int32), e.shape)
    _, ids_s, w_s = lax.sort((e, ids, wgt.reshape(nsp, n_asg)), dimension=1, num_keys=1, is_stable=True)
    rows_s = (ids_s % TOP_K) * (ts + 8) + ids_s // TOP_K
    counts = jnp.sum((e[..., None] == jnp.arange(N_EXPERTS, dtype=jnp.int32)).astype(jnp.int32), axis=1)
    run_start = jnp.cumsum(counts, axis=1) - counts
    padded = (counts + MOE_BLOCK - 1) // MOE_BLOCK * MOE_BLOCK
    pad_end = jnp.cumsum(padded, axis=1)
    starts = jnp.arange(nb, dtype=jnp.int32) * MOE_BLOCK
    blk_e = jnp.minimum(jnp.sum(pad_end[:, None, :] <= starts[None, :, None], axis=2), N_EXPERTS - 1)
    of_blk = lambda a: jnp.take_along_axis(a, blk_e, axis=1)
    rank0 = starts[None, :] - of_blk(pad_end - padded)
    blk_p0 = of_blk(run_start) + rank0
    nused = pad_end[:, -1] // MOE_BLOCK
    tail = lambda v, dt: jnp.full((nsp, MOE_BLOCK), v, dt)
    tok_t = jnp.concatenate([ids_s // TOP_K, tail(ts, jnp.int32)], axis=1)
    rows_t = jnp.concatenate([rows_s, tail(ts, jnp.int32)], axis=1)
    w_t = jnp.concatenate([w_s, tail(0.0, F32)], axis=1)
    as_i32 = lambda a: a.astype(jnp.int32).reshape(-1)
    scalars = (as_i32(blk_e), as_i32(nused), as_i32(blk_p0), as_i32(tok_t), as_i32(rows_t), w_t.reshape(-1))
    return scalars, ts, nb


def _split_pieces(k, ts, n_prompt, n_sample):
    lo, hi = k * ts, (k + 1) * ts
    pieces = []
    if lo < n_prompt:
        pieces.append((0, lo, 0, min(hi, n_prompt) - lo))
    if hi > n_prompt:
        start = max(lo, n_prompt)
        pieces.append((1, start - n_prompt, start - lo, hi - start))
    assert hi <= n_prompt + n_sample
    return pieces


def _moe_kernel(blk_ref, nused_ref, p0_ref, tok_ref, rows_ref, w_ref, hf_p, hf_q, xmid_p, xmid_q,
                wg_ref, wu_ref, wd_ref, nfin_ref, y_p, y_q, hf_s, comb, acc, xb, yb, sem,
                *, ts, nb, rc, nsp, n_prompt, n_sample):
    s = pl.program_id(0)
    j = pl.program_id(1)
    half = D_MODEL // 2
    stride = ts + 8

    def zero_spare_rows():
        hf_s[ts:ts + 8, :] = jnp.zeros((8, half), jnp.uint32)

    def split_copies(k, load):
        cps = []
        for src, r0, l0, n in _split_pieces(k, ts, n_prompt, n_sample):
            if load:
                cps.append(pltpu.make_async_copy((hf_p, hf_q)[src].at[pl.ds(r0, n)], hf_s.at[pl.ds(l0, n)],
                                                 sem.at[0, src]))
                cps.append(pltpu.make_async_copy((xmid_p, xmid_q)[src].at[pl.ds(r0, n)], acc.at[pl.ds(l0, n)],
                                                 sem.at[1, src]))
            else:
                cps.append(pltpu.make_async_copy(acc.at[pl.ds(l0, n)], (y_p, y_q)[src].at[pl.ds(r0, n)],
                                                 sem.at[2, src]))
        return cps

    def run_copies(load, between=None):
        for k in range(nsp):
            @pl.when(s == k)
            def _(k=k):
                cps = split_copies(k, load)
                for cp in cps:
                    cp.start()
                if between is not None:
                    between()
                for cp in cps:
                    cp.wait()

    @pl.when(j == 0)
    def _():
        run_copies(True, zero_spare_rows)

    @pl.when(j < nused_ref[s])
    def _():
        p0 = s * (ts * TOP_K + MOE_BLOCK) + p0_ref[s * nb + j]
        for r in range(MOE_BLOCK):
            xb[r:r + 1, :] = hf_s[pl.ds(tok_ref[p0 + r], 1), :]
        bits = xb[...]
        lo = lax.bitcast_convert_type(bits << 16, F32).astype(BF16)
        hi = lax.bitcast_convert_type(bits & jnp.uint32(0xFFFF0000), F32).astype(BF16)
        g = _dot(lo, wg_ref[0, :half, :]) + _dot(hi, wg_ref[0, half:, :])
        u = _dot(lo, wu_ref[0, :half, :]) + _dot(hi, wu_ref[0, half:, :])
        yb[...] = _dot((_silu(g) * u).astype(BF16), wd_ref[0])
        for r in range(MOE_BLOCK):
            comb[pl.ds(rows_ref[p0 + r], 1), :] = w_ref[p0 + r] * yb[r:r + 1, :]

    @pl.when(j == nb - 1)
    def _():
        def body(i, carry):
            r0 = pl.multiple_of(i * rc, 8)
            rows = pl.ds(r0, rc)
            moe = comb[rows, :]
            for k in range(1, TOP_K):
                moe = moe + comb[pl.ds(k * stride + r0, rc), :]
            acc[rows, :] = _rms(acc[rows, :] + moe, nfin_ref[...])
            return carry

        lax.fori_loop(0, ts // rc, body, 0)
        run_copies(False)


def _moe(plan, hf_p, hf_q, xmid_p, xmid_q, wg, wu, wd, norm_final):
    scalars, ts, nb = plan
    n_prompt, n_sample = xmid_p.shape[0], xmid_q.shape[0]
    nsp = (n_prompt + n_sample) // ts
    rc = _row_tile(ts, 256) if ts % 16 == 0 else 8
    kern = functools.partial(_moe_kernel, ts=ts, nb=nb, rc=rc, nsp=nsp, n_prompt=n_prompt, n_sample=n_sample)
    hbm = pl.BlockSpec(memory_space=pl.ANY)
    expert = lambda shape: pl.BlockSpec((1,) + shape, lambda s, j, be, *_: (be[s * nb + j], 0, 0))
    grid_spec = pltpu.PrefetchScalarGridSpec(
        num_scalar_prefetch=len(scalars),
        grid=(nsp, nb),
        in_specs=[
            hbm, hbm, hbm, hbm,
            expert((D_MODEL, D_EXPERT)), expert((D_MODEL, D_EXPERT)), expert((D_EXPERT, D_MODEL)),
            pl.BlockSpec((1, D_MODEL), lambda s, j, *_: (0, 0)),
        ],
        out_specs=(hbm, hbm),
        scratch_shapes=[
            pltpu.VMEM((ts + 8, D_MODEL // 2), jnp.uint32),
            pltpu.VMEM((TOP_K * (ts + 8), D_MODEL), F32),
            pltpu.VMEM((ts, D_MODEL), F32),
            pltpu.VMEM((MOE_BLOCK, D_MODEL // 2), jnp.uint32),
            pltpu.VMEM((MOE_BLOCK, D_MODEL), F32),
            pltpu.SemaphoreType.DMA((3, 2)),
        ],
    )
    return pl.pallas_call(
        kern,
        out_shape=(jax.ShapeDtypeStruct((n_prompt, D_MODEL), F32), jax.ShapeDtypeStruct((n_sample, D_MODEL), F32)),
        grid_spec=grid_spec,
        compiler_params=pltpu.CompilerParams(
            dimension_semantics=("arbitrary", "arbitrary"), vmem_limit_bytes=VMEM_LIMIT),
        name="moe",
    )(*scalars, hf_p, hf_q, xmid_p, xmid_q, wg, wu, wd, norm_final)


def _pack_w_in(w):
    kr = w[:, _OFF_KV + KV_LORA:_OFF_QKV]
    kr_sw = jnp.concatenate([kr[:, ROPE_DIM // 2:], kr[:, :ROPE_DIM // 2]], axis=1)
    small = jnp.concatenate([w[:, _OFF_B:_OFF_A], w[:, _OFF_A:_OFF_GM],
                             jnp.zeros((D_MODEL, SM_KR - 2 * GDN_HEADS), w.dtype), kr, kr_sw], axis=1)
    packed = jnp.concatenate([w[:, _OFF_QKV:_OFF_Z], w[:, _OFF_Z:_OFF_B], w[:, _OFF_GM:_OFF_GG], w[:, _OFF_GG:],
                              w[:, _OFF_KV:_OFF_KV + KV_LORA], small, w[:, :Q_LORA]], axis=1)
    return packed.astype(BF16)


def _pack_mla_weights(w_uq, w_uk, w_uv):
    zq = jnp.zeros((Q_LORA, MLA_HEADS, LANES - NOPE_DIM - ROPE_DIM), w_uq.dtype)
    wq = jnp.concatenate([w_uq, zq], axis=2).reshape(Q_LORA, MLA_HEADS * LANES)
    rope = w_uq[:, :, NOPE_DIM:]
    rope_sw = jnp.concatenate([rope[..., ROPE_DIM // 2:], rope[..., :ROPE_DIM // 2]], axis=2)
    wqs = jnp.concatenate([jnp.zeros((Q_LORA, MLA_HEADS, NOPE_DIM), w_uq.dtype), rope_sw, zq], axis=2)
    wqs = wqs.reshape(Q_LORA, MLA_HEADS * LANES)
    wk = jnp.concatenate([w_uk, jnp.zeros((KV_LORA, MLA_HEADS, LANES - NOPE_DIM), w_uk.dtype)], axis=2)
    wk = wk.reshape(KV_LORA, MLA_HEADS * LANES)
    zv = jnp.zeros((KV_LORA, MLA_HEADS // 2, V_DIM), w_uv.dtype)
    wv = jnp.stack([jnp.concatenate([w_uv[:, 0::2], zv], axis=2),
                    jnp.concatenate([zv, w_uv[:, 1::2]], axis=2)], axis=2)
    wv = wv.reshape(KV_LORA, MLA_HEADS * LANES)
    return wq.astype(BF16), wqs.astype(BF16), wk.astype(BF16), wv.astype(BF16)


def _rope_tables(pos):
    inv_freq = ROPE_THETA ** (-jnp.arange(0, ROPE_DIM, 2, dtype=F32) / ROPE_DIM)
    ang = pos.astype(F32)[:, None] * inv_freq[None, :]
    cos, sin = jnp.cos(ang), jnp.sin(ang)
    n = pos.shape[0]
    cos_t = jnp.concatenate([jnp.ones((n, NOPE_DIM), F32), cos, cos, jnp.zeros((n, ROPE_DIM), F32)], axis=1)
    sin_t = jnp.concatenate([jnp.zeros((n, NOPE_DIM), F32), -sin, sin, jnp.zeros((n, ROPE_DIM), F32)], axis=1)
    return cos_t, sin_t


def _head_lanes(v):
    return jnp.zeros((1, LANES), F32).at[0, SM_A:SM_A + GDN_HEADS].set(v.astype(F32))


def _moe_splits(nt):
    for nsp in (6, 3, 4, 2, 1):
        if nt % (nsp * 8) == 0:
            return nsp
    return 1


def kernel(x_prompt, x_sample, cache_kv_latent, cache_k_rope, page_table, state_conv, state_gdn, meta_tokens,
           norm_mix, w_in, q_norm, w_uq, kv_norm, w_uk, w_uv, conv_w, a_log, dt_bias, gdn_norm, w_out, norm_ffn,
           w_group, b_group, w_router, b_router, w_gate, w_up, w_down, norm_final):
    batch, seq, _ = x_prompt.shape
    ns, dec_seq, _ = x_sample.shape
    assert dec_seq == 1 and w_in.shape[0] == 1 and seq % CHUNK == 0
    n_pages = page_table.shape[1]
    n_prompt = batch * seq
    nt = n_prompt + ns
    meta_row = nt
    tq = _pick(seq, ATTN_TQ)
    n_rows = -(-(nt + N_META) // tq) * tq
    assert n_prompt % ns == 0 and nt % N_META == 0 and ns % LANES == 0 and meta_row % tq + N_META <= tq

    x_all = jnp.concatenate([x_prompt.reshape(n_prompt, D_MODEL), x_sample.reshape(ns, D_MODEL),
                             meta_tokens.astype(x_prompt.dtype),
                             jnp.zeros((n_rows - nt - N_META, D_MODEL), x_prompt.dtype)], axis=0)
    pos = jnp.concatenate([jnp.tile(N_META + jnp.arange(seq), batch), jnp.full((ns,), n_pages * PAGE_SIZE),
                           jnp.arange(N_META), jnp.zeros((n_rows - nt - N_META,), jnp.int32)])
    cos_t, sin_t = _rope_tables(pos)
    w_packed = _pack_w_in(w_in[0])
    wq, wqs, wk, wv = _pack_mla_weights(w_uq[0], w_uk[0], w_uv[0])
    wukt = jnp.concatenate([jnp.transpose(w_uk[0], (1, 2, 0)),
                            jnp.zeros((MLA_HEADS, LANES - NOPE_DIM, KV_LORA), w_uk.dtype)], axis=1).astype(BF16)
    alog_v, dtb_v = _head_lanes(a_log[0]), _head_lanes(dt_bias[0])
    cw = conv_w[0].astype(F32)
    gn_t = jnp.tile(gdn_norm[0].astype(F32), GDN_HEADS)[None]
    w_r = jnp.concatenate([w_group[0], w_router[0],
                           jnp.zeros((D_MODEL, LANES - N_GROUPS - N_EXPERTS), w_group.dtype)], axis=1).astype(BF16)
    b_r = jnp.concatenate([b_group[0], b_router[0], jnp.zeros((LANES - N_GROUPS - N_EXPERTS,), b_group.dtype)])[None]
    wg, wu, wd = w_gate[0].astype(BF16), w_up[0].astype(BF16), w_down[0].astype(BF16)

    proj = _inproj(x_all, norm_mix[0][None].astype(F32), w_packed)
    qt, k, vt, ckv, krot = _mla_prep(proj, cos_t, sin_t, q_norm[0][None].astype(F32), kv_norm[0][None].astype(F32),
                                     wq.T, wqs.T, wk, wv.T, tq)

    o_mla_p = _attn_prompt(qt, k, vt, batch, seq, meta_row, tq)
    ql, qrt = _q_absorb(qt, wukt, ns, n_prompt)
    o_lat = _mla_sample(page_table, jnp.transpose(ql, (1, 0, 2)), jnp.transpose(qrt, (2, 0, 1)),
                        ckv[n_prompt:nt].reshape(ns, 1, KV_LORA), krot[n_prompt:nt].reshape(ns, 1, LANES),
                        cache_kv_latent[0], jnp.swapaxes(cache_k_rope[0], 1, 2))
    o_mla_s = _o_proj_sample(jnp.transpose(o_lat, (1, 0, 2)), wv)

    s_meta = _gdn_meta(proj, cw, alog_v, dtb_v, meta_row)
    terms = _gdn_terms(proj, cw, alog_v, dtb_v, batch, seq, meta_row)
    o_gdn_p, gdn_p = _gdn_scan(*terms, s_meta, batch, seq)
    o_gdn_s, gdn_s = _gdn_sample(proj, state_conv[0], state_gdn[0], cw, alog_v, dtb_v, n_prompt)

    tail = (proj, w_out[0].astype(BF16), gn_t, norm_ffn[0][None].astype(F32), w_r, b_r.astype(F32))
    xmid_p, hf_p, route_p = _outproj(x_all, o_mla_p, o_gdn_p, *tail, 0)
    xmid_s, hf_s, route_s = _outproj(x_all, o_mla_s, o_gdn_s, *tail, n_prompt)
    route = jnp.concatenate([route_p[:, :2 * TOP_K], route_s[:, :2 * TOP_K]], axis=0)
    plan = _moe_plan(route[:, :TOP_K].astype(jnp.int32), route[:, TOP_K:], _moe_splits(nt))
    y_p, y_s = _moe(plan, hf_p, hf_s, xmid_p, xmid_s, wg, wu, wd, norm_final[None].astype(F32))

    def with_meta(rows, width):
        meta = jnp.broadcast_to(rows[meta_row:meta_row + N_META][None], (batch, N_META, width))
        return jnp.concatenate([meta, rows[:n_prompt].reshape(batch, seq, width)], axis=1)[None]

    k_rope = krot[:, SM_KR:SM_KR + ROPE_DIM]
    conv_p = jnp.stack([proj[(b + 1) * seq - (CONV_W - 1):(b + 1) * seq, :GDN_QKV] for b in range(batch)])
    conv_s = jnp.concatenate([state_conv[0][:, 1:].astype(F32), proj[n_prompt:nt, None, :GDN_QKV]], axis=1)
    return (y_p.reshape(batch, seq, D_MODEL), y_s.reshape(ns, 1, D_MODEL),
            with_meta(ckv, KV_LORA), with_meta(k_rope, ROPE_DIM),
            ckv[n_prompt:nt].reshape(1, ns, 1, KV_LORA), k_rope[n_prompt:nt].reshape(1, ns, 1, ROPE_DIM),
            conv_p[None], conv_s[None], gdn_p[None], gdn_s[None])
```

```python
import functools

import jax
import jax.numpy as jnp
from jax import lax
from jax.experimental import pallas as pl
from jax.experimental.pallas import tpu as pltpu

F32 = jnp.float32
BF16 = jnp.bfloat16
HIGHEST = lax.Precision.HIGHEST

D_MODEL = 1024
N_META = 16
RMS_EPS = 1e-6
MLA_HEADS = 16
Q_LORA = 384
KV_LORA = 256
NOPE_DIM = 64
ROPE_DIM = 32
V_DIM = 64
ROPE_THETA = 10000.0
MLA_SCALE = (NOPE_DIM + ROPE_DIM) ** -0.5
PAGE_SIZE = 128
GDN_HEADS = 8
GDN_DK = 128
GDN_DV = 128
GDN_KEY = GDN_HEADS * GDN_DK
GDN_QKV = 3 * GDN_KEY
CONV_W = 4
CHUNK = 64
N_GROUPS = 4
EXPERTS_PER_GROUP = 8
N_EXPERTS = 32
TOP_K = 2
D_EXPERT = 256
MOE_BLOCK = 128

_OFF_KV = Q_LORA
_OFF_QKV = _OFF_KV + KV_LORA + ROPE_DIM
_OFF_Z = _OFF_QKV + GDN_QKV
_OFF_B = _OFF_Z + GDN_KEY
_OFF_A = _OFF_B + GDN_HEADS
_OFF_GM = _OFF_A + GDN_HEADS
_OFF_GG = _OFF_GM + D_MODEL
P_QKV = 0
P_KVC = 3072
P_SMALL = 3328
P_QD = 3456
P_F32 = 3840
P_Z = 0
P_GM = 1024
P_GG = 2048
P_BF16 = 3072
P_TOTAL = P_F32 + P_BF16
INPROJ_TN = 768
SM_B = 0
SM_A = 8
SM_KR = 64

LANES = 128
VMEM_LIMIT = 56 * 1024 * 1024
ATTN_TQ = (512, 256, 128)


def _pick(n, candidates):
    for c in candidates:
        if n % c == 0:
            return c
    raise ValueError(f"no tile for {n} in {candidates}")


def _dot(a, b):
    return jnp.dot(a, b, preferred_element_type=F32)


def _dot_nt(a, b):
    return lax.dot_general(a, b, (((1,), (1,)), ((), ())), preferred_element_type=F32)


def _dot_tn(a, b):
    return lax.dot_general(a, b, (((0,), (0,)), ((), ())), preferred_element_type=F32)


def _sigmoid(x):
    return 1.0 / (1.0 + jnp.exp(-x))


def _silu(x):
    return x * _sigmoid(x)


def _softplus(x):
    return jnp.maximum(x, 0.0) + jnp.log1p(jnp.exp(-jnp.abs(x)))


def _rms(x, w):
    return x * lax.rsqrt(jnp.mean(x * x, axis=-1, keepdims=True) + RMS_EPS) * w


def _inproj_kernel(x_ref, nw_ref, w_ref, of_ref, ob_ref, hn_ref):
    j = pl.program_id(1)
    nf = P_F32 // INPROJ_TN

    @pl.when(j == 0)
    def _():
        hn_ref[...] = _rms(x_ref[...], nw_ref[...]).astype(BF16)

    r = _dot(hn_ref[...], w_ref[...])

    @pl.when(j < nf)
    def _():
        of_ref[...] = r

    @pl.when(j >= nf)
    def _():
        ob_ref[...] = r.astype(BF16)


def _inproj(x_all, norm_w, w_packed):
    r = x_all.shape[0]
    tm = _pick(r, (1536, 1280, 768, 640, 512, 256, 128))
    tn = INPROJ_TN
    nf = P_F32 // tn
    return pl.pallas_call(
        _inproj_kernel,
        out_shape=(jax.ShapeDtypeStruct((r, P_F32), F32), jax.ShapeDtypeStruct((r, P_BF16), BF16)),
        grid=(r // tm, P_TOTAL // tn),
        in_specs=[
            pl.BlockSpec((tm, D_MODEL), lambda i, j: (i, 0)),
            pl.BlockSpec((1, D_MODEL), lambda i, j: (0, 0)),
            pl.BlockSpec((D_MODEL, tn), lambda i, j: (0, j)),
        ],
        out_specs=(pl.BlockSpec((tm, tn), lambda i, j: (i, jnp.minimum(j, nf - 1))),
                   pl.BlockSpec((tm, tn), lambda i, j: (i, jnp.maximum(j - nf, 0)))),
        scratch_shapes=[pltpu.VMEM((tm, D_MODEL), BF16)],
        compiler_params=pltpu.CompilerParams(
            dimension_semantics=("parallel", "arbitrary"), vmem_limit_bytes=VMEM_LIMIT),
        name="inproj",
    )(x_all, norm_w, w_packed)


LOG2E = 1.4426950408889634


def _mla_prep_kernel(qd_ref, kvc_ref, sm_ref, c_ref, s_ref, qn_ref, kvn_ref, wqt_ref, wqst_ref, wk_ref, wvt_ref,
                     qt_out, k_out, vt_out, ckv_out, kr_out):
    cos = c_ref[...]
    sin = s_ref[...]
    cos_t, sin_t = cos.T, sin.T
    qn_t = _rms(qd_ref[...], qn_ref[...]).T.astype(BF16)
    qt = _dot(wqt_ref[...], qn_t)
    qst = _dot(wqst_ref[...], qn_t)
    for h in range(MLA_HEADS):
        sl = slice(h * LANES, (h + 1) * LANES)
        qt_out[h] = ((qt[sl] * cos_t + qst[sl] * sin_t) * (MLA_SCALE * LOG2E)).astype(BF16)
    ckv = _rms(kvc_ref[...], kvn_ref[...])
    ckv_out[...] = ckv
    sm = sm_ref[...]
    lane = lax.broadcasted_iota(jnp.int32, sm.shape, 1)
    cos_k = jnp.where((lane >= SM_KR) & (lane < SM_KR + ROPE_DIM), cos, 0.0)
    krot = sm * cos_k + pltpu.roll(sm, LANES - ROPE_DIM, 1) * sin
    kr_out[...] = krot
    kk = _dot(ckv.astype(BF16), wk_ref[...])
    vvt = _dot(wvt_ref[...], ckv.T.astype(BF16))
    for h in range(MLA_HEADS):
        sl = slice(h * LANES, (h + 1) * LANES)
        k_out[h] = (kk[:, sl] + krot).astype(BF16)
        vt_out[h, 0] = vvt[sl].astype(BF16)


def _mla_prep(proj, cos_t, sin_t, q_norm, kv_norm, wqt, wqst, wk, wvt, tm):
    r = proj.shape[0]
    hw = MLA_HEADS * LANES
    full = lambda shape: pl.BlockSpec(shape, lambda i: (0,) * len(shape))
    return pl.pallas_call(
        _mla_prep_kernel,
        out_shape=(
            jax.ShapeDtypeStruct((MLA_HEADS, LANES, r), BF16),
            jax.ShapeDtypeStruct((MLA_HEADS, r, LANES), BF16),
            jax.ShapeDtypeStruct((MLA_HEADS, r // tm, LANES, tm), BF16),
            jax.ShapeDtypeStruct((r, KV_LORA), F32),
            jax.ShapeDtypeStruct((r, LANES), F32),
        ),
        grid=(r // tm,),
        in_specs=[
            pl.BlockSpec((tm, Q_LORA), lambda i: (i, P_QD // Q_LORA)),
            pl.BlockSpec((tm, KV_LORA), lambda i: (i, P_KVC // KV_LORA)),
            pl.BlockSpec((tm, LANES), lambda i: (i, P_SMALL // LANES)),
            pl.BlockSpec((tm, LANES), lambda i: (i, 0)),
            pl.BlockSpec((tm, LANES), lambda i: (i, 0)),
            full((1, Q_LORA)), full((1, KV_LORA)),
            full((hw, Q_LORA)), full((hw, Q_LORA)), full((KV_LORA, hw)), full((hw, KV_LORA)),
        ],
        out_specs=(pl.BlockSpec((MLA_HEADS, LANES, tm), lambda i: (0, 0, i)),
                   pl.BlockSpec((MLA_HEADS, tm, LANES), lambda i: (0, i, 0)),
                   pl.BlockSpec((MLA_HEADS, 1, LANES, tm), lambda i: (0, i, 0, 0)),
                   pl.BlockSpec((tm, KV_LORA), lambda i: (i, 0)),
                   pl.BlockSpec((tm, LANES), lambda i: (i, 0))),
        compiler_params=pltpu.CompilerParams(dimension_semantics=("parallel",), vmem_limit_bytes=VMEM_LIMIT),
        name="mla_prep",
    )(proj, proj, proj, cos_t, sin_t, q_norm, kv_norm, wqt, wqst, wk, wvt)


def _attn_prompt_kernel(qt_ref, k_ref, vt_ref, km_ref, vmt_ref, o_ref, *, tq, meta_lane):
    qi = pl.program_id(2)
    hs = (0, 1)
    half = tq // 2
    qt = [qt_ref[h] for h in hs]

    def update(qts, k_blk, vt_blk, carry, mask):
        ms, ls, accs = carry
        s = [_dot(k_blk[h], qts[h]) for h in hs]
        if mask is not None:
            s = [jnp.where(mask, x, -1e30) for x in s]
        m_new = [jnp.maximum(ms[h], jnp.max(s[h], axis=0, keepdims=True)) for h in hs]
        a = [jnp.exp2(ms[h] - m_new[h]) for h in hs]
        p = [jnp.exp2(s[h] - m_new[h]) for h in hs]
        l_new = [a[h] * ls[h] + jnp.sum(p[h], axis=0, keepdims=True) for h in hs]
        acc_new = [a[h] * accs[h] + _dot(vt_blk[h], p[h].astype(BF16)) for h in hs]
        return m_new, l_new, acc_new

    s0 = [_dot(km_ref[h], qt[h]) for h in hs]
    m = [jnp.max(s0[h], axis=0, keepdims=True) for h in hs]
    p0 = [jnp.exp2(s0[h] - m[h]) for h in hs]
    l = [jnp.sum(p0[h], axis=0, keepdims=True) for h in hs]
    acc = [_dot(vmt_ref[h, 0][:, meta_lane:meta_lane + N_META], p0[h].astype(BF16)) for h in hs]

    def body(j, carry):
        rows = pl.ds(pl.multiple_of(j * tq, tq), tq)
        return update(qt, [k_ref[h, rows, :] for h in hs], [vt_ref[h, j] for h in hs], carry, None)

    m, l, acc = lax.fori_loop(0, qi, body, (m, l, acc))

    off = pl.multiple_of(qi * tq, tq)
    key = lax.broadcasted_iota(jnp.int32, (half, tq), 0)
    qry = lax.broadcasted_iota(jnp.int32, (half, tq), 1)
    vt_d = [vt_ref[h, qi] for h in hs]
    m, l, acc = update(qt, [k_ref[h, pl.ds(off, half), :] for h in hs], [v[:, :half] for v in vt_d],
                       (m, l, acc), key <= qry)
    late = lambda xs: [x[:, half:] for x in xs]
    mb, lb, accb = update(late(qt), [k_ref[h, pl.ds(pl.multiple_of(off + half, half), half), :] for h in hs],
                          late(vt_d), (late(m), late(l), late(acc)), (key <= qry)[:, :half])
    ot = [jnp.concatenate([acc[h][:, :half] / l[h][:, :half], accb[h] / lb[h]], axis=1) for h in hs]
    o_ref[...] = (ot[0] + ot[1]).T.astype(o_ref.dtype)


def _attn_prompt(qt, k, vt, batch, seq, meta_row, tq):
    nq = seq // tq
    kern = functools.partial(_attn_prompt_kernel, tq=tq, meta_lane=meta_row % tq)
    return pl.pallas_call(
        kern,
        out_shape=jax.ShapeDtypeStruct((batch * seq, D_MODEL), BF16),
        grid=(batch, MLA_HEADS // 2, nq),
        in_specs=[
            pl.BlockSpec((2, LANES, tq), lambda b, p, i: (p, 0, b * nq + i)),
            pl.BlockSpec((2, seq, LANES), lambda b, p, i: (p, b, 0)),
            pl.BlockSpec((2, nq, LANES, tq), lambda b, p, i: (p, b, 0, 0)),
            pl.BlockSpec((2, N_META, LANES), lambda b, p, i: (p, meta_row // N_META, 0)),
            pl.BlockSpec((2, 1, LANES, tq), lambda b, p, i: (p, meta_row // tq, 0, 0)),
        ],
        out_specs=pl.BlockSpec((tq, LANES), lambda b, p, i: (b * nq + i, p)),
        compiler_params=pltpu.CompilerParams(
            dimension_semantics=("parallel", "parallel", "arbitrary"), vmem_limit_bytes=VMEM_LIMIT),
        name="attn_prompt",
    )(qt, k, vt, k, vt)


def _gate_lanes(sm, alog_ref, dtb_ref):
    g = -jnp.exp(alog_ref[...]) * _softplus(sm + dtb_ref[...])
    beta = _sigmoid(sm)
    return g, beta


def _qkv_heads(xc):
    xf = _silu(xc)
    qs, ks, vs = [], [], []
    for h in range(GDN_HEADS):
        q = xf[:, h * GDN_DK:(h + 1) * GDN_DK]
        k = xf[:, GDN_KEY + h * GDN_DK:GDN_KEY + (h + 1) * GDN_DK]
        qs.append(q * lax.rsqrt(jnp.sum(q * q, axis=-1, keepdims=True) + RMS_EPS) * (GDN_DK ** -0.5))
        ks.append(k * lax.rsqrt(jnp.sum(k * k, axis=-1, keepdims=True) + RMS_EPS))
        vs.append(xf[:, 2 * GDN_KEY + h * GDN_DV:2 * GDN_KEY + (h + 1) * GDN_DV])
    return qs, ks, vs


def _split_bf16(x):
    hi = x.astype(BF16)
    return hi, (x - hi.astype(F32)).astype(BF16)


def _dot_split(a, b):
    return _dot(a[0], b[0]) + (_dot(a[0], b[1]) + _dot(a[1], b[0]))


def _unit_lower_inverses(mats, c):
    row = lax.broadcasted_iota(jnp.int32, (c, c), 0)
    col = lax.broadcasted_iota(jnp.int32, (c, c), 1)
    eye = jnp.where(row == col, 1.0, 0.0)
    ps = [-a for a in mats]
    ts = [eye + p for p in ps]
    span = 2
    while span < c:
        psp = [_split_bf16(p) for p in ps]
        ps = [_dot_split(p, p) for p in psp]
        psp = [_split_bf16(p) for p in ps]
        ts = [t + _dot_split(p, _split_bf16(t)) for p, t in zip(psp, ts)]
        span *= 2
    return ts


def _gdn_chunk_terms(xs, sm, conv_ref, alog_ref, dtb_ref, c):
    heads = range(GDN_HEADS)
    xc = xs[0] * conv_ref[0:1, :]
    for j in range(1, CONV_W):
        xc = xc + xs[j] * conv_ref[j:j + 1, :]
    qs, ks, vs = _qkv_heads(xc)
    g, beta = _gate_lanes(sm, alog_ref, dtb_ref)
    row = lax.broadcasted_iota(jnp.int32, (c, c), 0)
    col = lax.broadcasted_iota(jnp.int32, (c, c), 1)
    causal = col <= row
    strict = col < row
    gcum = jnp.dot(jnp.where(causal, 1.0, 0.0), g, precision=HIGHEST, preferred_element_type=F32)
    gcum_t = lax.dot_general(g, jnp.where(col >= row, 1.0, 0.0), (((0,), (0,)), ((), ())),
                             precision=HIGHEST, preferred_element_type=F32)
    gc = [gcum[:, SM_A + h:SM_A + h + 1] for h in heads]
    gr = [gcum_t[SM_A + h:SM_A + h + 1, :] for h in heads]
    bc = [beta[:, SM_B + h:SM_B + h + 1] for h in heads]
    decay = [jnp.where(causal, jnp.exp(jnp.where(causal, gc[h] - gr[h], 0.0)), 0.0) for h in heads]
    kb = [ks[h] * bc[h] for h in heads]
    kbf = [ks[h].astype(BF16) for h in heads]
    a = [jnp.where(strict, _dot_nt(kb[h].astype(BF16), kbf[h]) * decay[h], 0.0) for h in heads]
    t = _unit_lower_inverses(a, c)
    eg = [jnp.exp(gc[h]) for h in heads]
    sol = [_dot(t[h].astype(BF16), jnp.concatenate([vs[h] * bc[h], kb[h] * eg[h]], axis=1).astype(BF16))
           for h in heads]
    u = [sol[h][:, :GDN_DV] for h in heads]
    w = [sol[h][:, GDN_DV:] for h in heads]
    attn = [jnp.where(causal, _dot_nt(qs[h].astype(BF16), kbf[h]) * decay[h], 0.0) for h in heads]
    qg = [qs[h] * eg[h] for h in heads]
    g_last = gcum[c - 1:c, :]
    kd = [ks[h] * jnp.exp(g_last[:, SM_A + h:SM_A + h + 1] - gc[h]) for h in heads]
    return u, w, qg, kd, attn, jnp.exp(g_last)


def _conv_shifts(hist, x):
    xe = jnp.concatenate([hist, x], axis=0)
    return [pltpu.roll(xe, d, 0)[8:] for d in range(CONV_W - 1, 0, -1)] + [x]


def _gdn_meta_kernel(x_ref, sm_ref, conv_ref, alog_ref, dtb_ref, s_out):
    xs = _conv_shifts(jnp.zeros((8, GDN_QKV), F32), x_ref[...])
    u, _, _, kd, _, _ = _gdn_chunk_terms(xs, sm_ref[...], conv_ref, alog_ref, dtb_ref, N_META)
    for h in range(GDN_HEADS):
        s_out[h] = _dot_tn(kd[h].astype(BF16), u[h].astype(BF16))


def _gdn_meta(proj, conv_w, alog_v, dtb_v, meta_row):
    full = lambda shape: pl.BlockSpec(shape, lambda i: (0,) * len(shape))
    return pl.pallas_call(
        _gdn_meta_kernel,
        out_shape=jax.ShapeDtypeStruct((GDN_HEADS, GDN_DK, GDN_DV), F32),
        grid=(1,),
        in_specs=[
            pl.BlockSpec((N_META, GDN_QKV), lambda i: (meta_row // N_META, 0)),
            pl.BlockSpec((N_META, LANES), lambda i: (meta_row // N_META, P_SMALL // LANES)),
            full((CONV_W, GDN_QKV)), full((1, LANES)), full((1, LANES)),
        ],
        out_specs=full((GDN_HEADS, GDN_DK, GDN_DV)),
        compiler_params=pltpu.CompilerParams(vmem_limit_bytes=VMEM_LIMIT),
        name="gdn_meta",
    )(proj, proj, conv_w, alog_v, dtb_v)


def _gdn_terms_kernel(x_ref, hist_ref, sm_ref, conv_ref, alog_ref, dtb_ref,
                      u_out, w_out, qg_out, kd_out, attn_out, dec_out, *, cps):
    c = CHUNK
    hist = hist_ref[...]
    for cc in range(cps):
        rows = slice(cc * c, (cc + 1) * c)
        x = x_ref[rows, :]
        u, w, qg, kd, attn, dec = _gdn_chunk_terms(_conv_shifts(hist, x), sm_ref[rows, :], conv_ref, alog_ref,
                                                   dtb_ref, c)
        for h in range(GDN_HEADS):
            sl = slice(h * GDN_DV, (h + 1) * GDN_DV)
            u_out[rows, sl] = u[h]
            w_out[rows, sl] = w[h].astype(BF16)
            qg_out[rows, sl] = qg[h].astype(BF16)
            kd_out[rows, sl] = kd[h].astype(BF16)
            attn_out[h, rows, :] = attn[h].astype(BF16)
        dec_out[cc] = dec
        hist = x[c - 8:]


def _gdn_terms(proj, conv_w, alog_v, dtb_v, batch, seq, meta_row):
    nc = seq // CHUNK
    n = batch * seq
    cps = _pick(nc, (2, 1))
    step = cps * CHUNK
    spb = nc // cps
    full = lambda shape: pl.BlockSpec(shape, lambda i: (0,) * len(shape))
    rows = lambda: pl.BlockSpec((step, D_MODEL), lambda i: (i, 0))

    def hist_index(i):
        return (jnp.where(i % spb == 0, (meta_row + N_META) // 8, i * (step // 8)) - 1, 0)

    return pl.pallas_call(
        functools.partial(_gdn_terms_kernel, cps=cps),
        out_shape=(jax.ShapeDtypeStruct((n, D_MODEL), F32),
                   jax.ShapeDtypeStruct((n, D_MODEL), BF16),
                   jax.ShapeDtypeStruct((n, D_MODEL), BF16),
                   jax.ShapeDtypeStruct((n, D_MODEL), BF16),
                   jax.ShapeDtypeStruct((GDN_HEADS, n, CHUNK), BF16),
                   jax.ShapeDtypeStruct((batch * nc, 1, LANES), F32)),
        grid=(batch * spb,),
        in_specs=[
            pl.BlockSpec((step, GDN_QKV), lambda i: (i, 0)),
            pl.BlockSpec((8, GDN_QKV), hist_index),
            pl.BlockSpec((step, LANES), lambda i: (i, P_SMALL // LANES)),
            full((CONV_W, GDN_QKV)), full((1, LANES)), full((1, LANES)),
        ],
        out_specs=(rows(), rows(), rows(), rows(),
                   pl.BlockSpec((GDN_HEADS, step, CHUNK), lambda i: (0, i, 0)),
                   pl.BlockSpec((cps, 1, LANES), lambda i: (i, 0, 0))),
        compiler_params=pltpu.CompilerParams(dimension_semantics=("parallel",), vmem_limit_bytes=VMEM_LIMIT),
        name="gdn_terms",
    )(proj, proj, proj, conv_w, alog_v, dtb_v)


def _gdn_scan_kernel(u_ref, w_ref, qg_ref, kd_ref, attn_ref, dec_ref, s0_ref, o_ref, s_out, st_ref, *, cpg):
    c = CHUNK
    heads = range(GDN_HEADS)

    @pl.when(pl.program_id(1) == 0)
    def _():
        st_ref[...] = s0_ref[...]

    def chunk(ci, carry):
        rows = pl.ds(pl.multiple_of(ci * c, c), c)
        dec = dec_ref[ci]
        sl = [slice(h * GDN_DV, (h + 1) * GDN_DV) for h in heads]
        s_old = [st_ref[h] for h in heads]
        sb = [s.astype(BF16) for s in s_old]
        lhs = [jnp.concatenate([w_ref[rows, sl[h]], qg_ref[rows, sl[h]]], axis=0) for h in heads]
        r = [_dot(lhs[h], sb[h]) for h in heads]
        vnb = [(u_ref[rows, sl[h]] - r[h][:c]).astype(BF16) for h in heads]
        out = [r[h][c:] + _dot(attn_ref[h, rows, :], vnb[h]) for h in heads]
        upd = [_dot_tn(kd_ref[rows, sl[h]], vnb[h]) for h in heads]
        for h in heads:
            o_ref[rows, sl[h]] = out[h].astype(o_ref.dtype)
            st_ref[h] = s_old[h] * dec[:, SM_A + h:SM_A + h + 1] + upd[h]
        return carry

    lax.fori_loop(0, cpg, chunk, 0)

    @pl.when(pl.program_id(1) == pl.num_programs(1) - 1)
    def _():
        s_out[0] = st_ref[...]


def _gdn_scan(u, w, qg, kd, attn, dec, s_meta, batch, seq):
    nc = seq // CHUNK
    cpg = _pick(nc, (8, 4, 2, 1))
    ng = nc // cpg
    rows = lambda: pl.BlockSpec((cpg * CHUNK, D_MODEL), lambda b, g: (b * ng + g, 0))
    kern = functools.partial(_gdn_scan_kernel, cpg=cpg)
    return pl.pallas_call(
        kern,
        out_shape=(jax.ShapeDtypeStruct((batch * seq, D_MODEL), BF16),
                   jax.ShapeDtypeStruct((batch, GDN_HEADS, GDN_DK, GDN_DV), F32)),
        grid=(batch, ng),
        in_specs=[rows(), rows(), rows(), rows(),
                  pl.BlockSpec((GDN_HEADS, cpg * CHUNK, CHUNK), lambda b, g: (0, b * ng + g, 0)),
                  pl.BlockSpec((cpg, 1, LANES), lambda b, g: (b * ng + g, 0, 0)),
                  pl.BlockSpec((GDN_HEADS, GDN_DK, GDN_DV), lambda b, g: (0, 0, 0))],
        out_specs=(rows(),
                   pl.BlockSpec((1, GDN_HEADS, GDN_DK, GDN_DV), lambda b, g: (b, 0, 0, 0))),
        scratch_shapes=[pltpu.VMEM((GDN_HEADS, GDN_DK, GDN_DV), F32)],
        compiler_params=pltpu.CompilerParams(
            dimension_semantics=("parallel", "arbitrary"), vmem_limit_bytes=VMEM_LIMIT),
        name="gdn_scan",
    )(u, w, qg, kd, attn, dec, s_meta)


def _gdn_sample_kernel(x_ref, sm_ref, cs_ref, st_ref, conv_ref, alog_ref, dtb_ref, o_ref, s_out, *, nb):
    xc = x_ref[...] * conv_ref[CONV_W - 1:CONV_W, :]
    for j in range(CONV_W - 1):
        xc = xc + cs_ref[:, j, :] * conv_ref[j:j + 1, :]
    qs, ks, vs = _qkv_heads(xc)
    g, beta = _gate_lanes(sm_ref[...], alog_ref, dtb_ref)
    eg = jnp.exp(g)
    for h in range(GDN_HEADS):
        q_t = qs[h].T
        k_t = ks[h].T
        for b in range(nb):
            kcol = k_t[:, b:b + 1]
            s1 = st_ref[b, h] * eg[b:b + 1, SM_A + h:SM_A + h + 1]
            r = jnp.sum(s1 * kcol, axis=0, keepdims=True)
            delta = (vs[h][b:b + 1, :] - r) * beta[b:b + 1, SM_B + h:SM_B + h + 1]
            s2 = s1 + kcol * delta
            s_out[b, h] = s2
            o_ref[b:b + 1, h * GDN_DV:(h + 1) * GDN_DV] = jnp.sum(s2 * q_t[:, b:b + 1], axis=0, keepdims=True)


def _gdn_sample(proj, state_conv, state_gdn, conv_w, alog_v, dtb_v, row0):
    ns = state_gdn.shape[0]
    nb = 8
    full = lambda shape: pl.BlockSpec(shape, lambda i: (0,) * len(shape))
    kern = functools.partial(_gdn_sample_kernel, nb=nb)
    return pl.pallas_call(
        kern,
        out_shape=(jax.ShapeDtypeStruct((ns, D_MODEL), F32),
                   jax.ShapeDtypeStruct(state_gdn.shape, F32)),
        grid=(ns // nb,),
        in_specs=[
            pl.BlockSpec((nb, GDN_QKV), lambda i: (row0 // nb + i, 0)),
            pl.BlockSpec((nb, LANES), lambda i: (row0 // nb + i, P_SMALL // LANES)),
            pl.BlockSpec((nb, CONV_W - 1, GDN_QKV), lambda i: (i, 0, 0)),
            pl.BlockSpec((nb, GDN_HEADS, GDN_DK, GDN_DV), lambda i: (i, 0, 0, 0)),
            full((CONV_W, GDN_QKV)), full((1, LANES)), full((1, LANES)),
        ],
        out_specs=(pl.BlockSpec((nb, D_MODEL), lambda i: (i, 0)),
                   pl.BlockSpec((nb, GDN_HEADS, GDN_DK, GDN_DV), lambda i: (i, 0, 0, 0))),
        compiler_params=pltpu.CompilerParams(dimension_semantics=("parallel",), vmem_limit_bytes=VMEM_LIMIT),
        name="gdn_sample",
    )(proj, proj, state_conv, state_gdn, conv_w, alog_v, dtb_v)


def _q_absorb_kernel(qt_ref, wukt_ref, ql_out, qrt_out):
    for h in range(MLA_HEADS):
        qt = qt_ref[h]
        ql_out[h] = _dot_tn(qt, wukt_ref[h]).astype(BF16)
        qrt_out[h] = qt[NOPE_DIM:NOPE_DIM + ROPE_DIM]


def _q_absorb(qt, wukt, ns, row0):
    return pl.pallas_call(
        _q_absorb_kernel,
        out_shape=(jax.ShapeDtypeStruct((MLA_HEADS, ns, KV_LORA), BF16),
                   jax.ShapeDtypeStruct((MLA_HEADS, ROPE_DIM, ns), BF16)),
        grid=(1,),
        in_specs=[pl.BlockSpec((MLA_HEADS, LANES, ns), lambda i: (0, 0, row0 // ns)),
                  pl.BlockSpec((MLA_HEADS, LANES, KV_LORA), lambda i: (0, 0, 0))],
        out_specs=(pl.BlockSpec((MLA_HEADS, ns, KV_LORA), lambda i: (0, 0, 0)),
                   pl.BlockSpec((MLA_HEADS, ROPE_DIM, ns), lambda i: (0, 0, 0))),
        compiler_params=pltpu.CompilerParams(vmem_limit_bytes=VMEM_LIMIT),
        name="q_absorb",
    )(qt, wukt)


def _mla_sample_kernel(pt_ref, ql_ref, qr_ref, cn_ref, krn_ref, cc_hbm, cr_hbm, o_ref, cbuf, rbuf, sem,
                       *, n_pages, nsub):
    b = pl.program_id(0)
    slot = b % 2

    def page_copies(seq, slot_, i):
        page = pt_ref[seq * n_pages + i]
        return (pltpu.make_async_copy(cc_hbm.at[page], cbuf.at[slot_, i], sem.at[slot_, 0]),
                pltpu.make_async_copy(cr_hbm.at[page], rbuf.at[slot_, :, pl.ds(i * PAGE_SIZE, PAGE_SIZE)],
                                      sem.at[slot_, 1]))

    def start_pages(seq, slot_):
        for i in range(n_pages):
            cc, cr = page_copies(seq, slot_, i)
            cc.start()
            cr.start()

    @pl.when(b == 0)
    def _():
        start_pages(0, 0)

    @pl.when(b + 1 < pl.num_programs(0))
    def _():
        start_pages(b + 1, 1 - slot)

    for i in range(n_pages):
        cc, cr = page_copies(b, slot, i)
        cc.wait()
        cr.wait()

    ql = ql_ref[0]
    qr = qr_ref[0]
    pps = n_pages // nsub
    subs = range(nsub)
    c = [cbuf[slot, i * pps:(i + 1) * pps].reshape(pps * PAGE_SIZE, KV_LORA).astype(BF16) for i in subs]
    s = [_dot_nt(ql, c[i]) + _dot(qr, rbuf[slot, :, i * pps * PAGE_SIZE:(i + 1) * pps * PAGE_SIZE].astype(BF16))
         for i in subs]
    ms = [jnp.max(s[i], axis=1, keepdims=True) for i in subs]
    p = [jnp.exp2(s[i] - ms[i]) for i in subs]
    ls = [jnp.sum(p[i], axis=1, keepdims=True) for i in subs]
    accs = [_dot(p[i].astype(BF16), c[i]) for i in subs]

    cn = cn_ref[0]
    krn = krn_ref[0][:, SM_KR:SM_KR + ROPE_DIM]
    ms.append(jnp.sum(ql.astype(F32) * cn, axis=1, keepdims=True)
              + jnp.sum(qr.astype(F32) * krn, axis=1, keepdims=True))
    ls.append(jnp.ones_like(ms[-1]))
    accs.append(jnp.broadcast_to(cn, (MLA_HEADS, KV_LORA)))
    m = functools.reduce(jnp.maximum, ms)
    scale = [jnp.exp2(m_i - m) for m_i in ms]
    l = sum(a * l_i for a, l_i in zip(scale, ls))
    acc = sum(a * acc_i for a, acc_i in zip(scale, accs))
    o_ref[0] = acc / l


def _mla_sample(page_table, ql, qr, c_new, kr_new, cache_c, cache_r):
    ns, n_pages = page_table.shape
    kern = functools.partial(_mla_sample_kernel, n_pages=n_pages, nsub=_pick(n_pages, (8, 4, 2, 1)))
    grid_spec = pltpu.PrefetchScalarGridSpec(
        num_scalar_prefetch=1,
        grid=(ns,),
        in_specs=[
            pl.BlockSpec((1, MLA_HEADS, KV_LORA), lambda b, pt: (b, 0, 0)),
            pl.BlockSpec((1, MLA_HEADS, ROPE_DIM), lambda b, pt: (b, 0, 0)),
            pl.BlockSpec((1, 1, KV_LORA), lambda b, pt: (b, 0, 0)),
            pl.BlockSpec((1, 1, LANES), lambda b, pt: (b, 0, 0)),
            pl.BlockSpec(memory_space=pl.ANY),
            pl.BlockSpec(memory_space=pl.ANY),
        ],
        out_specs=pl.BlockSpec((1, MLA_HEADS, KV_LORA), lambda b, pt: (b, 0, 0)),
        scratch_shapes=[pltpu.VMEM((2, n_pages, PAGE_SIZE, KV_LORA), F32),
                        pltpu.VMEM((2, ROPE_DIM, n_pages * PAGE_SIZE), F32),
                        pltpu.SemaphoreType.DMA((2, 2))],
    )
    return pl.pallas_call(
        kern,
        out_shape=jax.ShapeDtypeStruct((ns, MLA_HEADS, KV_LORA), F32),
        grid_spec=grid_spec,
        compiler_params=pltpu.CompilerParams(dimension_semantics=("arbitrary",), vmem_limit_bytes=VMEM_LIMIT),
        name="mla_sample",
    )(page_table.reshape(-1), ql, qr, c_new, kr_new, cache_c, cache_r)


def _o_proj_sample_kernel(ol_ref, wv_ref, o_ref):
    for p in range(MLA_HEADS // 2):
        acc = None
        for h in (2 * p, 2 * p + 1):
            part = _dot(ol_ref[h].astype(BF16), wv_ref[:, h * LANES:(h + 1) * LANES])
            acc = part if acc is None else acc + part
        o_ref[:, p * LANES:(p + 1) * LANES] = acc.astype(o_ref.dtype)


def _o_proj_sample(o_lat_t, wv):
    ns = o_lat_t.shape[1]
    return pl.pallas_call(
        _o_proj_sample_kernel,
        out_shape=jax.ShapeDtypeStruct((ns, D_MODEL), BF16),
        grid=(1,),
        in_specs=[pl.BlockSpec((MLA_HEADS, ns, KV_LORA), lambda i: (0, 0, 0)),
                  pl.BlockSpec((KV_LORA, MLA_HEADS * LANES), lambda i: (0, 0))],
        out_specs=pl.BlockSpec((ns, D_MODEL), lambda i: (0, 0)),
        compiler_params=pltpu.CompilerParams(vmem_limit_bytes=VMEM_LIMIT),
        name="o_proj_sample",
    )(o_lat_t, wv)


def _outproj_kernel(x_ref, om_ref, og_ref, z_ref, gm_ref, gg_ref, wo_ref, gn_ref, nf_ref, wr_ref, br_ref,
                    xmid_out, hf_out, route_out):
    og = og_ref[...].astype(F32)
    parts = []
    for h in range(GDN_HEADS):
        oh = og[:, h * GDN_DV:(h + 1) * GDN_DV]
        parts.append(oh * lax.rsqrt(jnp.mean(oh * oh, axis=-1, keepdims=True) + RMS_EPS))
    o_gdn = jnp.concatenate(parts, axis=1) * gn_ref[...] * _silu(z_ref[...].astype(F32))
    merged = (_sigmoid(gm_ref[...].astype(F32)) * om_ref[...].astype(F32)
              + _sigmoid(gg_ref[...].astype(F32)) * o_gdn)
    x_mid = x_ref[...] + _dot(merged.astype(BF16), wo_ref[...])
    xmid_out[...] = x_mid
    hf = _rms(x_mid, nf_ref[...]).astype(BF16)
    bits = lax.bitcast_convert_type(hf.astype(F32), jnp.uint32)
    half = D_MODEL // 2
    hf_out[...] = bits[:, half:] | (bits[:, :half] >> 16)

    logits = _dot(hf, wr_ref[...]) + br_ref[...]
    lane = lax.broadcasted_iota(jnp.int32, logits.shape, 1)
    neg = -jnp.inf
    big = 4 * LANES
    is_g = lane < N_GROUPS
    lg = jnp.where(is_g, logits, neg)
    mg = jnp.max(lg, axis=1, keepdims=True)
    grp = jnp.min(jnp.where(lg == mg, lane, big), axis=1, keepdims=True)
    gate_g = 1.0 / jnp.sum(jnp.where(is_g, jnp.exp(logits - mg), 0.0), axis=1, keepdims=True)
    e_lane = lane - N_GROUPS
    in_grp = (e_lane >= 0) & (e_lane < N_EXPERTS) & ((e_lane >> 3) == grp)
    le = jnp.where(in_grp, logits, neg)
    v1 = jnp.max(le, axis=1, keepdims=True)
    i1 = jnp.min(jnp.where(le == v1, lane, big), axis=1, keepdims=True)
    le2 = jnp.where(lane == i1, neg, le)
    v2 = jnp.max(le2, axis=1, keepdims=True)
    i2 = jnp.min(jnp.where(le2 == v2, lane, big), axis=1, keepdims=True)
    e = jnp.exp(v2 - v1)
    w1 = gate_g / (1.0 + e)
    w2 = gate_g * e / (1.0 + e)
    route = jnp.where(lane == 0, (i1 - N_GROUPS).astype(F32),
                      jnp.where(lane == 1, (i2 - N_GROUPS).astype(F32),
                                jnp.where(lane == 2, w1, jnp.where(lane == 3, w2, 0.0))))
    route_out[...] = route


def _row_tile(n, limit):
    t = limit - limit % 16
    while t >= 16:
        if n % t == 0:
            return t
        t -= 16
    raise ValueError(f"no row tile for {n}")


def _outproj(x_all, o_mla, o_gdn, gates, w_out, gn_t, norm_ffn, w_r, b_r, row0):
    n = o_mla.shape[0]
    tm = _pick(n, (512, 256, 128, 64, 32, 16))
    assert row0 % tm == 0
    r0 = row0 // tm
    full = lambda shape: pl.BlockSpec(shape, lambda i: (0,) * len(shape))
    row = lambda w: pl.BlockSpec((tm, w), lambda i: (i, 0))
    shared = lambda w, j=0: pl.BlockSpec((tm, w), lambda i, j=j: (r0 + i, j))
    return pl.pallas_call(
        _outproj_kernel,
        out_shape=(jax.ShapeDtypeStruct((n, D_MODEL), F32),
                   jax.ShapeDtypeStruct((n, D_MODEL // 2), jnp.uint32),
                   jax.ShapeDtypeStruct((n, LANES), F32)),
        grid=(n // tm,),
        in_specs=[shared(D_MODEL), row(D_MODEL), row(D_MODEL),
                  shared(D_MODEL, P_Z // D_MODEL), shared(D_MODEL, P_GM // D_MODEL), shared(D_MODEL, P_GG // D_MODEL),
                  full((D_MODEL, D_MODEL)), full((1, D_MODEL)), full((1, D_MODEL)),
                  full((D_MODEL, LANES)), full((1, LANES))],
        out_specs=(row(D_MODEL), row(D_MODEL // 2), row(LANES)),
        compiler_params=pltpu.CompilerParams(dimension_semantics=("parallel",), vmem_limit_bytes=VMEM_LIMIT),
        name="outproj_route",
    )(x_all, o_mla, o_gdn, gates, gates, gates, w_out, gn_t, norm_ffn, w_r, b_r)


def _moe_plan(eid, wgt, nsp):
    nt = eid.shape[0]
    ts = nt // nsp
    n_asg = ts * TOP_K
    e = eid.reshape(nsp, n_asg)
    ids = jnp.broadcast_to(jnp.arange(n_asg, dtype=jnp.int32), e.shape)
    _, ids_s, w_s = lax.sort((e, ids, wgt.reshape(nsp, n_asg)), dimension=1, num_keys=1, is_stable=True)
    rows_s = (ids_s % TOP_K) * (ts + 8) + ids_s // TOP_K
    counts = jnp.sum((e[..., None] == jnp.arange(N_EXPERTS, dtype=jnp.int32)).astype(jnp.int32), axis=1)
    run_start = jnp.cumsum(counts, axis=1) - counts
    tail = lambda v, dt: jnp.full((nsp, MOE_BLOCK), v, dt)
    tok_t = jnp.concatenate([ids_s // TOP_K, tail(ts, jnp.int32)], axis=1)
    rows_t = jnp.concatenate([rows_s, tail(ts, jnp.int32)], axis=1)
    w_t = jnp.concatenate([w_s, tail(0.0, F32)], axis=1)
    as_i32 = lambda a: a.astype(jnp.int32).reshape(-1)
    scalars = (as_i32(run_start), as_i32(counts), as_i32(tok_t), as_i32(rows_t), w_t.reshape(-1))
    return scalars, ts


def _split_pieces(k, ts, n_prompt, n_sample):
    lo, hi = k * ts, (k + 1) * ts
    pieces = []
    if lo < n_prompt:
        pieces.append((0, lo, 0, min(hi, n_prompt) - lo))
    if hi > n_prompt:
        start = max(lo, n_prompt)
        pieces.append((1, start - n_prompt, start - lo, hi - start))
    assert hi <= n_prompt + n_sample
    return pieces


def _moe_kernel(start_ref, cnt_ref, tok_ref, rows_ref, w_ref, hf_p, hf_q, xmid_p, xmid_q,
                wg_ref, wu_ref, wd_ref, nfin_ref, y_p, y_q, hf_s, comb, acc, xb, yb, sem,
                *, ts, rc, nsp, n_prompt, n_sample):
    s = pl.program_id(0)
    j = pl.program_id(1)
    half = D_MODEL // 2
    stride = ts + 8

    def zero_spare_rows():
        hf_s[ts:ts + 8, :] = jnp.zeros((8, half), jnp.uint32)

    def split_copies(k, load):
        cps = []
        for src, r0, l0, n in _split_pieces(k, ts, n_prompt, n_sample):
            if load:
                cps.append(pltpu.make_async_copy((hf_p, hf_q)[src].at[pl.ds(r0, n)], hf_s.at[pl.ds(l0, n)],
                                                 sem.at[0, src]))
                cps.append(pltpu.make_async_copy((xmid_p, xmid_q)[src].at[pl.ds(r0, n)], acc.at[pl.ds(l0, n)],
                                                 sem.at[1, src]))
            else:
                cps.append(pltpu.make_async_copy(acc.at[pl.ds(l0, n)], (y_p, y_q)[src].at[pl.ds(r0, n)],
                                                 sem.at[2, src]))
        return cps

    def run_copies(load, between=None):
        for k in range(nsp):
            @pl.when(s == k)
            def _(k=k):
                cps = split_copies(k, load)
                for cp in cps:
                    cp.start()
                if between is not None:
                    between()
                for cp in cps:
                    cp.wait()

    @pl.when(j == 0)
    def _():
        run_copies(True, zero_spare_rows)

    cnt = cnt_ref[s * N_EXPERTS + j]
    first = s * (ts * TOP_K + MOE_BLOCK) + start_ref[s * N_EXPERTS + j]

    def block(i, carry):
        p0 = first + i * MOE_BLOCK
        for r in range(MOE_BLOCK):
            xb[r:r + 1, :] = hf_s[pl.ds(tok_ref[p0 + r], 1), :]
        bits = xb[...]
        lo = lax.bitcast_convert_type(bits << 16, F32).astype(BF16)
        hi = lax.bitcast_convert_type(bits & jnp.uint32(0xFFFF0000), F32).astype(BF16)
        g = _dot(lo, wg_ref[0, :half, :]) + _dot(hi, wg_ref[0, half:, :])
        u = _dot(lo, wu_ref[0, :half, :]) + _dot(hi, wu_ref[0, half:, :])
        yb[...] = _dot((_silu(g) * u).astype(BF16), wd_ref[0])
        for r in range(MOE_BLOCK):
            comb[pl.ds(rows_ref[p0 + r], 1), :] = w_ref[p0 + r] * yb[r:r + 1, :]
        return carry

    lax.fori_loop(0, (cnt + MOE_BLOCK - 1) // MOE_BLOCK, block, 0)

    @pl.when(j == N_EXPERTS - 1)
    def _():
        def body(i, carry):
            r0 = pl.multiple_of(i * rc, 8)
            rows = pl.ds(r0, rc)
            moe = comb[rows, :]
            for k in range(1, TOP_K):
                moe = moe + comb[pl.ds(k * stride + r0, rc), :]
            acc[rows, :] = _rms(acc[rows, :] + moe, nfin_ref[...])
            return carry

        lax.fori_loop(0, ts // rc, body, 0)
        run_copies(False)


def _moe(plan, hf_p, hf_q, xmid_p, xmid_q, wg, wu, wd, norm_final):
    scalars, ts = plan
    n_prompt, n_sample = xmid_p.shape[0], xmid_q.shape[0]
    nsp = (n_prompt + n_sample) // ts
    rc = _row_tile(ts, 256) if ts % 16 == 0 else 8
    kern = functools.partial(_moe_kernel, ts=ts, rc=rc, nsp=nsp, n_prompt=n_prompt, n_sample=n_sample)
    hbm = pl.BlockSpec(memory_space=pl.ANY)
    expert = lambda shape: pl.BlockSpec((1,) + shape, lambda s, j, *_: (j, 0, 0))
    grid_spec = pltpu.PrefetchScalarGridSpec(
        num_scalar_prefetch=len(scalars),
        grid=(nsp, N_EXPERTS),
        in_specs=[
            hbm, hbm, hbm, hbm,
            expert((D_MODEL, D_EXPERT)), expert((D_MODEL, D_EXPERT)), expert((D_EXPERT, D_MODEL)),
            pl.BlockSpec((1, D_MODEL), lambda s, j, *_: (0, 0)),
        ],
        out_specs=(hbm, hbm),
        scratch_shapes=[
            pltpu.VMEM((ts + 8, D_MODEL // 2), jnp.uint32),
            pltpu.VMEM((TOP_K * (ts + 8), D_MODEL), F32),
            pltpu.VMEM((ts, D_MODEL), F32),
            pltpu.VMEM((MOE_BLOCK, D_MODEL // 2), jnp.uint32),
            pltpu.VMEM((MOE_BLOCK, D_MODEL), F32),
            pltpu.SemaphoreType.DMA((3, 2)),
        ],
    )
    return pl.pallas_call(
        kern,
        out_shape=(jax.ShapeDtypeStruct((n_prompt, D_MODEL), F32), jax.ShapeDtypeStruct((n_sample, D_MODEL), F32)),
        grid_spec=grid_spec,
        compiler_params=pltpu.CompilerParams(
            dimension_semantics=("arbitrary", "arbitrary"), vmem_limit_bytes=VMEM_LIMIT),
        name="moe",
    )(*scalars, hf_p, hf_q, xmid_p, xmid_q, wg, wu, wd, norm_final)


def _pack_w_in(w):
    kr = w[:, _OFF_KV + KV_LORA:_OFF_QKV]
    kr_sw = jnp.concatenate([kr[:, ROPE_DIM // 2:], kr[:, :ROPE_DIM // 2]], axis=1)
    small = jnp.concatenate([w[:, _OFF_B:_OFF_A], w[:, _OFF_A:_OFF_GM],
                             jnp.zeros((D_MODEL, SM_KR - 2 * GDN_HEADS), w.dtype), kr, kr_sw], axis=1)
    packed = jnp.concatenate([w[:, _OFF_QKV:_OFF_Z], w[:, _OFF_KV:_OFF_KV + KV_LORA], small, w[:, :Q_LORA],
                              w[:, _OFF_Z:_OFF_B], w[:, _OFF_GM:_OFF_GG], w[:, _OFF_GG:]], axis=1)
    return packed.astype(BF16)


def _pack_mla_weights(w_uq, w_uk, w_uv):
    zq = jnp.zeros((Q_LORA, MLA_HEADS, LANES - NOPE_DIM - ROPE_DIM), w_uq.dtype)
    wq = jnp.concatenate([w_uq, zq], axis=2).reshape(Q_LORA, MLA_HEADS * LANES)
    rope = w_uq[:, :, NOPE_DIM:]
    rope_sw = jnp.concatenate([rope[..., ROPE_DIM // 2:], rope[..., :ROPE_DIM // 2]], axis=2)
    wqs = jnp.concatenate([jnp.zeros((Q_LORA, MLA_HEADS, NOPE_DIM), w_uq.dtype), rope_sw, zq], axis=2)
    wqs = wqs.reshape(Q_LORA, MLA_HEADS * LANES)
    wk = jnp.concatenate([w_uk, jnp.zeros((KV_LORA, MLA_HEADS, LANES - NOPE_DIM), w_uk.dtype)], axis=2)
    wk = wk.reshape(KV_LORA, MLA_HEADS * LANES)
    zv = jnp.zeros((KV_LORA, MLA_HEADS // 2, V_DIM), w_uv.dtype)
    wv = jnp.stack([jnp.concatenate([w_uv[:, 0::2], zv], axis=2),
                    jnp.concatenate([zv, w_uv[:, 1::2]], axis=2)], axis=2)
    wv = wv.reshape(KV_LORA, MLA_HEADS * LANES)
    return wq.astype(BF16), wqs.astype(BF16), wk.astype(BF16), wv.astype(BF16)


def _rope_tables(pos):
    inv_freq = ROPE_THETA ** (-jnp.arange(0, ROPE_DIM, 2, dtype=F32) / ROPE_DIM)
    ang = pos.astype(F32)[:, None] * inv_freq[None, :]
    cos, sin = jnp.cos(ang), jnp.sin(ang)
    n = pos.shape[0]
    cos_t = jnp.concatenate([jnp.ones((n, NOPE_DIM), F32), cos, cos, jnp.zeros((n, ROPE_DIM), F32)], axis=1)
    sin_t = jnp.concatenate([jnp.zeros((n, NOPE_DIM), F32), -sin, sin, jnp.zeros((n, ROPE_DIM), F32)], axis=1)
    return cos_t, sin_t


def _head_lanes(v):
    return jnp.zeros((1, LANES), F32).at[0, SM_A:SM_A + GDN_HEADS].set(v.astype(F32))


def _moe_splits(nt):
    for nsp in (6, 3, 4, 2, 1):
        if nt % (nsp * 8) == 0:
            return nsp
    return 1


def kernel(x_prompt, x_sample, cache_kv_latent, cache_k_rope, page_table, state_conv, state_gdn, meta_tokens,
           norm_mix, w_in, q_norm, w_uq, kv_norm, w_uk, w_uv, conv_w, a_log, dt_bias, gdn_norm, w_out, norm_ffn,
           w_group, b_group, w_router, b_router, w_gate, w_up, w_down, norm_final):
    batch, seq, _ = x_prompt.shape
    ns, dec_seq, _ = x_sample.shape
    assert dec_seq == 1 and w_in.shape[0] == 1 and seq % CHUNK == 0
    n_pages = page_table.shape[1]
    n_prompt = batch * seq
    nt = n_prompt + ns
    meta_row = nt
    tq = _pick(seq, ATTN_TQ)
    n_rows = -(-(nt + N_META) // tq) * tq
    assert n_prompt % ns == 0 and nt % N_META == 0 and ns % LANES == 0 and meta_row % tq + N_META <= tq

    x_all = jnp.concatenate([x_prompt.reshape(n_prompt, D_MODEL), x_sample.reshape(ns, D_MODEL),
                             meta_tokens.astype(x_prompt.dtype),
                             jnp.zeros((n_rows - nt - N_META, D_MODEL), x_prompt.dtype)], axis=0)
    pos = jnp.concatenate([jnp.tile(N_META + jnp.arange(seq), batch), jnp.full((ns,), n_pages * PAGE_SIZE),
                           jnp.arange(N_META), jnp.zeros((n_rows - nt - N_META,), jnp.int32)])
    cos_t, sin_t = _rope_tables(pos)
    w_packed = _pack_w_in(w_in[0])
    wq, wqs, wk, wv = _pack_mla_weights(w_uq[0], w_uk[0], w_uv[0])
    wukt = jnp.concatenate([jnp.transpose(w_uk[0], (1, 2, 0)),
                            jnp.zeros((MLA_HEADS, LANES - NOPE_DIM, KV_LORA), w_uk.dtype)], axis=1).astype(BF16)
    alog_v, dtb_v = _head_lanes(a_log[0]), _head_lanes(dt_bias[0])
    cw = conv_w[0].astype(F32)
    gn_t = jnp.tile(gdn_norm[0].astype(F32), GDN_HEADS)[None]
    w_r = jnp.concatenate([w_group[0], w_router[0],
                           jnp.zeros((D_MODEL, LANES - N_GROUPS - N_EXPERTS), w_group.dtype)], axis=1).astype(BF16)
    b_r = jnp.concatenate([b_group[0], b_router[0], jnp.zeros((LANES - N_GROUPS - N_EXPERTS,), b_group.dtype)])[None]
    wg, wu, wd = w_gate[0].astype(BF16), w_up[0].astype(BF16), w_down[0].astype(BF16)

    proj, gates = _inproj(x_all, norm_mix[0][None].astype(F32), w_packed)
    qt, k, vt, ckv, krot = _mla_prep(proj, cos_t, sin_t, q_norm[0][None].astype(F32), kv_norm[0][None].astype(F32),
                                     wq.T, wqs.T, wk, wv.T, tq)

    o_mla_p = _attn_prompt(qt, k, vt, batch, seq, meta_row, tq)
    ql, qrt = _q_absorb(qt, wukt, ns, n_prompt)
    o_lat = _mla_sample(page_table, jnp.transpose(ql, (1, 0, 2)), jnp.transpose(qrt, (2, 0, 1)),
                        ckv[n_prompt:nt].reshape(ns, 1, KV_LORA), krot[n_prompt:nt].reshape(ns, 1, LANES),
                        cache_kv_latent[0], jnp.swapaxes(cache_k_rope[0], 1, 2))
    o_mla_s = _o_proj_sample(jnp.transpose(o_lat, (1, 0, 2)), wv)

    s_meta = _gdn_meta(proj, cw, alog_v, dtb_v, meta_row)
    terms = _gdn_terms(proj, cw, alog_v, dtb_v, batch, seq, meta_row)
    o_gdn_p, gdn_p = _gdn_scan(*terms, s_meta, batch, seq)
    o_gdn_s, gdn_s = _gdn_sample(proj, state_conv[0], state_gdn[0], cw, alog_v, dtb_v, n_prompt)

    tail = (gates, w_out[0].astype(BF16), gn_t, norm_ffn[0][None].astype(F32), w_r, b_r.astype(F32))
    xmid_p, hf_p, route_p = _outproj(x_all, o_mla_p, o_gdn_p, *tail, 0)
    xmid_s, hf_s, route_s = _outproj(x_all, o_mla_s, o_gdn_s, *tail, n_prompt)
    route = jnp.concatenate([route_p[:, :2 * TOP_K], route_s[:, :2 * TOP_K]], axis=0)
    plan = _moe_plan(route[:, :TOP_K].astype(jnp.int32), route[:, TOP_K:], _moe_splits(nt))
    y_p, y_s = _moe(plan, hf_p, hf_s, xmid_p, xmid_s, wg, wu, wd, norm_final[None].astype(F32))

    def with_meta(rows, width):
        meta = jnp.broadcast_to(rows[meta_row:meta_row + N_META][None], (batch, N_META, width))
        return jnp.concatenate([meta, rows[:n_prompt].reshape(batch, seq, width)], axis=1)[None]

    k_rope = krot[:, SM_KR:SM_KR + ROPE_DIM]
    conv_p = jnp.stack([proj[(b + 1) * seq - (CONV_W - 1):(b + 1) * seq, :GDN_QKV] for b in range(batch)])
    conv_s = jnp.concatenate([state_conv[0][:, 1:].astype(F32), proj[n_prompt:nt, None, :GDN_QKV]], axis=1)
    return (y_p.reshape(batch, seq, D_MODEL), y_s.reshape(ns, 1, D_MODEL),
            with_meta(ckv, KV_LORA), with_meta(k_rope, ROPE_DIM),
            ckv[n_prompt:nt].reshape(1, ns, 1, KV_LORA), k_rope[n_prompt:nt].reshape(1, ns, 1, ROPE_DIM),
            conv_p[None], conv_s[None], gdn_p[None], gdn_s[None])
```

```python
import functools

import jax
import jax.numpy as jnp
from jax import lax
from jax.experimental import pallas as pl
from jax.experimental.pallas import tpu as pltpu

F32 = jnp.float32
BF16 = jnp.bfloat16
HIGHEST = lax.Precision.HIGHEST

D_MODEL = 1024
N_META = 16
RMS_EPS = 1e-6
MLA_HEADS = 16
Q_LORA = 384
KV_LORA = 256
NOPE_DIM = 64
ROPE_DIM = 32
V_DIM = 64
ROPE_THETA = 10000.0
MLA_SCALE = (NOPE_DIM + ROPE_DIM) ** -0.5
PAGE_SIZE = 128
GDN_HEADS = 8
GDN_DK = 128
GDN_DV = 128
GDN_KEY = GDN_HEADS * GDN_DK
GDN_QKV = 3 * GDN_KEY
CONV_W = 4
CHUNK = 64
N_GROUPS = 4
EXPERTS_PER_GROUP = 8
N_EXPERTS = 32
TOP_K = 2
D_EXPERT = 256
MOE_BLOCK = 128

_OFF_KV = Q_LORA
_OFF_QKV = _OFF_KV + KV_LORA + ROPE_DIM
_OFF_Z = _OFF_QKV + GDN_QKV
_OFF_B = _OFF_Z + GDN_KEY
_OFF_A = _OFF_B + GDN_HEADS
_OFF_GM = _OFF_A + GDN_HEADS
_OFF_GG = _OFF_GM + D_MODEL
P_QKV = 0
P_KVC = 3072
P_SMALL = 3328
P_QD = 3456
P_F32 = 3840
P_Z = 0
P_GM = 1024
P_GG = 2048
P_BF16 = 3072
P_TOTAL = P_F32 + P_BF16
INPROJ_TN = 768
SM_B = 0
SM_A = 8
SM_KR = 64

LANES = 128
VMEM_LIMIT = 56 * 1024 * 1024
ATTN_TQ = (512, 256, 128)


def _pick(n, candidates):
    for c in candidates:
        if n % c == 0:
            return c
    raise ValueError(f"no tile for {n} in {candidates}")


def _dot(a, b):
    return jnp.dot(a, b, preferred_element_type=F32)


def _dot_nt(a, b):
    return lax.dot_general(a, b, (((1,), (1,)), ((), ())), preferred_element_type=F32)


def _dot_tn(a, b):
    return lax.dot_general(a, b, (((0,), (0,)), ((), ())), preferred_element_type=F32)


def _sigmoid(x):
    return 1.0 / (1.0 + jnp.exp(-x))


def _silu(x):
    return x * _sigmoid(x)


def _softplus(x):
    return jnp.maximum(x, 0.0) + jnp.log1p(jnp.exp(-jnp.abs(x)))


def _rms(x, w):
    return x * lax.rsqrt(jnp.mean(x * x, axis=-1, keepdims=True) + RMS_EPS) * w


def _inproj_kernel(x_ref, nw_ref, w_ref, of_ref, ob_ref, hn_ref):
    j = pl.program_id(1)
    nf = P_F32 // INPROJ_TN

    @pl.when(j == 0)
    def _():
        hn_ref[...] = _rms(x_ref[...], nw_ref[...]).astype(BF16)

    r = _dot(hn_ref[...], w_ref[...])

    @pl.when(j < nf)
    def _():
        of_ref[...] = r

    @pl.when(j >= nf)
    def _():
        ob_ref[...] = r.astype(BF16)


def _inproj(x_all, norm_w, w_packed):
    r = x_all.shape[0]
    tm = _pick(r, (1536, 1280, 768, 640, 512, 256, 128))
    tn = INPROJ_TN
    nf = P_F32 // tn
    return pl.pallas_call(
        _inproj_kernel,
        out_shape=(jax.ShapeDtypeStruct((r, P_F32), F32), jax.ShapeDtypeStruct((r, P_BF16), BF16)),
        grid=(r // tm, P_TOTAL // tn),
        in_specs=[
            pl.BlockSpec((tm, D_MODEL), lambda i, j: (i, 0)),
            pl.BlockSpec((1, D_MODEL), lambda i, j: (0, 0)),
            pl.BlockSpec((D_MODEL, tn), lambda i, j: (0, j)),
        ],
        out_specs=(pl.BlockSpec((tm, tn), lambda i, j: (i, jnp.minimum(j, nf - 1))),
                   pl.BlockSpec((tm, tn), lambda i, j: (i, jnp.maximum(j - nf, 0)))),
        scratch_shapes=[pltpu.VMEM((tm, D_MODEL), BF16)],
        compiler_params=pltpu.CompilerParams(
            dimension_semantics=("parallel", "arbitrary"), vmem_limit_bytes=VMEM_LIMIT),
        name="inproj",
    )(x_all, norm_w, w_packed)


LOG2E = 1.4426950408889634


def _mla_prep_kernel(qd_ref, kvc_ref, sm_ref, c_ref, s_ref, qn_ref, kvn_ref, wqt_ref, wqst_ref, wk_ref, wvt_ref,
                     qt_out, k_out, vt_out, ckv_out, kr_out):
    cos = c_ref[...]
    sin = s_ref[...]
    cos_t, sin_t = cos.T, sin.T
    qn_t = _rms(qd_ref[...], qn_ref[...]).T.astype(BF16)
    qt = _dot(wqt_ref[...], qn_t)
    qst = _dot(wqst_ref[...], qn_t)
    for h in range(MLA_HEADS):
        sl = slice(h * LANES, (h + 1) * LANES)
        qt_out[h] = ((qt[sl] * cos_t + qst[sl] * sin_t) * (MLA_SCALE * LOG2E)).astype(BF16)
    ckv = _rms(kvc_ref[...], kvn_ref[...])
    ckv_out[...] = ckv
    sm = sm_ref[...]
    lane = lax.broadcasted_iota(jnp.int32, sm.shape, 1)
    cos_k = jnp.where((lane >= SM_KR) & (lane < SM_KR + ROPE_DIM), cos, 0.0)
    krot = sm * cos_k + pltpu.roll(sm, LANES - ROPE_DIM, 1) * sin
    kr_out[...] = krot
    kk = _dot(ckv.astype(BF16), wk_ref[...])
    vvt = _dot(wvt_ref[...], ckv.T.astype(BF16))
    for h in range(MLA_HEADS):
        sl = slice(h * LANES, (h + 1) * LANES)
        k_out[h] = (kk[:, sl] + krot).astype(BF16)
        vt_out[h, 0] = vvt[sl].astype(BF16)


def _mla_prep(proj, cos_t, sin_t, q_norm, kv_norm, wqt, wqst, wk, wvt, tm):
    r = proj.shape[0]
    hw = MLA_HEADS * LANES
    full = lambda shape: pl.BlockSpec(shape, lambda i: (0,) * len(shape))
    return pl.pallas_call(
        _mla_prep_kernel,
        out_shape=(
            jax.ShapeDtypeStruct((MLA_HEADS, LANES, r), BF16),
            jax.ShapeDtypeStruct((MLA_HEADS, r, LANES), BF16),
            jax.ShapeDtypeStruct((MLA_HEADS, r // tm, LANES, tm), BF16),
            jax.ShapeDtypeStruct((r, KV_LORA), F32),
            jax.ShapeDtypeStruct((r, LANES), F32),
        ),
        grid=(r // tm,),
        in_specs=[
            pl.BlockSpec((tm, Q_LORA), lambda i: (i, P_QD // Q_LORA)),
            pl.BlockSpec((tm, KV_LORA), lambda i: (i, P_KVC // KV_LORA)),
            pl.BlockSpec((tm, LANES), lambda i: (i, P_SMALL // LANES)),
            pl.BlockSpec((tm, LANES), lambda i: (i, 0)),
            pl.BlockSpec((tm, LANES), lambda i: (i, 0)),
            full((1, Q_LORA)), full((1, KV_LORA)),
            full((hw, Q_LORA)), full((hw, Q_LORA)), full((KV_LORA, hw)), full((hw, KV_LORA)),
        ],
        out_specs=(pl.BlockSpec((MLA_HEADS, LANES, tm), lambda i: (0, 0, i)),
                   pl.BlockSpec((MLA_HEADS, tm, LANES), lambda i: (0, i, 0)),
                   pl.BlockSpec((MLA_HEADS, 1, LANES, tm), lambda i: (0, i, 0, 0)),
                   pl.BlockSpec((tm, KV_LORA), lambda i: (i, 0)),
                   pl.BlockSpec((tm, LANES), lambda i: (i, 0))),
        compiler_params=pltpu.CompilerParams(dimension_semantics=("parallel",), vmem_limit_bytes=VMEM_LIMIT),
        name="mla_prep",
    )(proj, proj, proj, cos_t, sin_t, q_norm, kv_norm, wqt, wqst, wk, wvt)


def _attn_prompt_kernel(qt_ref, k_ref, vt_ref, km_ref, vmt_ref, o_ref, *, tq, meta_lane):
    qi = pl.program_id(2)
    hs = (0, 1)
    half = tq // 2
    qt = [qt_ref[h] for h in hs]

    def update(qts, k_blk, vt_blk, carry, mask):
        ms, ls, accs = carry
        s = [_dot(k_blk[h], qts[h]) for h in hs]
        if mask is not None:
            s = [jnp.where(mask, x, -1e30) for x in s]
        m_new = [jnp.maximum(ms[h], jnp.max(s[h], axis=0, keepdims=True)) for h in hs]
        a = [jnp.exp2(ms[h] - m_new[h]) for h in hs]
        p = [jnp.exp2(s[h] - m_new[h]) for h in hs]
        l_new = [a[h] * ls[h] + jnp.sum(p[h], axis=0, keepdims=True) for h in hs]
        acc_new = [a[h] * accs[h] + _dot(vt_blk[h], p[h].astype(BF16)) for h in hs]
        return m_new, l_new, acc_new

    s0 = [_dot(km_ref[h], qt[h]) for h in hs]
    m = [jnp.max(s0[h], axis=0, keepdims=True) for h in hs]
    p0 = [jnp.exp2(s0[h] - m[h]) for h in hs]
    l = [jnp.sum(p0[h], axis=0, keepdims=True) for h in hs]
    acc = [_dot(vmt_ref[h, 0][:, meta_lane:meta_lane + N_META], p0[h].astype(BF16)) for h in hs]

    def body(j, carry):
        rows = pl.ds(pl.multiple_of(j * tq, tq), tq)
        return update(qt, [k_ref[h, rows, :] for h in hs], [vt_ref[h, j] for h in hs], carry, None)

    m, l, acc = lax.fori_loop(0, qi, body, (m, l, acc))

    off = pl.multiple_of(qi * tq, tq)
    key = lax.broadcasted_iota(jnp.int32, (half, tq), 0)
    qry = lax.broadcasted_iota(jnp.int32, (half, tq), 1)
    vt_d = [vt_ref[h, qi] for h in hs]
    m, l, acc = update(qt, [k_ref[h, pl.ds(off, half), :] for h in hs], [v[:, :half] for v in vt_d],
                       (m, l, acc), key <= qry)
    late = lambda xs: [x[:, half:] for x in xs]
    mb, lb, accb = update(late(qt), [k_ref[h, pl.ds(pl.multiple_of(off + half, half), half), :] for h in hs],
                          late(vt_d), (late(m), late(l), late(acc)), (key <= qry)[:, :half])
    ot = [jnp.concatenate([acc[h][:, :half] / l[h][:, :half], accb[h] / lb[h]], axis=1) for h in hs]
    o_ref[...] = (ot[0] + ot[1]).T.astype(o_ref.dtype)


def _attn_prompt(qt, k, vt, batch, seq, meta_row, tq):
    nq = seq // tq
    kern = functools.partial(_attn_prompt_kernel, tq=tq, meta_lane=meta_row % tq)
    return pl.pallas_call(
        kern,
        out_shape=jax.ShapeDtypeStruct((batch * seq, D_MODEL), BF16),
        grid=(batch, MLA_HEADS // 2, nq),
        in_specs=[
            pl.BlockSpec((2, LANES, tq), lambda b, p, i: (p, 0, b * nq + i)),
            pl.BlockSpec((2, seq, LANES), lambda b, p, i: (p, b, 0)),
            pl.BlockSpec((2, nq, LANES, tq), lambda b, p, i: (p, b, 0, 0)),
            pl.BlockSpec((2, N_META, LANES), lambda b, p, i: (p, meta_row // N_META, 0)),
            pl.BlockSpec((2, 1, LANES, tq), lambda b, p, i: (p, meta_row // tq, 0, 0)),
        ],
        out_specs=pl.BlockSpec((tq, LANES), lambda b, p, i: (b * nq + i, p)),
        compiler_params=pltpu.CompilerParams(
            dimension_semantics=("parallel", "parallel", "arbitrary"), vmem_limit_bytes=VMEM_LIMIT),
        name="attn_prompt",
    )(qt, k, vt, k, vt)


def _gate_lanes(sm, alog_ref, dtb_ref):
    g = -jnp.exp(alog_ref[...]) * _softplus(sm + dtb_ref[...])
    beta = _sigmoid(sm)
    return g, beta


def _qkv_heads(xc):
    xf = _silu(xc)
    qs, ks, vs = [], [], []
    for h in range(GDN_HEADS):
        q = xf[:, h * GDN_DK:(h + 1) * GDN_DK]
        k = xf[:, GDN_KEY + h * GDN_DK:GDN_KEY + (h + 1) * GDN_DK]
        qs.append(q * lax.rsqrt(jnp.sum(q * q, axis=-1, keepdims=True) + RMS_EPS) * (GDN_DK ** -0.5))
        ks.append(k * lax.rsqrt(jnp.sum(k * k, axis=-1, keepdims=True) + RMS_EPS))
        vs.append(xf[:, 2 * GDN_KEY + h * GDN_DV:2 * GDN_KEY + (h + 1) * GDN_DV])
    return qs, ks, vs


def _split_bf16(x):
    hi = x.astype(BF16)
    return hi, (x - hi.astype(F32)).astype(BF16)


def _dot_split(a, b):
    return _dot(a[0], b[0]) + (_dot(a[0], b[1]) + _dot(a[1], b[0]))


def _unit_lower_inverses(mats, c):
    row = lax.broadcasted_iota(jnp.int32, (c, c), 0)
    col = lax.broadcasted_iota(jnp.int32, (c, c), 1)
    eye = jnp.where(row == col, 1.0, 0.0)
    ps = [-a for a in mats]
    ts = [eye + p for p in ps]
    span = 2
    while span < c:
        psp = [_split_bf16(p) for p in ps]
        ps = [_dot_split(p, p) for p in psp]
        psp = [_split_bf16(p) for p in ps]
        ts = [t + _dot_split(p, _split_bf16(t)) for p, t in zip(psp, ts)]
        span *= 2
    return ts


def _gdn_chunk_terms(xs, sm, conv_ref, alog_ref, dtb_ref, c):
    heads = range(GDN_HEADS)
    xc = xs[0] * conv_ref[0:1, :]
    for j in range(1, CONV_W):
        xc = xc + xs[j] * conv_ref[j:j + 1, :]
    qs, ks, vs = _qkv_heads(xc)
    g, beta = _gate_lanes(sm, alog_ref, dtb_ref)
    row = lax.broadcasted_iota(jnp.int32, (c, c), 0)
    col = lax.broadcasted_iota(jnp.int32, (c, c), 1)
    causal = col <= row
    strict = col < row
    gcum = jnp.dot(jnp.where(causal, 1.0, 0.0), g, precision=HIGHEST, preferred_element_type=F32)
    gcum_t = lax.dot_general(g, jnp.where(col >= row, 1.0, 0.0), (((0,), (0,)), ((), ())),
                             precision=HIGHEST, preferred_element_type=F32)
    gc = [gcum[:, SM_A + h:SM_A + h + 1] for h in heads]
    gr = [gcum_t[SM_A + h:SM_A + h + 1, :] for h in heads]
    bc = [beta[:, SM_B + h:SM_B + h + 1] for h in heads]
    decay = [jnp.where(causal, jnp.exp(jnp.where(causal, gc[h] - gr[h], 0.0)), 0.0) for h in heads]
    kb = [ks[h] * bc[h] for h in heads]
    kbf = [ks[h].astype(BF16) for h in heads]
    a = [jnp.where(strict, _dot_nt(kb[h].astype(BF16), kbf[h]) * decay[h], 0.0) for h in heads]
    t = _unit_lower_inverses(a, c)
    eg = [jnp.exp(gc[h]) for h in heads]
    sol = [_dot(t[h].astype(BF16), jnp.concatenate([vs[h] * bc[h], kb[h] * eg[h]], axis=1).astype(BF16))
           for h in heads]
    u = [sol[h][:, :GDN_DV] for h in heads]
    w = [sol[h][:, GDN_DV:] for h in heads]
    attn = [jnp.where(causal, _dot_nt(qs[h].astype(BF16), kbf[h]) * decay[h], 0.0) for h in heads]
    qg = [qs[h] * eg[h] for h in heads]
    g_last = gcum[c - 1:c, :]
    kd = [ks[h] * jnp.exp(g_last[:, SM_A + h:SM_A + h + 1] - gc[h]) for h in heads]
    return u, w, qg, kd, attn, jnp.exp(g_last)


def _conv_shifts(hist, x):
    xe = jnp.concatenate([hist, x], axis=0)
    return [pltpu.roll(xe, d, 0)[8:] for d in range(CONV_W - 1, 0, -1)] + [x]


def _gdn_meta_kernel(x_ref, sm_ref, conv_ref, alog_ref, dtb_ref, s_out):
    xs = _conv_shifts(jnp.zeros((8, GDN_QKV), F32), x_ref[...])
    u, _, _, kd, _, _ = _gdn_chunk_terms(xs, sm_ref[...], conv_ref, alog_ref, dtb_ref, N_META)
    for h in range(GDN_HEADS):
        s_out[h] = _dot_tn(kd[h].astype(BF16), u[h].astype(BF16))


def _gdn_meta(proj, conv_w, alog_v, dtb_v, meta_row):
    full = lambda shape: pl.BlockSpec(shape, lambda i: (0,) * len(shape))
    return pl.pallas_call(
        _gdn_meta_kernel,
        out_shape=jax.ShapeDtypeStruct((GDN_HEADS, GDN_DK, GDN_DV), F32),
        grid=(1,),
        in_specs=[
            pl.BlockSpec((N_META, GDN_QKV), lambda i: (meta_row // N_META, 0)),
            pl.BlockSpec((N_META, LANES), lambda i: (meta_row // N_META, P_SMALL // LANES)),
            full((CONV_W, GDN_QKV)), full((1, LANES)), full((1, LANES)),
        ],
        out_specs=full((GDN_HEADS, GDN_DK, GDN_DV)),
        compiler_params=pltpu.CompilerParams(vmem_limit_bytes=VMEM_LIMIT),
        name="gdn_meta",
    )(proj, proj, conv_w, alog_v, dtb_v)


def _gdn_terms_kernel(x_ref, hist_ref, sm_ref, conv_ref, alog_ref, dtb_ref,
                      u_out, w_out, qg_out, kd_out, attn_out, dec_out, *, cps):
    c = CHUNK
    hist = hist_ref[...]
    for cc in range(cps):
        rows = slice(cc * c, (cc + 1) * c)
        x = x_ref[rows, :]
        u, w, qg, kd, attn, dec = _gdn_chunk_terms(_conv_shifts(hist, x), sm_ref[rows, :], conv_ref, alog_ref,
                                                   dtb_ref, c)
        for h in range(GDN_HEADS):
            sl = slice(h * GDN_DV, (h + 1) * GDN_DV)
            u_out[rows, sl] = u[h]
            w_out[rows, sl] = w[h].astype(BF16)
            qg_out[rows, sl] = qg[h].astype(BF16)
            kd_out[rows, sl] = kd[h].astype(BF16)
            attn_out[h, rows, :] = attn[h].astype(BF16)
        dec_out[cc] = dec
        hist = x[c - 8:]


def _gdn_terms(proj, conv_w, alog_v, dtb_v, batch, seq, meta_row):
    nc = seq // CHUNK
    n = batch * seq
    cps = _pick(nc, (2, 1))
    step = cps * CHUNK
    spb = nc // cps
    full = lambda shape: pl.BlockSpec(shape, lambda i: (0,) * len(shape))
    rows = lambda: pl.BlockSpec((step, D_MODEL), lambda i: (i, 0))

    def hist_index(i):
        return (jnp.where(i % spb == 0, (meta_row + N_META) // 8, i * (step // 8)) - 1, 0)

    return pl.pallas_call(
        functools.partial(_gdn_terms_kernel, cps=cps),
        out_shape=(jax.ShapeDtypeStruct((n, D_MODEL), F32),
                   jax.ShapeDtypeStruct((n, D_MODEL), BF16),
                   jax.ShapeDtypeStruct((n, D_MODEL), BF16),
                   jax.ShapeDtypeStruct((n, D_MODEL), BF16),
                   jax.ShapeDtypeStruct((GDN_HEADS, n, CHUNK), BF16),
                   jax.ShapeDtypeStruct((batch * nc, 1, LANES), F32)),
        grid=(batch * spb,),
        in_specs=[
            pl.BlockSpec((step, GDN_QKV), lambda i: (i, 0)),
            pl.BlockSpec((8, GDN_QKV), hist_index),
            pl.BlockSpec((step, LANES), lambda i: (i, P_SMALL // LANES)),
            full((CONV_W, GDN_QKV)), full((1, LANES)), full((1, LANES)),
        ],
        out_specs=(rows(), rows(), rows(), rows(),
                   pl.BlockSpec((GDN_HEADS, step, CHUNK), lambda i: (0, i, 0)),
                   pl.BlockSpec((cps, 1, LANES), lambda i: (i, 0, 0))),
        compiler_params=pltpu.CompilerParams(dimension_semantics=("parallel",), vmem_limit_bytes=VMEM_LIMIT),
        name="gdn_terms",
    )(proj, proj, proj, conv_w, alog_v, dtb_v)


def _gdn_scan_kernel(u_ref, w_ref, qg_ref, kd_ref, attn_ref, dec_ref, s0_ref, o_ref, s_out, st_ref, *, cpg):
    c = CHUNK
    heads = range(GDN_HEADS)

    @pl.when(pl.program_id(1) == 0)
    def _():
        st_ref[...] = s0_ref[...]

    def chunk(ci, carry):
        rows = pl.ds(pl.multiple_of(ci * c, c), c)
        dec = dec_ref[ci]
        sl = [slice(h * GDN_DV, (h + 1) * GDN_DV) for h in heads]
        s_old = [st_ref[h] for h in heads]
        sb = [s.astype(BF16) for s in s_old]
        lhs = [jnp.concatenate([w_ref[rows, sl[h]], qg_ref[rows, sl[h]]], axis=0) for h in heads]
        r = [_dot(lhs[h], sb[h]) for h in heads]
        vnb = [(u_ref[rows, sl[h]] - r[h][:c]).astype(BF16) for h in heads]
        out = [r[h][c:] + _dot(attn_ref[h, rows, :], vnb[h]) for h in heads]
        upd = [_dot_tn(kd_ref[rows, sl[h]], vnb[h]) for h in heads]
        for h in heads:
            o_ref[rows, sl[h]] = out[h].astype(o_ref.dtype)
            st_ref[h] = s_old[h] * dec[:, SM_A + h:SM_A + h + 1] + upd[h]
        return carry

    lax.fori_loop(0, cpg, chunk, 0)

    @pl.when(pl.program_id(1) == pl.num_programs(1) - 1)
    def _():
        s_out[0] = st_ref[...]


def _gdn_scan(u, w, qg, kd, attn, dec, s_meta, batch, seq):
    nc = seq // CHUNK
    cpg = _pick(nc, (8, 4, 2, 1))
    ng = nc // cpg
    rows = lambda: pl.BlockSpec((cpg * CHUNK, D_MODEL), lambda b, g: (b * ng + g, 0))
    kern = functools.partial(_gdn_scan_kernel, cpg=cpg)
    return pl.pallas_call(
        kern,
        out_shape=(jax.ShapeDtypeStruct((batch * seq, D_MODEL), BF16),
                   jax.ShapeDtypeStruct((batch, GDN_HEADS, GDN_DK, GDN_DV), F32)),
        grid=(batch, ng),
        in_specs=[rows(), rows(), rows(), rows(),
                  pl.BlockSpec((GDN_HEADS, cpg * CHUNK, CHUNK), lambda b, g: (0, b * ng + g, 0)),
                  pl.BlockSpec((cpg, 1, LANES), lambda b, g: (b * ng + g, 0, 0)),
                  pl.BlockSpec((GDN_HEADS, GDN_DK, GDN_DV), lambda b, g: (0, 0, 0))],
        out_specs=(rows(),
                   pl.BlockSpec((1, GDN_HEADS, GDN_DK, GDN_DV), lambda b, g: (b, 0, 0, 0))),
        scratch_shapes=[pltpu.VMEM((GDN_HEADS, GDN_DK, GDN_DV), F32)],
        compiler_params=pltpu.CompilerParams(
            dimension_semantics=("parallel", "arbitrary"), vmem_limit_bytes=VMEM_LIMIT),
        name="gdn_scan",
    )(u, w, qg, kd, attn, dec, s_meta)


def _gdn_sample_kernel(x_ref, sm_ref, cs_ref, st_ref, conv_ref, alog_ref, dtb_ref, o_ref, s_out, *, nb):
    xc = x_ref[...] * conv_ref[CONV_W - 1:CONV_W, :]
    for j in range(CONV_W - 1):
        xc = xc + cs_ref[:, j, :] * conv_ref[j:j + 1, :]
    qs, ks, vs = _qkv_heads(xc)
    g, beta = _gate_lanes(sm_ref[...], alog_ref, dtb_ref)
    eg = jnp.exp(g)
    for h in range(GDN_HEADS):
        q_t = qs[h].T
        k_t = ks[h].T
        for b in range(nb):
            kcol = k_t[:, b:b + 1]
            s1 = st_ref[b, h] * eg[b:b + 1, SM_A + h:SM_A + h + 1]
            r = jnp.sum(s1 * kcol, axis=0, keepdims=True)
            delta = (vs[h][b:b + 1, :] - r) * beta[b:b + 1, SM_B + h:SM_B + h + 1]
            s2 = s1 + kcol * delta
            s_out[b, h] = s2
            o_ref[b:b + 1, h * GDN_DV:(h + 1) * GDN_DV] = jnp.sum(s2 * q_t[:, b:b + 1], axis=0, keepdims=True)


def _gdn_sample(proj, state_conv, state_gdn, conv_w, alog_v, dtb_v, row0):
    ns = state_gdn.shape[0]
    nb = 8
    full = lambda shape: pl.BlockSpec(shape, lambda i: (0,) * len(shape))
    kern = functools.partial(_gdn_sample_kernel, nb=nb)
    return pl.pallas_call(
        kern,
        out_shape=(jax.ShapeDtypeStruct((ns, D_MODEL), F32),
                   jax.ShapeDtypeStruct(state_gdn.shape, F32)),
        grid=(ns // nb,),
        in_specs=[
            pl.BlockSpec((nb, GDN_QKV), lambda i: (row0 // nb + i, 0)),
            pl.BlockSpec((nb, LANES), lambda i: (row0 // nb + i, P_SMALL // LANES)),
            pl.BlockSpec((nb, CONV_W - 1, GDN_QKV), lambda i: (i, 0, 0)),
            pl.BlockSpec((nb, GDN_HEADS, GDN_DK, GDN_DV), lambda i: (i, 0, 0, 0)),
            full((CONV_W, GDN_QKV)), full((1, LANES)), full((1, LANES)),
        ],
        out_specs=(pl.BlockSpec((nb, D_MODEL), lambda i: (i, 0)),
                   pl.BlockSpec((nb, GDN_HEADS, GDN_DK, GDN_DV), lambda i: (i, 0, 0, 0))),
        compiler_params=pltpu.CompilerParams(dimension_semantics=("parallel",), vmem_limit_bytes=VMEM_LIMIT),
        name="gdn_sample",
    )(proj, proj, state_conv, state_gdn, conv_w, alog_v, dtb_v)


def _q_absorb_kernel(qt_ref, wukt_ref, ql_out, qrt_out):
    for h in range(MLA_HEADS):
        qt = qt_ref[h]
        ql_out[h] = _dot_tn(qt, wukt_ref[h]).astype(BF16)
        qrt_out[h] = qt[NOPE_DIM:NOPE_DIM + ROPE_DIM]


def _q_absorb(qt, wukt, ns, row0):
    return pl.pallas_call(
        _q_absorb_kernel,
        out_shape=(jax.ShapeDtypeStruct((MLA_HEADS, ns, KV_LORA), BF16),
                   jax.ShapeDtypeStruct((MLA_HEADS, ROPE_DIM, ns), BF16)),
        grid=(1,),
        in_specs=[pl.BlockSpec((MLA_HEADS, LANES, ns), lambda i: (0, 0, row0 // ns)),
                  pl.BlockSpec((MLA_HEADS, LANES, KV_LORA), lambda i: (0, 0, 0))],
        out_specs=(pl.BlockSpec((MLA_HEADS, ns, KV_LORA), lambda i: (0, 0, 0)),
                   pl.BlockSpec((MLA_HEADS, ROPE_DIM, ns), lambda i: (0, 0, 0))),
        compiler_params=pltpu.CompilerParams(vmem_limit_bytes=VMEM_LIMIT),
        name="q_absorb",
    )(qt, wukt)


def _mla_sample_kernel(pt_ref, ql_ref, qr_ref, cn_ref, krn_ref, cc_hbm, cr_hbm, o_ref, cbuf, rbuf, sem,
                       *, n_pages, nsub):
    b = pl.program_id(0)
    slot = b % 2

    def page_copies(seq, slot_, i):
        page = pt_ref[seq * n_pages + i]
        return (pltpu.make_async_copy(cc_hbm.at[page], cbuf.at[slot_, i], sem.at[slot_, 0]),
                pltpu.make_async_copy(cr_hbm.at[page], rbuf.at[slot_, :, pl.ds(i * PAGE_SIZE, PAGE_SIZE)],
                                      sem.at[slot_, 1]))

    def start_pages(seq, slot_):
        for i in range(n_pages):
            cc, cr = page_copies(seq, slot_, i)
            cc.start()
            cr.start()

    @pl.when(b == 0)
    def _():
        start_pages(0, 0)

    @pl.when(b + 1 < pl.num_programs(0))
    def _():
        start_pages(b + 1, 1 - slot)

    for i in range(n_pages):
        cc, cr = page_copies(b, slot, i)
        cc.wait()
        cr.wait()

    ql = ql_ref[0]
    qr = qr_ref[0]
    pps = n_pages // nsub
    subs = range(nsub)
    c = [cbuf[slot, i * pps:(i + 1) * pps].reshape(pps * PAGE_SIZE, KV_LORA).astype(BF16) for i in subs]
    s = [_dot_nt(ql, c[i]) + _dot(qr, rbuf[slot, :, i * pps * PAGE_SIZE:(i + 1) * pps * PAGE_SIZE].astype(BF16))
         for i in subs]
    ms = [jnp.max(s[i], axis=1, keepdims=True) for i in subs]
    p = [jnp.exp2(s[i] - ms[i]) for i in subs]
    ls = [jnp.sum(p[i], axis=1, keepdims=True) for i in subs]
    accs = [_dot(p[i].astype(BF16), c[i]) for i in subs]

    cn = cn_ref[0]
    krn = krn_ref[0][:, SM_KR:SM_KR + ROPE_DIM]
    ms.append(jnp.sum(ql.astype(F32) * cn, axis=1, keepdims=True)
              + jnp.sum(qr.astype(F32) * krn, axis=1, keepdims=True))
    ls.append(jnp.ones_like(ms[-1]))
    accs.append(jnp.broadcast_to(cn, (MLA_HEADS, KV_LORA)))
    m = functools.reduce(jnp.maximum, ms)
    scale = [jnp.exp2(m_i - m) for m_i in ms]
    l = sum(a * l_i for a, l_i in zip(scale, ls))
    acc = sum(a * acc_i for a, acc_i in zip(scale, accs))
    o_ref[0] = acc / l


def _mla_sample(page_table, ql, qr, c_new, kr_new, cache_c, cache_r):
    ns, n_pages = page_table.shape
    kern = functools.partial(_mla_sample_kernel, n_pages=n_pages, nsub=_pick(n_pages, (8, 4, 2, 1)))
    grid_spec = pltpu.PrefetchScalarGridSpec(
        num_scalar_prefetch=1,
        grid=(ns,),
        in_specs=[
            pl.BlockSpec((1, MLA_HEADS, KV_LORA), lambda b, pt: (b, 0, 0)),
            pl.BlockSpec((1, MLA_HEADS, ROPE_DIM), lambda b, pt: (b, 0, 0)),
            pl.BlockSpec((1, 1, KV_LORA), lambda b, pt: (b, 0, 0)),
            pl.BlockSpec((1, 1, LANES), lambda b, pt: (b, 0, 0)),
            pl.BlockSpec(memory_space=pl.ANY),
            pl.BlockSpec(memory_space=pl.ANY),
        ],
        out_specs=pl.BlockSpec((1, MLA_HEADS, KV_LORA), lambda b, pt: (b, 0, 0)),
        scratch_shapes=[pltpu.VMEM((2, n_pages, PAGE_SIZE, KV_LORA), F32),
                        pltpu.VMEM((2, ROPE_DIM, n_pages * PAGE_SIZE), F32),
                        pltpu.SemaphoreType.DMA((2, 2))],
    )
    return pl.pallas_call(
        kern,
        out_shape=jax.ShapeDtypeStruct((ns, MLA_HEADS, KV_LORA), F32),
        grid_spec=grid_spec,
        compiler_params=pltpu.CompilerParams(dimension_semantics=("arbitrary",), vmem_limit_bytes=VMEM_LIMIT),
        name="mla_sample",
    )(page_table.reshape(-1), ql, qr, c_new, kr_new, cache_c, cache_r)


def _o_proj_sample_kernel(ol_ref, wv_ref, o_ref):
    for p in range(MLA_HEADS // 2):
        acc = None
        for h in (2 * p, 2 * p + 1):
            part = _dot(ol_ref[h].astype(BF16), wv_ref[:, h * LANES:(h + 1) * LANES])
            acc = part if acc is None else acc + part
        o_ref[:, p * LANES:(p + 1) * LANES] = acc.astype(o_ref.dtype)


def _o_proj_sample(o_lat_t, wv):
    ns = o_lat_t.shape[1]
    return pl.pallas_call(
        _o_proj_sample_kernel,
        out_shape=jax.ShapeDtypeStruct((ns, D_MODEL), BF16),
        grid=(1,),
        in_specs=[pl.BlockSpec((MLA_HEADS, ns, KV_LORA), lambda i: (0, 0, 0)),
                  pl.BlockSpec((KV_LORA, MLA_HEADS * LANES), lambda i: (0, 0))],
        out_specs=pl.BlockSpec((ns, D_MODEL), lambda i: (0, 0)),
        compiler_params=pltpu.CompilerParams(vmem_limit_bytes=VMEM_LIMIT),
        name="o_proj_sample",
    )(o_lat_t, wv)


def _outproj_kernel(x_ref, om_ref, og_ref, z_ref, gm_ref, gg_ref, wo_ref, gn_ref, nf_ref, wr_ref, br_ref,
                    xmid_out, hf_out, route_out):
    og = og_ref[...].astype(F32)
    parts = []
    for h in range(GDN_HEADS):
        oh = og[:, h * GDN_DV:(h + 1) * GDN_DV]
        parts.append(oh * lax.rsqrt(jnp.mean(oh * oh, axis=-1, keepdims=True) + RMS_EPS))
    o_gdn = jnp.concatenate(parts, axis=1) * gn_ref[...] * _silu(z_ref[...].astype(F32))
    merged = (_sigmoid(gm_ref[...].astype(F32)) * om_ref[...].astype(F32)
              + _sigmoid(gg_ref[...].astype(F32)) * o_gdn)
    x_mid = x_ref[...] + _dot(merged.astype(BF16), wo_ref[...])
    xmid_out[...] = x_mid
    hf = _rms(x_mid, nf_ref[...]).astype(BF16)
    bits = lax.bitcast_convert_type(hf.astype(F32), jnp.uint32)
    half = D_MODEL // 2
    hf_out[...] = bits[:, half:] | (bits[:, :half] >> 16)

    logits = _dot(hf, wr_ref[...]) + br_ref[...]
    lane = lax.broadcasted_iota(jnp.int32, logits.shape, 1)
    neg = -jnp.inf
    big = 4 * LANES
    is_g = lane < N_GROUPS
    lg = jnp.where(is_g, logits, neg)
    mg = jnp.max(lg, axis=1, keepdims=True)
    grp = jnp.min(jnp.where(lg == mg, lane, big), axis=1, keepdims=True)
    gate_g = 1.0 / jnp.sum(jnp.where(is_g, jnp.exp(logits - mg), 0.0), axis=1, keepdims=True)
    e_lane = lane - N_GROUPS
    in_grp = (e_lane >= 0) & (e_lane < N_EXPERTS) & ((e_lane >> 3) == grp)
    le = jnp.where(in_grp, logits, neg)
    v1 = jnp.max(le, axis=1, keepdims=True)
    i1 = jnp.min(jnp.where(le == v1, lane, big), axis=1, keepdims=True)
    le2 = jnp.where(lane == i1, neg, le)
    v2 = jnp.max(le2, axis=1, keepdims=True)
    i2 = jnp.min(jnp.where(le2 == v2, lane, big), axis=1, keepdims=True)
    e = jnp.exp(v2 - v1)
    w1 = gate_g / (1.0 + e)
    w2 = gate_g * e / (1.0 + e)
    route = jnp.where(lane == 0, (i1 - N_GROUPS).astype(F32),
                      jnp.where(lane == 1, (i2 - N_GROUPS).astype(F32),
                                jnp.where(lane == 2, w1, jnp.where(lane == 3, w2, 0.0))))
    route_out[...] = route


def _row_tile(n, limit):
    t = limit - limit % 16
    while t >= 16:
        if n % t == 0:
            return t
        t -= 16
    raise ValueError(f"no row tile for {n}")


def _outproj(x_all, o_mla, o_gdn, gates, w_out, gn_t, norm_ffn, w_r, b_r, row0):
    n = o_mla.shape[0]
    tm = _pick(n, (512, 256, 128, 64, 32, 16))
    assert row0 % tm == 0
    r0 = row0 // tm
    full = lambda shape: pl.BlockSpec(shape, lambda i: (0,) * len(shape))
    row = lambda w: pl.BlockSpec((tm, w), lambda i: (i, 0))
    shared = lambda w, j=0: pl.BlockSpec((tm, w), lambda i, j=j: (r0 + i, j))
    return pl.pallas_call(
        _outproj_kernel,
        out_shape=(jax.ShapeDtypeStruct((n, D_MODEL), F32),
                   jax.ShapeDtypeStruct((n, D_MODEL // 2), jnp.uint32),
                   jax.ShapeDtypeStruct((n, LANES), F32)),
        grid=(n // tm,),
        in_specs=[shared(D_MODEL), row(D_MODEL), row(D_MODEL),
                  shared(D_MODEL, P_Z // D_MODEL), shared(D_MODEL, P_GM // D_MODEL), shared(D_MODEL, P_GG // D_MODEL),
                  full((D_MODEL, D_MODEL)), full((1, D_MODEL)), full((1, D_MODEL)),
                  full((D_MODEL, LANES)), full((1, LANES))],
        out_specs=(row(D_MODEL), row(D_MODEL // 2), row(LANES)),
        compiler_params=pltpu.CompilerParams(dimension_semantics=("parallel",), vmem_limit_bytes=VMEM_LIMIT),
        name="outproj_route",
    )(x_all, o_mla, o_gdn, gates, gates, gates, w_out, gn_t, norm_ffn, w_r, b_r)


def _moe_plan(eid, wgt, nsp):
    nt = eid.shape[0]
    ts = nt // nsp
    n_asg = ts * TOP_K
    e = eid.reshape(nsp, n_asg)
    ids = jnp.broadcast_to(jnp.arange(n_asg, dtype=jnp.int32), e.shape)
    _, ids_s, w_s = lax.sort((e, ids, wgt.reshape(nsp, n_asg)), dimension=1, num_keys=1, is_stable=True)
    rows_s = (ids_s % TOP_K) * (ts + 8) + ids_s // TOP_K
    counts = jnp.sum((e[..., None] == jnp.arange(N_EXPERTS, dtype=jnp.int32)).astype(jnp.int32), axis=1)
    run_start = jnp.cumsum(counts, axis=1) - counts
    tail = lambda v, dt: jnp.full((nsp, MOE_BLOCK), v, dt)
    tok_t = jnp.concatenate([ids_s // TOP_K, tail(ts, jnp.int32)], axis=1)
    rows_t = jnp.concatenate([rows_s, tail(ts, jnp.int32)], axis=1)
    w_t = jnp.concatenate([w_s, tail(0.0, F32)], axis=1)
    as_i32 = lambda a: a.astype(jnp.int32).reshape(-1)
    scalars = (as_i32(run_start), as_i32(counts), as_i32(tok_t), as_i32(rows_t), w_t.reshape(-1))
    return scalars, ts


def _split_pieces(k, ts, n_prompt, n_sample):
    lo, hi = k * ts, (k + 1) * ts
    pieces = []
    if lo < n_prompt:
        pieces.append((0, lo, 0, min(hi, n_prompt) - lo))
    if hi > n_prompt:
        start = max(lo, n_prompt)
        pieces.append((1, start - n_prompt, start - lo, hi - start))
    assert hi <= n_prompt + n_sample
    return pieces


def _moe_kernel(start_ref, cnt_ref, tok_ref, rows_ref, w_ref, hf_p, hf_q, xmid_p, xmid_q,
                wg_ref, wu_ref, wd_ref, nfin_ref, y_p, y_q, hf_s, comb, acc, xb, yb, sem,
                *, ts, rc, nsp, n_prompt, n_sample):
    s = pl.program_id(0)
    j = pl.program_id(1)
    half = D_MODEL // 2
    stride = ts + 8

    def zero_spare_rows():
        hf_s[ts:ts + 8, :] = jnp.zeros((8, half), jnp.uint32)

    def split_copies(k, load):
        cps = []
        for src, r0, l0, n in _split_pieces(k, ts, n_prompt, n_sample):
            if load:
                cps.append(pltpu.make_async_copy((hf_p, hf_q)[src].at[pl.ds(r0, n)], hf_s.at[pl.ds(l0, n)],
                                                 sem.at[0, src]))
                cps.append(pltpu.make_async_copy((xmid_p, xmid_q)[src].at[pl.ds(r0, n)], acc.at[pl.ds(l0, n)],
                                                 sem.at[1, src]))
            else:
                cps.append(pltpu.make_async_copy(acc.at[pl.ds(l0, n)], (y_p, y_q)[src].at[pl.ds(r0, n)],
                                                 sem.at[2, src]))
        return cps

    def run_copies(load, between=None):
        for k in range(nsp):
            @pl.when(s == k)
            def _(k=k):
                cps = split_copies(k, load)
                for cp in cps:
                    cp.start()
                if between is not None:
                    between()
                for cp in cps:
                    cp.wait()

    @pl.when(j == 0)
    def _():
        run_copies(True, zero_spare_rows)

    cnt = cnt_ref[s * N_EXPERTS + j]
    first = s * (ts * TOP_K + MOE_BLOCK) + start_ref[s * N_EXPERTS + j]

    def block(p0, n):
        for r in range(n):
            xb[r:r + 1, :] = hf_s[pl.ds(tok_ref[p0 + r], 1), :]
        bits = xb[0:n, :]
        lo = lax.bitcast_convert_type(bits << 16, F32).astype(BF16)
        hi = lax.bitcast_convert_type(bits & jnp.uint32(0xFFFF0000), F32).astype(BF16)
        g = _dot(lo, wg_ref[0, :half, :]) + _dot(hi, wg_ref[0, half:, :])
        u = _dot(lo, wu_ref[0, :half, :]) + _dot(hi, wu_ref[0, half:, :])
        yb[0:n, :] = _dot((_silu(g) * u).astype(BF16), wd_ref[0])
        for r in range(n):
            comb[pl.ds(rows_ref[p0 + r], 1), :] = w_ref[p0 + r] * yb[r:r + 1, :]

    def full_block(i, carry):
        block(first + i * MOE_BLOCK, MOE_BLOCK)
        return carry

    n_full = cnt // MOE_BLOCK
    rest = cnt - n_full * MOE_BLOCK
    lax.fori_loop(0, n_full, full_block, 0)

    @pl.when(rest > MOE_BLOCK // 2)
    def _():
        block(first + n_full * MOE_BLOCK, MOE_BLOCK)

    @pl.when((rest > 0) & (rest <= MOE_BLOCK // 2))
    def _():
        block(first + n_full * MOE_BLOCK, MOE_BLOCK // 2)

    @pl.when(j == N_EXPERTS - 1)
    def _():
        def body(i, carry):
            r0 = pl.multiple_of(i * rc, 8)
            rows = pl.ds(r0, rc)
            moe = comb[rows, :]
            for k in range(1, TOP_K):
                moe = moe + comb[pl.ds(k * stride + r0, rc), :]
            acc[rows, :] = _rms(acc[rows, :] + moe, nfin_ref[...])
            return carry

        lax.fori_loop(0, ts // rc, body, 0)
        run_copies(False)


def _moe(plan, hf_p, hf_q, xmid_p, xmid_q, wg, wu, wd, norm_final):
    scalars, ts = plan
    n_prompt, n_sample = xmid_p.shape[0], xmid_q.shape[0]
    nsp = (n_prompt + n_sample) // ts
    rc = _row_tile(ts, 256) if ts % 16 == 0 else 8
    kern = functools.partial(_moe_kernel, ts=ts, rc=rc, nsp=nsp, n_prompt=n_prompt, n_sample=n_sample)
    hbm = pl.BlockSpec(memory_space=pl.ANY)
    expert = lambda shape: pl.BlockSpec((1,) + shape, lambda s, j, *_: (j, 0, 0))
    grid_spec = pltpu.PrefetchScalarGridSpec(
        num_scalar_prefetch=len(scalars),
        grid=(nsp, N_EXPERTS),
        in_specs=[
            hbm, hbm, hbm, hbm,
            expert((D_MODEL, D_EXPERT)), expert((D_MODEL, D_EXPERT)), expert((D_EXPERT, D_MODEL)),
            pl.BlockSpec((1, D_MODEL), lambda s, j, *_: (0, 0)),
        ],
        out_specs=(hbm, hbm),
        scratch_shapes=[
            pltpu.VMEM((ts + 8, D_MODEL // 2), jnp.uint32),
            pltpu.VMEM((TOP_K * (ts + 8), D_MODEL), F32),
            pltpu.VMEM((ts, D_MODEL), F32),
            pltpu.VMEM((MOE_BLOCK, D_MODEL // 2), jnp.uint32),
            pltpu.VMEM((MOE_BLOCK, D_MODEL), F32),
            pltpu.SemaphoreType.DMA((3, 2)),
        ],
    )
    return pl.pallas_call(
        kern,
        out_shape=(jax.ShapeDtypeStruct((n_prompt, D_MODEL), F32), jax.ShapeDtypeStruct((n_sample, D_MODEL), F32)),
        grid_spec=grid_spec,
        compiler_params=pltpu.CompilerParams(
            dimension_semantics=("arbitrary", "arbitrary"), vmem_limit_bytes=VMEM_LIMIT),
        name="moe",
    )(*scalars, hf_p, hf_q, xmid_p, xmid_q, wg, wu, wd, norm_final)


def _pack_w_in(w):
    kr = w[:, _OFF_KV + KV_LORA:_OFF_QKV]
    kr_sw = jnp.concatenate([kr[:, ROPE_DIM // 2:], kr[:, :ROPE_DIM // 2]], axis=1)
    small = jnp.concatenate([w[:, _OFF_B:_OFF_A], w[:, _OFF_A:_OFF_GM],
                             jnp.zeros((D_MODEL, SM_KR - 2 * GDN_HEADS), w.dtype), kr, kr_sw], axis=1)
    packed = jnp.concatenate([w[:, _OFF_QKV:_OFF_Z], w[:, _OFF_KV:_OFF_KV + KV_LORA], small, w[:, :Q_LORA],
                              w[:, _OFF_Z:_OFF_B], w[:, _OFF_GM:_OFF_GG], w[:, _OFF_GG:]], axis=1)
    return packed.astype(BF16)


def _pack_mla_weights(w_uq, w_uk, w_uv):
    zq = jnp.zeros((Q_LORA, MLA_HEADS, LANES - NOPE_DIM - ROPE_DIM), w_uq.dtype)
    wq = jnp.concatenate([w_uq, zq], axis=2).reshape(Q_LORA, MLA_HEADS * LANES)
    rope = w_uq[:, :, NOPE_DIM:]
    rope_sw = jnp.concatenate([rope[..., ROPE_DIM // 2:], rope[..., :ROPE_DIM // 2]], axis=2)
    wqs = jnp.concatenate([jnp.zeros((Q_LORA, MLA_HEADS, NOPE_DIM), w_uq.dtype), rope_sw, zq], axis=2)
    wqs = wqs.reshape(Q_LORA, MLA_HEADS * LANES)
    wk = jnp.concatenate([w_uk, jnp.zeros((KV_LORA, MLA_HEADS, LANES - NOPE_DIM), w_uk.dtype)], axis=2)
    wk = wk.reshape(KV_LORA, MLA_HEADS * LANES)
    zv = jnp.zeros((KV_LORA, MLA_HEADS // 2, V_DIM), w_uv.dtype)
    wv = jnp.stack([jnp.concatenate([w_uv[:, 0::2], zv], axis=2),
                    jnp.concatenate([zv, w_uv[:, 1::2]], axis=2)], axis=2)
    wv = wv.reshape(KV_LORA, MLA_HEADS * LANES)
    return wq.astype(BF16), wqs.astype(BF16), wk.astype(BF16), wv.astype(BF16)


def _rope_tables(pos):
    inv_freq = ROPE_THETA ** (-jnp.arange(0, ROPE_DIM, 2, dtype=F32) / ROPE_DIM)
    ang = pos.astype(F32)[:, None] * inv_freq[None, :]
    cos, sin = jnp.cos(ang), jnp.sin(ang)
    n = pos.shape[0]
    cos_t = jnp.concatenate([jnp.ones((n, NOPE_DIM), F32), cos, cos, jnp.zeros((n, ROPE_DIM), F32)], axis=1)
    sin_t = jnp.concatenate([jnp.zeros((n, NOPE_DIM), F32), -sin, sin, jnp.zeros((n, ROPE_DIM), F32)], axis=1)
    return cos_t, sin_t


def _head_lanes(v):
    return jnp.zeros((1, LANES), F32).at[0, SM_A:SM_A + GDN_HEADS].set(v.astype(F32))


def _moe_splits(nt):
    for nsp in (6, 3, 4, 2, 1):
        if nt % (nsp * 8) == 0:
            return nsp
    return 1


def kernel(x_prompt, x_sample, cache_kv_latent, cache_k_rope, page_table, state_conv, state_gdn, meta_tokens,
           norm_mix, w_in, q_norm, w_uq, kv_norm, w_uk, w_uv, conv_w, a_log, dt_bias, gdn_norm, w_out, norm_ffn,
           w_group, b_group, w_router, b_router, w_gate, w_up, w_down, norm_final):
    batch, seq, _ = x_prompt.shape
    ns, dec_seq, _ = x_sample.shape
    assert dec_seq == 1 and w_in.shape[0] == 1 and seq % CHUNK == 0
    n_pages = page_table.shape[1]
    n_prompt = batch * seq
    nt = n_prompt + ns
    meta_row = nt
    tq = _pick(seq, ATTN_TQ)
    n_rows = -(-(nt + N_META) // tq) * tq
    assert n_prompt % ns == 0 and nt % N_META == 0 and ns % LANES == 0 and meta_row % tq + N_META <= tq

    x_all = jnp.concatenate([x_prompt.reshape(n_prompt, D_MODEL), x_sample.reshape(ns, D_MODEL),
                             meta_tokens.astype(x_prompt.dtype),
                             jnp.zeros((n_rows - nt - N_META, D_MODEL), x_prompt.dtype)], axis=0)
    pos = jnp.concatenate([N_META + jnp.arange(seq), jnp.full((1,), n_pages * PAGE_SIZE), jnp.arange(N_META)])
    by_row = lambda t: jnp.concatenate([jnp.tile(t[:seq], (batch, 1)), jnp.broadcast_to(t[seq], (ns, LANES)),
                                        t[seq + 1:], jnp.zeros((n_rows - nt - N_META, LANES), F32)], axis=0)
    cos_t, sin_t = (by_row(t) for t in _rope_tables(pos))
    w_packed = _pack_w_in(w_in[0])
    wq, wqs, wk, wv = _pack_mla_weights(w_uq[0], w_uk[0], w_uv[0])
    wukt = jnp.concatenate([jnp.transpose(w_uk[0], (1, 2, 0)),
                            jnp.zeros((MLA_HEADS, LANES - NOPE_DIM, KV_LORA), w_uk.dtype)], axis=1).astype(BF16)
    alog_v, dtb_v = _head_lanes(a_log[0]), _head_lanes(dt_bias[0])
    cw = conv_w[0].astype(F32)
    gn_t = jnp.tile(gdn_norm[0].astype(F32), GDN_HEADS)[None]
    w_r = jnp.concatenate([w_group[0], w_router[0],
                           jnp.zeros((D_MODEL, LANES - N_GROUPS - N_EXPERTS), w_group.dtype)], axis=1).astype(BF16)
    b_r = jnp.concatenate([b_group[0], b_router[0], jnp.zeros((LANES - N_GROUPS - N_EXPERTS,), b_group.dtype)])[None]
    wg, wu, wd = w_gate[0].astype(BF16), w_up[0].astype(BF16), w_down[0].astype(BF16)

    proj, gates = _inproj(x_all, norm_mix[0][None].astype(F32), w_packed)
    qt, k, vt, ckv, krot = _mla_prep(proj, cos_t, sin_t, q_norm[0][None].astype(F32), kv_norm[0][None].astype(F32),
                                     wq.T, wqs.T, wk, wv.T, tq)

    o_mla_p = _attn_prompt(qt, k, vt, batch, seq, meta_row, tq)
    ql, qrt = _q_absorb(qt, wukt, ns, n_prompt)
    o_lat = _mla_sample(page_table, jnp.transpose(ql, (1, 0, 2)), jnp.transpose(qrt, (2, 0, 1)),
                        ckv[n_prompt:nt].reshape(ns, 1, KV_LORA), krot[n_prompt:nt].reshape(ns, 1, LANES),
                        cache_kv_latent[0], jnp.swapaxes(cache_k_rope[0], 1, 2))
    o_mla_s = _o_proj_sample(jnp.transpose(o_lat, (1, 0, 2)), wv)

    s_meta = _gdn_meta(proj, cw, alog_v, dtb_v, meta_row)
    terms = _gdn_terms(proj, cw, alog_v, dtb_v, batch, seq, meta_row)
    o_gdn_p, gdn_p = _gdn_scan(*terms, s_meta, batch, seq)
    o_gdn_s, gdn_s = _gdn_sample(proj, state_conv[0], state_gdn[0], cw, alog_v, dtb_v, n_prompt)

    tail = (gates, w_out[0].astype(BF16), gn_t, norm_ffn[0][None].astype(F32), w_r, b_r.astype(F32))
    xmid_p, hf_p, route_p = _outproj(x_all, o_mla_p, o_gdn_p, *tail, 0)
    xmid_s, hf_s, route_s = _outproj(x_all, o_mla_s, o_gdn_s, *tail, n_prompt)
    route = jnp.concatenate([route_p[:, :2 * TOP_K], route_s[:, :2 * TOP_K]], axis=0)
    plan = _moe_plan(route[:, :TOP_K].astype(jnp.int32), route[:, TOP_K:], _moe_splits(nt))
    y_p, y_s = _moe(plan, hf_p, hf_s, xmid_p, xmid_s, wg, wu, wd, norm_final[None].astype(F32))

    def with_meta(rows, width):
        meta = jnp.broadcast_to(rows[meta_row:meta_row + N_META][None], (batch, N_META, width))
        return jnp.concatenate([meta, rows[:n_prompt].reshape(batch, seq, width)], axis=1)[None]

    k_rope = krot[:, SM_KR:SM_KR + ROPE_DIM]
    conv_p = jnp.stack([proj[(b + 1) * seq - (CONV_W - 1):(b + 1) * seq, :GDN_QKV] for b in range(batch)])
    conv_s = jnp.concatenate([state_conv[0][:, 1:].astype(F32), proj[n_prompt:nt, None, :GDN_QKV]], axis=1)
    return (y_p.reshape(batch, seq, D_MODEL), y_s.reshape(ns, 1, D_MODEL),
            with_meta(ckv, KV_LORA), with_meta(k_rope, ROPE_DIM),
            ckv[n_prompt:nt].reshape(1, ns, 1, KV_LORA), k_rope[n_prompt:nt].reshape(1, ns, 1, ROPE_DIM),
            conv_p[None], conv_s[None], gdn_p[None], gdn_s[None])
```

```python
import functools

import jax
import jax.numpy as jnp
from jax import lax
from jax.experimental import pallas as pl
from jax.experimental.pallas import tpu as pltpu

F32 = jnp.float32
BF16 = jnp.bfloat16
HIGHEST = lax.Precision.HIGHEST

D_MODEL = 1024
N_META = 16
RMS_EPS = 1e-6
MLA_HEADS = 16
Q_LORA = 384
KV_LORA = 256
NOPE_DIM = 64
ROPE_DIM = 32
V_DIM = 64
ROPE_THETA = 10000.0
MLA_SCALE = (NOPE_DIM + ROPE_DIM) ** -0.5
PAGE_SIZE = 128
GDN_HEADS = 8
GDN_DK = 128
GDN_DV = 128
GDN_KEY = GDN_HEADS * GDN_DK
GDN_QKV = 3 * GDN_KEY
CONV_W = 4
CHUNK = 64
N_GROUPS = 4
EXPERTS_PER_GROUP = 8
N_EXPERTS = 32
TOP_K = 2
D_EXPERT = 256
MOE_BLOCK = 128

_OFF_KV = Q_LORA
_OFF_QKV = _OFF_KV + KV_LORA + ROPE_DIM
_OFF_Z = _OFF_QKV + GDN_QKV
_OFF_B = _OFF_Z + GDN_KEY
_OFF_A = _OFF_B + GDN_HEADS
_OFF_GM = _OFF_A + GDN_HEADS
_OFF_GG = _OFF_GM + D_MODEL
P_QKV = 0
P_KVC = 3072
P_SMALL = 3328
P_QD = 3456
P_F32 = 3840
P_Z = 0
P_GM = 1024
P_GG = 2048
P_BF16 = 3072
P_TOTAL = P_F32 + P_BF16
INPROJ_TN = 768
SM_B = 0
SM_A = 8
SM_KR = 64

LANES = 128
VMEM_LIMIT = 56 * 1024 * 1024
ATTN_TQ = (512, 256, 128)


def _pick(n, candidates):
    for c in candidates:
        if n % c == 0:
            return c
    raise ValueError(f"no tile for {n} in {candidates}")


def _dot(a, b):
    return jnp.dot(a, b, preferred_element_type=F32)


def _dot_nt(a, b):
    return lax.dot_general(a, b, (((1,), (1,)), ((), ())), preferred_element_type=F32)


def _dot_tn(a, b):
    return lax.dot_general(a, b, (((0,), (0,)), ((), ())), preferred_element_type=F32)


def _sigmoid(x):
    return 1.0 / (1.0 + jnp.exp(-x))


def _silu(x):
    return x * _sigmoid(x)


def _softplus(x):
    return jnp.maximum(x, 0.0) + jnp.log1p(jnp.exp(-jnp.abs(x)))


def _rms(x, w):
    return x * lax.rsqrt(jnp.mean(x * x, axis=-1, keepdims=True) + RMS_EPS) * w


def _inproj_kernel(x_ref, nw_ref, w_ref, of_ref, ob_ref, hn_ref):
    j = pl.program_id(1)
    nf = P_F32 // INPROJ_TN

    @pl.when(j == 0)
    def _():
        hn_ref[...] = _rms(x_ref[...], nw_ref[...]).astype(BF16)

    r = _dot(hn_ref[...], w_ref[...])

    @pl.when(j < nf)
    def _():
        of_ref[...] = r

    @pl.when(j >= nf)
    def _():
        ob_ref[...] = r.astype(BF16)


def _inproj(x_all, norm_w, w_packed):
    r = x_all.shape[0]
    tm = _pick(r, (1536, 1280, 768, 640, 512, 256, 128))
    tn = INPROJ_TN
    nf = P_F32 // tn
    return pl.pallas_call(
        _inproj_kernel,
        out_shape=(jax.ShapeDtypeStruct((r, P_F32), F32), jax.ShapeDtypeStruct((r, P_BF16), BF16)),
        grid=(r // tm, P_TOTAL // tn),
        in_specs=[
            pl.BlockSpec((tm, D_MODEL), lambda i, j: (i, 0)),
            pl.BlockSpec((1, D_MODEL), lambda i, j: (0, 0)),
            pl.BlockSpec((D_MODEL, tn), lambda i, j: (0, j)),
        ],
        out_specs=(pl.BlockSpec((tm, tn), lambda i, j: (i, jnp.minimum(j, nf - 1))),
                   pl.BlockSpec((tm, tn), lambda i, j: (i, jnp.maximum(j - nf, 0)))),
        scratch_shapes=[pltpu.VMEM((tm, D_MODEL), BF16)],
        compiler_params=pltpu.CompilerParams(
            dimension_semantics=("parallel", "arbitrary"), vmem_limit_bytes=VMEM_LIMIT),
        name="inproj",
    )(x_all, norm_w, w_packed)


LOG2E = 1.4426950408889634


def _mla_prep_kernel(qd_ref, kvc_ref, sm_ref, c_ref, s_ref, qn_ref, kvn_ref, wqt_ref, wqst_ref, wk_ref, wvt_ref,
                     qt_out, k_out, vt_out, ckv_out, kr_out):
    cos = c_ref[...]
    sin = s_ref[...]
    cos_t, sin_t = cos.T, sin.T
    qn_t = _rms(qd_ref[...], qn_ref[...]).T.astype(BF16)
    qt = _dot(wqt_ref[...], qn_t)
    qst = _dot(wqst_ref[...], qn_t)
    for h in range(MLA_HEADS):
        sl = slice(h * LANES, (h + 1) * LANES)
        qt_out[h, 0] = ((qt[sl] * cos_t + qst[sl] * sin_t) * (MLA_SCALE * LOG2E)).astype(BF16)
    ckv = _rms(kvc_ref[...], kvn_ref[...])
    ckv_out[...] = ckv
    sm = sm_ref[...]
    lane = lax.broadcasted_iota(jnp.int32, sm.shape, 1)
    cos_k = jnp.where((lane >= SM_KR) & (lane < SM_KR + ROPE_DIM), cos, 0.0)
    krot = sm * cos_k + pltpu.roll(sm, LANES - ROPE_DIM, 1) * sin
    kr_out[...] = krot
    kk = _dot(ckv.astype(BF16), wk_ref[...])
    vvt = _dot(wvt_ref[...], ckv.T.astype(BF16))
    for h in range(MLA_HEADS):
        sl = slice(h * LANES, (h + 1) * LANES)
        k_out[h] = (kk[:, sl] + krot).astype(BF16)
        vt_out[h, 0] = vvt[sl].astype(BF16)


def _mla_prep(proj, cos_t, sin_t, q_norm, kv_norm, wqt, wqst, wk, wvt, tm):
    r = proj.shape[0]
    hw = MLA_HEADS * LANES
    full = lambda shape: pl.BlockSpec(shape, lambda i: (0,) * len(shape))
    return pl.pallas_call(
        _mla_prep_kernel,
        out_shape=(
            jax.ShapeDtypeStruct((MLA_HEADS, r // tm, LANES, tm), BF16),
            jax.ShapeDtypeStruct((MLA_HEADS, r, LANES), BF16),
            jax.ShapeDtypeStruct((MLA_HEADS, r // tm, LANES, tm), BF16),
            jax.ShapeDtypeStruct((r, KV_LORA), F32),
            jax.ShapeDtypeStruct((r, LANES), F32),
        ),
        grid=(r // tm,),
        in_specs=[
            pl.BlockSpec((tm, Q_LORA), lambda i: (i, P_QD // Q_LORA)),
            pl.BlockSpec((tm, KV_LORA), lambda i: (i, P_KVC // KV_LORA)),
            pl.BlockSpec((tm, LANES), lambda i: (i, P_SMALL // LANES)),
            pl.BlockSpec((tm, LANES), lambda i: (i, 0)),
            pl.BlockSpec((tm, LANES), lambda i: (i, 0)),
            full((1, Q_LORA)), full((1, KV_LORA)),
            full((hw, Q_LORA)), full((hw, Q_LORA)), full((KV_LORA, hw)), full((hw, KV_LORA)),
        ],
        out_specs=(pl.BlockSpec((MLA_HEADS, 1, LANES, tm), lambda i: (0, i, 0, 0)),
                   pl.BlockSpec((MLA_HEADS, tm, LANES), lambda i: (0, i, 0)),
                   pl.BlockSpec((MLA_HEADS, 1, LANES, tm), lambda i: (0, i, 0, 0)),
                   pl.BlockSpec((tm, KV_LORA), lambda i: (i, 0)),
                   pl.BlockSpec((tm, LANES), lambda i: (i, 0))),
        compiler_params=pltpu.CompilerParams(dimension_semantics=("parallel",), vmem_limit_bytes=VMEM_LIMIT),
        name="mla_prep",
    )(proj, proj, proj, cos_t, sin_t, q_norm, kv_norm, wqt, wqst, wk, wvt)


def _attn_prompt_kernel(qt_ref, k_ref, vt_ref, km_ref, vmt_ref, o_ref, *, tq, meta_lane, nq, tiles_per_step):
    i = pl.program_id(2)
    tiles = (i,) if tiles_per_step == 1 else (i, nq - 1 - i)
    streams = [(t, h) for t in tiles for h in (0, 1)]
    half = tq // 2
    qt = [qt_ref[h, t] for t, h in streams]

    def update(qts, k_blk, vt_blk, carry, mask):
        ms, ls, accs = carry
        n = range(len(qts))
        s = [_dot(k_blk[x], qts[x]) for x in n]
        if mask is not None:
            s = [jnp.where(mask, v, -1e30) for v in s]
        m_new = [jnp.maximum(ms[x], jnp.max(s[x], axis=0, keepdims=True)) for x in n]
        a = [jnp.exp2(ms[x] - m_new[x]) for x in n]
        p = [jnp.exp2(s[x] - m_new[x]) for x in n]
        l_new = [a[x] * ls[x] + jnp.sum(p[x], axis=0, keepdims=True) for x in n]
        acc_new = [a[x] * accs[x] + _dot(vt_blk[x], p[x].astype(BF16)) for x in n]
        return m_new, l_new, acc_new

    s0 = [_dot(km_ref[h], q) for (_, h), q in zip(streams, qt)]
    m = [jnp.max(v, axis=0, keepdims=True) for v in s0]
    p0 = [jnp.exp2(v - mx) for v, mx in zip(s0, m)]
    l = [jnp.sum(v, axis=0, keepdims=True) for v in p0]
    acc = [_dot(vmt_ref[h, 0][:, meta_lane:meta_lane + N_META], v.astype(BF16)) for (_, h), v in zip(streams, p0)]

    for ti, t in enumerate(tiles):
        own = slice(2 * ti, 2 * ti + 2)

        def body(j, carry, own=own):
            rows = pl.ds(pl.multiple_of(j * tq, tq), tq)
            return update(qt[own], [k_ref[h, rows, :] for h in (0, 1)], [vt_ref[h, j] for h in (0, 1)], carry, None)

        m[own], l[own], acc[own] = lax.fori_loop(0, t, body, (m[own], l[own], acc[own]))

    key = lax.broadcasted_iota(jnp.int32, (half, tq), 0)
    qry = lax.broadcasted_iota(jnp.int32, (half, tq), 1)
    off = [pl.multiple_of(t * tq, tq) for t, _ in streams]
    vt_d = [vt_ref[h, t] for t, h in streams]
    m, l, acc = update(qt, [k_ref[h, pl.ds(o, half), :] for (_, h), o in zip(streams, off)],
                       [v[:, :half] for v in vt_d], (m, l, acc), key <= qry)
    late = lambda xs: [v[:, half:] for v in xs]
    mb, lb, accb = update(late(qt), [k_ref[h, pl.ds(pl.multiple_of(o + half, half), half), :]
                                     for (_, h), o in zip(streams, off)],
                          late(vt_d), (late(m), late(l), late(acc)), (key <= qry)[:, :half])
    ot = [jnp.concatenate([acc[x][:, :half] / l[x][:, :half], accb[x] / lb[x]], axis=1) for x in range(len(streams))]
    for ti, t in enumerate(tiles):
        o_ref[pl.ds(pl.multiple_of(t * tq, tq), tq), :] = (ot[2 * ti] + ot[2 * ti + 1]).T.astype(o_ref.dtype)


def _attn_prompt(qt, k, vt, batch, seq, meta_row, tq):
    nq = seq // tq
    tps = 2 if nq % 2 == 0 else 1
    kern = functools.partial(_attn_prompt_kernel, tq=tq, meta_lane=meta_row % tq, nq=nq, tiles_per_step=tps)
    tiled = lambda: pl.BlockSpec((2, nq, LANES, tq), lambda b, p, i: (p, b, 0, 0))
    return pl.pallas_call(
        kern,
        out_shape=jax.ShapeDtypeStruct((batch * seq, D_MODEL), BF16),
        grid=(batch, MLA_HEADS // 2, nq // tps),
        in_specs=[
            tiled(),
            pl.BlockSpec((2, seq, LANES), lambda b, p, i: (p, b, 0)),
            tiled(),
            pl.BlockSpec((2, N_META, LANES), lambda b, p, i: (p, meta_row // N_META, 0)),
            pl.BlockSpec((2, 1, LANES, tq), lambda b, p, i: (p, meta_row // tq, 0, 0)),
        ],
        out_specs=pl.BlockSpec((seq, LANES), lambda b, p, i: (b, p)),
        compiler_params=pltpu.CompilerParams(
            dimension_semantics=("parallel", "parallel", "arbitrary"), vmem_limit_bytes=VMEM_LIMIT),
        name="attn_prompt",
    )(qt, k, vt, k, vt)


def _gate_lanes(sm, alog_ref, dtb_ref):
    g = -jnp.exp(alog_ref[...]) * _softplus(sm + dtb_ref[...])
    beta = _sigmoid(sm)
    return g, beta


def _qkv_heads(xc):
    xf = _silu(xc)
    qs, ks, vs = [], [], []
    for h in range(GDN_HEADS):
        q = xf[:, h * GDN_DK:(h + 1) * GDN_DK]
        k = xf[:, GDN_KEY + h * GDN_DK:GDN_KEY + (h + 1) * GDN_DK]
        qs.append(q * lax.rsqrt(jnp.sum(q * q, axis=-1, keepdims=True) + RMS_EPS) * (GDN_DK ** -0.5))
        ks.append(k * lax.rsqrt(jnp.sum(k * k, axis=-1, keepdims=True) + RMS_EPS))
        vs.append(xf[:, 2 * GDN_KEY + h * GDN_DV:2 * GDN_KEY + (h + 1) * GDN_DV])
    return qs, ks, vs


def _split_bf16(x):
    hi = x.astype(BF16)
    return hi, (x - hi.astype(F32)).astype(BF16)


def _dot_split(a, b):
    return _dot(a[0], b[0]) + (_dot(a[0], b[1]) + _dot(a[1], b[0]))


def _unit_lower_inverses(mats, c):
    row = lax.broadcasted_iota(jnp.int32, (c, c), 0)
    col = lax.broadcasted_iota(jnp.int32, (c, c), 1)
    eye = jnp.where(row == col, 1.0, 0.0)
    ps = [-a for a in mats]
    ts = [eye + p for p in ps]
    span = 2
    while span < c:
        psp = [_split_bf16(p) for p in ps]
        ps = [_dot_split(p, p) for p in psp]
        psp = [_split_bf16(p) for p in ps]
        ts = [t + _dot_split(p, _split_bf16(t)) for p, t in zip(psp, ts)]
        span *= 2
    return ts


def _gdn_chunk_terms(xs, sm, conv_ref, alog_ref, dtb_ref, c):
    heads = range(GDN_HEADS)
    xc = xs[0] * conv_ref[0:1, :]
    for j in range(1, CONV_W):
        xc = xc + xs[j] * conv_ref[j:j + 1, :]
    qs, ks, vs = _qkv_heads(xc)
    g, beta = _gate_lanes(sm, alog_ref, dtb_ref)
    row = lax.broadcasted_iota(jnp.int32, (c, c), 0)
    col = lax.broadcasted_iota(jnp.int32, (c, c), 1)
    causal = col <= row
    strict = col < row
    gcum = jnp.dot(jnp.where(causal, 1.0, 0.0), g, precision=HIGHEST, preferred_element_type=F32)
    gcum_t = lax.dot_general(g, jnp.where(col >= row, 1.0, 0.0), (((0,), (0,)), ((), ())),
                             precision=HIGHEST, preferred_element_type=F32)
    gc = [gcum[:, SM_A + h:SM_A + h + 1] for h in heads]
    gr = [gcum_t[SM_A + h:SM_A + h + 1, :] for h in heads]
    bc = [beta[:, SM_B + h:SM_B + h + 1] for h in heads]
    decay = [jnp.where(causal, jnp.exp(jnp.where(causal, gc[h] - gr[h], 0.0)), 0.0) for h in heads]
    kb = [ks[h] * bc[h] for h in heads]
    kbf = [ks[h].astype(BF16) for h in heads]
    a = [jnp.where(strict, _dot_nt(kb[h].astype(BF16), kbf[h]) * decay[h], 0.0) for h in heads]
    t = _unit_lower_inverses(a, c)
    eg = [jnp.exp(gc[h]) for h in heads]
    sol = [_dot(t[h].astype(BF16), jnp.concatenate([vs[h] * bc[h], kb[h] * eg[h]], axis=1).astype(BF16))
           for h in heads]
    u = [sol[h][:, :GDN_DV] for h in heads]
    w = [sol[h][:, GDN_DV:] for h in heads]
    attn = [jnp.where(causal, _dot_nt(qs[h].astype(BF16), kbf[h]) * decay[h], 0.0) for h in heads]
    qg = [qs[h] * eg[h] for h in heads]
    g_last = gcum[c - 1:c, :]
    kd = [ks[h] * jnp.exp(g_last[:, SM_A + h:SM_A + h + 1] - gc[h]) for h in heads]
    return u, w, qg, kd, attn, jnp.exp(g_last)


def _conv_shifts(hist, x):
    xe = jnp.concatenate([hist, x], axis=0)
    return [pltpu.roll(xe, d, 0)[8:] for d in range(CONV_W - 1, 0, -1)] + [x]


def _gdn_meta_kernel(x_ref, sm_ref, conv_ref, alog_ref, dtb_ref, s_out):
    xs = _conv_shifts(jnp.zeros((8, GDN_QKV), F32), x_ref[...])
    u, _, _, kd, _, _ = _gdn_chunk_terms(xs, sm_ref[...], conv_ref, alog_ref, dtb_ref, N_META)
    for h in range(GDN_HEADS):
        s_out[h] = _dot_tn(kd[h].astype(BF16), u[h].astype(BF16))


def _gdn_meta(proj, conv_w, alog_v, dtb_v, meta_row):
    full = lambda shape: pl.BlockSpec(shape, lambda i: (0,) * len(shape))
    return pl.pallas_call(
        _gdn_meta_kernel,
        out_shape=jax.ShapeDtypeStruct((GDN_HEADS, GDN_DK, GDN_DV), F32),
        grid=(1,),
        in_specs=[
            pl.BlockSpec((N_META, GDN_QKV), lambda i: (meta_row // N_META, 0)),
            pl.BlockSpec((N_META, LANES), lambda i: (meta_row // N_META, P_SMALL // LANES)),
            full((CONV_W, GDN_QKV)), full((1, LANES)), full((1, LANES)),
        ],
        out_specs=full((GDN_HEADS, GDN_DK, GDN_DV)),
        compiler_params=pltpu.CompilerParams(vmem_limit_bytes=VMEM_LIMIT),
        name="gdn_meta",
    )(proj, proj, conv_w, alog_v, dtb_v)


def _gdn_terms_kernel(x_ref, hist_ref, sm_ref, conv_ref, alog_ref, dtb_ref,
                      u_out, w_out, qg_out, kd_out, attn_out, dec_out, *, cps):
    c = CHUNK
    hist = hist_ref[...]
    for cc in range(cps):
        rows = slice(cc * c, (cc + 1) * c)
        x = x_ref[rows, :]
        u, w, qg, kd, attn, dec = _gdn_chunk_terms(_conv_shifts(hist, x), sm_ref[rows, :], conv_ref, alog_ref,
                                                   dtb_ref, c)
        for h in range(GDN_HEADS):
            sl = slice(h * GDN_DV, (h + 1) * GDN_DV)
            u_out[rows, sl] = u[h]
            w_out[rows, sl] = w[h].astype(BF16)
            qg_out[rows, sl] = qg[h].astype(BF16)
            kd_out[rows, sl] = kd[h].astype(BF16)
            attn_out[h, rows, :] = attn[h].astype(BF16)
        dec_out[cc] = dec
        hist = x[c - 8:]


def _gdn_terms(proj, conv_w, alog_v, dtb_v, batch, seq, meta_row):
    nc = seq // CHUNK
    n = batch * seq
    cps = _pick(nc, (2, 1))
    step = cps * CHUNK
    spb = nc // cps
    full = lambda shape: pl.BlockSpec(shape, lambda i: (0,) * len(shape))
    rows = lambda: pl.BlockSpec((step, D_MODEL), lambda i: (i, 0))

    def hist_index(i):
        return (jnp.where(i % spb == 0, (meta_row + N_META) // 8, i * (step // 8)) - 1, 0)

    return pl.pallas_call(
        functools.partial(_gdn_terms_kernel, cps=cps),
        out_shape=(jax.ShapeDtypeStruct((n, D_MODEL), F32),
                   jax.ShapeDtypeStruct((n, D_MODEL), BF16),
                   jax.ShapeDtypeStruct((n, D_MODEL), BF16),
                   jax.ShapeDtypeStruct((n, D_MODEL), BF16),
                   jax.ShapeDtypeStruct((GDN_HEADS, n, CHUNK), BF16),
                   jax.ShapeDtypeStruct((batch * nc, 1, LANES), F32)),
        grid=(batch * spb,),
        in_specs=[
            pl.BlockSpec((step, GDN_QKV), lambda i: (i, 0)),
            pl.BlockSpec((8, GDN_QKV), hist_index),
            pl.BlockSpec((step, LANES), lambda i: (i, P_SMALL // LANES)),
            full((CONV_W, GDN_QKV)), full((1, LANES)), full((1, LANES)),
        ],
        out_specs=(rows(), rows(), rows(), rows(),
                   pl.BlockSpec((GDN_HEADS, step, CHUNK), lambda i: (0, i, 0)),
                   pl.BlockSpec((cps, 1, LANES), lambda i: (i, 0, 0))),
        compiler_params=pltpu.CompilerParams(dimension_semantics=("parallel",), vmem_limit_bytes=VMEM_LIMIT),
        name="gdn_terms",
    )(proj, proj, proj, conv_w, alog_v, dtb_v)


def _gdn_scan_kernel(u_ref, w_ref, qg_ref, kd_ref, attn_ref, dec_ref, s0_ref, o_ref, s_out, st_ref, *, cpg):
    c = CHUNK
    heads = range(GDN_HEADS)

    @pl.when(pl.program_id(1) == 0)
    def _():
        st_ref[...] = s0_ref[...]

    def chunk(ci, carry):
        rows = pl.ds(pl.multiple_of(ci * c, c), c)
        dec = dec_ref[ci]
        sl = [slice(h * GDN_DV, (h + 1) * GDN_DV) for h in heads]
        s_old = [st_ref[h] for h in heads]
        sb = [s.astype(BF16) for s in s_old]
        lhs = [jnp.concatenate([w_ref[rows, sl[h]], qg_ref[rows, sl[h]]], axis=0) for h in heads]
        r = [_dot(lhs[h], sb[h]) for h in heads]
        vnb = [(u_ref[rows, sl[h]] - r[h][:c]).astype(BF16) for h in heads]
        out = [r[h][c:] + _dot(attn_ref[h, rows, :], vnb[h]) for h in heads]
        upd = [_dot_tn(kd_ref[rows, sl[h]], vnb[h]) for h in heads]
        for h in heads:
            o_ref[rows, sl[h]] = out[h].astype(o_ref.dtype)
            st_ref[h] = s_old[h] * dec[:, SM_A + h:SM_A + h + 1] + upd[h]
        return carry

    lax.fori_loop(0, cpg, chunk, 0)

    @pl.when(pl.program_id(1) == pl.num_programs(1) - 1)
    def _():
        s_out[0] = st_ref[...]


def _gdn_scan(u, w, qg, kd, attn, dec, s_meta, batch, seq):
    nc = seq // CHUNK
    cpg = _pick(nc, (8, 4, 2, 1))
    ng = nc // cpg
    rows = lambda: pl.BlockSpec((cpg * CHUNK, D_MODEL), lambda b, g: (b * ng + g, 0))
    kern = functools.partial(_gdn_scan_kernel, cpg=cpg)
    return pl.pallas_call(
        kern,
        out_shape=(jax.ShapeDtypeStruct((batch * seq, D_MODEL), BF16),
                   jax.ShapeDtypeStruct((batch, GDN_HEADS, GDN_DK, GDN_DV), F32)),
        grid=(batch, ng),
        in_specs=[rows(), rows(), rows(), rows(),
                  pl.BlockSpec((GDN_HEADS, cpg * CHUNK, CHUNK), lambda b, g: (0, b * ng + g, 0)),
                  pl.BlockSpec((cpg, 1, LANES), lambda b, g: (b * ng + g, 0, 0)),
                  pl.BlockSpec((GDN_HEADS, GDN_DK, GDN_DV), lambda b, g: (0, 0, 0))],
        out_specs=(rows(),
                   pl.BlockSpec((1, GDN_HEADS, GDN_DK, GDN_DV), lambda b, g: (b, 0, 0, 0))),
        scratch_shapes=[pltpu.VMEM((GDN_HEADS, GDN_DK, GDN_DV), F32)],
        compiler_params=pltpu.CompilerParams(
            dimension_semantics=("parallel", "arbitrary"), vmem_limit_bytes=VMEM_LIMIT),
        name="gdn_scan",
    )(u, w, qg, kd, attn, dec, s_meta)


def _gdn_sample_kernel(x_ref, sm_ref, cs_ref, st_ref, conv_ref, alog_ref, dtb_ref, o_ref, s_out, *, nb):
    xc = x_ref[...] * conv_ref[CONV_W - 1:CONV_W, :]
    for j in range(CONV_W - 1):
        xc = xc + cs_ref[:, j, :] * conv_ref[j:j + 1, :]
    qs, ks, vs = _qkv_heads(xc)
    g, beta = _gate_lanes(sm_ref[...], alog_ref, dtb_ref)
    eg = jnp.exp(g)
    for h in range(GDN_HEADS):
        q_t = qs[h].T
        k_t = ks[h].T
        for b in range(nb):
            kcol = k_t[:, b:b + 1]
            s1 = st_ref[b, h] * eg[b:b + 1, SM_A + h:SM_A + h + 1]
            r = jnp.sum(s1 * kcol, axis=0, keepdims=True)
            delta = (vs[h][b:b + 1, :] - r) * beta[b:b + 1, SM_B + h:SM_B + h + 1]
            s2 = s1 + kcol * delta
            s_out[b, h] = s2
            o_ref[b:b + 1, h * GDN_DV:(h + 1) * GDN_DV] = jnp.sum(s2 * q_t[:, b:b + 1], axis=0, keepdims=True)


def _gdn_sample(proj, state_conv, state_gdn, conv_w, alog_v, dtb_v, row0):
    ns = state_gdn.shape[0]
    nb = 8
    full = lambda shape: pl.BlockSpec(shape, lambda i: (0,) * len(shape))
    kern = functools.partial(_gdn_sample_kernel, nb=nb)
    return pl.pallas_call(
        kern,
        out_shape=(jax.ShapeDtypeStruct((ns, D_MODEL), F32),
                   jax.ShapeDtypeStruct(state_gdn.shape, F32)),
        grid=(ns // nb,),
        in_specs=[
            pl.BlockSpec((nb, GDN_QKV), lambda i: (row0 // nb + i, 0)),
            pl.BlockSpec((nb, LANES), lambda i: (row0 // nb + i, P_SMALL // LANES)),
            pl.BlockSpec((nb, CONV_W - 1, GDN_QKV), lambda i: (i, 0, 0)),
            pl.BlockSpec((nb, GDN_HEADS, GDN_DK, GDN_DV), lambda i: (i, 0, 0, 0)),
            full((CONV_W, GDN_QKV)), full((1, LANES)), full((1, LANES)),
        ],
        out_specs=(pl.BlockSpec((nb, D_MODEL), lambda i: (i, 0)),
                   pl.BlockSpec((nb, GDN_HEADS, GDN_DK, GDN_DV), lambda i: (i, 0, 0, 0))),
        compiler_params=pltpu.CompilerParams(dimension_semantics=("parallel",), vmem_limit_bytes=VMEM_LIMIT),
        name="gdn_sample",
    )(proj, proj, state_conv, state_gdn, conv_w, alog_v, dtb_v)


def _q_absorb_kernel(qt_ref, wukt_ref, ql_out, qrt_out, *, lane0, ns):
    for h in range(MLA_HEADS):
        qt = qt_ref[h, 0][:, lane0:lane0 + ns]
        ql_out[h] = _dot_tn(qt, wukt_ref[h]).astype(BF16)
        qrt_out[h] = qt[NOPE_DIM:NOPE_DIM + ROPE_DIM]


def _q_absorb(qt, wukt, ns, row0):
    tm = qt.shape[-1]
    assert row0 % tm + ns <= tm and (row0 % tm) % LANES == 0
    return pl.pallas_call(
        functools.partial(_q_absorb_kernel, lane0=row0 % tm, ns=ns),
        out_shape=(jax.ShapeDtypeStruct((MLA_HEADS, ns, KV_LORA), BF16),
                   jax.ShapeDtypeStruct((MLA_HEADS, ROPE_DIM, ns), BF16)),
        grid=(1,),
        in_specs=[pl.BlockSpec((MLA_HEADS, 1, LANES, tm), lambda i: (0, row0 // tm, 0, 0)),
                  pl.BlockSpec((MLA_HEADS, LANES, KV_LORA), lambda i: (0, 0, 0))],
        out_specs=(pl.BlockSpec((MLA_HEADS, ns, KV_LORA), lambda i: (0, 0, 0)),
                   pl.BlockSpec((MLA_HEADS, ROPE_DIM, ns), lambda i: (0, 0, 0))),
        compiler_params=pltpu.CompilerParams(vmem_limit_bytes=VMEM_LIMIT),
        name="q_absorb",
    )(qt, wukt)


def _mla_sample_kernel(pt_ref, ql_ref, qr_ref, cn_ref, krn_ref, cc_hbm, cr_hbm, o_ref, cbuf, rbuf, sem,
                       *, n_pages, nsub):
    b = pl.program_id(0)
    slot = b % 2

    def page_copies(seq, slot_, i):
        page = pt_ref[seq * n_pages + i]
        return (pltpu.make_async_copy(cc_hbm.at[page], cbuf.at[slot_, i], sem.at[slot_, 0]),
                pltpu.make_async_copy(cr_hbm.at[page], rbuf.at[slot_, :, pl.ds(i * PAGE_SIZE, PAGE_SIZE)],
                                      sem.at[slot_, 1]))

    def start_pages(seq, slot_):
        for i in range(n_pages):
            cc, cr = page_copies(seq, slot_, i)
            cc.start()
            cr.start()

    @pl.when(b == 0)
    def _():
        start_pages(0, 0)

    @pl.when(b + 1 < pl.num_programs(0))
    def _():
        start_pages(b + 1, 1 - slot)

    for i in range(n_pages):
        cc, cr = page_copies(b, slot, i)
        cc.wait()
        cr.wait()

    ql = ql_ref[0]
    qr = qr_ref[0]
    pps = n_pages // nsub
    subs = range(nsub)
    c = [cbuf[slot, i * pps:(i + 1) * pps].reshape(pps * PAGE_SIZE, KV_LORA).astype(BF16) for i in subs]
    s = [_dot_nt(ql, c[i]) + _dot(qr, rbuf[slot, :, i * pps * PAGE_SIZE:(i + 1) * pps * PAGE_SIZE].astype(BF16))
         for i in subs]
    ms = [jnp.max(s[i], axis=1, keepdims=True) for i in subs]
    p = [jnp.exp2(s[i] - ms[i]) for i in subs]
    ls = [jnp.sum(p[i], axis=1, keepdims=True) for i in subs]
    accs = [_dot(p[i].astype(BF16), c[i]) for i in subs]

    cn = cn_ref[0]
    krn = krn_ref[0][:, SM_KR:SM_KR + ROPE_DIM]
    ms.append(jnp.sum(ql.astype(F32) * cn, axis=1, keepdims=True)
              + jnp.sum(qr.astype(F32) * krn, axis=1, keepdims=True))
    ls.append(jnp.ones_like(ms[-1]))
    accs.append(jnp.broadcast_to(cn, (MLA_HEADS, KV_LORA)))
    m = functools.reduce(jnp.maximum, ms)
    scale = [jnp.exp2(m_i - m) for m_i in ms]
    l = sum(a * l_i for a, l_i in zip(scale, ls))
    acc = sum(a * acc_i for a, acc_i in zip(scale, accs))
    o_ref[0] = acc / l


def _mla_sample(page_table, ql, qr, c_new, kr_new, cache_c, cache_r):
    ns, n_pages = page_table.shape
    kern = functools.partial(_mla_sample_kernel, n_pages=n_pages, nsub=_pick(n_pages, (8, 4, 2, 1)))
    grid_spec = pltpu.PrefetchScalarGridSpec(
        num_scalar_prefetch=1,
        grid=(ns,),
        in_specs=[
            pl.BlockSpec((1, MLA_HEADS, KV_LORA), lambda b, pt: (b, 0, 0)),
            pl.BlockSpec((1, MLA_HEADS, ROPE_DIM), lambda b, pt: (b, 0, 0)),
            pl.BlockSpec((1, 1, KV_LORA), lambda b, pt: (b, 0, 0)),
            pl.BlockSpec((1, 1, LANES), lambda b, pt: (b, 0, 0)),
            pl.BlockSpec(memory_space=pl.ANY),
            pl.BlockSpec(memory_space=pl.ANY),
        ],
        out_specs=pl.BlockSpec((1, MLA_HEADS, KV_LORA), lambda b, pt: (b, 0, 0)),
        scratch_shapes=[pltpu.VMEM((2, n_pages, PAGE_SIZE, KV_LORA), F32),
                        pltpu.VMEM((2, ROPE_DIM, n_pages * PAGE_SIZE), F32),
                        pltpu.SemaphoreType.DMA((2, 2))],
    )
    return pl.pallas_call(
        kern,
        out_shape=jax.ShapeDtypeStruct((ns, MLA_HEADS, KV_LORA), F32),
        grid_spec=grid_spec,
        compiler_params=pltpu.CompilerParams(dimension_semantics=("arbitrary",), vmem_limit_bytes=VMEM_LIMIT),
        name="mla_sample",
    )(page_table.reshape(-1), ql, qr, c_new, kr_new, cache_c, cache_r)


def _o_proj_sample_kernel(ol_ref, wv_ref, o_ref):
    for p in range(MLA_HEADS // 2):
        acc = None
        for h in (2 * p, 2 * p + 1):
            part = _dot(ol_ref[h].astype(BF16), wv_ref[:, h * LANES:(h + 1) * LANES])
            acc = part if acc is None else acc + part
        o_ref[:, p * LANES:(p + 1) * LANES] = acc.astype(o_ref.dtype)


def _o_proj_sample(o_lat_t, wv):
    ns = o_lat_t.shape[1]
    return pl.pallas_call(
        _o_proj_sample_kernel,
        out_shape=jax.ShapeDtypeStruct((ns, D_MODEL), BF16),
        grid=(1,),
        in_specs=[pl.BlockSpec((MLA_HEADS, ns, KV_LORA), lambda i: (0, 0, 0)),
                  pl.BlockSpec((KV_LORA, MLA_HEADS * LANES), lambda i: (0, 0))],
        out_specs=pl.BlockSpec((ns, D_MODEL), lambda i: (0, 0)),
        compiler_params=pltpu.CompilerParams(vmem_limit_bytes=VMEM_LIMIT),
        name="o_proj_sample",
    )(o_lat_t, wv)


def _outproj_kernel(x_ref, om_ref, og_ref, z_ref, gm_ref, gg_ref, wo_ref, gn_ref, nf_ref, wr_ref, br_ref,
                    xmid_out, hf_out, route_out):
    og = og_ref[...].astype(F32)
    parts = []
    for h in range(GDN_HEADS):
        oh = og[:, h * GDN_DV:(h + 1) * GDN_DV]
        parts.append(oh * lax.rsqrt(jnp.mean(oh * oh, axis=-1, keepdims=True) + RMS_EPS))
    o_gdn = jnp.concatenate(parts, axis=1) * gn_ref[...] * _silu(z_ref[...].astype(F32))
    merged = (_sigmoid(gm_ref[...].astype(F32)) * om_ref[...].astype(F32)
              + _sigmoid(gg_ref[...].astype(F32)) * o_gdn)
    x_mid = x_ref[...] + _dot(merged.astype(BF16), wo_ref[...])
    xmid_out[...] = x_mid
    hf = _rms(x_mid, nf_ref[...]).astype(BF16)
    bits = lax.bitcast_convert_type(hf.astype(F32), jnp.uint32)
    half = D_MODEL // 2
    hf_out[...] = bits[:, half:] | (bits[:, :half] >> 16)

    logits = _dot(hf, wr_ref[...]) + br_ref[...]
    lane = lax.broadcasted_iota(jnp.int32, logits.shape, 1)
    neg = -jnp.inf
    big = 4 * LANES
    is_g = lane < N_GROUPS
    lg = jnp.where(is_g, logits, neg)
    mg = jnp.max(lg, axis=1, keepdims=True)
    grp = jnp.min(jnp.where(lg == mg, lane, big), axis=1, keepdims=True)
    gate_g = 1.0 / jnp.sum(jnp.where(is_g, jnp.exp(logits - mg), 0.0), axis=1, keepdims=True)
    e_lane = lane - N_GROUPS
    in_grp = (e_lane >= 0) & (e_lane < N_EXPERTS) & ((e_lane >> 3) == grp)
    le = jnp.where(in_grp, logits, neg)
    v1 = jnp.max(le, axis=1, keepdims=True)
    i1 = jnp.min(jnp.where(le == v1, lane, big), axis=1, keepdims=True)
    le2 = jnp.where(lane == i1, neg, le)
    v2 = jnp.max(le2, axis=1, keepdims=True)
    i2 = jnp.min(jnp.where(le2 == v2, lane, big), axis=1, keepdims=True)
    e = jnp.exp(v2 - v1)
    w1 = gate_g / (1.0 + e)
    w2 = gate_g * e / (1.0 + e)
    route = jnp.where(lane == 0, (i1 - N_GROUPS).astype(F32),
                      jnp.where(lane == 1, (i2 - N_GROUPS).astype(F32),
                                jnp.where(lane == 2, w1, jnp.where(lane == 3, w2, 0.0))))
    route_out[...] = route


def _row_tile(n, limit):
    t = limit - limit % 16
    while t >= 16:
        if n % t == 0:
            return t
        t -= 16
    raise ValueError(f"no row tile for {n}")


def _outproj(x_all, o_mla, o_gdn, gates, w_out, gn_t, norm_ffn, w_r, b_r, row0):
    n = o_mla.shape[0]
    tm = _pick(n, (512, 256, 128, 64, 32, 16))
    assert row0 % tm == 0
    r0 = row0 // tm
    full = lambda shape: pl.BlockSpec(shape, lambda i: (0,) * len(shape))
    row = lambda w: pl.BlockSpec((tm, w), lambda i: (i, 0))
    shared = lambda w, j=0: pl.BlockSpec((tm, w), lambda i, j=j: (r0 + i, j))
    return pl.pallas_call(
        _outproj_kernel,
        out_shape=(jax.ShapeDtypeStruct((n, D_MODEL), F32),
                   jax.ShapeDtypeStruct((n, D_MODEL // 2), jnp.uint32),
                   jax.ShapeDtypeStruct((n, LANES), F32)),
        grid=(n // tm,),
        in_specs=[shared(D_MODEL), row(D_MODEL), row(D_MODEL),
                  shared(D_MODEL, P_Z // D_MODEL), shared(D_MODEL, P_GM // D_MODEL), shared(D_MODEL, P_GG // D_MODEL),
                  full((D_MODEL, D_MODEL)), full((1, D_MODEL)), full((1, D_MODEL)),
                  full((D_MODEL, LANES)), full((1, LANES))],
        out_specs=(row(D_MODEL), row(D_MODEL // 2), row(LANES)),
        compiler_params=pltpu.CompilerParams(dimension_semantics=("parallel",), vmem_limit_bytes=VMEM_LIMIT),
        name="outproj_route",
    )(x_all, o_mla, o_gdn, gates, gates, gates, w_out, gn_t, norm_ffn, w_r, b_r)


def _moe_plan(eid, wgt, nsp):
    nt = eid.shape[0]
    ts = nt // nsp
    n_asg = ts * TOP_K
    e = eid.reshape(nsp, n_asg)
    ids = jnp.broadcast_to(jnp.arange(n_asg, dtype=jnp.int32), e.shape)
    _, ids_s, w_s = lax.sort((e, ids, wgt.reshape(nsp, n_asg)), dimension=1, num_keys=1, is_stable=True)
    rows_s = (ids_s % TOP_K) * (ts + 8) + ids_s // TOP_K
    counts = jnp.sum((e[..., None] == jnp.arange(N_EXPERTS, dtype=jnp.int32)).astype(jnp.int32), axis=1)
    run_start = jnp.cumsum(counts, axis=1) - counts
    tail = lambda v, dt: jnp.full((nsp, MOE_BLOCK), v, dt)
    tok_t = jnp.concatenate([ids_s // TOP_K, tail(ts, jnp.int32)], axis=1)
    rows_t = jnp.concatenate([rows_s, tail(ts, jnp.int32)], axis=1)
    w_t = jnp.concatenate([w_s, tail(0.0, F32)], axis=1)
    as_i32 = lambda a: a.astype(jnp.int32).reshape(-1)
    scalars = (as_i32(run_start), as_i32(counts), as_i32(tok_t), as_i32(rows_t), w_t.reshape(-1))
    return scalars, ts


def _split_pieces(k, ts, n_prompt, n_sample):
    lo, hi = k * ts, (k + 1) * ts
    pieces = []
    if lo < n_prompt:
        pieces.append((0, lo, 0, min(hi, n_prompt) - lo))
    if hi > n_prompt:
        start = max(lo, n_prompt)
        pieces.append((1, start - n_prompt, start - lo, hi - start))
    assert hi <= n_prompt + n_sample
    return pieces


def _moe_kernel(start_ref, cnt_ref, tok_ref, rows_ref, w_ref, hf_p, hf_q, xmid_p, xmid_q,
                wg_ref, wu_ref, wd_ref, nfin_ref, y_p, y_q, hf_s, comb, acc, xb, yb, sem,
                *, ts, rc, nsp, n_prompt, n_sample):
    s = pl.program_id(0)
    j = pl.program_id(1)
    half = D_MODEL // 2
    stride = ts + 8

    def zero_spare_rows():
        hf_s[ts:ts + 8, :] = jnp.zeros((8, half), jnp.uint32)

    def split_copies(k, load):
        cps = []
        for src, r0, l0, n in _split_pieces(k, ts, n_prompt, n_sample):
            if load:
                cps.append(pltpu.make_async_copy((hf_p, hf_q)[src].at[pl.ds(r0, n)], hf_s.at[pl.ds(l0, n)],
                                                 sem.at[0, src]))
                cps.append(pltpu.make_async_copy((xmid_p, xmid_q)[src].at[pl.ds(r0, n)], acc.at[pl.ds(l0, n)],
                                                 sem.at[1, src]))
            else:
                cps.append(pltpu.make_async_copy(acc.at[pl.ds(l0, n)], (y_p, y_q)[src].at[pl.ds(r0, n)],
                                                 sem.at[2, src]))
        return cps

    def run_copies(load, between=None):
        for k in range(nsp):
            @pl.when(s == k)
            def _(k=k):
                cps = split_copies(k, load)
                for cp in cps:
                    cp.start()
                if between is not None:
                    between()
                for cp in cps:
                    cp.wait()

    @pl.when(j == 0)
    def _():
        run_copies(True, zero_spare_rows)

    cnt = cnt_ref[s * N_EXPERTS + j]
    first = s * (ts * TOP_K + MOE_BLOCK) + start_ref[s * N_EXPERTS + j]

    def block(p0, n):
        for r in range(n):
            xb[r:r + 1, :] = hf_s[pl.ds(tok_ref[p0 + r], 1), :]
        bits = xb[0:n, :]
        lo = lax.bitcast_convert_type(bits << 16, F32).astype(BF16)
        hi = lax.bitcast_convert_type(bits & jnp.uint32(0xFFFF0000), F32).astype(BF16)
        g = _dot(lo, wg_ref[0, :half, :]) + _dot(hi, wg_ref[0, half:, :])
        u = _dot(lo, wu_ref[0, :half, :]) + _dot(hi, wu_ref[0, half:, :])
        yb[0:n, :] = _dot((_silu(g) * u).astype(BF16), wd_ref[0])
        for r in range(n):
            comb[pl.ds(rows_ref[p0 + r], 1), :] = w_ref[p0 + r] * yb[r:r + 1, :]

    def full_block(i, carry):
        block(first + i * MOE_BLOCK, MOE_BLOCK)
        return carry

    n_full = cnt // MOE_BLOCK
    rest = cnt - n_full * MOE_BLOCK
    lax.fori_loop(0, n_full, full_block, 0)

    @pl.when(rest > MOE_BLOCK // 2)
    def _():
        block(first + n_full * MOE_BLOCK, MOE_BLOCK)

    @pl.when((rest > 0) & (rest <= MOE_BLOCK // 2))
    def _():
        block(first + n_full * MOE_BLOCK, MOE_BLOCK // 2)

    @pl.when(j == N_EXPERTS - 1)
    def _():
        def body(i, carry):
            r0 = pl.multiple_of(i * rc, 8)
            rows = pl.ds(r0, rc)
            moe = comb[rows, :]
            for k in range(1, TOP_K):
                moe = moe + comb[pl.ds(k * stride + r0, rc), :]
            acc[rows, :] = _rms(acc[rows, :] + moe, nfin_ref[...])
            return carry

        lax.fori_loop(0, ts // rc, body, 0)
        run_copies(False)


def _moe(plan, hf_p, hf_q, xmid_p, xmid_q, wg, wu, wd, norm_final):
    scalars, ts = plan
    n_prompt, n_sample = xmid_p.shape[0], xmid_q.shape[0]
    nsp = (n_prompt + n_sample) // ts
    rc = _row_tile(ts, 256) if ts % 16 == 0 else 8
    kern = functools.partial(_moe_kernel, ts=ts, rc=rc, nsp=nsp, n_prompt=n_prompt, n_sample=n_sample)
    hbm = pl.BlockSpec(memory_space=pl.ANY)
    expert = lambda shape: pl.BlockSpec((1,) + shape, lambda s, j, *_: (j, 0, 0))
    grid_spec = pltpu.PrefetchScalarGridSpec(
        num_scalar_prefetch=len(scalars),
        grid=(nsp, N_EXPERTS),
        in_specs=[
            hbm, hbm, hbm, hbm,
            expert((D_MODEL, D_EXPERT)), expert((D_MODEL, D_EXPERT)), expert((D_EXPERT, D_MODEL)),
            pl.BlockSpec((1, D_MODEL), lambda s, j, *_: (0, 0)),
        ],
        out_specs=(hbm, hbm),
        scratch_shapes=[
            pltpu.VMEM((ts + 8, D_MODEL // 2), jnp.uint32),
            pltpu.VMEM((TOP_K * (ts + 8), D_MODEL), F32),
            pltpu.VMEM((ts, D_MODEL), F32),
            pltpu.VMEM((MOE_BLOCK, D_MODEL // 2), jnp.uint32),
            pltpu.VMEM((MOE_BLOCK, D_MODEL), F32),
            pltpu.SemaphoreType.DMA((3, 2)),
        ],
    )
    return pl.pallas_call(
        kern,
        out_shape=(jax.ShapeDtypeStruct((n_prompt, D_MODEL), F32), jax.ShapeDtypeStruct((n_sample, D_MODEL), F32)),
        grid_spec=grid_spec,
        compiler_params=pltpu.CompilerParams(
            dimension_semantics=("arbitrary", "arbitrary"), vmem_limit_bytes=VMEM_LIMIT),
        name="moe",
    )(*scalars, hf_p, hf_q, xmid_p, xmid_q, wg, wu, wd, norm_final)


def _pack_w_in(w):
    kr = w[:, _OFF_KV + KV_LORA:_OFF_QKV]
    kr_sw = jnp.concatenate([kr[:, ROPE_DIM // 2:], kr[:, :ROPE_DIM // 2]], axis=1)
    small = jnp.concatenate([w[:, _OFF_B:_OFF_A], w[:, _OFF_A:_OFF_GM],
                             jnp.zeros((D_MODEL, SM_KR - 2 * GDN_HEADS), w.dtype), kr, kr_sw], axis=1)
    packed = jnp.concatenate([w[:, _OFF_QKV:_OFF_Z], w[:, _OFF_KV:_OFF_KV + KV_LORA], small, w[:, :Q_LORA],
                              w[:, _OFF_Z:_OFF_B], w[:, _OFF_GM:_OFF_GG], w[:, _OFF_GG:]], axis=1)
    return packed.astype(BF16)


def _pack_mla_weights(w_uq, w_uk, w_uv):
    zq = jnp.zeros((Q_LORA, MLA_HEADS, LANES - NOPE_DIM - ROPE_DIM), w_uq.dtype)
    wq = jnp.concatenate([w_uq, zq], axis=2).reshape(Q_LORA, MLA_HEADS * LANES)
    rope = w_uq[:, :, NOPE_DIM:]
    rope_sw = jnp.concatenate([rope[..., ROPE_DIM // 2:], rope[..., :ROPE_DIM // 2]], axis=2)
    wqs = jnp.concatenate([jnp.zeros((Q_LORA, MLA_HEADS, NOPE_DIM), w_uq.dtype), rope_sw, zq], axis=2)
    wqs = wqs.reshape(Q_LORA, MLA_HEADS * LANES)
    wk = jnp.concatenate([w_uk, jnp.zeros((KV_LORA, MLA_HEADS, LANES - NOPE_DIM), w_uk.dtype)], axis=2)
    wk = wk.reshape(KV_LORA, MLA_HEADS * LANES)
    zv = jnp.zeros((KV_LORA, MLA_HEADS // 2, V_DIM), w_uv.dtype)
    wv = jnp.stack([jnp.concatenate([w_uv[:, 0::2], zv], axis=2),
                    jnp.concatenate([zv, w_uv[:, 1::2]], axis=2)], axis=2)
    wv = wv.reshape(KV_LORA, MLA_HEADS * LANES)
    return wq.astype(BF16), wqs.astype(BF16), wk.astype(BF16), wv.astype(BF16)


def _rope_tables(pos):
    inv_freq = ROPE_THETA ** (-jnp.arange(0, ROPE_DIM, 2, dtype=F32) / ROPE_DIM)
    ang = pos.astype(F32)[:, None] * inv_freq[None, :]
    cos, sin = jnp.cos(ang), jnp.sin(ang)
    n = pos.shape[0]
    cos_t = jnp.concatenate([jnp.ones((n, NOPE_DIM), F32), cos, cos, jnp.zeros((n, ROPE_DIM), F32)], axis=1)
    sin_t = jnp.concatenate([jnp.zeros((n, NOPE_DIM), F32), -sin, sin, jnp.zeros((n, ROPE_DIM), F32)], axis=1)
    return cos_t, sin_t


def _head_lanes(v):
    return jnp.zeros((1, LANES), F32).at[0, SM_A:SM_A + GDN_HEADS].set(v.astype(F32))


def _moe_splits(nt):
    for nsp in (6, 3, 4, 2, 1):
        if nt % (nsp * 8) == 0:
            return nsp
    return 1


def kernel(x_prompt, x_sample, cache_kv_latent, cache_k_rope, page_table, state_conv, state_gdn, meta_tokens,
           norm_mix, w_in, q_norm, w_uq, kv_norm, w_uk, w_uv, conv_w, a_log, dt_bias, gdn_norm, w_out, norm_ffn,
           w_group, b_group, w_router, b_router, w_gate, w_up, w_down, norm_final):
    batch, seq, _ = x_prompt.shape
    ns, dec_seq, _ = x_sample.shape
    assert dec_seq == 1 and w_in.shape[0] == 1 and seq % CHUNK == 0
    n_pages = page_table.shape[1]
    n_prompt = batch * seq
    nt = n_prompt + ns
    meta_row = nt
    tq = _pick(seq, ATTN_TQ)
    n_rows = -(-(nt + N_META) // tq) * tq
    assert n_prompt % ns == 0 and nt % N_META == 0 and ns % LANES == 0 and meta_row % tq + N_META <= tq

    x_all = jnp.concatenate([x_prompt.reshape(n_prompt, D_MODEL), x_sample.reshape(ns, D_MODEL),
                             meta_tokens.astype(x_prompt.dtype),
                             jnp.zeros((n_rows - nt - N_META, D_MODEL), x_prompt.dtype)], axis=0)
    pos = jnp.concatenate([N_META + jnp.arange(seq), jnp.full((1,), n_pages * PAGE_SIZE), jnp.arange(N_META)])
    by_row = lambda t: jnp.concatenate([jnp.tile(t[:seq], (batch, 1)), jnp.broadcast_to(t[seq], (ns, LANES)),
                                        t[seq + 1:], jnp.zeros((n_rows - nt - N_META, LANES), F32)], axis=0)
    cos_t, sin_t = (by_row(t) for t in _rope_tables(pos))
    w_packed = _pack_w_in(w_in[0])
    wq, wqs, wk, wv = _pack_mla_weights(w_uq[0], w_uk[0], w_uv[0])
    wukt = jnp.concatenate([jnp.transpose(w_uk[0], (1, 2, 0)),
                            jnp.zeros((MLA_HEADS, LANES - NOPE_DIM, KV_LORA), w_uk.dtype)], axis=1).astype(BF16)
    alog_v, dtb_v = _head_lanes(a_log[0]), _head_lanes(dt_bias[0])
    cw = conv_w[0].astype(F32)
    gn_t = jnp.tile(gdn_norm[0].astype(F32), GDN_HEADS)[None]
    w_r = jnp.concatenate([w_group[0], w_router[0],
                           jnp.zeros((D_MODEL, LANES - N_GROUPS - N_EXPERTS), w_group.dtype)], axis=1).astype(BF16)
    b_r = jnp.concatenate([b_group[0], b_router[0], jnp.zeros((LANES - N_GROUPS - N_EXPERTS,), b_group.dtype)])[None]
    wg, wu, wd = w_gate[0].astype(BF16), w_up[0].astype(BF16), w_down[0].astype(BF16)

    proj, gates = _inproj(x_all, norm_mix[0][None].astype(F32), w_packed)
    qt, k, vt, ckv, krot = _mla_prep(proj, cos_t, sin_t, q_norm[0][None].astype(F32), kv_norm[0][None].astype(F32),
                                     wq.T, wqs.T, wk, wv.T, tq)

    o_mla_p = _attn_prompt(qt, k, vt, batch, seq, meta_row, tq)
    ql, qrt = _q_absorb(qt, wukt, ns, n_prompt)
    o_lat = _mla_sample(page_table, jnp.transpose(ql, (1, 0, 2)), jnp.transpose(qrt, (2, 0, 1)),
                        ckv[n_prompt:nt].reshape(ns, 1, KV_LORA), krot[n_prompt:nt].reshape(ns, 1, LANES),
                        cache_kv_latent[0], jnp.swapaxes(cache_k_rope[0], 1, 2))
    o_mla_s = _o_proj_sample(jnp.transpose(o_lat, (1, 0, 2)), wv)

    s_meta = _gdn_meta(proj, cw, alog_v, dtb_v, meta_row)
    terms = _gdn_terms(proj, cw, alog_v, dtb_v, batch, seq, meta_row)
    o_gdn_p, gdn_p = _gdn_scan(*terms, s_meta, batch, seq)
    o_gdn_s, gdn_s = _gdn_sample(proj, state_conv[0], state_gdn[0], cw, alog_v, dtb_v, n_prompt)

    tail = (gates, w_out[0].astype(BF16), gn_t, norm_ffn[0][None].astype(F32), w_r, b_r.astype(F32))
    xmid_p, hf_p, route_p = _outproj(x_all, o_mla_p, o_gdn_p, *tail, 0)
    xmid_s, hf_s, route_s = _outproj(x_all, o_mla_s, o_gdn_s, *tail, n_prompt)
    route = jnp.concatenate([route_p[:, :2 * TOP_K], route_s[:, :2 * TOP_K]], axis=0)
    plan = _moe_plan(route[:, :TOP_K].astype(jnp.int32), route[:, TOP_K:], _moe_splits(nt))
    y_p, y_s = _moe(plan, hf_p, hf_s, xmid_p, xmid_s, wg, wu, wd, norm_final[None].astype(F32))

    def with_meta(rows, width):
        meta = jnp.broadcast_to(rows[meta_row:meta_row + N_META][None], (batch, N_META, width))
        return jnp.concatenate([meta, rows[:n_prompt].reshape(batch, seq, width)], axis=1)[None]

    k_rope = krot[:, SM_KR:SM_KR + ROPE_DIM]
    conv_p = jnp.stack([proj[(b + 1) * seq - (CONV_W - 1):(b + 1) * seq, :GDN_QKV] for b in range(batch)])
    conv_s = jnp.concatenate([state_conv[0][:, 1:].astype(F32), proj[n_prompt:nt, None, :GDN_QKV]], axis=1)
    return (y_p.reshape(batch, seq, D_MODEL), y_s.reshape(ns, 1, D_MODEL),
            with_meta(ckv, KV_LORA), with_meta(k_rope, ROPE_DIM),
            ckv[n_prompt:nt].reshape(1, ns, 1, KV_LORA), k_rope[n_prompt:nt].reshape(1, ns, 1, ROPE_DIM),
            conv_p[None], conv_s[None], gdn_p[None], gdn_s[None])
```

```python
import functools

import jax
import jax.numpy as jnp
from jax import lax
from jax.experimental import pallas as pl
from jax.experimental.pallas import tpu as pltpu

F32 = jnp.float32
BF16 = jnp.bfloat16
HIGHEST = lax.Precision.HIGHEST

D_MODEL = 1024
N_META = 16
RMS_EPS = 1e-6
MLA_HEADS = 16
Q_LORA = 384
KV_LORA = 256
NOPE_DIM = 64
ROPE_DIM = 32
V_DIM = 64
ROPE_THETA = 10000.0
MLA_SCALE = (NOPE_DIM + ROPE_DIM) ** -0.5
PAGE_SIZE = 128
GDN_HEADS = 8
GDN_DK = 128
GDN_DV = 128
GDN_KEY = GDN_HEADS * GDN_DK
GDN_QKV = 3 * GDN_KEY
CONV_W = 4
CHUNK = 64
N_GROUPS = 4
EXPERTS_PER_GROUP = 8
N_EXPERTS = 32
TOP_K = 2
D_EXPERT = 256
MOE_BLOCK = 128

_OFF_KV = Q_LORA
_OFF_QKV = _OFF_KV + KV_LORA + ROPE_DIM
_OFF_Z = _OFF_QKV + GDN_QKV
_OFF_B = _OFF_Z + GDN_KEY
_OFF_A = _OFF_B + GDN_HEADS
_OFF_GM = _OFF_A + GDN_HEADS
_OFF_GG = _OFF_GM + D_MODEL
P_QKV = 0
P_KVC = 3072
P_SMALL = 3328
P_QD = 3456
P_F32 = 3840
P_Z = 0
P_GM = 1024
P_GG = 2048
P_BF16 = 3072
P_TOTAL = P_F32 + P_BF16
INPROJ_TN = 768
SM_B = 0
SM_A = 8
SM_KR = 64

LANES = 128
VMEM_LIMIT = 56 * 1024 * 1024
ATTN_TQ = (512, 256, 128)


def _pick(n, candidates):
    for c in candidates:
        if n % c == 0:
            return c
    raise ValueError(f"no tile for {n} in {candidates}")


def _dot(a, b):
    return jnp.dot(a, b, preferred_element_type=F32)


def _dot_nt(a, b):
    return lax.dot_general(a, b, (((1,), (1,)), ((), ())), preferred_element_type=F32)


def _dot_tn(a, b):
    return lax.dot_general(a, b, (((0,), (0,)), ((), ())), preferred_element_type=F32)


def _sigmoid(x):
    return 1.0 / (1.0 + jnp.exp(-x))


def _silu(x):
    return x * _sigmoid(x)


def _softplus(x):
    return jnp.maximum(x, 0.0) + jnp.log1p(jnp.exp(-jnp.abs(x)))


def _rms(x, w):
    return x * lax.rsqrt(jnp.mean(x * x, axis=-1, keepdims=True) + RMS_EPS) * w


def _inproj_kernel(x_ref, nw_ref, w_ref, of_ref, ob_ref, hn_ref):
    j = pl.program_id(1)
    nf = P_F32 // INPROJ_TN

    @pl.when(j == 0)
    def _():
        hn_ref[...] = _rms(x_ref[...], nw_ref[...]).astype(BF16)

    r = _dot(hn_ref[...], w_ref[...])

    @pl.when(j < nf)
    def _():
        of_ref[...] = r

    @pl.when(j >= nf)
    def _():
        ob_ref[...] = r.astype(BF16)


def _inproj(x_all, norm_w, w_packed):
    r = x_all.shape[0]
    tm = _pick(r, (1536, 1280, 768, 640, 512, 256, 128))
    tn = INPROJ_TN
    nf = P_F32 // tn
    return pl.pallas_call(
        _inproj_kernel,
        out_shape=(jax.ShapeDtypeStruct((r, P_F32), F32), jax.ShapeDtypeStruct((r, P_BF16), BF16)),
        grid=(r // tm, P_TOTAL // tn),
        in_specs=[
            pl.BlockSpec((tm, D_MODEL), lambda i, j: (i, 0)),
            pl.BlockSpec((1, D_MODEL), lambda i, j: (0, 0)),
            pl.BlockSpec((D_MODEL, tn), lambda i, j: (0, j)),
        ],
        out_specs=(pl.BlockSpec((tm, tn), lambda i, j: (i, jnp.minimum(j, nf - 1))),
                   pl.BlockSpec((tm, tn), lambda i, j: (i, jnp.maximum(j - nf, 0)))),
        scratch_shapes=[pltpu.VMEM((tm, D_MODEL), BF16)],
        compiler_params=pltpu.CompilerParams(
            dimension_semantics=("parallel", "arbitrary"), vmem_limit_bytes=VMEM_LIMIT),
        name="inproj",
    )(x_all, norm_w, w_packed)


LOG2E = 1.4426950408889634


def _mla_prep_kernel(qd_ref, kvc_ref, sm_ref, c_ref, s_ref, qn_ref, kvn_ref, wqt_ref, wqst_ref, wk_ref, wvt_ref,
                     qt_out, k_out, vt_out, ckv_out, kr_out):
    cos = c_ref[...]
    sin = s_ref[...]
    cos_t, sin_t = cos.T, sin.T
    qn_t = _rms(qd_ref[...], qn_ref[...]).T.astype(BF16)
    qt = _dot(wqt_ref[...], qn_t)
    qst = _dot(wqst_ref[...], qn_t)
    for h in range(MLA_HEADS):
        sl = slice(h * LANES, (h + 1) * LANES)
        qt_out[h, 0] = ((qt[sl] * cos_t + qst[sl] * sin_t) * (MLA_SCALE * LOG2E)).astype(BF16)
    ckv = _rms(kvc_ref[...], kvn_ref[...])
    ckv_out[...] = ckv
    sm = sm_ref[...]
    lane = lax.broadcasted_iota(jnp.int32, sm.shape, 1)
    cos_k = jnp.where((lane >= SM_KR) & (lane < SM_KR + ROPE_DIM), cos, 0.0)
    krot = sm * cos_k + pltpu.roll(sm, LANES - ROPE_DIM, 1) * sin
    kr_out[...] = krot
    kk = _dot(ckv.astype(BF16), wk_ref[...])
    vvt = _dot(wvt_ref[...], ckv.T.astype(BF16))
    for h in range(MLA_HEADS):
        sl = slice(h * LANES, (h + 1) * LANES)
        k_out[h] = (kk[:, sl] + krot).astype(BF16)
        vt_out[h, 0] = vvt[sl].astype(BF16)


def _mla_prep(proj, cos_t, sin_t, q_norm, kv_norm, wqt, wqst, wk, wvt, tm):
    r = proj.shape[0]
    hw = MLA_HEADS * LANES
    full = lambda shape: pl.BlockSpec(shape, lambda i: (0,) * len(shape))
    return pl.pallas_call(
        _mla_prep_kernel,
        out_shape=(
            jax.ShapeDtypeStruct((MLA_HEADS, r // tm, LANES, tm), BF16),
            jax.ShapeDtypeStruct((MLA_HEADS, r, LANES), BF16),
            jax.ShapeDtypeStruct((MLA_HEADS, r // tm, LANES, tm), BF16),
            jax.ShapeDtypeStruct((r, KV_LORA), F32),
            jax.ShapeDtypeStruct((r, LANES), F32),
        ),
        grid=(r // tm,),
        in_specs=[
            pl.BlockSpec((tm, Q_LORA), lambda i: (i, P_QD // Q_LORA)),
            pl.BlockSpec((tm, KV_LORA), lambda i: (i, P_KVC // KV_LORA)),
            pl.BlockSpec((tm, LANES), lambda i: (i, P_SMALL // LANES)),
            pl.BlockSpec((tm, LANES), lambda i: (i, 0)),
            pl.BlockSpec((tm, LANES), lambda i: (i, 0)),
            full((1, Q_LORA)), full((1, KV_LORA)),
            full((hw, Q_LORA)), full((hw, Q_LORA)), full((KV_LORA, hw)), full((hw, KV_LORA)),
        ],
        out_specs=(pl.BlockSpec((MLA_HEADS, 1, LANES, tm), lambda i: (0, i, 0, 0)),
                   pl.BlockSpec((MLA_HEADS, tm, LANES), lambda i: (0, i, 0)),
                   pl.BlockSpec((MLA_HEADS, 1, LANES, tm), lambda i: (0, i, 0, 0)),
                   pl.BlockSpec((tm, KV_LORA), lambda i: (i, 0)),
                   pl.BlockSpec((tm, LANES), lambda i: (i, 0))),
        compiler_params=pltpu.CompilerParams(dimension_semantics=("parallel",), vmem_limit_bytes=VMEM_LIMIT),
        name="mla_prep",
    )(proj, proj, proj, cos_t, sin_t, q_norm, kv_norm, wqt, wqst, wk, wvt)


def _attn_prompt_kernel(qt_ref, k_ref, vt_ref, km_ref, vmt_ref, o_ref, *, tq, meta_lane, nq, tiles_per_step):
    i = pl.program_id(2)
    tiles = (i,) if tiles_per_step == 1 else (i, nq - 1 - i)
    streams = [(t, h) for t in tiles for h in (0, 1)]
    half = tq // 2
    qt = [qt_ref[h, t] for t, h in streams]

    def update(qts, k_blk, vt_blk, carry, mask):
        ms, ls, accs = carry
        n = range(len(qts))
        s = [_dot(k_blk[x], qts[x]) for x in n]
        if mask is not None:
            s = [jnp.where(mask, v, -1e30) for v in s]
        m_new = [jnp.maximum(ms[x], jnp.max(s[x], axis=0, keepdims=True)) for x in n]
        a = [jnp.exp2(ms[x] - m_new[x]) for x in n]
        p = [jnp.exp2(s[x] - m_new[x]) for x in n]
        l_new = [a[x] * ls[x] + jnp.sum(p[x], axis=0, keepdims=True) for x in n]
        acc_new = [a[x] * accs[x] + _dot(vt_blk[x], p[x].astype(BF16)) for x in n]
        return m_new, l_new, acc_new

    s0 = [_dot(km_ref[h], q) for (_, h), q in zip(streams, qt)]
    m = [jnp.max(v, axis=0, keepdims=True) for v in s0]
    p0 = [jnp.exp2(v - mx) for v, mx in zip(s0, m)]
    l = [jnp.sum(v, axis=0, keepdims=True) for v in p0]
    acc = [_dot(vmt_ref[h, 0][:, meta_lane:meta_lane + N_META], v.astype(BF16)) for (_, h), v in zip(streams, p0)]

    for ti, t in enumerate(tiles):
        own = slice(2 * ti, 2 * ti + 2)

        def body(j, carry, own=own):
            rows = pl.ds(pl.multiple_of(j * tq, tq), tq)
            return update(qt[own], [k_ref[h, rows, :] for h in (0, 1)], [vt_ref[h, j] for h in (0, 1)], carry, None)

        m[own], l[own], acc[own] = lax.fori_loop(0, t, body, (m[own], l[own], acc[own]))

    key = lax.broadcasted_iota(jnp.int32, (half, tq), 0)
    qry = lax.broadcasted_iota(jnp.int32, (half, tq), 1)
    off = [pl.multiple_of(t * tq, tq) for t, _ in streams]
    vt_d = [vt_ref[h, t] for t, h in streams]
    m, l, acc = update(qt, [k_ref[h, pl.ds(o, half), :] for (_, h), o in zip(streams, off)],
                       [v[:, :half] for v in vt_d], (m, l, acc), key <= qry)
    late = lambda xs: [v[:, half:] for v in xs]
    mb, lb, accb = update(late(qt), [k_ref[h, pl.ds(pl.multiple_of(o + half, half), half), :]
                                     for (_, h), o in zip(streams, off)],
                          late(vt_d), (late(m), late(l), late(acc)), (key <= qry)[:, :half])
    ot = [jnp.concatenate([acc[x][:, :half] / l[x][:, :half], accb[x] / lb[x]], axis=1) for x in range(len(streams))]
    for ti, t in enumerate(tiles):
        o_ref[pl.ds(pl.multiple_of(t * tq, tq), tq), :] = (ot[2 * ti] + ot[2 * ti + 1]).T.astype(o_ref.dtype)


def _attn_prompt(qt, k, vt, batch, seq, meta_row, tq):
    nq = seq // tq
    tps = 2 if nq % 2 == 0 else 1
    kern = functools.partial(_attn_prompt_kernel, tq=tq, meta_lane=meta_row % tq, nq=nq, tiles_per_step=tps)
    tiled = lambda: pl.BlockSpec((2, nq, LANES, tq), lambda b, p, i: (p, b, 0, 0))
    return pl.pallas_call(
        kern,
        out_shape=jax.ShapeDtypeStruct((batch * seq, D_MODEL), BF16),
        grid=(batch, MLA_HEADS // 2, nq // tps),
        in_specs=[
            tiled(),
            pl.BlockSpec((2, seq, LANES), lambda b, p, i: (p, b, 0)),
            tiled(),
            pl.BlockSpec((2, N_META, LANES), lambda b, p, i: (p, meta_row // N_META, 0)),
            pl.BlockSpec((2, 1, LANES, tq), lambda b, p, i: (p, meta_row // tq, 0, 0)),
        ],
        out_specs=pl.BlockSpec((seq, LANES), lambda b, p, i: (b, p)),
        compiler_params=pltpu.CompilerParams(
            dimension_semantics=("parallel", "parallel", "arbitrary"), vmem_limit_bytes=VMEM_LIMIT),
        name="attn_prompt",
    )(qt, k, vt, k, vt)


def _gate_lanes(sm, alog_ref, dtb_ref):
    g = -jnp.exp(alog_ref[...]) * _softplus(sm + dtb_ref[...])
    beta = _sigmoid(sm)
    return g, beta


def _qkv_heads(xc):
    xf = _silu(xc)
    qs, ks, vs = [], [], []
    for h in range(GDN_HEADS):
        q = xf[:, h * GDN_DK:(h + 1) * GDN_DK]
        k = xf[:, GDN_KEY + h * GDN_DK:GDN_KEY + (h + 1) * GDN_DK]
        qs.append(q * lax.rsqrt(jnp.sum(q * q, axis=-1, keepdims=True) + RMS_EPS) * (GDN_DK ** -0.5))
        ks.append(k * lax.rsqrt(jnp.sum(k * k, axis=-1, keepdims=True) + RMS_EPS))
        vs.append(xf[:, 2 * GDN_KEY + h * GDN_DV:2 * GDN_KEY + (h + 1) * GDN_DV])
    return qs, ks, vs


def _split_bf16(x):
    hi = x.astype(BF16)
    return hi, (x - hi.astype(F32)).astype(BF16)


def _dot_split(a, b):
    return _dot(a[0], b[0]) + (_dot(a[0], b[1]) + _dot(a[1], b[0]))


def _unit_lower_inverses(mats, c):
    row = lax.broadcasted_iota(jnp.int32, (c, c), 0)
    col = lax.broadcasted_iota(jnp.int32, (c, c), 1)
    eye = jnp.where(row == col, 1.0, 0.0)
    ps = [-a for a in mats]
    ts = [eye + p for p in ps]
    span = 2
    while span < c:
        psp = [_split_bf16(p) for p in ps]
        ps = [_dot_split(p, p) for p in psp]
        psp = [_split_bf16(p) for p in ps]
        ts = [t + _dot_split(p, _split_bf16(t)) for p, t in zip(psp, ts)]
        span *= 2
    return ts


def _gdn_chunk_terms(xs, sm, conv_ref, alog_ref, dtb_ref, c):
    heads = range(GDN_HEADS)
    xc = xs[0] * conv_ref[0:1, :]
    for j in range(1, CONV_W):
        xc = xc + xs[j] * conv_ref[j:j + 1, :]
    qs, ks, vs = _qkv_heads(xc)
    g, beta = _gate_lanes(sm, alog_ref, dtb_ref)
    row = lax.broadcasted_iota(jnp.int32, (c, c), 0)
    col = lax.broadcasted_iota(jnp.int32, (c, c), 1)
    causal = col <= row
    strict = col < row
    gcum = jnp.dot(jnp.where(causal, 1.0, 0.0), g, precision=HIGHEST, preferred_element_type=F32)
    gcum_t = lax.dot_general(g, jnp.where(col >= row, 1.0, 0.0), (((0,), (0,)), ((), ())),
                             precision=HIGHEST, preferred_element_type=F32)
    gc = [gcum[:, SM_A + h:SM_A + h + 1] for h in heads]
    gr = [gcum_t[SM_A + h:SM_A + h + 1, :] for h in heads]
    bc = [beta[:, SM_B + h:SM_B + h + 1] for h in heads]
    decay = [jnp.where(causal, jnp.exp(jnp.where(causal, gc[h] - gr[h], 0.0)), 0.0) for h in heads]
    kb = [ks[h] * bc[h] for h in heads]
    kbf = [ks[h].astype(BF16) for h in heads]
    a = [jnp.where(strict, _dot_nt(kb[h].astype(BF16), kbf[h]) * decay[h], 0.0) for h in heads]
    t = _unit_lower_inverses(a, c)
    eg = [jnp.exp(gc[h]) for h in heads]
    sol = [_dot(t[h].astype(BF16), jnp.concatenate([vs[h] * bc[h], kb[h] * eg[h]], axis=1).astype(BF16))
           for h in heads]
    u = [sol[h][:, :GDN_DV] for h in heads]
    w = [sol[h][:, GDN_DV:] for h in heads]
    attn = [jnp.where(causal, _dot_nt(qs[h].astype(BF16), kbf[h]) * decay[h], 0.0) for h in heads]
    qg = [qs[h] * eg[h] for h in heads]
    g_last = gcum[c - 1:c, :]
    kd = [ks[h] * jnp.exp(g_last[:, SM_A + h:SM_A + h + 1] - gc[h]) for h in heads]
    return u, w, qg, kd, attn, jnp.exp(g_last)


def _conv_shifts(hist, x):
    xe = jnp.concatenate([hist, x], axis=0)
    return [pltpu.roll(xe, d, 0)[8:] for d in range(CONV_W - 1, 0, -1)] + [x]


def _gdn_meta_kernel(x_ref, sm_ref, conv_ref, alog_ref, dtb_ref, s_out):
    xs = _conv_shifts(jnp.zeros((8, GDN_QKV), F32), x_ref[...])
    u, _, _, kd, _, _ = _gdn_chunk_terms(xs, sm_ref[...], conv_ref, alog_ref, dtb_ref, N_META)
    for h in range(GDN_HEADS):
        s_out[h] = _dot_tn(kd[h].astype(BF16), u[h].astype(BF16))


def _gdn_meta(proj, conv_w, alog_v, dtb_v, meta_row):
    full = lambda shape: pl.BlockSpec(shape, lambda i: (0,) * len(shape))
    return pl.pallas_call(
        _gdn_meta_kernel,
        out_shape=jax.ShapeDtypeStruct((GDN_HEADS, GDN_DK, GDN_DV), F32),
        grid=(1,),
        in_specs=[
            pl.BlockSpec((N_META, GDN_QKV), lambda i: (meta_row // N_META, 0)),
            pl.BlockSpec((N_META, LANES), lambda i: (meta_row // N_META, P_SMALL // LANES)),
            full((CONV_W, GDN_QKV)), full((1, LANES)), full((1, LANES)),
        ],
        out_specs=full((GDN_HEADS, GDN_DK, GDN_DV)),
        compiler_params=pltpu.CompilerParams(vmem_limit_bytes=VMEM_LIMIT),
        name="gdn_meta",
    )(proj, proj, conv_w, alog_v, dtb_v)


def _gdn_terms_kernel(x_ref, hist_ref, sm_ref, conv_ref, alog_ref, dtb_ref,
                      u_out, w_out, qg_out, kd_out, attn_out, dec_out, *, cps):
    c = CHUNK
    hist = hist_ref[...]
    for cc in range(cps):
        rows = slice(cc * c, (cc + 1) * c)
        x = x_ref[rows, :]
        u, w, qg, kd, attn, dec = _gdn_chunk_terms(_conv_shifts(hist, x), sm_ref[rows, :], conv_ref, alog_ref,
                                                   dtb_ref, c)
        for h in range(GDN_HEADS):
            sl = slice(h * GDN_DV, (h + 1) * GDN_DV)
            u_out[rows, sl] = u[h]
            w_out[rows, sl] = w[h].astype(BF16)
            qg_out[rows, sl] = qg[h].astype(BF16)
            kd_out[rows, sl] = kd[h].astype(BF16)
            attn_out[h, rows, :] = attn[h].astype(BF16)
        dec_out[cc] = dec
        hist = x[c - 8:]


def _gdn_terms(proj, conv_w, alog_v, dtb_v, batch, seq, meta_row):
    nc = seq // CHUNK
    n = batch * seq
    cps = _pick(nc, (4, 2, 1))
    step = cps * CHUNK
    spb = nc // cps
    full = lambda shape: pl.BlockSpec(shape, lambda i: (0,) * len(shape))
    rows = lambda: pl.BlockSpec((step, D_MODEL), lambda i: (i, 0))

    def hist_index(i):
        return (jnp.where(i % spb == 0, (meta_row + N_META) // 8, i * (step // 8)) - 1, 0)

    return pl.pallas_call(
        functools.partial(_gdn_terms_kernel, cps=cps),
        out_shape=(jax.ShapeDtypeStruct((n, D_MODEL), F32),
                   jax.ShapeDtypeStruct((n, D_MODEL), BF16),
                   jax.ShapeDtypeStruct((n, D_MODEL), BF16),
                   jax.ShapeDtypeStruct((n, D_MODEL), BF16),
                   jax.ShapeDtypeStruct((GDN_HEADS, n, CHUNK), BF16),
                   jax.ShapeDtypeStruct((batch * nc, 1, LANES), F32)),
        grid=(batch * spb,),
        in_specs=[
            pl.BlockSpec((step, GDN_QKV), lambda i: (i, 0)),
            pl.BlockSpec((8, GDN_QKV), hist_index),
            pl.BlockSpec((step, LANES), lambda i: (i, P_SMALL // LANES)),
            full((CONV_W, GDN_QKV)), full((1, LANES)), full((1, LANES)),
        ],
        out_specs=(rows(), rows(), rows(), rows(),
                   pl.BlockSpec((GDN_HEADS, step, CHUNK), lambda i: (0, i, 0)),
                   pl.BlockSpec((cps, 1, LANES), lambda i: (i, 0, 0))),
        compiler_params=pltpu.CompilerParams(dimension_semantics=("parallel",), vmem_limit_bytes=VMEM_LIMIT),
        name="gdn_terms",
    )(proj, proj, proj, conv_w, alog_v, dtb_v)


def _gdn_scan_kernel(u_ref, w_ref, qg_ref, kd_ref, attn_ref, dec_ref, s0_ref, o_ref, s_out, st_ref, *, cpg):
    c = CHUNK
    heads = range(GDN_HEADS)

    @pl.when(pl.program_id(1) == 0)
    def _():
        st_ref[...] = s0_ref[...]

    def chunk(ci, carry):
        rows = pl.ds(pl.multiple_of(ci * c, c), c)
        dec = dec_ref[ci]
        sl = [slice(h * GDN_DV, (h + 1) * GDN_DV) for h in heads]
        s_old = [st_ref[h] for h in heads]
        sb = [s.astype(BF16) for s in s_old]
        lhs = [jnp.concatenate([w_ref[rows, sl[h]], qg_ref[rows, sl[h]]], axis=0) for h in heads]
        r = [_dot(lhs[h], sb[h]) for h in heads]
        vnb = [(u_ref[rows, sl[h]] - r[h][:c]).astype(BF16) for h in heads]
        out = [r[h][c:] + _dot(attn_ref[h, rows, :], vnb[h]) for h in heads]
        upd = [_dot_tn(kd_ref[rows, sl[h]], vnb[h]) for h in heads]
        for h in heads:
            o_ref[rows, sl[h]] = out[h].astype(o_ref.dtype)
            st_ref[h] = s_old[h] * dec[:, SM_A + h:SM_A + h + 1] + upd[h]
        return carry

    lax.fori_loop(0, cpg, chunk, 0)

    @pl.when(pl.program_id(1) == pl.num_programs(1) - 1)
    def _():
        s_out[0] = st_ref[...]


def _gdn_scan(u, w, qg, kd, attn, dec, s_meta, batch, seq):
    nc = seq // CHUNK
    cpg = _pick(nc, (16, 8, 4, 2, 1))
    ng = nc // cpg
    rows = lambda: pl.BlockSpec((cpg * CHUNK, D_MODEL), lambda b, g: (b * ng + g, 0))
    kern = functools.partial(_gdn_scan_kernel, cpg=cpg)
    return pl.pallas_call(
        kern,
        out_shape=(jax.ShapeDtypeStruct((batch * seq, D_MODEL), BF16),
                   jax.ShapeDtypeStruct((batch, GDN_HEADS, GDN_DK, GDN_DV), F32)),
        grid=(batch, ng),
        in_specs=[rows(), rows(), rows(), rows(),
                  pl.BlockSpec((GDN_HEADS, cpg * CHUNK, CHUNK), lambda b, g: (0, b * ng + g, 0)),
                  pl.BlockSpec((cpg, 1, LANES), lambda b, g: (b * ng + g, 0, 0)),
                  pl.BlockSpec((GDN_HEADS, GDN_DK, GDN_DV), lambda b, g: (0, 0, 0))],
        out_specs=(rows(),
                   pl.BlockSpec((1, GDN_HEADS, GDN_DK, GDN_DV), lambda b, g: (b, 0, 0, 0))),
        scratch_shapes=[pltpu.VMEM((GDN_HEADS, GDN_DK, GDN_DV), F32)],
        compiler_params=pltpu.CompilerParams(
            dimension_semantics=("parallel", "arbitrary"), vmem_limit_bytes=VMEM_LIMIT),
        name="gdn_scan",
    )(u, w, qg, kd, attn, dec, s_meta)


def _gdn_sample_kernel(x_ref, sm_ref, cs_ref, st_ref, conv_ref, alog_ref, dtb_ref, o_ref, s_out, *, nb):
    xc = x_ref[...] * conv_ref[CONV_W - 1:CONV_W, :]
    for j in range(CONV_W - 1):
        xc = xc + cs_ref[:, j, :] * conv_ref[j:j + 1, :]
    qs, ks, vs = _qkv_heads(xc)
    g, beta = _gate_lanes(sm_ref[...], alog_ref, dtb_ref)
    eg = jnp.exp(g)
    for h in range(GDN_HEADS):
        q_t = qs[h].T
        k_t = ks[h].T
        for b in range(nb):
            kcol = k_t[:, b:b + 1]
            s1 = st_ref[b, h] * eg[b:b + 1, SM_A + h:SM_A + h + 1]
            r = jnp.sum(s1 * kcol, axis=0, keepdims=True)
            delta = (vs[h][b:b + 1, :] - r) * beta[b:b + 1, SM_B + h:SM_B + h + 1]
            s2 = s1 + kcol * delta
            s_out[b, h] = s2
            o_ref[b:b + 1, h * GDN_DV:(h + 1) * GDN_DV] = jnp.sum(s2 * q_t[:, b:b + 1], axis=0, keepdims=True)


def _gdn_sample(proj, state_conv, state_gdn, conv_w, alog_v, dtb_v, row0):
    ns = state_gdn.shape[0]
    nb = 8
    full = lambda shape: pl.BlockSpec(shape, lambda i: (0,) * len(shape))
    kern = functools.partial(_gdn_sample_kernel, nb=nb)
    return pl.pallas_call(
        kern,
        out_shape=(jax.ShapeDtypeStruct((ns, D_MODEL), F32),
                   jax.ShapeDtypeStruct(state_gdn.shape, F32)),
        grid=(ns // nb,),
        in_specs=[
            pl.BlockSpec((nb, GDN_QKV), lambda i: (row0 // nb + i, 0)),
            pl.BlockSpec((nb, LANES), lambda i: (row0 // nb + i, P_SMALL // LANES)),
            pl.BlockSpec((nb, CONV_W - 1, GDN_QKV), lambda i: (i, 0, 0)),
            pl.BlockSpec((nb, GDN_HEADS, GDN_DK, GDN_DV), lambda i: (i, 0, 0, 0)),
            full((CONV_W, GDN_QKV)), full((1, LANES)), full((1, LANES)),
        ],
        out_specs=(pl.BlockSpec((nb, D_MODEL), lambda i: (i, 0)),
                   pl.BlockSpec((nb, GDN_HEADS, GDN_DK, GDN_DV), lambda i: (i, 0, 0, 0))),
        compiler_params=pltpu.CompilerParams(dimension_semantics=("parallel",), vmem_limit_bytes=VMEM_LIMIT),
        name="gdn_sample",
    )(proj, proj, state_conv, state_gdn, conv_w, alog_v, dtb_v)


def _q_absorb_kernel(qt_ref, wukt_ref, ql_out, qrt_out, *, lane0, ns):
    for h in range(MLA_HEADS):
        qt = qt_ref[h, 0][:, lane0:lane0 + ns]
        ql_out[h] = _dot_tn(qt, wukt_ref[h]).astype(BF16)
        qrt_out[h] = qt[NOPE_DIM:NOPE_DIM + ROPE_DIM]


def _q_absorb(qt, wukt, ns, row0):
    tm = qt.shape[-1]
    assert row0 % tm + ns <= tm and (row0 % tm) % LANES == 0
    return pl.pallas_call(
        functools.partial(_q_absorb_kernel, lane0=row0 % tm, ns=ns),
        out_shape=(jax.ShapeDtypeStruct((MLA_HEADS, ns, KV_LORA), BF16),
                   jax.ShapeDtypeStruct((MLA_HEADS, ROPE_DIM, ns), BF16)),
        grid=(1,),
        in_specs=[pl.BlockSpec((MLA_HEADS, 1, LANES, tm), lambda i: (0, row0 // tm, 0, 0)),
                  pl.BlockSpec((MLA_HEADS, LANES, KV_LORA), lambda i: (0, 0, 0))],
        out_specs=(pl.BlockSpec((MLA_HEADS, ns, KV_LORA), lambda i: (0, 0, 0)),
                   pl.BlockSpec((MLA_HEADS, ROPE_DIM, ns), lambda i: (0, 0, 0))),
        compiler_params=pltpu.CompilerParams(vmem_limit_bytes=VMEM_LIMIT),
        name="q_absorb",
    )(qt, wukt)


def _mla_sample_kernel(pt_ref, ql_ref, qr_ref, cn_ref, krn_ref, cc_hbm, cr_hbm, o_ref, cbuf, rbuf, sem,
                       *, n_pages, nsub):
    g = pl.program_id(0)

    def page_copies(seq, slot, i):
        page = pt_ref[seq * n_pages + i]
        return (pltpu.make_async_copy(cc_hbm.at[page], cbuf.at[slot, i], sem.at[slot, 0]),
                pltpu.make_async_copy(cr_hbm.at[page], rbuf.at[slot, :, pl.ds(i * PAGE_SIZE, PAGE_SIZE)],
                                      sem.at[slot, 1]))

    def start_pages(seq, slot):
        for i in range(n_pages):
            cc, cr = page_copies(seq, slot, i)
            cc.start()
            cr.start()

    def wait_pages(seq, slot):
        for i in range(n_pages):
            cc, cr = page_copies(seq, slot, i)
            cc.wait()
            cr.wait()

    def attend(slot):
        ql = ql_ref[slot]
        qr = qr_ref[slot]
        pps = n_pages // nsub
        subs = range(nsub)
        c = [cbuf[slot, i * pps:(i + 1) * pps].reshape(pps * PAGE_SIZE, KV_LORA).astype(BF16) for i in subs]
        s = [_dot_nt(ql, c[i]) + _dot(qr, rbuf[slot, :, i * pps * PAGE_SIZE:(i + 1) * pps * PAGE_SIZE].astype(BF16))
             for i in subs]
        ms = [jnp.max(s[i], axis=1, keepdims=True) for i in subs]
        p = [jnp.exp2(s[i] - ms[i]) for i in subs]
        ls = [jnp.sum(p[i], axis=1, keepdims=True) for i in subs]
        accs = [_dot(p[i].astype(BF16), c[i]) for i in subs]
        cn = cn_ref[slot]
        krn = krn_ref[slot][:, SM_KR:SM_KR + ROPE_DIM]
        ms.append(jnp.sum(ql.astype(F32) * cn, axis=1, keepdims=True)
                  + jnp.sum(qr.astype(F32) * krn, axis=1, keepdims=True))
        ls.append(jnp.ones_like(ms[-1]))
        accs.append(jnp.broadcast_to(cn, (MLA_HEADS, KV_LORA)))
        m = functools.reduce(jnp.maximum, ms)
        scale = [jnp.exp2(m_i - m) for m_i in ms]
        l = sum(a * l_i for a, l_i in zip(scale, ls))
        acc = sum(a * acc_i for a, acc_i in zip(scale, accs))
        o_ref[slot] = acc / l

    @pl.when(g == 0)
    def _():
        start_pages(0, 0)

    start_pages(2 * g + 1, 1)
    wait_pages(2 * g, 0)
    attend(0)

    @pl.when(g + 1 < pl.num_programs(0))
    def _():
        start_pages(2 * g + 2, 0)

    wait_pages(2 * g + 1, 1)
    attend(1)


def _mla_sample(page_table, ql, qr, c_new, kr_new, cache_c, cache_r):
    ns, n_pages = page_table.shape
    assert ns % 2 == 0
    kern = functools.partial(_mla_sample_kernel, n_pages=n_pages, nsub=_pick(n_pages, (8, 4, 2, 1)))
    grid_spec = pltpu.PrefetchScalarGridSpec(
        num_scalar_prefetch=1,
        grid=(ns // 2,),
        in_specs=[
            pl.BlockSpec((2, MLA_HEADS, KV_LORA), lambda b, pt: (b, 0, 0)),
            pl.BlockSpec((2, MLA_HEADS, ROPE_DIM), lambda b, pt: (b, 0, 0)),
            pl.BlockSpec((2, 1, KV_LORA), lambda b, pt: (b, 0, 0)),
            pl.BlockSpec((2, 1, LANES), lambda b, pt: (b, 0, 0)),
            pl.BlockSpec(memory_space=pl.ANY),
            pl.BlockSpec(memory_space=pl.ANY),
        ],
        out_specs=pl.BlockSpec((2, MLA_HEADS, KV_LORA), lambda b, pt: (b, 0, 0)),
        scratch_shapes=[pltpu.VMEM((2, n_pages, PAGE_SIZE, KV_LORA), F32),
                        pltpu.VMEM((2, ROPE_DIM, n_pages * PAGE_SIZE), F32),
                        pltpu.SemaphoreType.DMA((2, 2))],
    )
    return pl.pallas_call(
        kern,
        out_shape=jax.ShapeDtypeStruct((ns, MLA_HEADS, KV_LORA), F32),
        grid_spec=grid_spec,
        compiler_params=pltpu.CompilerParams(dimension_semantics=("arbitrary",), vmem_limit_bytes=VMEM_LIMIT),
        name="mla_sample",
    )(page_table.reshape(-1), ql, qr, c_new, kr_new, cache_c, cache_r)


def _o_proj_sample_kernel(ol_ref, wv_ref, o_ref):
    for p in range(MLA_HEADS // 2):
        acc = None
        for h in (2 * p, 2 * p + 1):
            part = _dot(ol_ref[h].astype(BF16), wv_ref[:, h * LANES:(h + 1) * LANES])
            acc = part if acc is None else acc + part
        o_ref[:, p * LANES:(p + 1) * LANES] = acc.astype(o_ref.dtype)


def _o_proj_sample(o_lat_t, wv):
    ns = o_lat_t.shape[1]
    return pl.pallas_call(
        _o_proj_sample_kernel,
        out_shape=jax.ShapeDtypeStruct((ns, D_MODEL), BF16),
        grid=(1,),
        in_specs=[pl.BlockSpec((MLA_HEADS, ns, KV_LORA), lambda i: (0, 0, 0)),
                  pl.BlockSpec((KV_LORA, MLA_HEADS * LANES), lambda i: (0, 0))],
        out_specs=pl.BlockSpec((ns, D_MODEL), lambda i: (0, 0)),
        compiler_params=pltpu.CompilerParams(vmem_limit_bytes=VMEM_LIMIT),
        name="o_proj_sample",
    )(o_lat_t, wv)


def _outproj_kernel(x_ref, om_ref, og_ref, z_ref, gm_ref, gg_ref, wo_ref, gn_ref, nf_ref, wr_ref, br_ref,
                    xmid_out, hf_out, route_out):
    og = og_ref[...].astype(F32)
    parts = []
    for h in range(GDN_HEADS):
        oh = og[:, h * GDN_DV:(h + 1) * GDN_DV]
        parts.append(oh * lax.rsqrt(jnp.mean(oh * oh, axis=-1, keepdims=True) + RMS_EPS))
    o_gdn = jnp.concatenate(parts, axis=1) * gn_ref[...] * _silu(z_ref[...].astype(F32))
    merged = (_sigmoid(gm_ref[...].astype(F32)) * om_ref[...].astype(F32)
              + _sigmoid(gg_ref[...].astype(F32)) * o_gdn)
    x_mid = x_ref[...] + _dot(merged.astype(BF16), wo_ref[...])
    xmid_out[...] = x_mid
    hf = _rms(x_mid, nf_ref[...]).astype(BF16)
    bits = lax.bitcast_convert_type(hf.astype(F32), jnp.uint32)
    half = D_MODEL // 2
    hf_out[...] = bits[:, half:] | (bits[:, :half] >> 16)

    logits = _dot(hf, wr_ref[...]) + br_ref[...]
    lane = lax.broadcasted_iota(jnp.int32, logits.shape, 1)
    neg = -jnp.inf
    big = 4 * LANES
    is_g = lane < N_GROUPS
    lg = jnp.where(is_g, logits, neg)
    mg = jnp.max(lg, axis=1, keepdims=True)
    grp = jnp.min(jnp.where(lg == mg, lane, big), axis=1, keepdims=True)
    gate_g = 1.0 / jnp.sum(jnp.where(is_g, jnp.exp(logits - mg), 0.0), axis=1, keepdims=True)
    e_lane = lane - N_GROUPS
    in_grp = (e_lane >= 0) & (e_lane < N_EXPERTS) & ((e_lane >> 3) == grp)
    le = jnp.where(in_grp, logits, neg)
    v1 = jnp.max(le, axis=1, keepdims=True)
    i1 = jnp.min(jnp.where(le == v1, lane, big), axis=1, keepdims=True)
    le2 = jnp.where(lane == i1, neg, le)
    v2 = jnp.max(le2, axis=1, keepdims=True)
    i2 = jnp.min(jnp.where(le2 == v2, lane, big), axis=1, keepdims=True)
    e = jnp.exp(v2 - v1)
    w1 = gate_g / (1.0 + e)
    w2 = gate_g * e / (1.0 + e)
    route = jnp.where(lane == 0, (i1 - N_GROUPS).astype(F32),
                      jnp.where(lane == 1, (i2 - N_GROUPS).astype(F32),
                                jnp.where(lane == 2, w1, jnp.where(lane == 3, w2, 0.0))))
    route_out[...] = route


def _row_tile(n, limit):
    t = limit - limit % 16
    while t >= 16:
        if n % t == 0:
            return t
        t -= 16
    raise ValueError(f"no row tile for {n}")


def _outproj(x_all, o_mla, o_gdn, gates, w_out, gn_t, norm_ffn, w_r, b_r, row0):
    n = o_mla.shape[0]
    tm = _pick(n, (512, 256, 128, 64, 32, 16))
    assert row0 % tm == 0
    r0 = row0 // tm
    full = lambda shape: pl.BlockSpec(shape, lambda i: (0,) * len(shape))
    row = lambda w: pl.BlockSpec((tm, w), lambda i: (i, 0))
    shared = lambda w, j=0: pl.BlockSpec((tm, w), lambda i, j=j: (r0 + i, j))
    return pl.pallas_call(
        _outproj_kernel,
        out_shape=(jax.ShapeDtypeStruct((n, D_MODEL), F32),
                   jax.ShapeDtypeStruct((n, D_MODEL // 2), jnp.uint32),
                   jax.ShapeDtypeStruct((n, LANES), F32)),
        grid=(n // tm,),
        in_specs=[shared(D_MODEL), row(D_MODEL), row(D_MODEL),
                  shared(D_MODEL, P_Z // D_MODEL), shared(D_MODEL, P_GM // D_MODEL), shared(D_MODEL, P_GG // D_MODEL),
                  full((D_MODEL, D_MODEL)), full((1, D_MODEL)), full((1, D_MODEL)),
                  full((D_MODEL, LANES)), full((1, LANES))],
        out_specs=(row(D_MODEL), row(D_MODEL // 2), row(LANES)),
        compiler_params=pltpu.CompilerParams(dimension_semantics=("parallel",), vmem_limit_bytes=VMEM_LIMIT),
        name="outproj_route",
    )(x_all, o_mla, o_gdn, gates, gates, gates, w_out, gn_t, norm_ffn, w_r, b_r)


def _moe_plan(eid, wgt, nsp):
    nt = eid.shape[0]
    ts = nt // nsp
    n_asg = ts * TOP_K
    e = eid.reshape(nsp, n_asg)
    ids = jnp.broadcast_to(jnp.arange(n_asg, dtype=jnp.int32), e.shape)
    _, ids_s, w_s = lax.sort((e, ids, wgt.reshape(nsp, n_asg)), dimension=1, num_keys=1, is_stable=True)
    rows_s = (ids_s % TOP_K) * (ts + 8) + ids_s // TOP_K
    counts = jnp.sum((e[..., None] == jnp.arange(N_EXPERTS, dtype=jnp.int32)).astype(jnp.int32), axis=1)
    run_start = jnp.cumsum(counts, axis=1) - counts
    tail = lambda v, dt: jnp.full((nsp, MOE_BLOCK), v, dt)
    tok_t = jnp.concatenate([ids_s // TOP_K, tail(ts, jnp.int32)], axis=1)
    rows_t = jnp.concatenate([rows_s, tail(ts, jnp.int32)], axis=1)
    w_t = jnp.concatenate([w_s, tail(0.0, F32)], axis=1)
    as_i32 = lambda a: a.astype(jnp.int32).reshape(-1)
    scalars = (as_i32(run_start), as_i32(counts), as_i32(tok_t), as_i32(rows_t), w_t.reshape(-1))
    return scalars, ts


def _split_pieces(k, ts, n_prompt, n_sample):
    lo, hi = k * ts, (k + 1) * ts
    pieces = []
    if lo < n_prompt:
        pieces.append((0, lo, 0, min(hi, n_prompt) - lo))
    if hi > n_prompt:
        start = max(lo, n_prompt)
        pieces.append((1, start - n_prompt, start - lo, hi - start))
    assert hi <= n_prompt + n_sample
    return pieces


def _moe_kernel(start_ref, cnt_ref, tok_ref, rows_ref, w_ref, hf_p, hf_q, xmid_p, xmid_q,
                wg_ref, wu_ref, wd_ref, nfin_ref, y_p, y_q, hf_s, comb, acc, xb, yb, sem,
                *, ts, rc, nsp, n_prompt, n_sample):
    s = pl.program_id(0)
    j = pl.program_id(1)
    half = D_MODEL // 2
    stride = ts + 8

    def zero_spare_rows():
        hf_s[ts:ts + 8, :] = jnp.zeros((8, half), jnp.uint32)

    def split_copies(k, load):
        cps = []
        for src, r0, l0, n in _split_pieces(k, ts, n_prompt, n_sample):
            if load:
                cps.append(pltpu.make_async_copy((hf_p, hf_q)[src].at[pl.ds(r0, n)], hf_s.at[pl.ds(l0, n)],
                                                 sem.at[0, src]))
                cps.append(pltpu.make_async_copy((xmid_p, xmid_q)[src].at[pl.ds(r0, n)], acc.at[pl.ds(l0, n)],
                                                 sem.at[1, src]))
            else:
                cps.append(pltpu.make_async_copy(acc.at[pl.ds(l0, n)], (y_p, y_q)[src].at[pl.ds(r0, n)],
                                                 sem.at[2, src]))
        return cps

    def run_copies(load, between=None):
        for k in range(nsp):
            @pl.when(s == k)
            def _(k=k):
                cps = split_copies(k, load)
                for cp in cps:
                    cp.start()
                if between is not None:
                    between()
                for cp in cps:
                    cp.wait()

    @pl.when(j == 0)
    def _():
        run_copies(True, zero_spare_rows)

    cnt = cnt_ref[s * N_EXPERTS + j]
    first = s * (ts * TOP_K + MOE_BLOCK) + start_ref[s * N_EXPERTS + j]

    def block(p0, n):
        for r in range(n):
            xb[r:r + 1, :] = hf_s[pl.ds(tok_ref[p0 + r], 1), :]
        bits = xb[0:n, :]
        lo = lax.bitcast_convert_type(bits << 16, F32).astype(BF16)
        hi = lax.bitcast_convert_type(bits & jnp.uint32(0xFFFF0000), F32).astype(BF16)
        g = _dot(lo, wg_ref[0, :half, :]) + _dot(hi, wg_ref[0, half:, :])
        u = _dot(lo, wu_ref[0, :half, :]) + _dot(hi, wu_ref[0, half:, :])
        yb[0:n, :] = _dot((_silu(g) * u).astype(BF16), wd_ref[0])
        for r in range(n):
            comb[pl.ds(rows_ref[p0 + r], 1), :] = w_ref[p0 + r] * yb[r:r + 1, :]

    def full_block(i, carry):
        block(first + i * MOE_BLOCK, MOE_BLOCK)
        return carry

    n_full = cnt // MOE_BLOCK
    rest = cnt - n_full * MOE_BLOCK
    lax.fori_loop(0, n_full, full_block, 0)

    @pl.when(rest > MOE_BLOCK // 2)
    def _():
        block(first + n_full * MOE_BLOCK, MOE_BLOCK)

    @pl.when((rest > 0) & (rest <= MOE_BLOCK // 2))
    def _():
        block(first + n_full * MOE_BLOCK, MOE_BLOCK // 2)

    @pl.when(j == N_EXPERTS - 1)
    def _():
        def body(i, carry):
            r0 = pl.multiple_of(i * rc, 8)
            rows = pl.ds(r0, rc)
            moe = comb[rows, :]
            for k in range(1, TOP_K):
                moe = moe + comb[pl.ds(k * stride + r0, rc), :]
            acc[rows, :] = _rms(acc[rows, :] + moe, nfin_ref[...])
            return carry

        lax.fori_loop(0, ts // rc, body, 0)
        run_copies(False)


def _moe(plan, hf_p, hf_q, xmid_p, xmid_q, wg, wu, wd, norm_final):
    scalars, ts = plan
    n_prompt, n_sample = xmid_p.shape[0], xmid_q.shape[0]
    nsp = (n_prompt + n_sample) // ts
    rc = _row_tile(ts, 256) if ts % 16 == 0 else 8
    kern = functools.partial(_moe_kernel, ts=ts, rc=rc, nsp=nsp, n_prompt=n_prompt, n_sample=n_sample)
    hbm = pl.BlockSpec(memory_space=pl.ANY)
    expert = lambda shape: pl.BlockSpec((1,) + shape, lambda s, j, *_: (j, 0, 0))
    grid_spec = pltpu.PrefetchScalarGridSpec(
        num_scalar_prefetch=len(scalars),
        grid=(nsp, N_EXPERTS),
        in_specs=[
            hbm, hbm, hbm, hbm,
            expert((D_MODEL, D_EXPERT)), expert((D_MODEL, D_EXPERT)), expert((D_EXPERT, D_MODEL)),
            pl.BlockSpec((1, D_MODEL), lambda s, j, *_: (0, 0)),
        ],
        out_specs=(hbm, hbm),
        scratch_shapes=[
            pltpu.VMEM((ts + 8, D_MODEL // 2), jnp.uint32),
            pltpu.VMEM((TOP_K * (ts + 8), D_MODEL), F32),
            pltpu.VMEM((ts, D_MODEL), F32),
            pltpu.VMEM((MOE_BLOCK, D_MODEL // 2), jnp.uint32),
            pltpu.VMEM((MOE_BLOCK, D_MODEL), F32),
            pltpu.SemaphoreType.DMA((3, 2)),
        ],
    )
    return pl.pallas_call(
        kern,
        out_shape=(jax.ShapeDtypeStruct((n_prompt, D_MODEL), F32), jax.ShapeDtypeStruct((n_sample, D_MODEL), F32)),
        grid_spec=grid_spec,
        compiler_params=pltpu.CompilerParams(
            dimension_semantics=("arbitrary", "arbitrary"), vmem_limit_bytes=VMEM_LIMIT),
        name="moe",
    )(*scalars, hf_p, hf_q, xmid_p, xmid_q, wg, wu, wd, norm_final)


def _pack_w_in(w):
    kr = w[:, _OFF_KV + KV_LORA:_OFF_QKV]
    kr_sw = jnp.concatenate([kr[:, ROPE_DIM // 2:], kr[:, :ROPE_DIM // 2]], axis=1)
    small = jnp.concatenate([w[:, _OFF_B:_OFF_A], w[:, _OFF_A:_OFF_GM],
                             jnp.zeros((D_MODEL, SM_KR - 2 * GDN_HEADS), w.dtype), kr, kr_sw], axis=1)
    packed = jnp.concatenate([w[:, _OFF_QKV:_OFF_Z], w[:, _OFF_KV:_OFF_KV + KV_LORA], small, w[:, :Q_LORA],
                              w[:, _OFF_Z:_OFF_B], w[:, _OFF_GM:_OFF_GG], w[:, _OFF_GG:]], axis=1)
    return packed.astype(BF16)


def _pack_mla_weights(w_uq, w_uk, w_uv):
    zq = jnp.zeros((Q_LORA, MLA_HEADS, LANES - NOPE_DIM - ROPE_DIM), w_uq.dtype)
    wq = jnp.concatenate([w_uq, zq], axis=2).reshape(Q_LORA, MLA_HEADS * LANES)
    rope = w_uq[:, :, NOPE_DIM:]
    rope_sw = jnp.concatenate([rope[..., ROPE_DIM // 2:], rope[..., :ROPE_DIM // 2]], axis=2)
    wqs = jnp.concatenate([jnp.zeros((Q_LORA, MLA_HEADS, NOPE_DIM), w_uq.dtype), rope_sw, zq], axis=2)
    wqs = wqs.reshape(Q_LORA, MLA_HEADS * LANES)
    wk = jnp.concatenate([w_uk, jnp.zeros((KV_LORA, MLA_HEADS, LANES - NOPE_DIM), w_uk.dtype)], axis=2)
    wk = wk.reshape(KV_LORA, MLA_HEADS * LANES)
    zv = jnp.zeros((KV_LORA, MLA_HEADS // 2, V_DIM), w_uv.dtype)
    wv = jnp.stack([jnp.concatenate([w_uv[:, 0::2], zv], axis=2),
                    jnp.concatenate([zv, w_uv[:, 1::2]], axis=2)], axis=2)
    wv = wv.reshape(KV_LORA, MLA_HEADS * LANES)
    return wq.astype(BF16), wqs.astype(BF16), wk.astype(BF16), wv.astype(BF16)


def _rope_tables(pos):
    inv_freq = ROPE_THETA ** (-jnp.arange(0, ROPE_DIM, 2, dtype=F32) / ROPE_DIM)
    ang = pos.astype(F32)[:, None] * inv_freq[None, :]
    cos, sin = jnp.cos(ang), jnp.sin(ang)
    n = pos.shape[0]
    cos_t = jnp.concatenate([jnp.ones((n, NOPE_DIM), F32), cos, cos, jnp.zeros((n, ROPE_DIM), F32)], axis=1)
    sin_t = jnp.concatenate([jnp.zeros((n, NOPE_DIM), F32), -sin, sin, jnp.zeros((n, ROPE_DIM), F32)], axis=1)
    return cos_t, sin_t


def _head_lanes(v):
    return jnp.zeros((1, LANES), F32).at[0, SM_A:SM_A + GDN_HEADS].set(v.astype(F32))


def _moe_splits(nt):
    for nsp in (6, 3, 4, 2, 1):
        if nt % (nsp * 8) == 0:
            return nsp
    return 1


def kernel(x_prompt, x_sample, cache_kv_latent, cache_k_rope, page_table, state_conv, state_gdn, meta_tokens,
           norm_mix, w_in, q_norm, w_uq, kv_norm, w_uk, w_uv, conv_w, a_log, dt_bias, gdn_norm, w_out, norm_ffn,
           w_group, b_group, w_router, b_router, w_gate, w_up, w_down, norm_final):
    batch, seq, _ = x_prompt.shape
    ns, dec_seq, _ = x_sample.shape
    assert dec_seq == 1 and w_in.shape[0] == 1 and seq % CHUNK == 0
    n_pages = page_table.shape[1]
    n_prompt = batch * seq
    nt = n_prompt + ns
    meta_row = nt
    tq = _pick(seq, ATTN_TQ)
    n_rows = -(-(nt + N_META) // tq) * tq
    assert n_prompt % ns == 0 and nt % N_META == 0 and ns % LANES == 0 and meta_row % tq + N_META <= tq

    x_all = jnp.concatenate([x_prompt.reshape(n_prompt, D_MODEL), x_sample.reshape(ns, D_MODEL),
                             meta_tokens.astype(x_prompt.dtype),
                             jnp.zeros((n_rows - nt - N_META, D_MODEL), x_prompt.dtype)], axis=0)
    pos = jnp.concatenate([N_META + jnp.arange(seq), jnp.full((1,), n_pages * PAGE_SIZE), jnp.arange(N_META)])
    by_row = lambda t: jnp.concatenate([jnp.tile(t[:seq], (batch, 1)), jnp.broadcast_to(t[seq], (ns, LANES)),
                                        t[seq + 1:], jnp.zeros((n_rows - nt - N_META, LANES), F32)], axis=0)
    cos_t, sin_t = (by_row(t) for t in _rope_tables(pos))
    w_packed = _pack_w_in(w_in[0])
    wq, wqs, wk, wv = _pack_mla_weights(w_uq[0], w_uk[0], w_uv[0])
    wukt = jnp.concatenate([jnp.transpose(w_uk[0], (1, 2, 0)),
                            jnp.zeros((MLA_HEADS, LANES - NOPE_DIM, KV_LORA), w_uk.dtype)], axis=1).astype(BF16)
    alog_v, dtb_v = _head_lanes(a_log[0]), _head_lanes(dt_bias[0])
    cw = conv_w[0].astype(F32)
    gn_t = jnp.tile(gdn_norm[0].astype(F32), GDN_HEADS)[None]
    w_r = jnp.concatenate([w_group[0], w_router[0],
                           jnp.zeros((D_MODEL, LANES - N_GROUPS - N_EXPERTS), w_group.dtype)], axis=1).astype(BF16)
    b_r = jnp.concatenate([b_group[0], b_router[0], jnp.zeros((LANES - N_GROUPS - N_EXPERTS,), b_group.dtype)])[None]
    wg, wu, wd = w_gate[0].astype(BF16), w_up[0].astype(BF16), w_down[0].astype(BF16)

    proj, gates = _inproj(x_all, norm_mix[0][None].astype(F32), w_packed)
    qt, k, vt, ckv, krot = _mla_prep(proj, cos_t, sin_t, q_norm[0][None].astype(F32), kv_norm[0][None].astype(F32),
                                     wq.T, wqs.T, wk, wv.T, tq)

    o_mla_p = _attn_prompt(qt, k, vt, batch, seq, meta_row, tq)
    ql, qrt = _q_absorb(qt, wukt, ns, n_prompt)
    o_lat = _mla_sample(page_table, jnp.transpose(ql, (1, 0, 2)), jnp.transpose(qrt, (2, 0, 1)),
                        ckv[n_prompt:nt].reshape(ns, 1, KV_LORA), krot[n_prompt:nt].reshape(ns, 1, LANES),
                        cache_kv_latent[0], jnp.swapaxes(cache_k_rope[0], 1, 2))
    o_mla_s = _o_proj_sample(jnp.transpose(o_lat, (1, 0, 2)), wv)

    s_meta = _gdn_meta(proj, cw, alog_v, dtb_v, meta_row)
    terms = _gdn_terms(proj, cw, alog_v, dtb_v, batch, seq, meta_row)
    o_gdn_p, gdn_p = _gdn_scan(*terms, s_meta, batch, seq)
    o_gdn_s, gdn_s = _gdn_sample(proj, state_conv[0], state_gdn[0], cw, alog_v, dtb_v, n_prompt)

    tail = (gates, w_out[0].astype(BF16), gn_t, norm_ffn[0][None].astype(F32), w_r, b_r.astype(F32))
    xmid_p, hf_p, route_p = _outproj(x_all, o_mla_p, o_gdn_p, *tail, 0)
    xmid_s, hf_s, route_s = _outproj(x_all, o_mla_s, o_gdn_s, *tail, n_prompt)
    route = jnp.concatenate([route_p[:, :2 * TOP_K], route_s[:, :2 * TOP_K]], axis=0)
    plan = _moe_plan(route[:, :TOP_K].astype(jnp.int32), route[:, TOP_K:], _moe_splits(nt))
    y_p, y_s = _moe(plan, hf_p, hf_s, xmid_p, xmid_s, wg, wu, wd, norm_final[None].astype(F32))

    def with_meta(rows, width):
        meta = jnp.broadcast_to(rows[meta_row:meta_row + N_META][None], (batch, N_META, width))
        return jnp.concatenate([meta, rows[:n_prompt].reshape(batch, seq, width)], axis=1)[None]

    k_rope = krot[:, SM_KR:SM_KR + ROPE_DIM]
    conv_p = jnp.stack([proj[(b + 1) * seq - (CONV_W - 1):(b + 1) * seq, :GDN_QKV] for b in range(batch)])
    conv_s = jnp.concatenate([state_conv[0][:, 1:].astype(F32), proj[n_prompt:nt, None, :GDN_QKV]], axis=1)
    return (y_p.reshape(batch, seq, D_MODEL), y_s.reshape(ns, 1, D_MODEL),
            with_meta(ckv, KV_LORA), with_meta(k_rope, ROPE_DIM),
            ckv[n_prompt:nt].reshape(1, ns, 1, KV_LORA), k_rope[n_prompt:nt].reshape(1, ns, 1, ROPE_DIM),
            conv_p[None], conv_s[None], gdn_p[None], gdn_s[None])
```

```python
import functools

import jax
import jax.numpy as jnp
from jax import lax
from jax.experimental import pallas as pl
from jax.experimental.pallas import tpu as pltpu

F32 = jnp.float32
BF16 = jnp.bfloat16
HIGHEST = lax.Precision.HIGHEST

D_MODEL = 1024
N_META = 16
RMS_EPS = 1e-6
MLA_HEADS = 16
Q_LORA = 384
KV_LORA = 256
NOPE_DIM = 64
ROPE_DIM = 32
V_DIM = 64
ROPE_THETA = 10000.0
MLA_SCALE = (NOPE_DIM + ROPE_DIM) ** -0.5
PAGE_SIZE = 128
GDN_HEADS = 8
GDN_DK = 128
GDN_DV = 128
GDN_KEY = GDN_HEADS * GDN_DK
GDN_QKV = 3 * GDN_KEY
CONV_W = 4
CHUNK = 64
N_GROUPS = 4
EXPERTS_PER_GROUP = 8
N_EXPERTS = 32
TOP_K = 2
D_EXPERT = 256
MOE_BLOCK = 128

_OFF_KV = Q_LORA
_OFF_QKV = _OFF_KV + KV_LORA + ROPE_DIM
_OFF_Z = _OFF_QKV + GDN_QKV
_OFF_B = _OFF_Z + GDN_KEY
_OFF_A = _OFF_B + GDN_HEADS
_OFF_GM = _OFF_A + GDN_HEADS
_OFF_GG = _OFF_GM + D_MODEL
P_QKV = 0
P_KVC = 3072
P_SMALL = 3328
P_QD = 3456
P_F32 = 3840
P_Z = 0
P_GM = 1024
P_GG = 2048
P_BF16 = 3072
P_TOTAL = P_F32 + P_BF16
INPROJ_TN = 768
SM_B = 0
SM_A = 8
SM_KR = 64

LANES = 128
VMEM_LIMIT = 56 * 1024 * 1024
ATTN_TQ = (512, 256, 128)


def _pick(n, candidates):
    for c in candidates:
        if n % c == 0:
            return c
    raise ValueError(f"no tile for {n} in {candidates}")


def _dot(a, b):
    return jnp.dot(a, b, preferred_element_type=F32)


def _dot_nt(a, b):
    return lax.dot_general(a, b, (((1,), (1,)), ((), ())), preferred_element_type=F32)


def _dot_tn(a, b):
    return lax.dot_general(a, b, (((0,), (0,)), ((), ())), preferred_element_type=F32)


def _sigmoid(x):
    return 1.0 / (1.0 + jnp.exp(-x))


def _silu(x):
    return x * _sigmoid(x)


def _softplus(x):
    return jnp.maximum(x, 0.0) + jnp.log1p(jnp.exp(-jnp.abs(x)))


def _rms(x, w):
    return x * lax.rsqrt(jnp.mean(x * x, axis=-1, keepdims=True) + RMS_EPS) * w


def _inproj_kernel(x_ref, nw_ref, w_ref, of_ref, ob_ref, hn_ref):
    j = pl.program_id(1)
    nf = P_F32 // INPROJ_TN

    @pl.when(j == 0)
    def _():
        hn_ref[...] = _rms(x_ref[...], nw_ref[...]).astype(BF16)

    r = _dot(hn_ref[...], w_ref[...])

    @pl.when(j < nf)
    def _():
        of_ref[...] = r

    @pl.when(j >= nf)
    def _():
        ob_ref[...] = r.astype(BF16)


def _inproj(x_all, norm_w, w_packed):
    r = x_all.shape[0]
    tm = _pick(r, (1536, 1280, 768, 640, 512, 256, 128))
    tn = INPROJ_TN
    nf = P_F32 // tn
    return pl.pallas_call(
        _inproj_kernel,
        out_shape=(jax.ShapeDtypeStruct((r, P_F32), F32), jax.ShapeDtypeStruct((r, P_BF16), BF16)),
        grid=(r // tm, P_TOTAL // tn),
        in_specs=[
            pl.BlockSpec((tm, D_MODEL), lambda i, j: (i, 0)),
            pl.BlockSpec((1, D_MODEL), lambda i, j: (0, 0)),
            pl.BlockSpec((D_MODEL, tn), lambda i, j: (0, j)),
        ],
        out_specs=(pl.BlockSpec((tm, tn), lambda i, j: (i, jnp.minimum(j, nf - 1))),
                   pl.BlockSpec((tm, tn), lambda i, j: (i, jnp.maximum(j - nf, 0)))),
        scratch_shapes=[pltpu.VMEM((tm, D_MODEL), BF16)],
        compiler_params=pltpu.CompilerParams(
            dimension_semantics=("parallel", "arbitrary"), vmem_limit_bytes=VMEM_LIMIT),
        name="inproj",
    )(x_all, norm_w, w_packed)


LOG2E = 1.4426950408889634


def _mla_prep_kernel(qd_ref, kvc_ref, sm_ref, c_ref, s_ref, qn_ref, kvn_ref, wqt_ref, wqst_ref, wk_ref, wvt_ref,
                     qt_out, k_out, vt_out, ckv_out, kr_out):
    cos = c_ref[...]
    sin = s_ref[...]
    cos_t, sin_t = cos.T, sin.T
    qn_t = _rms(qd_ref[...], qn_ref[...]).T.astype(BF16)
    qt = _dot(wqt_ref[...], qn_t)
    qst = _dot(wqst_ref[...], qn_t)
    for h in range(MLA_HEADS):
        sl = slice(h * LANES, (h + 1) * LANES)
        qt_out[h, 0] = ((qt[sl] * cos_t + qst[sl] * sin_t) * (MLA_SCALE * LOG2E)).astype(BF16)
    ckv = _rms(kvc_ref[...], kvn_ref[...])
    ckv_out[...] = ckv
    sm = sm_ref[...]
    lane = lax.broadcasted_iota(jnp.int32, sm.shape, 1)
    cos_k = jnp.where((lane >= SM_KR) & (lane < SM_KR + ROPE_DIM), cos, 0.0)
    krot = sm * cos_k + pltpu.roll(sm, LANES - ROPE_DIM, 1) * sin
    kr_out[...] = krot
    kk = _dot(ckv.astype(BF16), wk_ref[...])
    vvt = _dot(wvt_ref[...], ckv.T.astype(BF16))
    for h in range(MLA_HEADS):
        sl = slice(h * LANES, (h + 1) * LANES)
        k_out[h] = (kk[:, sl] + krot).astype(BF16)
        vt_out[h, 0] = vvt[sl].astype(BF16)


def _mla_prep(proj, cos_t, sin_t, q_norm, kv_norm, wqt, wqst, wk, wvt, tm):
    r = proj.shape[0]
    hw = MLA_HEADS * LANES
    full = lambda shape: pl.BlockSpec(shape, lambda i: (0,) * len(shape))
    return pl.pallas_call(
        _mla_prep_kernel,
        out_shape=(
            jax.ShapeDtypeStruct((MLA_HEADS, r // tm, LANES, tm), BF16),
            jax.ShapeDtypeStruct((MLA_HEADS, r, LANES), BF16),
            jax.ShapeDtypeStruct((MLA_HEADS, r // tm, LANES, tm), BF16),
            jax.ShapeDtypeStruct((r, KV_LORA), F32),
            jax.ShapeDtypeStruct((r, LANES), F32),
        ),
        grid=(r // tm,),
        in_specs=[
            pl.BlockSpec((tm, Q_LORA), lambda i: (i, P_QD // Q_LORA)),
            pl.BlockSpec((tm, KV_LORA), lambda i: (i, P_KVC // KV_LORA)),
            pl.BlockSpec((tm, LANES), lambda i: (i, P_SMALL // LANES)),
            pl.BlockSpec((tm, LANES), lambda i: (i, 0)),
            pl.BlockSpec((tm, LANES), lambda i: (i, 0)),
            full((1, Q_LORA)), full((1, KV_LORA)),
            full((hw, Q_LORA)), full((hw, Q_LORA)), full((KV_LORA, hw)), full((hw, KV_LORA)),
        ],
        out_specs=(pl.BlockSpec((MLA_HEADS, 1, LANES, tm), lambda i: (0, i, 0, 0)),
                   pl.BlockSpec((MLA_HEADS, tm, LANES), lambda i: (0, i, 0)),
                   pl.BlockSpec((MLA_HEADS, 1, LANES, tm), lambda i: (0, i, 0, 0)),
                   pl.BlockSpec((tm, KV_LORA), lambda i: (i, 0)),
                   pl.BlockSpec((tm, LANES), lambda i: (i, 0))),
        compiler_params=pltpu.CompilerParams(dimension_semantics=("parallel",), vmem_limit_bytes=VMEM_LIMIT),
        name="mla_prep",
    )(proj, proj, proj, cos_t, sin_t, q_norm, kv_norm, wqt, wqst, wk, wvt)


def _attn_prompt_kernel(qt_ref, k_ref, vt_ref, km_ref, vmt_ref, o_ref, *, tq, meta_lane, nq, tiles_per_step):
    i = pl.program_id(2)
    tiles = (i,) if tiles_per_step == 1 else (i, nq - 1 - i)
    streams = [(t, h) for t in tiles for h in (0, 1)]
    half = tq // 2
    qt = [qt_ref[h, t] for t, h in streams]

    def update(qts, k_blk, vt_blk, carry, mask):
        ms, ls, accs = carry
        n = range(len(qts))
        s = [_dot(k_blk[x], qts[x]) for x in n]
        if mask is not None:
            s = [jnp.where(mask, v, -1e30) for v in s]
        m_new = [jnp.maximum(ms[x], jnp.max(s[x], axis=0, keepdims=True)) for x in n]
        a = [jnp.exp2(ms[x] - m_new[x]) for x in n]
        p = [jnp.exp2(s[x] - m_new[x]) for x in n]
        l_new = [a[x] * ls[x] + jnp.sum(p[x], axis=0, keepdims=True) for x in n]
        acc_new = [a[x] * accs[x] + _dot(vt_blk[x], p[x].astype(BF16)) for x in n]
        return m_new, l_new, acc_new

    s0 = [_dot(km_ref[h], q) for (_, h), q in zip(streams, qt)]
    m = [jnp.max(v, axis=0, keepdims=True) for v in s0]
    p0 = [jnp.exp2(v - mx) for v, mx in zip(s0, m)]
    l = [jnp.sum(v, axis=0, keepdims=True) for v in p0]
    acc = [_dot(vmt_ref[h, 0][:, meta_lane:meta_lane + N_META], v.astype(BF16)) for (_, h), v in zip(streams, p0)]

    for ti, t in enumerate(tiles):
        own = slice(2 * ti, 2 * ti + 2)

        def body(j, carry, own=own):
            rows = pl.ds(pl.multiple_of(j * tq, tq), tq)
            return update(qt[own], [k_ref[h, rows, :] for h in (0, 1)], [vt_ref[h, j] for h in (0, 1)], carry, None)

        m[own], l[own], acc[own] = lax.fori_loop(0, t, body, (m[own], l[own], acc[own]))

    key = lax.broadcasted_iota(jnp.int32, (half, tq), 0)
    qry = lax.broadcasted_iota(jnp.int32, (half, tq), 1)
    off = [pl.multiple_of(t * tq, tq) for t, _ in streams]
    vt_d = [vt_ref[h, t] for t, h in streams]
    m, l, acc = update(qt, [k_ref[h, pl.ds(o, half), :] for (_, h), o in zip(streams, off)],
                       [v[:, :half] for v in vt_d], (m, l, acc), key <= qry)
    late = lambda xs: [v[:, half:] for v in xs]
    mb, lb, accb = update(late(qt), [k_ref[h, pl.ds(pl.multiple_of(o + half, half), half), :]
                                     for (_, h), o in zip(streams, off)],
                          late(vt_d), (late(m), late(l), late(acc)), (key <= qry)[:, :half])
    ot = [jnp.concatenate([acc[x][:, :half] / l[x][:, :half], accb[x] / lb[x]], axis=1) for x in range(len(streams))]
    for ti, t in enumerate(tiles):
        o_ref[pl.ds(pl.multiple_of(t * tq, tq), tq), :] = (ot[2 * ti] + ot[2 * ti + 1]).T.astype(o_ref.dtype)


def _attn_prompt(qt, k, vt, batch, seq, meta_row, tq):
    nq = seq // tq
    tps = 2 if nq % 2 == 0 else 1
    kern = functools.partial(_attn_prompt_kernel, tq=tq, meta_lane=meta_row % tq, nq=nq, tiles_per_step=tps)
    tiled = lambda: pl.BlockSpec((2, nq, LANES, tq), lambda b, p, i: (p, b, 0, 0))
    return pl.pallas_call(
        kern,
        out_shape=jax.ShapeDtypeStruct((batch * seq, D_MODEL), BF16),
        grid=(batch, MLA_HEADS // 2, nq // tps),
        in_specs=[
            tiled(),
            pl.BlockSpec((2, seq, LANES), lambda b, p, i: (p, b, 0)),
            tiled(),
            pl.BlockSpec((2, N_META, LANES), lambda b, p, i: (p, meta_row // N_META, 0)),
            pl.BlockSpec((2, 1, LANES, tq), lambda b, p, i: (p, meta_row // tq, 0, 0)),
        ],
        out_specs=pl.BlockSpec((seq, LANES), lambda b, p, i: (b, p)),
        compiler_params=pltpu.CompilerParams(
            dimension_semantics=("parallel", "parallel", "arbitrary"), vmem_limit_bytes=VMEM_LIMIT),
        name="attn_prompt",
    )(qt, k, vt, k, vt)


def _gate_lanes(sm, alog_ref, dtb_ref):
    g = -jnp.exp(alog_ref[...]) * _softplus(sm + dtb_ref[...])
    beta = _sigmoid(sm)
    return g, beta


def _qkv_heads(xc):
    xf = _silu(xc)
    qs, ks, vs = [], [], []
    for h in range(GDN_HEADS):
        q = xf[:, h * GDN_DK:(h + 1) * GDN_DK]
        k = xf[:, GDN_KEY + h * GDN_DK:GDN_KEY + (h + 1) * GDN_DK]
        qs.append(q * lax.rsqrt(jnp.sum(q * q, axis=-1, keepdims=True) + RMS_EPS) * (GDN_DK ** -0.5))
        ks.append(k * lax.rsqrt(jnp.sum(k * k, axis=-1, keepdims=True) + RMS_EPS))
        vs.append(xf[:, 2 * GDN_KEY + h * GDN_DV:2 * GDN_KEY + (h + 1) * GDN_DV])
    return qs, ks, vs


def _split_bf16(x):
    hi = x.astype(BF16)
    return hi, (x - hi.astype(F32)).astype(BF16)


def _dot_split(a, b):
    return _dot(a[0], b[0]) + (_dot(a[0], b[1]) + _dot(a[1], b[0]))


def _unit_lower_inverses(mats, c):
    row = lax.broadcasted_iota(jnp.int32, (c, c), 0)
    col = lax.broadcasted_iota(jnp.int32, (c, c), 1)
    eye = jnp.where(row == col, 1.0, 0.0)
    ps = [-a for a in mats]
    ts = [eye + p for p in ps]
    span = 2
    while span < c:
        psp = [_split_bf16(p) for p in ps]
        ps = [_dot_split(p, p) for p in psp]
        psp = [_split_bf16(p) for p in ps]
        ts = [t + _dot_split(p, _split_bf16(t)) for p, t in zip(psp, ts)]
        span *= 2
    return ts


def _gdn_chunk_terms(xs, sm, conv_ref, alog_ref, dtb_ref, c):
    heads = range(GDN_HEADS)
    xc = xs[0] * conv_ref[0:1, :]
    for j in range(1, CONV_W):
        xc = xc + xs[j] * conv_ref[j:j + 1, :]
    qs, ks, vs = _qkv_heads(xc)
    g, beta = _gate_lanes(sm, alog_ref, dtb_ref)
    row = lax.broadcasted_iota(jnp.int32, (c, c), 0)
    col = lax.broadcasted_iota(jnp.int32, (c, c), 1)
    causal = col <= row
    strict = col < row
    gcum = jnp.dot(jnp.where(causal, 1.0, 0.0), g, precision=HIGHEST, preferred_element_type=F32)
    gcum_t = lax.dot_general(g, jnp.where(col >= row, 1.0, 0.0), (((0,), (0,)), ((), ())),
                             precision=HIGHEST, preferred_element_type=F32)
    gc = [gcum[:, SM_A + h:SM_A + h + 1] for h in heads]
    gr = [gcum_t[SM_A + h:SM_A + h + 1, :] for h in heads]
    bc = [beta[:, SM_B + h:SM_B + h + 1] for h in heads]
    decay = [jnp.where(causal, jnp.exp(jnp.where(causal, gc[h] - gr[h], 0.0)), 0.0) for h in heads]
    kb = [ks[h] * bc[h] for h in heads]
    kbf = [ks[h].astype(BF16) for h in heads]
    a = [jnp.where(strict, _dot_nt(kb[h].astype(BF16), kbf[h]) * decay[h], 0.0) for h in heads]
    t = _unit_lower_inverses(a, c)
    eg = [jnp.exp(gc[h]) for h in heads]
    sol = [_dot(t[h].astype(BF16), jnp.concatenate([vs[h] * bc[h], kb[h] * eg[h]], axis=1).astype(BF16))
           for h in heads]
    u = [sol[h][:, :GDN_DV] for h in heads]
    w = [sol[h][:, GDN_DV:] for h in heads]
    attn = [jnp.where(causal, _dot_nt(qs[h].astype(BF16), kbf[h]) * decay[h], 0.0) for h in heads]
    qg = [qs[h] * eg[h] for h in heads]
    g_last = gcum[c - 1:c, :]
    kd = [ks[h] * jnp.exp(g_last[:, SM_A + h:SM_A + h + 1] - gc[h]) for h in heads]
    return u, w, qg, kd, attn, jnp.exp(g_last)


def _conv_shifts(hist, x):
    xe = jnp.concatenate([hist, x], axis=0)
    return [pltpu.roll(xe, d, 0)[8:] for d in range(CONV_W - 1, 0, -1)] + [x]


def _gdn_meta_kernel(x_ref, sm_ref, conv_ref, alog_ref, dtb_ref, s_out):
    xs = _conv_shifts(jnp.zeros((8, GDN_QKV), F32), x_ref[...])
    u, _, _, kd, _, _ = _gdn_chunk_terms(xs, sm_ref[...], conv_ref, alog_ref, dtb_ref, N_META)
    for h in range(GDN_HEADS):
        s_out[h] = _dot_tn(kd[h].astype(BF16), u[h].astype(BF16))


def _gdn_meta(proj, conv_w, alog_v, dtb_v, meta_row):
    full = lambda shape: pl.BlockSpec(shape, lambda i: (0,) * len(shape))
    return pl.pallas_call(
        _gdn_meta_kernel,
        out_shape=jax.ShapeDtypeStruct((GDN_HEADS, GDN_DK, GDN_DV), F32),
        grid=(1,),
        in_specs=[
            pl.BlockSpec((N_META, GDN_QKV), lambda i: (meta_row // N_META, 0)),
            pl.BlockSpec((N_META, LANES), lambda i: (meta_row // N_META, P_SMALL // LANES)),
            full((CONV_W, GDN_QKV)), full((1, LANES)), full((1, LANES)),
        ],
        out_specs=full((GDN_HEADS, GDN_DK, GDN_DV)),
        compiler_params=pltpu.CompilerParams(vmem_limit_bytes=VMEM_LIMIT),
        name="gdn_meta",
    )(proj, proj, conv_w, alog_v, dtb_v)


def _gdn_terms_kernel(x_ref, hist_ref, sm_ref, conv_ref, alog_ref, dtb_ref,
                      u_out, w_out, qg_out, kd_out, attn_out, dec_out, *, cps):
    c = CHUNK
    hist = hist_ref[...]
    for cc in range(cps):
        rows = slice(cc * c, (cc + 1) * c)
        x = x_ref[rows, :]
        u, w, qg, kd, attn, dec = _gdn_chunk_terms(_conv_shifts(hist, x), sm_ref[rows, :], conv_ref, alog_ref,
                                                   dtb_ref, c)
        for h in range(GDN_HEADS):
            sl = slice(h * GDN_DV, (h + 1) * GDN_DV)
            u_out[rows, sl] = u[h]
            w_out[rows, sl] = w[h].astype(BF16)
            qg_out[rows, sl] = qg[h].astype(BF16)
            kd_out[rows, sl] = kd[h].astype(BF16)
            attn_out[h, rows, :] = attn[h].astype(BF16)
        dec_out[cc] = dec
        hist = x[c - 8:]


def _gdn_terms(proj, conv_w, alog_v, dtb_v, batch, seq, meta_row):
    nc = seq // CHUNK
    n = batch * seq
    cps = _pick(nc, (4, 2, 1))
    step = cps * CHUNK
    spb = nc // cps
    full = lambda shape: pl.BlockSpec(shape, lambda i: (0,) * len(shape))
    rows = lambda: pl.BlockSpec((step, D_MODEL), lambda i: (i, 0))

    def hist_index(i):
        return (jnp.where(i % spb == 0, (meta_row + N_META) // 8, i * (step // 8)) - 1, 0)

    return pl.pallas_call(
        functools.partial(_gdn_terms_kernel, cps=cps),
        out_shape=(jax.ShapeDtypeStruct((n, D_MODEL), F32),
                   jax.ShapeDtypeStruct((n, D_MODEL), BF16),
                   jax.ShapeDtypeStruct((n, D_MODEL), BF16),
                   jax.ShapeDtypeStruct((n, D_MODEL), BF16),
                   jax.ShapeDtypeStruct((GDN_HEADS, n, CHUNK), BF16),
                   jax.ShapeDtypeStruct((batch * nc, 1, LANES), F32)),
        grid=(batch * spb,),
        in_specs=[
            pl.BlockSpec((step, GDN_QKV), lambda i: (i, 0)),
            pl.BlockSpec((8, GDN_QKV), hist_index),
            pl.BlockSpec((step, LANES), lambda i: (i, P_SMALL // LANES)),
            full((CONV_W, GDN_QKV)), full((1, LANES)), full((1, LANES)),
        ],
        out_specs=(rows(), rows(), rows(), rows(),
                   pl.BlockSpec((GDN_HEADS, step, CHUNK), lambda i: (0, i, 0)),
                   pl.BlockSpec((cps, 1, LANES), lambda i: (i, 0, 0))),
        compiler_params=pltpu.CompilerParams(dimension_semantics=("parallel",), vmem_limit_bytes=VMEM_LIMIT),
        name="gdn_terms",
    )(proj, proj, proj, conv_w, alog_v, dtb_v)


def _gdn_scan_kernel(u_ref, w_ref, qg_ref, kd_ref, attn_ref, dec_ref, s0_ref, o_ref, s_out, st_ref, *, cpg, nseq):
    c = CHUNK
    streams = [(q, h) for q in range(nseq) for h in range(GDN_HEADS)]
    sl = [slice(h * GDN_DV, (h + 1) * GDN_DV) for h in range(GDN_HEADS)]

    @pl.when(pl.program_id(1) == 0)
    def _():
        for q in range(nseq):
            st_ref[q] = s0_ref[...]

    def chunk(ci, carry):
        rows = pl.ds(pl.multiple_of(ci * c, c), c)
        dec = [dec_ref[q, ci] for q in range(nseq)]
        s_old = [st_ref[q, h] for q, h in streams]
        sb = [s.astype(BF16) for s in s_old]
        lhs = [jnp.concatenate([w_ref[q, rows, sl[h]], qg_ref[q, rows, sl[h]]], axis=0) for q, h in streams]
        r = [_dot(a, b) for a, b in zip(lhs, sb)]
        vnb = [(u_ref[q, rows, sl[h]] - rr[:c]).astype(BF16) for (q, h), rr in zip(streams, r)]
        out = [rr[c:] + _dot(attn_ref[h, q, rows, :], v) for (q, h), rr, v in zip(streams, r, vnb)]
        upd = [_dot_tn(kd_ref[q, rows, sl[h]], v) for (q, h), v in zip(streams, vnb)]
        for x, (q, h) in enumerate(streams):
            o_ref[q, rows, sl[h]] = out[x].astype(o_ref.dtype)
            st_ref[q, h] = s_old[x] * dec[q][:, SM_A + h:SM_A + h + 1] + upd[x]
        return carry

    lax.fori_loop(0, cpg, chunk, 0)

    @pl.when(pl.program_id(1) == pl.num_programs(1) - 1)
    def _():
        s_out[...] = st_ref[...]


def _gdn_scan(u, w, qg, kd, attn, dec, s_meta, batch, seq):
    nc = seq // CHUNK
    nseq = 2 if batch % 2 == 0 else 1
    cpg = _pick(nc, (8, 4, 2, 1))
    ng = nc // cpg
    by_seq = lambda a: a.reshape(batch, seq, D_MODEL)
    rows = lambda: pl.BlockSpec((nseq, cpg * CHUNK, D_MODEL), lambda b, g: (b, g, 0))
    kern = functools.partial(_gdn_scan_kernel, cpg=cpg, nseq=nseq)
    o, s_fin = pl.pallas_call(
        kern,
        out_shape=(jax.ShapeDtypeStruct((batch, seq, D_MODEL), BF16),
                   jax.ShapeDtypeStruct((batch, GDN_HEADS, GDN_DK, GDN_DV), F32)),
        grid=(batch // nseq, ng),
        in_specs=[rows(), rows(), rows(), rows(),
                  pl.BlockSpec((GDN_HEADS, nseq, cpg * CHUNK, CHUNK), lambda b, g: (0, b, g, 0)),
                  pl.BlockSpec((nseq, cpg, 1, LANES), lambda b, g: (b, g, 0, 0)),
                  pl.BlockSpec((GDN_HEADS, GDN_DK, GDN_DV), lambda b, g: (0, 0, 0))],
        out_specs=(rows(),
                   pl.BlockSpec((nseq, GDN_HEADS, GDN_DK, GDN_DV), lambda b, g: (b, 0, 0, 0))),
        scratch_shapes=[pltpu.VMEM((nseq, GDN_HEADS, GDN_DK, GDN_DV), F32)],
        compiler_params=pltpu.CompilerParams(
            dimension_semantics=("parallel", "arbitrary"), vmem_limit_bytes=VMEM_LIMIT),
        name="gdn_scan",
    )(by_seq(u), by_seq(w), by_seq(qg), by_seq(kd), attn.reshape(GDN_HEADS, batch, seq, CHUNK),
      dec.reshape(batch, nc, 1, LANES), s_meta)
    return o.reshape(batch * seq, D_MODEL), s_fin


def _gdn_sample_kernel(x_ref, sm_ref, cs_ref, st_ref, conv_ref, alog_ref, dtb_ref, o_ref, s_out, *, nb):
    xc = x_ref[...] * conv_ref[CONV_W - 1:CONV_W, :]
    for j in range(CONV_W - 1):
        xc = xc + cs_ref[:, j, :] * conv_ref[j:j + 1, :]
    qs, ks, vs = _qkv_heads(xc)
    g, beta = _gate_lanes(sm_ref[...], alog_ref, dtb_ref)
    eg = jnp.exp(g)
    for h in range(GDN_HEADS):
        q_t = qs[h].T
        k_t = ks[h].T
        for b in range(nb):
            kcol = k_t[:, b:b + 1]
            s1 = st_ref[b, h] * eg[b:b + 1, SM_A + h:SM_A + h + 1]
            r = jnp.sum(s1 * kcol, axis=0, keepdims=True)
            delta = (vs[h][b:b + 1, :] - r) * beta[b:b + 1, SM_B + h:SM_B + h + 1]
            s2 = s1 + kcol * delta
            s_out[b, h] = s2
            o_ref[b:b + 1, h * GDN_DV:(h + 1) * GDN_DV] = jnp.sum(s2 * q_t[:, b:b + 1], axis=0, keepdims=True)


def _gdn_sample(proj, state_conv, state_gdn, conv_w, alog_v, dtb_v, row0):
    ns = state_gdn.shape[0]
    nb = 8
    full = lambda shape: pl.BlockSpec(shape, lambda i: (0,) * len(shape))
    kern = functools.partial(_gdn_sample_kernel, nb=nb)
    return pl.pallas_call(
        kern,
        out_shape=(jax.ShapeDtypeStruct((ns, D_MODEL), F32),
                   jax.ShapeDtypeStruct(state_gdn.shape, F32)),
        grid=(ns // nb,),
        in_specs=[
            pl.BlockSpec((nb, GDN_QKV), lambda i: (row0 // nb + i, 0)),
            pl.BlockSpec((nb, LANES), lambda i: (row0 // nb + i, P_SMALL // LANES)),
            pl.BlockSpec((nb, CONV_W - 1, GDN_QKV), lambda i: (i, 0, 0)),
            pl.BlockSpec((nb, GDN_HEADS, GDN_DK, GDN_DV), lambda i: (i, 0, 0, 0)),
            full((CONV_W, GDN_QKV)), full((1, LANES)), full((1, LANES)),
        ],
        out_specs=(pl.BlockSpec((nb, D_MODEL), lambda i: (i, 0)),
                   pl.BlockSpec((nb, GDN_HEADS, GDN_DK, GDN_DV), lambda i: (i, 0, 0, 0))),
        compiler_params=pltpu.CompilerParams(dimension_semantics=("parallel",), vmem_limit_bytes=VMEM_LIMIT),
        name="gdn_sample",
    )(proj, proj, state_conv, state_gdn, conv_w, alog_v, dtb_v)


def _q_absorb_kernel(qt_ref, wukt_ref, ql_out, qrt_out, *, lane0, ns):
    for h in range(MLA_HEADS):
        qt = qt_ref[h, 0][:, lane0:lane0 + ns]
        ql_out[h] = _dot_tn(qt, wukt_ref[h]).astype(BF16)
        qrt_out[h] = qt[NOPE_DIM:NOPE_DIM + ROPE_DIM]


def _q_absorb(qt, wukt, ns, row0):
    tm = qt.shape[-1]
    assert row0 % tm + ns <= tm and (row0 % tm) % LANES == 0
    return pl.pallas_call(
        functools.partial(_q_absorb_kernel, lane0=row0 % tm, ns=ns),
        out_shape=(jax.ShapeDtypeStruct((MLA_HEADS, ns, KV_LORA), BF16),
                   jax.ShapeDtypeStruct((MLA_HEADS, ROPE_DIM, ns), BF16)),
        grid=(1,),
        in_specs=[pl.BlockSpec((MLA_HEADS, 1, LANES, tm), lambda i: (0, row0 // tm, 0, 0)),
                  pl.BlockSpec((MLA_HEADS, LANES, KV_LORA), lambda i: (0, 0, 0))],
        out_specs=(pl.BlockSpec((MLA_HEADS, ns, KV_LORA), lambda i: (0, 0, 0)),
                   pl.BlockSpec((MLA_HEADS, ROPE_DIM, ns), lambda i: (0, 0, 0))),
        compiler_params=pltpu.CompilerParams(vmem_limit_bytes=VMEM_LIMIT),
        name="q_absorb",
    )(qt, wukt)


def _mla_sample_kernel(pt_ref, ql_ref, qr_ref, cn_ref, krn_ref, cc_hbm, cr_hbm, o_ref, cbuf, rbuf, sem,
                       *, n_pages, nsub):
    g = pl.program_id(0)

    def page_copies(seq, slot, i):
        page = pt_ref[seq * n_pages + i]
        return (pltpu.make_async_copy(cc_hbm.at[page], cbuf.at[slot, i], sem.at[slot, 0]),
                pltpu.make_async_copy(cr_hbm.at[page], rbuf.at[slot, :, pl.ds(i * PAGE_SIZE, PAGE_SIZE)],
                                      sem.at[slot, 1]))

    def start_pages(seq, slot):
        for i in range(n_pages):
            cc, cr = page_copies(seq, slot, i)
            cc.start()
            cr.start()

    def wait_pages(seq, slot):
        for i in range(n_pages):
            cc, cr = page_copies(seq, slot, i)
            cc.wait()
            cr.wait()

    def attend(slot):
        ql = ql_ref[slot]
        qr = qr_ref[slot]
        pps = n_pages // nsub
        subs = range(nsub)
        c = [cbuf[slot, i * pps:(i + 1) * pps].reshape(pps * PAGE_SIZE, KV_LORA).astype(BF16) for i in subs]
        s = [_dot_nt(ql, c[i]) + _dot(qr, rbuf[slot, :, i * pps * PAGE_SIZE:(i + 1) * pps * PAGE_SIZE].astype(BF16))
             for i in subs]
        ms = [jnp.max(s[i], axis=1, keepdims=True) for i in subs]
        p = [jnp.exp2(s[i] - ms[i]) for i in subs]
        ls = [jnp.sum(p[i], axis=1, keepdims=True) for i in subs]
        accs = [_dot(p[i].astype(BF16), c[i]) for i in subs]
        cn = cn_ref[slot]
        krn = krn_ref[slot][:, SM_KR:SM_KR + ROPE_DIM]
        ms.append(jnp.sum(ql.astype(F32) * cn, axis=1, keepdims=True)
                  + jnp.sum(qr.astype(F32) * krn, axis=1, keepdims=True))
        ls.append(jnp.ones_like(ms[-1]))
        accs.append(jnp.broadcast_to(cn, (MLA_HEADS, KV_LORA)))
        m = functools.reduce(jnp.maximum, ms)
        scale = [jnp.exp2(m_i - m) for m_i in ms]
        l = sum(a * l_i for a, l_i in zip(scale, ls))
        acc = sum(a * acc_i for a, acc_i in zip(scale, accs))
        o_ref[slot] = acc / l

    @pl.when(g == 0)
    def _():
        start_pages(0, 0)

    start_pages(2 * g + 1, 1)
    wait_pages(2 * g, 0)
    attend(0)

    @pl.when(g + 1 < pl.num_programs(0))
    def _():
        start_pages(2 * g + 2, 0)

    wait_pages(2 * g + 1, 1)
    attend(1)


def _mla_sample(page_table, ql, qr, c_new, kr_new, cache_c, cache_r):
    ns, n_pages = page_table.shape
    assert ns % 2 == 0
    kern = functools.partial(_mla_sample_kernel, n_pages=n_pages, nsub=_pick(n_pages, (8, 4, 2, 1)))
    grid_spec = pltpu.PrefetchScalarGridSpec(
        num_scalar_prefetch=1,
        grid=(ns // 2,),
        in_specs=[
            pl.BlockSpec((2, MLA_HEADS, KV_LORA), lambda b, pt: (b, 0, 0)),
            pl.BlockSpec((2, MLA_HEADS, ROPE_DIM), lambda b, pt: (b, 0, 0)),
            pl.BlockSpec((2, 1, KV_LORA), lambda b, pt: (b, 0, 0)),
            pl.BlockSpec((2, 1, LANES), lambda b, pt: (b, 0, 0)),
            pl.BlockSpec(memory_space=pl.ANY),
            pl.BlockSpec(memory_space=pl.ANY),
        ],
        out_specs=pl.BlockSpec((2, MLA_HEADS, KV_LORA), lambda b, pt: (b, 0, 0)),
        scratch_shapes=[pltpu.VMEM((2, n_pages, PAGE_SIZE, KV_LORA), F32),
                        pltpu.VMEM((2, ROPE_DIM, n_pages * PAGE_SIZE), F32),
                        pltpu.SemaphoreType.DMA((2, 2))],
    )
    return pl.pallas_call(
        kern,
        out_shape=jax.ShapeDtypeStruct((ns, MLA_HEADS, KV_LORA), F32),
        grid_spec=grid_spec,
        compiler_params=pltpu.CompilerParams(dimension_semantics=("arbitrary",), vmem_limit_bytes=VMEM_LIMIT),
        name="mla_sample",
    )(page_table.reshape(-1), ql, qr, c_new, kr_new, cache_c, cache_r)


def _o_proj_sample_kernel(ol_ref, wv_ref, o_ref):
    for p in range(MLA_HEADS // 2):
        acc = None
        for h in (2 * p, 2 * p + 1):
            part = _dot(ol_ref[h].astype(BF16), wv_ref[:, h * LANES:(h + 1) * LANES])
            acc = part if acc is None else acc + part
        o_ref[:, p * LANES:(p + 1) * LANES] = acc.astype(o_ref.dtype)


def _o_proj_sample(o_lat_t, wv):
    ns = o_lat_t.shape[1]
    return pl.pallas_call(
        _o_proj_sample_kernel,
        out_shape=jax.ShapeDtypeStruct((ns, D_MODEL), BF16),
        grid=(1,),
        in_specs=[pl.BlockSpec((MLA_HEADS, ns, KV_LORA), lambda i: (0, 0, 0)),
                  pl.BlockSpec((KV_LORA, MLA_HEADS * LANES), lambda i: (0, 0))],
        out_specs=pl.BlockSpec((ns, D_MODEL), lambda i: (0, 0)),
        compiler_params=pltpu.CompilerParams(vmem_limit_bytes=VMEM_LIMIT),
        name="o_proj_sample",
    )(o_lat_t, wv)


def _outproj_kernel(x_ref, om_ref, og_ref, z_ref, gm_ref, gg_ref, wo_ref, gn_ref, nf_ref, wr_ref, br_ref,
                    xmid_out, hf_out, route_out):
    og = og_ref[...].astype(F32)
    parts = []
    for h in range(GDN_HEADS):
        oh = og[:, h * GDN_DV:(h + 1) * GDN_DV]
        parts.append(oh * lax.rsqrt(jnp.mean(oh * oh, axis=-1, keepdims=True) + RMS_EPS))
    o_gdn = jnp.concatenate(parts, axis=1) * gn_ref[...] * _silu(z_ref[...].astype(F32))
    merged = (_sigmoid(gm_ref[...].astype(F32)) * om_ref[...].astype(F32)
              + _sigmoid(gg_ref[...].astype(F32)) * o_gdn)
    x_mid = x_ref[...] + _dot(merged.astype(BF16), wo_ref[...])
    xmid_out[...] = x_mid
    hf = _rms(x_mid, nf_ref[...]).astype(BF16)
    bits = lax.bitcast_convert_type(hf.astype(F32), jnp.uint32)
    half = D_MODEL // 2
    hf_out[...] = bits[:, half:] | (bits[:, :half] >> 16)

    logits = _dot(hf, wr_ref[...]) + br_ref[...]
    lane = lax.broadcasted_iota(jnp.int32, logits.shape, 1)
    neg = -jnp.inf
    big = 4 * LANES
    is_g = lane < N_GROUPS
    lg = jnp.where(is_g, logits, neg)
    mg = jnp.max(lg, axis=1, keepdims=True)
    grp = jnp.min(jnp.where(lg == mg, lane, big), axis=1, keepdims=True)
    gate_g = 1.0 / jnp.sum(jnp.where(is_g, jnp.exp(logits - mg), 0.0), axis=1, keepdims=True)
    e_lane = lane - N_GROUPS
    in_grp = (e_lane >= 0) & (e_lane < N_EXPERTS) & ((e_lane >> 3) == grp)
    le = jnp.where(in_grp, logits, neg)
    v1 = jnp.max(le, axis=1, keepdims=True)
    i1 = jnp.min(jnp.where(le == v1, lane, big), axis=1, keepdims=True)
    le2 = jnp.where(lane == i1, neg, le)
    v2 = jnp.max(le2, axis=1, keepdims=True)
    i2 = jnp.min(jnp.where(le2 == v2, lane, big), axis=1, keepdims=True)
    e = jnp.exp(v2 - v1)
    w1 = gate_g / (1.0 + e)
    w2 = gate_g * e / (1.0 + e)
    route = jnp.where(lane == 0, (i1 - N_GROUPS).astype(F32),
                      jnp.where(lane == 1, (i2 - N_GROUPS).astype(F32),
                                jnp.where(lane == 2, w1, jnp.where(lane == 3, w2, 0.0))))
    route_out[...] = route


def _row_tile(n, limit):
    t = limit - limit % 16
    while t >= 16:
        if n % t == 0:
            return t
        t -= 16
    raise ValueError(f"no row tile for {n}")


def _outproj(x_all, o_mla, o_gdn, gates, w_out, gn_t, norm_ffn, w_r, b_r, row0):
    n = o_mla.shape[0]
    tm = _pick(n, (512, 256, 128, 64, 32, 16))
    assert row0 % tm == 0
    r0 = row0 // tm
    full = lambda shape: pl.BlockSpec(shape, lambda i: (0,) * len(shape))
    row = lambda w: pl.BlockSpec((tm, w), lambda i: (i, 0))
    shared = lambda w, j=0: pl.BlockSpec((tm, w), lambda i, j=j: (r0 + i, j))
    return pl.pallas_call(
        _outproj_kernel,
        out_shape=(jax.ShapeDtypeStruct((n, D_MODEL), F32),
                   jax.ShapeDtypeStruct((n, D_MODEL // 2), jnp.uint32),
                   jax.ShapeDtypeStruct((n, LANES), F32)),
        grid=(n // tm,),
        in_specs=[shared(D_MODEL), row(D_MODEL), row(D_MODEL),
                  shared(D_MODEL, P_Z // D_MODEL), shared(D_MODEL, P_GM // D_MODEL), shared(D_MODEL, P_GG // D_MODEL),
                  full((D_MODEL, D_MODEL)), full((1, D_MODEL)), full((1, D_MODEL)),
                  full((D_MODEL, LANES)), full((1, LANES))],
        out_specs=(row(D_MODEL), row(D_MODEL // 2), row(LANES)),
        compiler_params=pltpu.CompilerParams(dimension_semantics=("parallel",), vmem_limit_bytes=VMEM_LIMIT),
        name="outproj_route",
    )(x_all, o_mla, o_gdn, gates, gates, gates, w_out, gn_t, norm_ffn, w_r, b_r)


def _moe_plan(eid, wgt, nsp):
    nt = eid.shape[0]
    ts = nt // nsp
    n_asg = ts * TOP_K
    e = eid.reshape(nsp, n_asg)
    ids = jnp.broadcast_to(jnp.arange(n_asg, dtype=jnp.int32), e.shape)
    _, ids_s, w_s = lax.sort((e, ids, wgt.reshape(nsp, n_asg)), dimension=1, num_keys=1, is_stable=True)
    rows_s = (ids_s % TOP_K) * (ts + 8) + ids_s // TOP_K
    counts = jnp.sum((e[..., None] == jnp.arange(N_EXPERTS, dtype=jnp.int32)).astype(jnp.int32), axis=1)
    run_start = jnp.cumsum(counts, axis=1) - counts
    tail = lambda v, dt: jnp.full((nsp, MOE_BLOCK), v, dt)
    tok_t = jnp.concatenate([ids_s // TOP_K, tail(ts, jnp.int32)], axis=1)
    rows_t = jnp.concatenate([rows_s, tail(ts, jnp.int32)], axis=1)
    w_t = jnp.concatenate([w_s, tail(0.0, F32)], axis=1)
    as_i32 = lambda a: a.astype(jnp.int32).reshape(-1)
    scalars = (as_i32(run_start), as_i32(counts), as_i32(tok_t), as_i32(rows_t), w_t.reshape(-1))
    return scalars, ts


def _split_pieces(k, ts, n_prompt, n_sample):
    lo, hi = k * ts, (k + 1) * ts
    pieces = []
    if lo < n_prompt:
        pieces.append((0, lo, 0, min(hi, n_prompt) - lo))
    if hi > n_prompt:
        start = max(lo, n_prompt)
        pieces.append((1, start - n_prompt, start - lo, hi - start))
    assert hi <= n_prompt + n_sample
    return pieces


def _moe_kernel(start_ref, cnt_ref, tok_ref, rows_ref, w_ref, hf_p, hf_q, xmid_p, xmid_q,
                wg_ref, wu_ref, wd_ref, nfin_ref, y_p, y_q, hf_s, comb, acc, xb, yb, sem,
                *, ts, rc, nsp, n_prompt, n_sample):
    s = pl.program_id(0)
    j = pl.program_id(1)
    half = D_MODEL // 2
    stride = ts + 8

    def zero_spare_rows():
        hf_s[ts:ts + 8, :] = jnp.zeros((8, half), jnp.uint32)

    def split_copies(k, load):
        cps = []
        for src, r0, l0, n in _split_pieces(k, ts, n_prompt, n_sample):
            if load:
                cps.append(pltpu.make_async_copy((hf_p, hf_q)[src].at[pl.ds(r0, n)], hf_s.at[pl.ds(l0, n)],
                                                 sem.at[0, src]))
                cps.append(pltpu.make_async_copy((xmid_p, xmid_q)[src].at[pl.ds(r0, n)], acc.at[pl.ds(l0, n)],
                                                 sem.at[1, src]))
            else:
                cps.append(pltpu.make_async_copy(acc.at[pl.ds(l0, n)], (y_p, y_q)[src].at[pl.ds(r0, n)],
                                                 sem.at[2, src]))
        return cps

    def run_copies(load, between=None):
        for k in range(nsp):
            @pl.when(s == k)
            def _(k=k):
                cps = split_copies(k, load)
                for cp in cps:
                    cp.start()
                if between is not None:
                    between()
                for cp in cps:
                    cp.wait()

    @pl.when(j == 0)
    def _():
        run_copies(True, zero_spare_rows)

    cnt = cnt_ref[s * N_EXPERTS + j]
    first = s * (ts * TOP_K + MOE_BLOCK) + start_ref[s * N_EXPERTS + j]

    def block(p0, n):
        for r in range(n):
            xb[r:r + 1, :] = hf_s[pl.ds(tok_ref[p0 + r], 1), :]
        bits = xb[0:n, :]
        lo = lax.bitcast_convert_type(bits << 16, F32).astype(BF16)
        hi = lax.bitcast_convert_type(bits & jnp.uint32(0xFFFF0000), F32).astype(BF16)
        g = _dot(lo, wg_ref[0, :half, :]) + _dot(hi, wg_ref[0, half:, :])
        u = _dot(lo, wu_ref[0, :half, :]) + _dot(hi, wu_ref[0, half:, :])
        yb[0:n, :] = _dot((_silu(g) * u).astype(BF16), wd_ref[0])
        for r in range(n):
            comb[pl.ds(rows_ref[p0 + r], 1), :] = w_ref[p0 + r] * yb[r:r + 1, :]

    def full_block(i, carry):
        block(first + i * MOE_BLOCK, MOE_BLOCK)
        return carry

    n_full = cnt // MOE_BLOCK
    rest = cnt - n_full * MOE_BLOCK
    lax.fori_loop(0, n_full, full_block, 0)

    @pl.when(rest > MOE_BLOCK // 2)
    def _():
        block(first + n_full * MOE_BLOCK, MOE_BLOCK)

    @pl.when((rest > 0) & (rest <= MOE_BLOCK // 2))
    def _():
        block(first + n_full * MOE_BLOCK, MOE_BLOCK // 2)

    @pl.when(j == N_EXPERTS - 1)
    def _():
        def body(i, carry):
            r0 = pl.multiple_of(i * rc, 8)
            rows = pl.ds(r0, rc)
            moe = comb[rows, :]
            for k in range(1, TOP_K):
                moe = moe + comb[pl.ds(k * stride + r0, rc), :]
            acc[rows, :] = _rms(acc[rows, :] + moe, nfin_ref[...])
            return carry

        lax.fori_loop(0, ts // rc, body, 0)
        run_copies(False)


def _moe(plan, hf_p, hf_q, xmid_p, xmid_q, wg, wu, wd, norm_final):
    scalars, ts = plan
    n_prompt, n_sample = xmid_p.shape[0], xmid_q.shape[0]
    nsp = (n_prompt + n_sample) // ts
    rc = _row_tile(ts, 256) if ts % 16 == 0 else 8
    kern = functools.partial(_moe_kernel, ts=ts, rc=rc, nsp=nsp, n_prompt=n_prompt, n_sample=n_sample)
    hbm = pl.BlockSpec(memory_space=pl.ANY)
    expert = lambda shape: pl.BlockSpec((1,) + shape, lambda s, j, *_: (j, 0, 0))
    grid_spec = pltpu.PrefetchScalarGridSpec(
        num_scalar_prefetch=len(scalars),
        grid=(nsp, N_EXPERTS),
        in_specs=[
            hbm, hbm, hbm, hbm,
            expert((D_MODEL, D_EXPERT)), expert((D_MODEL, D_EXPERT)), expert((D_EXPERT, D_MODEL)),
            pl.BlockSpec((1, D_MODEL), lambda s, j, *_: (0, 0)),
        ],
        out_specs=(hbm, hbm),
        scratch_shapes=[
            pltpu.VMEM((ts + 8, D_MODEL // 2), jnp.uint32),
            pltpu.VMEM((TOP_K * (ts + 8), D_MODEL), F32),
            pltpu.VMEM((ts, D_MODEL), F32),
            pltpu.VMEM((MOE_BLOCK, D_MODEL // 2), jnp.uint32),
            pltpu.VMEM((MOE_BLOCK, D_MODEL), F32),
            pltpu.SemaphoreType.DMA((3, 2)),
        ],
    )
    return pl.pallas_call(
        kern,
        out_shape=(jax.ShapeDtypeStruct((n_prompt, D_MODEL), F32), jax.ShapeDtypeStruct((n_sample, D_MODEL), F32)),
        grid_spec=grid_spec,
        compiler_params=pltpu.CompilerParams(
            dimension_semantics=("arbitrary", "arbitrary"), vmem_limit_bytes=VMEM_LIMIT),
        name="moe",
    )(*scalars, hf_p, hf_q, xmid_p, xmid_q, wg, wu, wd, norm_final)


def _pack_w_in(w):
    kr = w[:, _OFF_KV + KV_LORA:_OFF_QKV]
    kr_sw = jnp.concatenate([kr[:, ROPE_DIM // 2:], kr[:, :ROPE_DIM // 2]], axis=1)
    small = jnp.concatenate([w[:, _OFF_B:_OFF_A], w[:, _OFF_A:_OFF_GM],
                             jnp.zeros((D_MODEL, SM_KR - 2 * GDN_HEADS), w.dtype), kr, kr_sw], axis=1)
    packed = jnp.concatenate([w[:, _OFF_QKV:_OFF_Z], w[:, _OFF_KV:_OFF_KV + KV_LORA], small, w[:, :Q_LORA],
                              w[:, _OFF_Z:_OFF_B], w[:, _OFF_GM:_OFF_GG], w[:, _OFF_GG:]], axis=1)
    return packed.astype(BF16)


def _pack_mla_weights(w_uq, w_uk, w_uv):
    zq = jnp.zeros((Q_LORA, MLA_HEADS, LANES - NOPE_DIM - ROPE_DIM), w_uq.dtype)
    wq = jnp.concatenate([w_uq, zq], axis=2).reshape(Q_LORA, MLA_HEADS * LANES)
    rope = w_uq[:, :, NOPE_DIM:]
    rope_sw = jnp.concatenate([rope[..., ROPE_DIM // 2:], rope[..., :ROPE_DIM // 2]], axis=2)
    wqs = jnp.concatenate([jnp.zeros((Q_LORA, MLA_HEADS, NOPE_DIM), w_uq.dtype), rope_sw, zq], axis=2)
    wqs = wqs.reshape(Q_LORA, MLA_HEADS * LANES)
    wk = jnp.concatenate([w_uk, jnp.zeros((KV_LORA, MLA_HEADS, LANES - NOPE_DIM), w_uk.dtype)], axis=2)
    wk = wk.reshape(KV_LORA, MLA_HEADS * LANES)
    zv = jnp.zeros((KV_LORA, MLA_HEADS // 2, V_DIM), w_uv.dtype)
    wv = jnp.stack([jnp.concatenate([w_uv[:, 0::2], zv], axis=2),
                    jnp.concatenate([zv, w_uv[:, 1::2]], axis=2)], axis=2)
    wv = wv.reshape(KV_LORA, MLA_HEADS * LANES)
    return wq.astype(BF16), wqs.astype(BF16), wk.astype(BF16), wv.astype(BF16)


def _rope_tables(pos):
    inv_freq = ROPE_THETA ** (-jnp.arange(0, ROPE_DIM, 2, dtype=F32) / ROPE_DIM)
    ang = pos.astype(F32)[:, None] * inv_freq[None, :]
    cos, sin = jnp.cos(ang), jnp.sin(ang)
    n = pos.shape[0]
    cos_t = jnp.concatenate([jnp.ones((n, NOPE_DIM), F32), cos, cos, jnp.zeros((n, ROPE_DIM), F32)], axis=1)
    sin_t = jnp.concatenate([jnp.zeros((n, NOPE_DIM), F32), -sin, sin, jnp.zeros((n, ROPE_DIM), F32)], axis=1)
    return cos_t, sin_t


def _head_lanes(v):
    return jnp.zeros((1, LANES), F32).at[0, SM_A:SM_A + GDN_HEADS].set(v.astype(F32))


def _moe_splits(nt):
    for nsp in (6, 3, 4, 2, 1):
        if nt % (nsp * 8) == 0:
            return nsp
    return 1


def kernel(x_prompt, x_sample, cache_kv_latent, cache_k_rope, page_table, state_conv, state_gdn, meta_tokens,
           norm_mix, w_in, q_norm, w_uq, kv_norm, w_uk, w_uv, conv_w, a_log, dt_bias, gdn_norm, w_out, norm_ffn,
           w_group, b_group, w_router, b_router, w_gate, w_up, w_down, norm_final):
    batch, seq, _ = x_prompt.shape
    ns, dec_seq, _ = x_sample.shape
    assert dec_seq == 1 and w_in.shape[0] == 1 and seq % CHUNK == 0
    n_pages = page_table.shape[1]
    n_prompt = batch * seq
    nt = n_prompt + ns
    meta_row = nt
    tq = _pick(seq, ATTN_TQ)
    n_rows = -(-(nt + N_META) // tq) * tq
    assert n_prompt % ns == 0 and nt % N_META == 0 and ns % LANES == 0 and meta_row % tq + N_META <= tq

    x_all = jnp.concatenate([x_prompt.reshape(n_prompt, D_MODEL), x_sample.reshape(ns, D_MODEL),
                             meta_tokens.astype(x_prompt.dtype),
                             jnp.zeros((n_rows - nt - N_META, D_MODEL), x_prompt.dtype)], axis=0)
    pos = jnp.concatenate([N_META + jnp.arange(seq), jnp.full((1,), n_pages * PAGE_SIZE), jnp.arange(N_META)])
    by_row = lambda t: jnp.concatenate([jnp.tile(t[:seq], (batch, 1)), jnp.broadcast_to(t[seq], (ns, LANES)),
                                        t[seq + 1:], jnp.zeros((n_rows - nt - N_META, LANES), F32)], axis=0)
    cos_t, sin_t = (by_row(t) for t in _rope_tables(pos))
    w_packed = _pack_w_in(w_in[0])
    wq, wqs, wk, wv = _pack_mla_weights(w_uq[0], w_uk[0], w_uv[0])
    wukt = jnp.concatenate([jnp.transpose(w_uk[0], (1, 2, 0)),
                            jnp.zeros((MLA_HEADS, LANES - NOPE_DIM, KV_LORA), w_uk.dtype)], axis=1).astype(BF16)
    alog_v, dtb_v = _head_lanes(a_log[0]), _head_lanes(dt_bias[0])
    cw = conv_w[0].astype(F32)
    gn_t = jnp.tile(gdn_norm[0].astype(F32), GDN_HEADS)[None]
    w_r = jnp.concatenate([w_group[0], w_router[0],
                           jnp.zeros((D_MODEL, LANES - N_GROUPS - N_EXPERTS), w_group.dtype)], axis=1).astype(BF16)
    b_r = jnp.concatenate([b_group[0], b_router[0], jnp.zeros((LANES - N_GROUPS - N_EXPERTS,), b_group.dtype)])[None]
    wg, wu, wd = w_gate[0].astype(BF16), w_up[0].astype(BF16), w_down[0].astype(BF16)

    proj, gates = _inproj(x_all, norm_mix[0][None].astype(F32), w_packed)
    qt, k, vt, ckv, krot = _mla_prep(proj, cos_t, sin_t, q_norm[0][None].astype(F32), kv_norm[0][None].astype(F32),
                                     wq.T, wqs.T, wk, wv.T, tq)

    o_mla_p = _attn_prompt(qt, k, vt, batch, seq, meta_row, tq)
    ql, qrt = _q_absorb(qt, wukt, ns, n_prompt)
    o_lat = _mla_sample(page_table, jnp.transpose(ql, (1, 0, 2)), jnp.transpose(qrt, (2, 0, 1)),
                        ckv[n_prompt:nt].reshape(ns, 1, KV_LORA), krot[n_prompt:nt].reshape(ns, 1, LANES),
                        cache_kv_latent[0], jnp.swapaxes(cache_k_rope[0], 1, 2))
    o_mla_s = _o_proj_sample(jnp.transpose(o_lat, (1, 0, 2)), wv)

    s_meta = _gdn_meta(proj, cw, alog_v, dtb_v, meta_row)
    terms = _gdn_terms(proj, cw, alog_v, dtb_v, batch, seq, meta_row)
    o_gdn_p, gdn_p = _gdn_scan(*terms, s_meta, batch, seq)
    o_gdn_s, gdn_s = _gdn_sample(proj, state_conv[0], state_gdn[0], cw, alog_v, dtb_v, n_prompt)

    tail = (gates, w_out[0].astype(BF16), gn_t, norm_ffn[0][None].astype(F32), w_r, b_r.astype(F32))
    xmid_p, hf_p, route_p = _outproj(x_all, o_mla_p, o_gdn_p, *tail, 0)
    xmid_s, hf_s, route_s = _outproj(x_all, o_mla_s, o_gdn_s, *tail, n_prompt)
    route = jnp.concatenate([route_p[:, :2 * TOP_K], route_s[:, :2 * TOP_K]], axis=0)
    plan = _moe_plan(route[:, :TOP_K].astype(jnp.int32), route[:, TOP_K:], _moe_splits(nt))
    y_p, y_s = _moe(plan, hf_p, hf_s, xmid_p, xmid_s, wg, wu, wd, norm_final[None].astype(F32))

    def with_meta(rows, width):
        meta = jnp.broadcast_to(rows[meta_row:meta_row + N_META][None], (batch, N_META, width))
        return jnp.concatenate([meta, rows[:n_prompt].reshape(batch, seq, width)], axis=1)[None]

    k_rope = krot[:, SM_KR:SM_KR + ROPE_DIM]
    conv_p = jnp.stack([proj[(b + 1) * seq - (CONV_W - 1):(b + 1) * seq, :GDN_QKV] for b in range(batch)])
    conv_s = jnp.concatenate([state_conv[0][:, 1:].astype(F32), proj[n_prompt:nt, None, :GDN_QKV]], axis=1)
    return (y_p.reshape(batch, seq, D_MODEL), y_s.reshape(ns, 1, D_MODEL),
            with_meta(ckv, KV_LORA), with_meta(k_rope, ROPE_DIM),
            ckv[n_prompt:nt].reshape(1, ns, 1, KV_LORA), k_rope[n_prompt:nt].reshape(1, ns, 1, ROPE_DIM),
            conv_p[None], conv_s[None], gdn_p[None], gdn_s[None])
```

```python
import functools

import jax
import jax.numpy as jnp
from jax import lax
from jax.experimental import pallas as pl
from jax.experimental.pallas import tpu as pltpu

F32 = jnp.float32
BF16 = jnp.bfloat16
HIGHEST = lax.Precision.HIGHEST

D_MODEL = 1024
N_META = 16
RMS_EPS = 1e-6
MLA_HEADS = 16
Q_LORA = 384
KV_LORA = 256
NOPE_DIM = 64
ROPE_DIM = 32
V_DIM = 64
ROPE_THETA = 10000.0
MLA_SCALE = (NOPE_DIM + ROPE_DIM) ** -0.5
PAGE_SIZE = 128
GDN_HEADS = 8
GDN_DK = 128
GDN_DV = 128
GDN_KEY = GDN_HEADS * GDN_DK
GDN_QKV = 3 * GDN_KEY
CONV_W = 4
CHUNK = 64
N_GROUPS = 4
EXPERTS_PER_GROUP = 8
N_EXPERTS = 32
TOP_K = 2
D_EXPERT = 256
MOE_BLOCK = 128

_OFF_KV = Q_LORA
_OFF_QKV = _OFF_KV + KV_LORA + ROPE_DIM
_OFF_Z = _OFF_QKV + GDN_QKV
_OFF_B = _OFF_Z + GDN_KEY
_OFF_A = _OFF_B + GDN_HEADS
_OFF_GM = _OFF_A + GDN_HEADS
_OFF_GG = _OFF_GM + D_MODEL
P_QKV = 0
P_KVC = 3072
P_SMALL = 3328
P_QD = 3456
P_F32 = 3840
P_Z = 0
P_GM = 1024
P_GG = 2048
P_BF16 = 3072
P_TOTAL = P_F32 + P_BF16
INPROJ_TN = 768
SM_B = 0
SM_A = 8
SM_KR = 64

LANES = 128
VMEM_LIMIT = 56 * 1024 * 1024
ATTN_TQ = (512, 256, 128)


def _pick(n, candidates):
    for c in candidates:
        if n % c == 0:
            return c
    raise ValueError(f"no tile for {n} in {candidates}")


def _dot(a, b):
    return jnp.dot(a, b, preferred_element_type=F32)


def _dot_nt(a, b):
    return lax.dot_general(a, b, (((1,), (1,)), ((), ())), preferred_element_type=F32)


def _dot_tn(a, b):
    return lax.dot_general(a, b, (((0,), (0,)), ((), ())), preferred_element_type=F32)


def _sigmoid(x):
    return 1.0 / (1.0 + jnp.exp(-x))


def _silu(x):
    return x * _sigmoid(x)


def _softplus(x):
    return jnp.maximum(x, 0.0) + jnp.log1p(jnp.exp(-jnp.abs(x)))


def _rms(x, w):
    return x * lax.rsqrt(jnp.mean(x * x, axis=-1, keepdims=True) + RMS_EPS) * w


def _inproj_kernel(x_ref, nw_ref, w_ref, of_ref, ob_ref, hn_ref):
    j = pl.program_id(1)
    nf = P_F32 // INPROJ_TN

    @pl.when(j == 0)
    def _():
        hn_ref[...] = _rms(x_ref[...], nw_ref[...]).astype(BF16)

    r = _dot(hn_ref[...], w_ref[...])

    @pl.when(j < nf)
    def _():
        of_ref[...] = r

    @pl.when(j >= nf)
    def _():
        ob_ref[...] = r.astype(BF16)


def _inproj(x_all, norm_w, w_packed):
    r = x_all.shape[0]
    tm = _pick(r, (1536, 1280, 768, 640, 512, 256, 128))
    tn = INPROJ_TN
    nf = P_F32 // tn
    return pl.pallas_call(
        _inproj_kernel,
        out_shape=(jax.ShapeDtypeStruct((r, P_F32), F32), jax.ShapeDtypeStruct((r, P_BF16), BF16)),
        grid=(r // tm, P_TOTAL // tn),
        in_specs=[
            pl.BlockSpec((tm, D_MODEL), lambda i, j: (i, 0)),
            pl.BlockSpec((1, D_MODEL), lambda i, j: (0, 0)),
            pl.BlockSpec((D_MODEL, tn), lambda i, j: (0, j)),
        ],
        out_specs=(pl.BlockSpec((tm, tn), lambda i, j: (i, jnp.minimum(j, nf - 1))),
                   pl.BlockSpec((tm, tn), lambda i, j: (i, jnp.maximum(j - nf, 0)))),
        scratch_shapes=[pltpu.VMEM((tm, D_MODEL), BF16)],
        compiler_params=pltpu.CompilerParams(
            dimension_semantics=("parallel", "arbitrary"), vmem_limit_bytes=VMEM_LIMIT),
        name="inproj",
    )(x_all, norm_w, w_packed)


LOG2E = 1.4426950408889634


def _mla_prep_kernel(qd_ref, kvc_ref, sm_ref, c_ref, s_ref, qn_ref, kvn_ref, wqt_ref, wqst_ref, wk_ref, wvt_ref,
                     qt_out, k_out, vt_out, ckv_out, kr_out):
    cos = c_ref[...]
    sin = s_ref[...]
    cos_t, sin_t = cos.T, sin.T
    qn_t = _rms(qd_ref[...], qn_ref[...]).T.astype(BF16)
    qt = _dot(wqt_ref[...], qn_t)
    qst = _dot(wqst_ref[...], qn_t)
    for h in range(MLA_HEADS):
        sl = slice(h * LANES, (h + 1) * LANES)
        qt_out[h, 0] = ((qt[sl] * cos_t + qst[sl] * sin_t) * (MLA_SCALE * LOG2E)).astype(BF16)
    ckv = _rms(kvc_ref[...], kvn_ref[...])
    ckv_out[...] = ckv
    sm = sm_ref[...]
    lane = lax.broadcasted_iota(jnp.int32, sm.shape, 1)
    cos_k = jnp.where((lane >= SM_KR) & (lane < SM_KR + ROPE_DIM), cos, 0.0)
    krot = sm * cos_k + pltpu.roll(sm, LANES - ROPE_DIM, 1) * sin
    kr_out[...] = krot
    kk = _dot(ckv.astype(BF16), wk_ref[...])
    vvt = _dot(wvt_ref[...], ckv.T.astype(BF16))
    for h in range(MLA_HEADS):
        sl = slice(h * LANES, (h + 1) * LANES)
        k_out[h] = (kk[:, sl] + krot).astype(BF16)
        vt_out[h, 0] = vvt[sl].astype(BF16)


def _mla_prep(proj, cos_t, sin_t, q_norm, kv_norm, wqt, wqst, wk, wvt, tm):
    r = proj.shape[0]
    hw = MLA_HEADS * LANES
    full = lambda shape: pl.BlockSpec(shape, lambda i: (0,) * len(shape))
    return pl.pallas_call(
        _mla_prep_kernel,
        out_shape=(
            jax.ShapeDtypeStruct((MLA_HEADS, r // tm, LANES, tm), BF16),
            jax.ShapeDtypeStruct((MLA_HEADS, r, LANES), BF16),
            jax.ShapeDtypeStruct((MLA_HEADS, r // tm, LANES, tm), BF16),
            jax.ShapeDtypeStruct((r, KV_LORA), F32),
            jax.ShapeDtypeStruct((r, LANES), F32),
        ),
        grid=(r // tm,),
        in_specs=[
            pl.BlockSpec((tm, Q_LORA), lambda i: (i, P_QD // Q_LORA)),
            pl.BlockSpec((tm, KV_LORA), lambda i: (i, P_KVC // KV_LORA)),
            pl.BlockSpec((tm, LANES), lambda i: (i, P_SMALL // LANES)),
            pl.BlockSpec((tm, LANES), lambda i: (i, 0)),
            pl.BlockSpec((tm, LANES), lambda i: (i, 0)),
            full((1, Q_LORA)), full((1, KV_LORA)),
            full((hw, Q_LORA)), full((hw, Q_LORA)), full((KV_LORA, hw)), full((hw, KV_LORA)),
        ],
        out_specs=(pl.BlockSpec((MLA_HEADS, 1, LANES, tm), lambda i: (0, i, 0, 0)),
                   pl.BlockSpec((MLA_HEADS, tm, LANES), lambda i: (0, i, 0)),
                   pl.BlockSpec((MLA_HEADS, 1, LANES, tm), lambda i: (0, i, 0, 0)),
                   pl.BlockSpec((tm, KV_LORA), lambda i: (i, 0)),
                   pl.BlockSpec((tm, LANES), lambda i: (i, 0))),
        compiler_params=pltpu.CompilerParams(dimension_semantics=("parallel",), vmem_limit_bytes=VMEM_LIMIT),
        name="mla_prep",
    )(proj, proj, proj, cos_t, sin_t, q_norm, kv_norm, wqt, wqst, wk, wvt)


def _attn_prompt_kernel(qt_ref, k_ref, vt_ref, km_ref, vmt_ref, o_ref, *, tq, meta_lane, nq, tiles_per_step):
    i = pl.program_id(2)
    tiles = (i,) if tiles_per_step == 1 else (i, nq - 1 - i)
    streams = [(t, h) for t in tiles for h in (0, 1)]
    half = tq // 2
    qt = [qt_ref[h, t] for t, h in streams]

    def update(qts, k_blk, vt_blk, carry, mask):
        ms, ls, accs = carry
        n = range(len(qts))
        s = [_dot(k_blk[x], qts[x]) for x in n]
        if mask is not None:
            s = [jnp.where(mask, v, -1e30) for v in s]
        m_new = [jnp.maximum(ms[x], jnp.max(s[x], axis=0, keepdims=True)) for x in n]
        a = [jnp.exp2(ms[x] - m_new[x]) for x in n]
        p = [jnp.exp2(s[x] - m_new[x]) for x in n]
        l_new = [a[x] * ls[x] + jnp.sum(p[x], axis=0, keepdims=True) for x in n]
        acc_new = [a[x] * accs[x] + _dot(vt_blk[x], p[x].astype(BF16)) for x in n]
        return m_new, l_new, acc_new

    s0 = [_dot(km_ref[h], q) for (_, h), q in zip(streams, qt)]
    m = [jnp.max(v, axis=0, keepdims=True) for v in s0]
    p0 = [jnp.exp2(v - mx) for v, mx in zip(s0, m)]
    l = [jnp.sum(v, axis=0, keepdims=True) for v in p0]
    acc = [_dot(vmt_ref[h, 0][:, meta_lane:meta_lane + N_META], v.astype(BF16)) for (_, h), v in zip(streams, p0)]

    for ti, t in enumerate(tiles):
        own = slice(2 * ti, 2 * ti + 2)

        def body(j, carry, own=own):
            rows = pl.ds(pl.multiple_of(j * tq, tq), tq)
            return update(qt[own], [k_ref[h, rows, :] for h in (0, 1)], [vt_ref[h, j] for h in (0, 1)], carry, None)

        m[own], l[own], acc[own] = lax.fori_loop(0, t, body, (m[own], l[own], acc[own]))

    key = lax.broadcasted_iota(jnp.int32, (half, tq), 0)
    qry = lax.broadcasted_iota(jnp.int32, (half, tq), 1)
    off = [pl.multiple_of(t * tq, tq) for t, _ in streams]
    vt_d = [vt_ref[h, t] for t, h in streams]
    m, l, acc = update(qt, [k_ref[h, pl.ds(o, half), :] for (_, h), o in zip(streams, off)],
                       [v[:, :half] for v in vt_d], (m, l, acc), key <= qry)
    late = lambda xs: [v[:, half:] for v in xs]
    mb, lb, accb = update(late(qt), [k_ref[h, pl.ds(pl.multiple_of(o + half, half), half), :]
                                     for (_, h), o in zip(streams, off)],
                          late(vt_d), (late(m), late(l), late(acc)), (key <= qry)[:, :half])
    ot = [jnp.concatenate([acc[x][:, :half] / l[x][:, :half], accb[x] / lb[x]], axis=1) for x in range(len(streams))]
    for ti, t in enumerate(tiles):
        o_ref[pl.ds(pl.multiple_of(t * tq, tq), tq), :] = (ot[2 * ti] + ot[2 * ti + 1]).T.astype(o_ref.dtype)


def _attn_prompt(qt, k, vt, batch, seq, meta_row, tq):
    nq = seq // tq
    tps = 2 if nq % 2 == 0 else 1
    kern = functools.partial(_attn_prompt_kernel, tq=tq, meta_lane=meta_row % tq, nq=nq, tiles_per_step=tps)
    tiled = lambda: pl.BlockSpec((2, nq, LANES, tq), lambda b, p, i: (p, b, 0, 0))
    return pl.pallas_call(
        kern,
        out_shape=jax.ShapeDtypeStruct((batch * seq, D_MODEL), BF16),
        grid=(batch, MLA_HEADS // 2, nq // tps),
        in_specs=[
            tiled(),
            pl.BlockSpec((2, seq, LANES), lambda b, p, i: (p, b, 0)),
            tiled(),
            pl.BlockSpec((2, N_META, LANES), lambda b, p, i: (p, meta_row // N_META, 0)),
            pl.BlockSpec((2, 1, LANES, tq), lambda b, p, i: (p, meta_row // tq, 0, 0)),
        ],
        out_specs=pl.BlockSpec((seq, LANES), lambda b, p, i: (b, p)),
        compiler_params=pltpu.CompilerParams(
            dimension_semantics=("parallel", "parallel", "arbitrary"), vmem_limit_bytes=VMEM_LIMIT),
        name="attn_prompt",
    )(qt, k, vt, k, vt)


def _gate_lanes(sm, alog_ref, dtb_ref):
    g = -jnp.exp(alog_ref[...]) * _softplus(sm + dtb_ref[...])
    beta = _sigmoid(sm)
    return g, beta


def _qkv_heads(xc):
    xf = _silu(xc)
    qs, ks, vs = [], [], []
    for h in range(GDN_HEADS):
        q = xf[:, h * GDN_DK:(h + 1) * GDN_DK]
        k = xf[:, GDN_KEY + h * GDN_DK:GDN_KEY + (h + 1) * GDN_DK]
        qs.append(q * lax.rsqrt(jnp.sum(q * q, axis=-1, keepdims=True) + RMS_EPS) * (GDN_DK ** -0.5))
        ks.append(k * lax.rsqrt(jnp.sum(k * k, axis=-1, keepdims=True) + RMS_EPS))
        vs.append(xf[:, 2 * GDN_KEY + h * GDN_DV:2 * GDN_KEY + (h + 1) * GDN_DV])
    return qs, ks, vs


def _split_bf16(x):
    hi = x.astype(BF16)
    return hi, (x - hi.astype(F32)).astype(BF16)


def _dot_split(a, b):
    return _dot(a[0], b[0]) + (_dot(a[0], b[1]) + _dot(a[1], b[0]))


def _unit_lower_inverses(mats, c):
    row = lax.broadcasted_iota(jnp.int32, (c, c), 0)
    col = lax.broadcasted_iota(jnp.int32, (c, c), 1)
    eye = jnp.where(row == col, 1.0, 0.0)
    ps = [-a for a in mats]
    ts = [eye + p for p in ps]
    span = 2
    while span < c:
        psp = [_split_bf16(p) for p in ps]
        ps = [_dot_split(p, p) for p in psp]
        psp = [_split_bf16(p) for p in ps]
        ts = [t + _dot_split(p, _split_bf16(t)) for p, t in zip(psp, ts)]
        span *= 2
    return ts


def _gdn_chunk_terms(xs, sm, conv_ref, alog_ref, dtb_ref, c):
    heads = range(GDN_HEADS)
    xc = xs[0] * conv_ref[0:1, :]
    for j in range(1, CONV_W):
        xc = xc + xs[j] * conv_ref[j:j + 1, :]
    qs, ks, vs = _qkv_heads(xc)
    g, beta = _gate_lanes(sm, alog_ref, dtb_ref)
    row = lax.broadcasted_iota(jnp.int32, (c, c), 0)
    col = lax.broadcasted_iota(jnp.int32, (c, c), 1)
    causal = col <= row
    strict = col < row
    gcum = jnp.dot(jnp.where(causal, 1.0, 0.0), g, precision=HIGHEST, preferred_element_type=F32)
    gcum_t = lax.dot_general(g, jnp.where(col >= row, 1.0, 0.0), (((0,), (0,)), ((), ())),
                             precision=HIGHEST, preferred_element_type=F32)
    gc = [gcum[:, SM_A + h:SM_A + h + 1] for h in heads]
    gr = [gcum_t[SM_A + h:SM_A + h + 1, :] for h in heads]
    bc = [beta[:, SM_B + h:SM_B + h + 1] for h in heads]
    decay = [jnp.where(causal, jnp.exp(jnp.where(causal, gc[h] - gr[h], 0.0)), 0.0) for h in heads]
    kb = [ks[h] * bc[h] for h in heads]
    kbf = [ks[h].astype(BF16) for h in heads]
    a = [jnp.where(strict, _dot_nt(kb[h].astype(BF16), kbf[h]) * decay[h], 0.0) for h in heads]
    t = _unit_lower_inverses(a, c)
    eg = [jnp.exp(gc[h]) for h in heads]
    sol = [_dot(t[h].astype(BF16), jnp.concatenate([vs[h] * bc[h], kb[h] * eg[h]], axis=1).astype(BF16))
           for h in heads]
    u = [sol[h][:, :GDN_DV] for h in heads]
    w = [sol[h][:, GDN_DV:] for h in heads]
    attn = [jnp.where(causal, _dot_nt(qs[h].astype(BF16), kbf[h]) * decay[h], 0.0) for h in heads]
    qg = [qs[h] * eg[h] for h in heads]
    g_last = gcum[c - 1:c, :]
    kd = [ks[h] * jnp.exp(g_last[:, SM_A + h:SM_A + h + 1] - gc[h]) for h in heads]
    return u, w, qg, kd, attn, jnp.exp(g_last)


def _conv_shifts(hist, x):
    xe = jnp.concatenate([hist, x], axis=0)
    return [pltpu.roll(xe, d, 0)[8:] for d in range(CONV_W - 1, 0, -1)] + [x]


def _gdn_meta_kernel(x_ref, sm_ref, conv_ref, alog_ref, dtb_ref, s_out):
    xs = _conv_shifts(jnp.zeros((8, GDN_QKV), F32), x_ref[...])
    u, _, _, kd, _, _ = _gdn_chunk_terms(xs, sm_ref[...], conv_ref, alog_ref, dtb_ref, N_META)
    for h in range(GDN_HEADS):
        s_out[h] = _dot_tn(kd[h].astype(BF16), u[h].astype(BF16))


def _gdn_meta(proj, conv_w, alog_v, dtb_v, meta_row):
    full = lambda shape: pl.BlockSpec(shape, lambda i: (0,) * len(shape))
    return pl.pallas_call(
        _gdn_meta_kernel,
        out_shape=jax.ShapeDtypeStruct((GDN_HEADS, GDN_DK, GDN_DV), F32),
        grid=(1,),
        in_specs=[
            pl.BlockSpec((N_META, GDN_QKV), lambda i: (meta_row // N_META, 0)),
            pl.BlockSpec((N_META, LANES), lambda i: (meta_row // N_META, P_SMALL // LANES)),
            full((CONV_W, GDN_QKV)), full((1, LANES)), full((1, LANES)),
        ],
        out_specs=full((GDN_HEADS, GDN_DK, GDN_DV)),
        compiler_params=pltpu.CompilerParams(vmem_limit_bytes=VMEM_LIMIT),
        name="gdn_meta",
    )(proj, proj, conv_w, alog_v, dtb_v)


def _gdn_terms_kernel(x_ref, hist_ref, sm_ref, conv_ref, alog_ref, dtb_ref,
                      u_out, w_out, qg_out, kd_out, attn_out, dec_out, *, cps):
    c = CHUNK
    hist = hist_ref[...]
    for cc in range(cps):
        rows = slice(cc * c, (cc + 1) * c)
        x = x_ref[rows, :]
        u, w, qg, kd, attn, dec = _gdn_chunk_terms(_conv_shifts(hist, x), sm_ref[rows, :], conv_ref, alog_ref,
                                                   dtb_ref, c)
        for h in range(GDN_HEADS):
            sl = slice(h * GDN_DV, (h + 1) * GDN_DV)
            u_out[rows, sl] = u[h]
            w_out[rows, sl] = w[h].astype(BF16)
            qg_out[rows, sl] = qg[h].astype(BF16)
            kd_out[rows, sl] = kd[h].astype(BF16)
            attn_out[h, rows, :] = attn[h].astype(BF16)
        dec_out[cc] = dec
        hist = x[c - 8:]


def _gdn_terms(proj, conv_w, alog_v, dtb_v, batch, seq, meta_row):
    nc = seq // CHUNK
    n = batch * seq
    cps = _pick(nc, (4, 2, 1))
    step = cps * CHUNK
    spb = nc // cps
    full = lambda shape: pl.BlockSpec(shape, lambda i: (0,) * len(shape))
    rows = lambda: pl.BlockSpec((step, D_MODEL), lambda i: (i, 0))

    def hist_index(i):
        return (jnp.where(i % spb == 0, (meta_row + N_META) // 8, i * (step // 8)) - 1, 0)

    return pl.pallas_call(
        functools.partial(_gdn_terms_kernel, cps=cps),
        out_shape=(jax.ShapeDtypeStruct((n, D_MODEL), F32),
                   jax.ShapeDtypeStruct((n, D_MODEL), BF16),
                   jax.ShapeDtypeStruct((n, D_MODEL), BF16),
                   jax.ShapeDtypeStruct((n, D_MODEL), BF16),
                   jax.ShapeDtypeStruct((GDN_HEADS, n, CHUNK), BF16),
                   jax.ShapeDtypeStruct((batch * nc, 1, LANES), F32)),
        grid=(batch * spb,),
        in_specs=[
            pl.BlockSpec((step, GDN_QKV), lambda i: (i, 0)),
            pl.BlockSpec((8, GDN_QKV), hist_index),
            pl.BlockSpec((step, LANES), lambda i: (i, P_SMALL // LANES)),
            full((CONV_W, GDN_QKV)), full((1, LANES)), full((1, LANES)),
        ],
        out_specs=(rows(), rows(), rows(), rows(),
                   pl.BlockSpec((GDN_HEADS, step, CHUNK), lambda i: (0, i, 0)),
                   pl.BlockSpec((cps, 1, LANES), lambda i: (i, 0, 0))),
        compiler_params=pltpu.CompilerParams(dimension_semantics=("parallel",), vmem_limit_bytes=VMEM_LIMIT),
        name="gdn_terms",
    )(proj, proj, proj, conv_w, alog_v, dtb_v)


def _gdn_scan_kernel(u_ref, w_ref, qg_ref, kd_ref, attn_ref, dec_ref, s0_ref, o_ref, s_out, st_ref, *, cpg, nseq):
    c = CHUNK
    streams = [(q, h) for q in range(nseq) for h in range(GDN_HEADS)]
    sl = [slice(h * GDN_DV, (h + 1) * GDN_DV) for h in range(GDN_HEADS)]

    @pl.when(pl.program_id(1) == 0)
    def _():
        for q in range(nseq):
            st_ref[q] = s0_ref[...]

    def chunk(ci, carry):
        rows = pl.ds(pl.multiple_of(ci * c, c), c)
        dec = [dec_ref[q, ci] for q in range(nseq)]
        s_old = [st_ref[q, h] for q, h in streams]
        sb = [s.astype(BF16) for s in s_old]
        lhs = [jnp.concatenate([w_ref[q, rows, sl[h]], qg_ref[q, rows, sl[h]]], axis=0) for q, h in streams]
        r = [_dot(a, b) for a, b in zip(lhs, sb)]
        vnb = [(u_ref[q, rows, sl[h]] - rr[:c]).astype(BF16) for (q, h), rr in zip(streams, r)]
        out = [rr[c:] + _dot(attn_ref[h, q, rows, :], v) for (q, h), rr, v in zip(streams, r, vnb)]
        upd = [_dot_tn(kd_ref[q, rows, sl[h]], v) for (q, h), v in zip(streams, vnb)]
        for x, (q, h) in enumerate(streams):
            o_ref[q, rows, sl[h]] = out[x].astype(o_ref.dtype)
            st_ref[q, h] = s_old[x] * dec[q][:, SM_A + h:SM_A + h + 1] + upd[x]
        return carry

    lax.fori_loop(0, cpg, chunk, 0)

    @pl.when(pl.program_id(1) == pl.num_programs(1) - 1)
    def _():
        s_out[...] = st_ref[...]


def _gdn_scan(u, w, qg, kd, attn, dec, s_meta, batch, seq):
    nc = seq // CHUNK
    nseq = 2 if batch % 2 == 0 else 1
    cpg = _pick(nc, (8, 4, 2, 1))
    ng = nc // cpg
    by_seq = lambda a: a.reshape(batch, seq, D_MODEL)
    rows = lambda: pl.BlockSpec((nseq, cpg * CHUNK, D_MODEL), lambda b, g: (b, g, 0))
    kern = functools.partial(_gdn_scan_kernel, cpg=cpg, nseq=nseq)
    o, s_fin = pl.pallas_call(
        kern,
        out_shape=(jax.ShapeDtypeStruct((batch, seq, D_MODEL), BF16),
                   jax.ShapeDtypeStruct((batch, GDN_HEADS, GDN_DK, GDN_DV), F32)),
        grid=(batch // nseq, ng),
        in_specs=[rows(), rows(), rows(), rows(),
                  pl.BlockSpec((GDN_HEADS, nseq, cpg * CHUNK, CHUNK), lambda b, g: (0, b, g, 0)),
                  pl.BlockSpec((nseq, cpg, 1, LANES), lambda b, g: (b, g, 0, 0)),
                  pl.BlockSpec((GDN_HEADS, GDN_DK, GDN_DV), lambda b, g: (0, 0, 0))],
        out_specs=(rows(),
                   pl.BlockSpec((nseq, GDN_HEADS, GDN_DK, GDN_DV), lambda b, g: (b, 0, 0, 0))),
        scratch_shapes=[pltpu.VMEM((nseq, GDN_HEADS, GDN_DK, GDN_DV), F32)],
        compiler_params=pltpu.CompilerParams(
            dimension_semantics=("parallel", "arbitrary"), vmem_limit_bytes=VMEM_LIMIT),
        name="gdn_scan",
    )(by_seq(u), by_seq(w), by_seq(qg), by_seq(kd), attn.reshape(GDN_HEADS, batch, seq, CHUNK),
      dec.reshape(batch, nc, 1, LANES), s_meta)
    return o.reshape(batch * seq, D_MODEL), s_fin


def _gdn_sample_kernel(x_ref, sm_ref, cs_ref, st_ref, conv_ref, alog_ref, dtb_ref, o_ref, s_out, *, nb):
    xc = x_ref[...] * conv_ref[CONV_W - 1:CONV_W, :]
    for j in range(CONV_W - 1):
        xc = xc + cs_ref[:, j, :] * conv_ref[j:j + 1, :]
    qs, ks, vs = _qkv_heads(xc)
    g, beta = _gate_lanes(sm_ref[...], alog_ref, dtb_ref)
    eg = jnp.exp(g)
    for h in range(GDN_HEADS):
        q_t = qs[h].T
        k_t = ks[h].T
        for b in range(nb):
            kcol = k_t[:, b:b + 1]
            s1 = st_ref[b, h] * eg[b:b + 1, SM_A + h:SM_A + h + 1]
            r = jnp.sum(s1 * kcol, axis=0, keepdims=True)
            delta = (vs[h][b:b + 1, :] - r) * beta[b:b + 1, SM_B + h:SM_B + h + 1]
            s2 = s1 + kcol * delta
            s_out[b, h] = s2
            o_ref[b:b + 1, h * GDN_DV:(h + 1) * GDN_DV] = jnp.sum(s2 * q_t[:, b:b + 1], axis=0, keepdims=True)


def _gdn_sample(proj, state_conv, state_gdn, conv_w, alog_v, dtb_v, row0):
    ns = state_gdn.shape[0]
    nb = 8
    full = lambda shape: pl.BlockSpec(shape, lambda i: (0,) * len(shape))
    kern = functools.partial(_gdn_sample_kernel, nb=nb)
    return pl.pallas_call(
        kern,
        out_shape=(jax.ShapeDtypeStruct((ns, D_MODEL), F32),
                   jax.ShapeDtypeStruct(state_gdn.shape, F32)),
        grid=(ns // nb,),
        in_specs=[
            pl.BlockSpec((nb, GDN_QKV), lambda i: (row0 // nb + i, 0)),
            pl.BlockSpec((nb, LANES), lambda i: (row0 // nb + i, P_SMALL // LANES)),
            pl.BlockSpec((nb, CONV_W - 1, GDN_QKV), lambda i: (i, 0, 0)),
            pl.BlockSpec((nb, GDN_HEADS, GDN_DK, GDN_DV), lambda i: (i, 0, 0, 0)),
            full((CONV_W, GDN_QKV)), full((1, LANES)), full((1, LANES)),
        ],
        out_specs=(pl.BlockSpec((nb, D_MODEL), lambda i: (i, 0)),
                   pl.BlockSpec((nb, GDN_HEADS, GDN_DK, GDN_DV), lambda i: (i, 0, 0, 0))),
        compiler_params=pltpu.CompilerParams(dimension_semantics=("parallel",), vmem_limit_bytes=VMEM_LIMIT),
        name="gdn_sample",
    )(proj, proj, state_conv, state_gdn, conv_w, alog_v, dtb_v)


def _q_absorb_kernel(qt_ref, wukt_ref, ql_out, qrt_out, *, lane0, ns):
    for h in range(MLA_HEADS):
        qt = qt_ref[h, 0][:, lane0:lane0 + ns]
        ql_out[h] = _dot_tn(qt, wukt_ref[h]).astype(BF16)
        qrt_out[h] = qt[NOPE_DIM:NOPE_DIM + ROPE_DIM]


def _q_absorb(qt, wukt, ns, row0):
    tm = qt.shape[-1]
    assert row0 % tm + ns <= tm and (row0 % tm) % LANES == 0
    return pl.pallas_call(
        functools.partial(_q_absorb_kernel, lane0=row0 % tm, ns=ns),
        out_shape=(jax.ShapeDtypeStruct((MLA_HEADS, ns, KV_LORA), BF16),
                   jax.ShapeDtypeStruct((MLA_HEADS, ROPE_DIM, ns), BF16)),
        grid=(1,),
        in_specs=[pl.BlockSpec((MLA_HEADS, 1, LANES, tm), lambda i: (0, row0 // tm, 0, 0)),
                  pl.BlockSpec((MLA_HEADS, LANES, KV_LORA), lambda i: (0, 0, 0))],
        out_specs=(pl.BlockSpec((MLA_HEADS, ns, KV_LORA), lambda i: (0, 0, 0)),
                   pl.BlockSpec((MLA_HEADS, ROPE_DIM, ns), lambda i: (0, 0, 0))),
        compiler_params=pltpu.CompilerParams(vmem_limit_bytes=VMEM_LIMIT),
        name="q_absorb",
    )(qt, wukt)


def _mla_sample_kernel(pt_ref, ql_ref, qr_ref, cn_ref, krn_ref, cc_hbm, cr_hbm, o_ref, cbuf, rbuf, sem,
                       *, n_pages, nsub):
    g = pl.program_id(0)

    def page_copies(seq, slot, i):
        page = pt_ref[seq * n_pages + i]
        return (pltpu.make_async_copy(cc_hbm.at[page], cbuf.at[slot, i], sem.at[slot, 0]),
                pltpu.make_async_copy(cr_hbm.at[page], rbuf.at[slot, :, pl.ds(i * PAGE_SIZE, PAGE_SIZE)],
                                      sem.at[slot, 1]))

    def start_pages(seq, slot):
        for i in range(n_pages):
            cc, cr = page_copies(seq, slot, i)
            cc.start()
            cr.start()

    def wait_pages(seq, slot):
        for i in range(n_pages):
            cc, cr = page_copies(seq, slot, i)
            cc.wait()
            cr.wait()

    def attend(slot):
        ql = ql_ref[slot]
        qr = qr_ref[slot]
        pps = n_pages // nsub
        subs = range(nsub)
        c = [cbuf[slot, i * pps:(i + 1) * pps].reshape(pps * PAGE_SIZE, KV_LORA).astype(BF16) for i in subs]
        s = [_dot_nt(ql, c[i]) + _dot(qr, rbuf[slot, :, i * pps * PAGE_SIZE:(i + 1) * pps * PAGE_SIZE].astype(BF16))
             for i in subs]
        ms = [jnp.max(s[i], axis=1, keepdims=True) for i in subs]
        p = [jnp.exp2(s[i] - ms[i]) for i in subs]
        ls = [jnp.sum(p[i], axis=1, keepdims=True) for i in subs]
        accs = [_dot(p[i].astype(BF16), c[i]) for i in subs]
        cn = cn_ref[slot]
        krn = krn_ref[slot][:, SM_KR:SM_KR + ROPE_DIM]
        ms.append(jnp.sum(ql.astype(F32) * cn, axis=1, keepdims=True)
                  + jnp.sum(qr.astype(F32) * krn, axis=1, keepdims=True))
        ls.append(jnp.ones_like(ms[-1]))
        accs.append(jnp.broadcast_to(cn, (MLA_HEADS, KV_LORA)))
        m = functools.reduce(jnp.maximum, ms)
        scale = [jnp.exp2(m_i - m) for m_i in ms]
        l = sum(a * l_i for a, l_i in zip(scale, ls))
        acc = sum(a * acc_i for a, acc_i in zip(scale, accs))
        o_ref[slot] = acc / l

    @pl.when(g == 0)
    def _():
        start_pages(0, 0)

    start_pages(2 * g + 1, 1)
    wait_pages(2 * g, 0)
    attend(0)

    @pl.when(g + 1 < pl.num_programs(0))
    def _():
        start_pages(2 * g + 2, 0)

    wait_pages(2 * g + 1, 1)
    attend(1)


def _mla_sample(page_table, ql, qr, c_new, kr_new, cache_c, cache_r):
    ns, n_pages = page_table.shape
    assert ns % 2 == 0
    kern = functools.partial(_mla_sample_kernel, n_pages=n_pages, nsub=_pick(n_pages, (8, 4, 2, 1)))
    grid_spec = pltpu.PrefetchScalarGridSpec(
        num_scalar_prefetch=1,
        grid=(ns // 2,),
        in_specs=[
            pl.BlockSpec((2, MLA_HEADS, KV_LORA), lambda b, pt: (b, 0, 0)),
            pl.BlockSpec((2, MLA_HEADS, ROPE_DIM), lambda b, pt: (b, 0, 0)),
            pl.BlockSpec((2, 1, KV_LORA), lambda b, pt: (b, 0, 0)),
            pl.BlockSpec((2, 1, LANES), lambda b, pt: (b, 0, 0)),
            pl.BlockSpec(memory_space=pl.ANY),
            pl.BlockSpec(memory_space=pl.ANY),
        ],
        out_specs=pl.BlockSpec((2, MLA_HEADS, KV_LORA), lambda b, pt: (b, 0, 0)),
        scratch_shapes=[pltpu.VMEM((2, n_pages, PAGE_SIZE, KV_LORA), F32),
                        pltpu.VMEM((2, ROPE_DIM, n_pages * PAGE_SIZE), F32),
                        pltpu.SemaphoreType.DMA((2, 2))],
    )
    return pl.pallas_call(
        kern,
        out_shape=jax.ShapeDtypeStruct((ns, MLA_HEADS, KV_LORA), F32),
        grid_spec=grid_spec,
        compiler_params=pltpu.CompilerParams(dimension_semantics=("arbitrary",), vmem_limit_bytes=VMEM_LIMIT),
        name="mla_sample",
    )(page_table.reshape(-1), ql, qr, c_new, kr_new, cache_c, cache_r)


def _o_proj_sample_kernel(ol_ref, wv_ref, o_ref):
    for p in range(MLA_HEADS // 2):
        acc = None
        for h in (2 * p, 2 * p + 1):
            part = _dot(ol_ref[h].astype(BF16), wv_ref[:, h * LANES:(h + 1) * LANES])
            acc = part if acc is None else acc + part
        o_ref[:, p * LANES:(p + 1) * LANES] = acc.astype(o_ref.dtype)


def _o_proj_sample(o_lat_t, wv):
    ns = o_lat_t.shape[1]
    return pl.pallas_call(
        _o_proj_sample_kernel,
        out_shape=jax.ShapeDtypeStruct((ns, D_MODEL), BF16),
        grid=(1,),
        in_specs=[pl.BlockSpec((MLA_HEADS, ns, KV_LORA), lambda i: (0, 0, 0)),
                  pl.BlockSpec((KV_LORA, MLA_HEADS * LANES), lambda i: (0, 0))],
        out_specs=pl.BlockSpec((ns, D_MODEL), lambda i: (0, 0)),
        compiler_params=pltpu.CompilerParams(vmem_limit_bytes=VMEM_LIMIT),
        name="o_proj_sample",
    )(o_lat_t, wv)


def _outproj_kernel(x_ref, om_ref, og_ref, z_ref, gm_ref, gg_ref, wo_ref, gn_ref, nf_ref, wr_ref, br_ref,
                    xmid_out, hf_out, route_out):
    og = og_ref[...].astype(F32)
    parts = []
    for h in range(GDN_HEADS):
        oh = og[:, h * GDN_DV:(h + 1) * GDN_DV]
        parts.append(oh * lax.rsqrt(jnp.mean(oh * oh, axis=-1, keepdims=True) + RMS_EPS))
    o_gdn = jnp.concatenate(parts, axis=1) * gn_ref[...] * _silu(z_ref[...].astype(F32))
    merged = (_sigmoid(gm_ref[...].astype(F32)) * om_ref[...].astype(F32)
              + _sigmoid(gg_ref[...].astype(F32)) * o_gdn)
    x_mid = x_ref[...] + _dot(merged.astype(BF16), wo_ref[...])
    xmid_out[...] = x_mid
    hf = _rms(x_mid, nf_ref[...]).astype(BF16)
    bits = lax.bitcast_convert_type(hf.astype(F32), jnp.uint32)
    half = D_MODEL // 2
    hf_out[...] = bits[:, half:] | (bits[:, :half] >> 16)

    logits = _dot(hf, wr_ref[...]) + br_ref[...]
    lane = lax.broadcasted_iota(jnp.int32, logits.shape, 1)
    neg = -jnp.inf
    big = 4 * LANES
    is_g = lane < N_GROUPS
    lg = jnp.where(is_g, logits, neg)
    mg = jnp.max(lg, axis=1, keepdims=True)
    grp = jnp.min(jnp.where(lg == mg, lane, big), axis=1, keepdims=True)
    gate_g = 1.0 / jnp.sum(jnp.where(is_g, jnp.exp(logits - mg), 0.0), axis=1, keepdims=True)
    e_lane = lane - N_GROUPS
    in_grp = (e_lane >= 0) & (e_lane < N_EXPERTS) & ((e_lane >> 3) == grp)
    le = jnp.where(in_grp, logits, neg)
    v1 = jnp.max(le, axis=1, keepdims=True)
    i1 = jnp.min(jnp.where(le == v1, lane, big), axis=1, keepdims=True)
    le2 = jnp.where(lane == i1, neg, le)
    v2 = jnp.max(le2, axis=1, keepdims=True)
    i2 = jnp.min(jnp.where(le2 == v2, lane, big), axis=1, keepdims=True)
    e = jnp.exp(v2 - v1)
    w1 = gate_g / (1.0 + e)
    w2 = gate_g * e / (1.0 + e)
    route = jnp.where(lane == 0, (i1 - N_GROUPS).astype(F32),
                      jnp.where(lane == 1, (i2 - N_GROUPS).astype(F32),
                                jnp.where(lane == 2, w1, jnp.where(lane == 3, w2, 0.0))))
    route_out[...] = route


def _row_tile(n, limit):
    t = limit - limit % 16
    while t >= 16:
        if n % t == 0:
            return t
        t -= 16
    raise ValueError(f"no row tile for {n}")


def _outproj(x_all, o_mla, o_gdn, gates, w_out, gn_t, norm_ffn, w_r, b_r, row0):
    n = o_mla.shape[0]
    tm = _pick(n, (512, 256, 128, 64, 32, 16))
    assert row0 % tm == 0
    r0 = row0 // tm
    full = lambda shape: pl.BlockSpec(shape, lambda i: (0,) * len(shape))
    row = lambda w: pl.BlockSpec((tm, w), lambda i: (i, 0))
    shared = lambda w, j=0: pl.BlockSpec((tm, w), lambda i, j=j: (r0 + i, j))
    return pl.pallas_call(
        _outproj_kernel,
        out_shape=(jax.ShapeDtypeStruct((n, D_MODEL), F32),
                   jax.ShapeDtypeStruct((n, D_MODEL // 2), jnp.uint32),
                   jax.ShapeDtypeStruct((n, LANES), F32)),
        grid=(n // tm,),
        in_specs=[shared(D_MODEL), row(D_MODEL), row(D_MODEL),
                  shared(D_MODEL, P_Z // D_MODEL), shared(D_MODEL, P_GM // D_MODEL), shared(D_MODEL, P_GG // D_MODEL),
                  full((D_MODEL, D_MODEL)), full((1, D_MODEL)), full((1, D_MODEL)),
                  full((D_MODEL, LANES)), full((1, LANES))],
        out_specs=(row(D_MODEL), row(D_MODEL // 2), row(LANES)),
        compiler_params=pltpu.CompilerParams(dimension_semantics=("parallel",), vmem_limit_bytes=VMEM_LIMIT),
        name="outproj_route",
    )(x_all, o_mla, o_gdn, gates, gates, gates, w_out, gn_t, norm_ffn, w_r, b_r)


def _moe_plan(eid, wgt, nsp):
    nt = eid.shape[0]
    ts = nt // nsp
    n_asg = ts * TOP_K
    e = eid.reshape(nsp, n_asg)
    ids = jnp.broadcast_to(jnp.arange(n_asg, dtype=jnp.int32), e.shape)
    _, ids_s, w_s = lax.sort((e, ids, wgt.reshape(nsp, n_asg)), dimension=1, num_keys=1, is_stable=True)
    rows_s = (ids_s % TOP_K) * (ts + 8) + ids_s // TOP_K
    counts = jnp.sum((e[..., None] == jnp.arange(N_EXPERTS, dtype=jnp.int32)).astype(jnp.int32), axis=1)
    run_start = jnp.cumsum(counts, axis=1) - counts
    tail = lambda v, dt: jnp.full((nsp, MOE_BLOCK), v, dt)
    tok_t = jnp.concatenate([ids_s // TOP_K, tail(ts, jnp.int32)], axis=1)
    rows_t = jnp.concatenate([rows_s, tail(ts, jnp.int32)], axis=1)
    w_t = jnp.concatenate([w_s, tail(0.0, F32)], axis=1)
    as_i32 = lambda a: a.astype(jnp.int32).reshape(-1)
    scalars = (as_i32(run_start), as_i32(counts), as_i32(tok_t), as_i32(rows_t), w_t.reshape(-1))
    return scalars, ts


def _split_pieces(k, ts, n_prompt, n_sample):
    lo, hi = k * ts, (k + 1) * ts
    pieces = []
    if lo < n_prompt:
        pieces.append((0, lo, 0, min(hi, n_prompt) - lo))
    if hi > n_prompt:
        start = max(lo, n_prompt)
        pieces.append((1, start - n_prompt, start - lo, hi - start))
    assert hi <= n_prompt + n_sample
    return pieces


def _moe_kernel(start_ref, cnt_ref, tok_ref, rows_ref, w_ref, hf_p, hf_q, xmid_p, xmid_q,
                wg_ref, wu_ref, wd_ref, nfin_ref, y_p, y_q, hf_s, comb, acc, xb, yb, sem,
                *, ts, rc, nsp, n_prompt, n_sample):
    s = pl.program_id(0)
    j = pl.program_id(1)
    half = D_MODEL // 2
    stride = ts + 8

    def zero_spare_rows():
        hf_s[ts:ts + 8, :] = jnp.zeros((8, half), jnp.uint32)

    def split_copies(k, what):
        cps = []
        for src, r0, l0, n in _split_pieces(k, ts, n_prompt, n_sample):
            hbm_rows, vmem_rows = pl.ds(r0, n), pl.ds(l0, n)
            if what == 0:
                cps.append(pltpu.make_async_copy((hf_p, hf_q)[src].at[hbm_rows], hf_s.at[vmem_rows], sem.at[0, src]))
            elif what == 1:
                cps.append(pltpu.make_async_copy((xmid_p, xmid_q)[src].at[hbm_rows], acc.at[vmem_rows],
                                                 sem.at[1, src]))
            else:
                cps.append(pltpu.make_async_copy(acc.at[vmem_rows], (y_p, y_q)[src].at[hbm_rows], sem.at[2, src]))
        return cps

    def for_split(fn):
        for k in range(nsp):
            @pl.when(s == k)
            def _(k=k):
                fn(k)

    def begin_split(k):
        loads = split_copies(k, 0) + split_copies(k, 1)
        for cp in loads:
            cp.start()
        zero_spare_rows()
        for cp in split_copies(k, 0):
            cp.wait()

    def finish_split(k):
        for cp in split_copies(k, 2):
            cp.start()
        for cp in split_copies(k, 2):
            cp.wait()

    @pl.when(j == 0)
    def _():
        for_split(begin_split)

    cnt = cnt_ref[s * N_EXPERTS + j]
    first = s * (ts * TOP_K + MOE_BLOCK) + start_ref[s * N_EXPERTS + j]

    def block(p0, n):
        for r in range(n):
            xb[r:r + 1, :] = hf_s[pl.ds(tok_ref[p0 + r], 1), :]
        bits = xb[0:n, :]
        lo = lax.bitcast_convert_type(bits << 16, F32).astype(BF16)
        hi = lax.bitcast_convert_type(bits & jnp.uint32(0xFFFF0000), F32).astype(BF16)
        g = _dot(lo, wg_ref[0, :half, :]) + _dot(hi, wg_ref[0, half:, :])
        u = _dot(lo, wu_ref[0, :half, :]) + _dot(hi, wu_ref[0, half:, :])
        yb[0:n, :] = _dot((_silu(g) * u).astype(BF16), wd_ref[0])
        for r in range(n):
            comb[pl.ds(rows_ref[p0 + r], 1), :] = w_ref[p0 + r] * yb[r:r + 1, :]

    def full_block(i, carry):
        block(first + i * MOE_BLOCK, MOE_BLOCK)
        return carry

    n_full = cnt // MOE_BLOCK
    rest = cnt - n_full * MOE_BLOCK
    lax.fori_loop(0, n_full, full_block, 0)

    @pl.when(rest > MOE_BLOCK // 2)
    def _():
        block(first + n_full * MOE_BLOCK, MOE_BLOCK)

    @pl.when((rest > 0) & (rest <= MOE_BLOCK // 2))
    def _():
        block(first + n_full * MOE_BLOCK, MOE_BLOCK // 2)

    @pl.when(j == N_EXPERTS - 1)
    def _():
        for_split(lambda k: [cp.wait() for cp in split_copies(k, 1)])

        def body(i, carry):
            r0 = pl.multiple_of(i * rc, 8)
            rows = pl.ds(r0, rc)
            moe = comb[rows, :]
            for k in range(1, TOP_K):
                moe = moe + comb[pl.ds(k * stride + r0, rc), :]
            acc[rows, :] = _rms(acc[rows, :] + moe, nfin_ref[...])
            return carry

        lax.fori_loop(0, ts // rc, body, 0)
        for_split(finish_split)


def _moe(plan, hf_p, hf_q, xmid_p, xmid_q, wg, wu, wd, norm_final):
    scalars, ts = plan
    n_prompt, n_sample = xmid_p.shape[0], xmid_q.shape[0]
    nsp = (n_prompt + n_sample) // ts
    rc = _row_tile(ts, 256) if ts % 16 == 0 else 8
    kern = functools.partial(_moe_kernel, ts=ts, rc=rc, nsp=nsp, n_prompt=n_prompt, n_sample=n_sample)
    hbm = pl.BlockSpec(memory_space=pl.ANY)
    expert = lambda shape: pl.BlockSpec((1,) + shape, lambda s, j, *_: (j, 0, 0))
    grid_spec = pltpu.PrefetchScalarGridSpec(
        num_scalar_prefetch=len(scalars),
        grid=(nsp, N_EXPERTS),
        in_specs=[
            hbm, hbm, hbm, hbm,
            expert((D_MODEL, D_EXPERT)), expert((D_MODEL, D_EXPERT)), expert((D_EXPERT, D_MODEL)),
            pl.BlockSpec((1, D_MODEL), lambda s, j, *_: (0, 0)),
        ],
        out_specs=(hbm, hbm),
        scratch_shapes=[
            pltpu.VMEM((ts + 8, D_MODEL // 2), jnp.uint32),
            pltpu.VMEM((TOP_K * (ts + 8), D_MODEL), F32),
            pltpu.VMEM((ts, D_MODEL), F32),
            pltpu.VMEM((MOE_BLOCK, D_MODEL // 2), jnp.uint32),
            pltpu.VMEM((MOE_BLOCK, D_MODEL), F32),
            pltpu.SemaphoreType.DMA((3, 2)),
        ],
    )
    return pl.pallas_call(
        kern,
        out_shape=(jax.ShapeDtypeStruct((n_prompt, D_MODEL), F32), jax.ShapeDtypeStruct((n_sample, D_MODEL), F32)),
        grid_spec=grid_spec,
        compiler_params=pltpu.CompilerParams(
            dimension_semantics=("arbitrary", "arbitrary"), vmem_limit_bytes=VMEM_LIMIT),
        name="moe",
    )(*scalars, hf_p, hf_q, xmid_p, xmid_q, wg, wu, wd, norm_final)


def _pack_w_in(w):
    kr = w[:, _OFF_KV + KV_LORA:_OFF_QKV]
    kr_sw = jnp.concatenate([kr[:, ROPE_DIM // 2:], kr[:, :ROPE_DIM // 2]], axis=1)
    small = jnp.concatenate([w[:, _OFF_B:_OFF_A], w[:, _OFF_A:_OFF_GM],
                             jnp.zeros((D_MODEL, SM_KR - 2 * GDN_HEADS), w.dtype), kr, kr_sw], axis=1)
    packed = jnp.concatenate([w[:, _OFF_QKV:_OFF_Z], w[:, _OFF_KV:_OFF_KV + KV_LORA], small, w[:, :Q_LORA],
                              w[:, _OFF_Z:_OFF_B], w[:, _OFF_GM:_OFF_GG], w[:, _OFF_GG:]], axis=1)
    return packed.astype(BF16)


def _pack_mla_weights(w_uq, w_uk, w_uv):
    zq = jnp.zeros((Q_LORA, MLA_HEADS, LANES - NOPE_DIM - ROPE_DIM), w_uq.dtype)
    wq = jnp.concatenate([w_uq, zq], axis=2).reshape(Q_LORA, MLA_HEADS * LANES)
    rope = w_uq[:, :, NOPE_DIM:]
    rope_sw = jnp.concatenate([rope[..., ROPE_DIM // 2:], rope[..., :ROPE_DIM // 2]], axis=2)
    wqs = jnp.concatenate([jnp.zeros((Q_LORA, MLA_HEADS, NOPE_DIM), w_uq.dtype), rope_sw, zq], axis=2)
    wqs = wqs.reshape(Q_LORA, MLA_HEADS * LANES)
    wk = jnp.concatenate([w_uk, jnp.zeros((KV_LORA, MLA_HEADS, LANES - NOPE_DIM), w_uk.dtype)], axis=2)
    wk = wk.reshape(KV_LORA, MLA_HEADS * LANES)
    zv = jnp.zeros((KV_LORA, MLA_HEADS // 2, V_DIM), w_uv.dtype)
    wv = jnp.stack([jnp.concatenate([w_uv[:, 0::2], zv], axis=2),
                    jnp.concatenate([zv, w_uv[:, 1::2]], axis=2)], axis=2)
    wv = wv.reshape(KV_LORA, MLA_HEADS * LANES)
    return wq.astype(BF16), wqs.astype(BF16), wk.astype(BF16), wv.astype(BF16)


def _rope_tables(pos):
    inv_freq = ROPE_THETA ** (-jnp.arange(0, ROPE_DIM, 2, dtype=F32) / ROPE_DIM)
    ang = pos.astype(F32)[:, None] * inv_freq[None, :]
    cos, sin = jnp.cos(ang), jnp.sin(ang)
    n = pos.shape[0]
    cos_t = jnp.concatenate([jnp.ones((n, NOPE_DIM), F32), cos, cos, jnp.zeros((n, ROPE_DIM), F32)], axis=1)
    sin_t = jnp.concatenate([jnp.zeros((n, NOPE_DIM), F32), -sin, sin, jnp.zeros((n, ROPE_DIM), F32)], axis=1)
    return cos_t, sin_t


def _head_lanes(v):
    return jnp.zeros((1, LANES), F32).at[0, SM_A:SM_A + GDN_HEADS].set(v.astype(F32))


def _moe_splits(nt):
    for nsp in (6, 3, 4, 2, 1):
        if nt % (nsp * 8) == 0:
            return nsp
    return 1


def kernel(x_prompt, x_sample, cache_kv_latent, cache_k_rope, page_table, state_conv, state_gdn, meta_tokens,
           norm_mix, w_in, q_norm, w_uq, kv_norm, w_uk, w_uv, conv_w, a_log, dt_bias, gdn_norm, w_out, norm_ffn,
           w_group, b_group, w_router, b_router, w_gate, w_up, w_down, norm_final):
    batch, seq, _ = x_prompt.shape
    ns, dec_seq, _ = x_sample.shape
    assert dec_seq == 1 and w_in.shape[0] == 1 and seq % CHUNK == 0
    n_pages = page_table.shape[1]
    n_prompt = batch * seq
    nt = n_prompt + ns
    meta_row = nt
    tq = _pick(seq, ATTN_TQ)
    n_rows = -(-(nt + N_META) // tq) * tq
    assert n_prompt % ns == 0 and nt % N_META == 0 and ns % LANES == 0 and meta_row % tq + N_META <= tq

    x_all = jnp.concatenate([x_prompt.reshape(n_prompt, D_MODEL), x_sample.reshape(ns, D_MODEL),
                             meta_tokens.astype(x_prompt.dtype),
                             jnp.zeros((n_rows - nt - N_META, D_MODEL), x_prompt.dtype)], axis=0)
    pos = jnp.concatenate([N_META + jnp.arange(seq), jnp.full((1,), n_pages * PAGE_SIZE), jnp.arange(N_META)])
    by_row = lambda t: jnp.concatenate([jnp.tile(t[:seq], (batch, 1)), jnp.broadcast_to(t[seq], (ns, LANES)),
                                        t[seq + 1:], jnp.zeros((n_rows - nt - N_META, LANES), F32)], axis=0)
    cos_t, sin_t = (by_row(t) for t in _rope_tables(pos))
    w_packed = _pack_w_in(w_in[0])
    wq, wqs, wk, wv = _pack_mla_weights(w_uq[0], w_uk[0], w_uv[0])
    wukt = jnp.concatenate([jnp.transpose(w_uk[0], (1, 2, 0)),
                            jnp.zeros((MLA_HEADS, LANES - NOPE_DIM, KV_LORA), w_uk.dtype)], axis=1).astype(BF16)
    alog_v, dtb_v = _head_lanes(a_log[0]), _head_lanes(dt_bias[0])
    cw = conv_w[0].astype(F32)
    gn_t = jnp.tile(gdn_norm[0].astype(F32), GDN_HEADS)[None]
    w_r = jnp.concatenate([w_group[0], w_router[0],
                           jnp.zeros((D_MODEL, LANES - N_GROUPS - N_EXPERTS), w_group.dtype)], axis=1).astype(BF16)
    b_r = jnp.concatenate([b_group[0], b_router[0], jnp.zeros((LANES - N_GROUPS - N_EXPERTS,), b_group.dtype)])[None]
    wg, wu, wd = w_gate[0].astype(BF16), w_up[0].astype(BF16), w_down[0].astype(BF16)

    proj, gates = _inproj(x_all, norm_mix[0][None].astype(F32), w_packed)
    qt, k, vt, ckv, krot = _mla_prep(proj, cos_t, sin_t, q_norm[0][None].astype(F32), kv_norm[0][None].astype(F32),
                                     wq.T, wqs.T, wk, wv.T, tq)

    o_mla_p = _attn_prompt(qt, k, vt, batch, seq, meta_row, tq)
    ql, qrt = _q_absorb(qt, wukt, ns, n_prompt)
    o_lat = _mla_sample(page_table, jnp.transpose(ql, (1, 0, 2)), jnp.transpose(qrt, (2, 0, 1)),
                        ckv[n_prompt:nt].reshape(ns, 1, KV_LORA), krot[n_prompt:nt].reshape(ns, 1, LANES),
                        cache_kv_latent[0], jnp.swapaxes(cache_k_rope[0], 1, 2))
    o_mla_s = _o_proj_sample(jnp.transpose(o_lat, (1, 0, 2)), wv)

    s_meta = _gdn_meta(proj, cw, alog_v, dtb_v, meta_row)
    terms = _gdn_terms(proj, cw, alog_v, dtb_v, batch, seq, meta_row)
    o_gdn_p, gdn_p = _gdn_scan(*terms, s_meta, batch, seq)
    o_gdn_s, gdn_s = _gdn_sample(proj, state_conv[0], state_gdn[0], cw, alog_v, dtb_v, n_prompt)

    tail = (gates, w_out[0].astype(BF16), gn_t, norm_ffn[0][None].astype(F32), w_r, b_r.astype(F32))
    xmid_p, hf_p, route_p = _outproj(x_all, o_mla_p, o_gdn_p, *tail, 0)
    xmid_s, hf_s, route_s = _outproj(x_all, o_mla_s, o_gdn_s, *tail, n_prompt)
    route = jnp.concatenate([route_p[:, :2 * TOP_K], route_s[:, :2 * TOP_K]], axis=0)
    plan = _moe_plan(route[:, :TOP_K].astype(jnp.int32), route[:, TOP_K:], _moe_splits(nt))
    y_p, y_s = _moe(plan, hf_p, hf_s, xmid_p, xmid_s, wg, wu, wd, norm_final[None].astype(F32))

    def with_meta(rows, width):
        meta = jnp.broadcast_to(rows[meta_row:meta_row + N_META][None], (batch, N_META, width))
        return jnp.concatenate([meta, rows[:n_prompt].reshape(batch, seq, width)], axis=1)[None]

    k_rope = krot[:, SM_KR:SM_KR + ROPE_DIM]
    conv_p = jnp.stack([proj[(b + 1) * seq - (CONV_W - 1):(b + 1) * seq, :GDN_QKV] for b in range(batch)])
    conv_s = jnp.concatenate([state_conv[0][:, 1:].astype(F32), proj[n_prompt:nt, None, :GDN_QKV]], axis=1)
    return (y_p.reshape(batch, seq, D_MODEL), y_s.reshape(ns, 1, D_MODEL),
            with_meta(ckv, KV_LORA), with_meta(k_rope, ROPE_DIM),
            ckv[n_prompt:nt].reshape(1, ns, 1, KV_LORA), k_rope[n_prompt:nt].reshape(1, ns, 1, ROPE_DIM),
            conv_p[None], conv_s[None], gdn_p[None], gdn_s[None])
```

```python
import functools

import jax
import jax.numpy as jnp
from jax import lax
from jax.experimental import pallas as pl
from jax.experimental.pallas import tpu as pltpu

F32 = jnp.float32
BF16 = jnp.bfloat16
HIGHEST = lax.Precision.HIGHEST

D_MODEL = 1024
N_META = 16
RMS_EPS = 1e-6
MLA_HEADS = 16
Q_LORA = 384
KV_LORA = 256
NOPE_DIM = 64
ROPE_DIM = 32
V_DIM = 64
ROPE_THETA = 10000.0
MLA_SCALE = (NOPE_DIM + ROPE_DIM) ** -0.5
PAGE_SIZE = 128
GDN_HEADS = 8
GDN_DK = 128
GDN_DV = 128
GDN_KEY = GDN_HEADS * GDN_DK
GDN_QKV = 3 * GDN_KEY
CONV_W = 4
CHUNK = 64
N_GROUPS = 4
EXPERTS_PER_GROUP = 8
N_EXPERTS = 32
TOP_K = 2
D_EXPERT = 256
MOE_BLOCK = 128

_OFF_KV = Q_LORA
_OFF_QKV = _OFF_KV + KV_LORA + ROPE_DIM
_OFF_Z = _OFF_QKV + GDN_QKV
_OFF_B = _OFF_Z + GDN_KEY
_OFF_A = _OFF_B + GDN_HEADS
_OFF_GM = _OFF_A + GDN_HEADS
_OFF_GG = _OFF_GM + D_MODEL
P_QKV = 0
P_KVC = 3072
P_SMALL = 3328
P_QD = 3456
P_F32 = 3840
P_Z = 0
P_GM = 1024
P_GG = 2048
P_BF16 = 3072
P_TOTAL = P_F32 + P_BF16
INPROJ_TN = 768
SM_B = 0
SM_A = 8
SM_KR = 64

LANES = 128
VMEM_LIMIT = 56 * 1024 * 1024
ATTN_TQ = (512, 256, 128)


def _pick(n, candidates):
    for c in candidates:
        if n % c == 0:
            return c
    raise ValueError(f"no tile for {n} in {candidates}")


def _dot(a, b):
    return jnp.dot(a, b, preferred_element_type=F32)


def _dot_nt(a, b):
    return lax.dot_general(a, b, (((1,), (1,)), ((), ())), preferred_element_type=F32)


def _dot_tn(a, b):
    return lax.dot_general(a, b, (((0,), (0,)), ((), ())), preferred_element_type=F32)


def _sigmoid(x):
    return 1.0 / (1.0 + jnp.exp(-x))


def _silu(x):
    return x * _sigmoid(x)


def _softplus(x):
    return jnp.maximum(x, 0.0) + jnp.log1p(jnp.exp(-jnp.abs(x)))


def _rms(x, w):
    return x * lax.rsqrt(jnp.mean(x * x, axis=-1, keepdims=True) + RMS_EPS) * w


def _inproj_kernel(x_ref, nw_ref, w_ref, of_ref, ob_ref, hn_ref):
    j = pl.program_id(1)
    nf = P_F32 // INPROJ_TN

    @pl.when(j == 0)
    def _():
        hn_ref[...] = _rms(x_ref[...], nw_ref[...]).astype(BF16)

    r = _dot(hn_ref[...], w_ref[...])

    @pl.when(j < nf)
    def _():
        of_ref[...] = r

    @pl.when(j >= nf)
    def _():
        ob_ref[...] = r.astype(BF16)


def _inproj(x_all, norm_w, w_packed):
    r = x_all.shape[0]
    tm = _pick(r, (1536, 1280, 768, 640, 512, 256, 128))
    tn = INPROJ_TN
    nf = P_F32 // tn
    return pl.pallas_call(
        _inproj_kernel,
        out_shape=(jax.ShapeDtypeStruct((r, P_F32), F32), jax.ShapeDtypeStruct((r, P_BF16), BF16)),
        grid=(r // tm, P_TOTAL // tn),
        in_specs=[
            pl.BlockSpec((tm, D_MODEL), lambda i, j: (i, 0)),
            pl.BlockSpec((1, D_MODEL), lambda i, j: (0, 0)),
            pl.BlockSpec((D_MODEL, tn), lambda i, j: (0, j)),
        ],
        out_specs=(pl.BlockSpec((tm, tn), lambda i, j: (i, jnp.minimum(j, nf - 1))),
                   pl.BlockSpec((tm, tn), lambda i, j: (i, jnp.maximum(j - nf, 0)))),
        scratch_shapes=[pltpu.VMEM((tm, D_MODEL), BF16)],
        compiler_params=pltpu.CompilerParams(
            dimension_semantics=("parallel", "arbitrary"), vmem_limit_bytes=VMEM_LIMIT),
        name="inproj",
    )(x_all, norm_w, w_packed)


LOG2E = 1.4426950408889634


def _mla_prep_kernel(qd_ref, kvc_ref, sm_ref, c_ref, s_ref, qn_ref, kvn_ref, wqt_ref, wqst_ref, wk_ref, wvt_ref,
                     qt_out, k_out, vt_out, ckv_out, kr_out):
    cos = c_ref[...]
    sin = s_ref[...]
    cos_t, sin_t = cos.T, sin.T
    qn_t = _rms(qd_ref[...], qn_ref[...]).T.astype(BF16)
    qt = _dot(wqt_ref[...], qn_t)
    qst = _dot(wqst_ref[...], qn_t)
    for h in range(MLA_HEADS):
        sl = slice(h * LANES, (h + 1) * LANES)
        qt_out[h, 0] = ((qt[sl] * cos_t + qst[sl] * sin_t) * (MLA_SCALE * LOG2E)).astype(BF16)
    ckv = _rms(kvc_ref[...], kvn_ref[...])
    ckv_out[...] = ckv
    sm = sm_ref[...]
    lane = lax.broadcasted_iota(jnp.int32, sm.shape, 1)
    cos_k = jnp.where((lane >= SM_KR) & (lane < SM_KR + ROPE_DIM), cos, 0.0)
    krot = sm * cos_k + pltpu.roll(sm, LANES - ROPE_DIM, 1) * sin
    kr_out[...] = krot
    kk = _dot(ckv.astype(BF16), wk_ref[...])
    vvt = _dot(wvt_ref[...], ckv.T.astype(BF16))
    for h in range(MLA_HEADS):
        sl = slice(h * LANES, (h + 1) * LANES)
        k_out[h] = (kk[:, sl] + krot).astype(BF16)
        vt_out[h, 0] = vvt[sl].astype(BF16)


def _mla_prep(proj, cos_t, sin_t, q_norm, kv_norm, wqt, wqst, wk, wvt, tm):
    r = proj.shape[0]
    hw = MLA_HEADS * LANES
    full = lambda shape: pl.BlockSpec(shape, lambda i: (0,) * len(shape))
    return pl.pallas_call(
        _mla_prep_kernel,
        out_shape=(
            jax.ShapeDtypeStruct((MLA_HEADS, r // tm, LANES, tm), BF16),
            jax.ShapeDtypeStruct((MLA_HEADS, r, LANES), BF16),
            jax.ShapeDtypeStruct((MLA_HEADS, r // tm, LANES, tm), BF16),
            jax.ShapeDtypeStruct((r, KV_LORA), F32),
            jax.ShapeDtypeStruct((r, LANES), F32),
        ),
        grid=(r // tm,),
        in_specs=[
            pl.BlockSpec((tm, Q_LORA), lambda i: (i, P_QD // Q_LORA)),
            pl.BlockSpec((tm, KV_LORA), lambda i: (i, P_KVC // KV_LORA)),
            pl.BlockSpec((tm, LANES), lambda i: (i, P_SMALL // LANES)),
            pl.BlockSpec((tm, LANES), lambda i: (i, 0)),
            pl.BlockSpec((tm, LANES), lambda i: (i, 0)),
            full((1, Q_LORA)), full((1, KV_LORA)),
            full((hw, Q_LORA)), full((hw, Q_LORA)), full((KV_LORA, hw)), full((hw, KV_LORA)),
        ],
        out_specs=(pl.BlockSpec((MLA_HEADS, 1, LANES, tm), lambda i: (0, i, 0, 0)),
                   pl.BlockSpec((MLA_HEADS, tm, LANES), lambda i: (0, i, 0)),
                   pl.BlockSpec((MLA_HEADS, 1, LANES, tm), lambda i: (0, i, 0, 0)),
                   pl.BlockSpec((tm, KV_LORA), lambda i: (i, 0)),
                   pl.BlockSpec((tm, LANES), lambda i: (i, 0))),
        compiler_params=pltpu.CompilerParams(dimension_semantics=("parallel",), vmem_limit_bytes=VMEM_LIMIT),
        name="mla_prep",
    )(proj, proj, proj, cos_t, sin_t, q_norm, kv_norm, wqt, wqst, wk, wvt)


def _attn_prompt_kernel(qt_ref, k_ref, vt_ref, km_ref, vmt_ref, o_ref, *, tq, meta_lane, nq, tiles_per_step):
    i = pl.program_id(2)
    tiles = (i,) if tiles_per_step == 1 else (i, nq - 1 - i)
    streams = [(t, h) for t in tiles for h in (0, 1)]
    half = tq // 2
    qt = [qt_ref[h, t] for t, h in streams]

    def update(qts, k_blk, vt_blk, carry, mask):
        ms, ls, accs = carry
        n = range(len(qts))
        s = [_dot(k_blk[x], qts[x]) for x in n]
        if mask is not None:
            s = [jnp.where(mask, v, -1e30) for v in s]
        m_new = [jnp.maximum(ms[x], jnp.max(s[x], axis=0, keepdims=True)) for x in n]
        a = [jnp.exp2(ms[x] - m_new[x]) for x in n]
        p = [jnp.exp2(s[x] - m_new[x]) for x in n]
        l_new = [a[x] * ls[x] + jnp.sum(p[x], axis=0, keepdims=True) for x in n]
        acc_new = [a[x] * accs[x] + _dot(vt_blk[x], p[x].astype(BF16)) for x in n]
        return m_new, l_new, acc_new

    s0 = [_dot(km_ref[h], q) for (_, h), q in zip(streams, qt)]
    m = [jnp.max(v, axis=0, keepdims=True) for v in s0]
    p0 = [jnp.exp2(v - mx) for v, mx in zip(s0, m)]
    l = [jnp.sum(v, axis=0, keepdims=True) for v in p0]
    acc = [_dot(vmt_ref[h, 0][:, meta_lane:meta_lane + N_META], v.astype(BF16)) for (_, h), v in zip(streams, p0)]

    for ti, t in enumerate(tiles):
        own = slice(2 * ti, 2 * ti + 2)

        def body(j, carry, own=own):
            rows = pl.ds(pl.multiple_of(j * tq, tq), tq)
            return update(qt[own], [k_ref[h, rows, :] for h in (0, 1)], [vt_ref[h, j] for h in (0, 1)], carry, None)

        m[own], l[own], acc[own] = lax.fori_loop(0, t, body, (m[own], l[own], acc[own]))

    key = lax.broadcasted_iota(jnp.int32, (half, tq), 0)
    qry = lax.broadcasted_iota(jnp.int32, (half, tq), 1)
    off = [pl.multiple_of(t * tq, tq) for t, _ in streams]
    vt_d = [vt_ref[h, t] for t, h in streams]
    m, l, acc = update(qt, [k_ref[h, pl.ds(o, half), :] for (_, h), o in zip(streams, off)],
                       [v[:, :half] for v in vt_d], (m, l, acc), key <= qry)
    late = lambda xs: [v[:, half:] for v in xs]
    mb, lb, accb = update(late(qt), [k_ref[h, pl.ds(pl.multiple_of(o + half, half), half), :]
                                     for (_, h), o in zip(streams, off)],
                          late(vt_d), (late(m), late(l), late(acc)), (key <= qry)[:, :half])
    ot = [jnp.concatenate([acc[x][:, :half] / l[x][:, :half], accb[x] / lb[x]], axis=1) for x in range(len(streams))]
    for ti, t in enumerate(tiles):
        o_ref[pl.ds(pl.multiple_of(t * tq, tq), tq), :] = (ot[2 * ti] + ot[2 * ti + 1]).T.astype(o_ref.dtype)


def _attn_prompt(qt, k, vt, batch, seq, meta_row, tq):
    nq = seq // tq
    tps = 2 if nq % 2 == 0 else 1
    kern = functools.partial(_attn_prompt_kernel, tq=tq, meta_lane=meta_row % tq, nq=nq, tiles_per_step=tps)
    tiled = lambda: pl.BlockSpec((2, nq, LANES, tq), lambda b, p, i: (p, b, 0, 0))
    return pl.pallas_call(
        kern,
        out_shape=jax.ShapeDtypeStruct((batch * seq, D_MODEL), BF16),
        grid=(batch, MLA_HEADS // 2, nq // tps),
        in_specs=[
            tiled(),
            pl.BlockSpec((2, seq, LANES), lambda b, p, i: (p, b, 0)),
            tiled(),
            pl.BlockSpec((2, N_META, LANES), lambda b, p, i: (p, meta_row // N_META, 0)),
            pl.BlockSpec((2, 1, LANES, tq), lambda b, p, i: (p, meta_row // tq, 0, 0)),
        ],
        out_specs=pl.BlockSpec((seq, LANES), lambda b, p, i: (b, p)),
        compiler_params=pltpu.CompilerParams(
            dimension_semantics=("parallel", "parallel", "arbitrary"), vmem_limit_bytes=VMEM_LIMIT),
        name="attn_prompt",
    )(qt, k, vt, k, vt)


def _gate_lanes(sm, alog_ref, dtb_ref):
    g = -jnp.exp(alog_ref[...]) * _softplus(sm + dtb_ref[...])
    beta = _sigmoid(sm)
    return g, beta


def _qkv_heads(xc):
    xf = _silu(xc)
    qs, ks, vs = [], [], []
    for h in range(GDN_HEADS):
        q = xf[:, h * GDN_DK:(h + 1) * GDN_DK]
        k = xf[:, GDN_KEY + h * GDN_DK:GDN_KEY + (h + 1) * GDN_DK]
        qs.append(q * lax.rsqrt(jnp.sum(q * q, axis=-1, keepdims=True) + RMS_EPS) * (GDN_DK ** -0.5))
        ks.append(k * lax.rsqrt(jnp.sum(k * k, axis=-1, keepdims=True) + RMS_EPS))
        vs.append(xf[:, 2 * GDN_KEY + h * GDN_DV:2 * GDN_KEY + (h + 1) * GDN_DV])
    return qs, ks, vs


def _split_bf16(x):
    hi = x.astype(BF16)
    return hi, (x - hi.astype(F32)).astype(BF16)


def _dot_split(a, b):
    return _dot(a[0], b[0]) + (_dot(a[0], b[1]) + _dot(a[1], b[0]))


def _unit_lower_inverses(mats, c):
    row = lax.broadcasted_iota(jnp.int32, (c, c), 0)
    col = lax.broadcasted_iota(jnp.int32, (c, c), 1)
    eye = jnp.where(row == col, 1.0, 0.0)
    ps = [-a for a in mats]
    ts = [eye + p for p in ps]
    span = 2
    while span < c:
        psp = [_split_bf16(p) for p in ps]
        ps = [_dot_split(p, p) for p in psp]
        psp = [_split_bf16(p) for p in ps]
        ts = [t + _dot_split(p, _split_bf16(t)) for p, t in zip(psp, ts)]
        span *= 2
    return ts


def _gdn_chunk_terms(xs, sm, conv_ref, alog_ref, dtb_ref, c):
    heads = range(GDN_HEADS)
    xc = xs[0] * conv_ref[0:1, :]
    for j in range(1, CONV_W):
        xc = xc + xs[j] * conv_ref[j:j + 1, :]
    qs, ks, vs = _qkv_heads(xc)
    g, beta = _gate_lanes(sm, alog_ref, dtb_ref)
    row = lax.broadcasted_iota(jnp.int32, (c, c), 0)
    col = lax.broadcasted_iota(jnp.int32, (c, c), 1)
    causal = col <= row
    strict = col < row
    gcum = jnp.dot(jnp.where(causal, 1.0, 0.0), g, precision=HIGHEST, preferred_element_type=F32)
    gcum_t = lax.dot_general(g, jnp.where(col >= row, 1.0, 0.0), (((0,), (0,)), ((), ())),
                             precision=HIGHEST, preferred_element_type=F32)
    gc = [gcum[:, SM_A + h:SM_A + h + 1] for h in heads]
    gr = [gcum_t[SM_A + h:SM_A + h + 1, :] for h in heads]
    bc = [beta[:, SM_B + h:SM_B + h + 1] for h in heads]
    decay = [jnp.where(causal, jnp.exp(jnp.where(causal, gc[h] - gr[h], 0.0)), 0.0) for h in heads]
    kb = [ks[h] * bc[h] for h in heads]
    kbf = [ks[h].astype(BF16) for h in heads]
    a = [jnp.where(strict, _dot_nt(kb[h].astype(BF16), kbf[h]) * decay[h], 0.0) for h in heads]
    t = _unit_lower_inverses(a, c)
    eg = [jnp.exp(gc[h]) for h in heads]
    sol = [_dot(t[h].astype(BF16), jnp.concatenate([vs[h] * bc[h], kb[h] * eg[h]], axis=1).astype(BF16))
           for h in heads]
    u = [sol[h][:, :GDN_DV] for h in heads]
    w = [sol[h][:, GDN_DV:] for h in heads]
    attn = [jnp.where(causal, _dot_nt(qs[h].astype(BF16), kbf[h]) * decay[h], 0.0) for h in heads]
    qg = [qs[h] * eg[h] for h in heads]
    g_last = gcum[c - 1:c, :]
    kd = [ks[h] * jnp.exp(g_last[:, SM_A + h:SM_A + h + 1] - gc[h]) for h in heads]
    return u, w, qg, kd, attn, jnp.exp(g_last)


def _conv_shifts(hist, x):
    xe = jnp.concatenate([hist, x], axis=0)
    return [pltpu.roll(xe, d, 0)[8:] for d in range(CONV_W - 1, 0, -1)] + [x]


def _gdn_meta_kernel(x_ref, sm_ref, conv_ref, alog_ref, dtb_ref, s_out):
    xs = _conv_shifts(jnp.zeros((8, GDN_QKV), F32), x_ref[...])
    u, _, _, kd, _, _ = _gdn_chunk_terms(xs, sm_ref[...], conv_ref, alog_ref, dtb_ref, N_META)
    for h in range(GDN_HEADS):
        s_out[h] = _dot_tn(kd[h].astype(BF16), u[h].astype(BF16))


def _gdn_meta(proj, conv_w, alog_v, dtb_v, meta_row):
    full = lambda shape: pl.BlockSpec(shape, lambda i: (0,) * len(shape))
    return pl.pallas_call(
        _gdn_meta_kernel,
        out_shape=jax.ShapeDtypeStruct((GDN_HEADS, GDN_DK, GDN_DV), F32),
        grid=(1,),
        in_specs=[
            pl.BlockSpec((N_META, GDN_QKV), lambda i: (meta_row // N_META, 0)),
            pl.BlockSpec((N_META, LANES), lambda i: (meta_row // N_META, P_SMALL // LANES)),
            full((CONV_W, GDN_QKV)), full((1, LANES)), full((1, LANES)),
        ],
        out_specs=full((GDN_HEADS, GDN_DK, GDN_DV)),
        compiler_params=pltpu.CompilerParams(vmem_limit_bytes=VMEM_LIMIT),
        name="gdn_meta",
    )(proj, proj, conv_w, alog_v, dtb_v)


def _gdn_terms_kernel(x_ref, hist_ref, sm_ref, conv_ref, alog_ref, dtb_ref,
                      u_out, w_out, qg_out, kd_out, attn_out, dec_out, *, cps):
    c = CHUNK
    hist = hist_ref[...]
    for cc in range(cps):
        rows = slice(cc * c, (cc + 1) * c)
        x = x_ref[rows, :]
        u, w, qg, kd, attn, dec = _gdn_chunk_terms(_conv_shifts(hist, x), sm_ref[rows, :], conv_ref, alog_ref,
                                                   dtb_ref, c)
        for h in range(GDN_HEADS):
            sl = slice(h * GDN_DV, (h + 1) * GDN_DV)
            u_out[rows, sl] = u[h]
            w_out[rows, sl] = w[h].astype(BF16)
            qg_out[rows, sl] = qg[h].astype(BF16)
            kd_out[rows, sl] = kd[h].astype(BF16)
            attn_out[h, rows, :] = attn[h].astype(BF16)
        dec_out[cc] = dec
        hist = x[c - 8:]


def _gdn_terms(proj, conv_w, alog_v, dtb_v, batch, seq, meta_row):
    nc = seq // CHUNK
    n = batch * seq
    cps = _pick(nc, (4, 2, 1))
    step = cps * CHUNK
    spb = nc // cps
    full = lambda shape: pl.BlockSpec(shape, lambda i: (0,) * len(shape))
    rows = lambda: pl.BlockSpec((step, D_MODEL), lambda i: (i, 0))

    def hist_index(i):
        return (jnp.where(i % spb == 0, (meta_row + N_META) // 8, i * (step // 8)) - 1, 0)

    return pl.pallas_call(
        functools.partial(_gdn_terms_kernel, cps=cps),
        out_shape=(jax.ShapeDtypeStruct((n, D_MODEL), F32),
                   jax.ShapeDtypeStruct((n, D_MODEL), BF16),
                   jax.ShapeDtypeStruct((n, D_MODEL), BF16),
                   jax.ShapeDtypeStruct((n, D_MODEL), BF16),
                   jax.ShapeDtypeStruct((GDN_HEADS, n, CHUNK), BF16),
                   jax.ShapeDtypeStruct((batch * nc, 1, LANES), F32)),
        grid=(batch * spb,),
        in_specs=[
            pl.BlockSpec((step, GDN_QKV), lambda i: (i, 0)),
            pl.BlockSpec((8, GDN_QKV), hist_index),
            pl.BlockSpec((step, LANES), lambda i: (i, P_SMALL // LANES)),
            full((CONV_W, GDN_QKV)), full((1, LANES)), full((1, LANES)),
        ],
        out_specs=(rows(), rows(), rows(), rows(),
                   pl.BlockSpec((GDN_HEADS, step, CHUNK), lambda i: (0, i, 0)),
                   pl.BlockSpec((cps, 1, LANES), lambda i: (i, 0, 0))),
        compiler_params=pltpu.CompilerParams(dimension_semantics=("parallel",), vmem_limit_bytes=VMEM_LIMIT),
        name="gdn_terms",
    )(proj, proj, proj, conv_w, alog_v, dtb_v)


def _gdn_scan_kernel(u_ref, w_ref, qg_ref, kd_ref, attn_ref, dec_ref, s0_ref, o_ref, s_out, st_ref, *, cpg, nseq):
    c = CHUNK
    streams = [(q, h) for q in range(nseq) for h in range(GDN_HEADS)]
    sl = [slice(h * GDN_DV, (h + 1) * GDN_DV) for h in range(GDN_HEADS)]

    @pl.when(pl.program_id(1) == 0)
    def _():
        for q in range(nseq):
            st_ref[q] = s0_ref[...]

    def chunk(ci, carry):
        rows = pl.ds(pl.multiple_of(ci * c, c), c)
        dec = [dec_ref[q, ci] for q in range(nseq)]
        s_old = [st_ref[q, h] for q, h in streams]
        sb = [s.astype(BF16) for s in s_old]
        lhs = [jnp.concatenate([w_ref[q, rows, sl[h]], qg_ref[q, rows, sl[h]]], axis=0) for q, h in streams]
        r = [_dot(a, b) for a, b in zip(lhs, sb)]
        vnb = [(u_ref[q, rows, sl[h]] - rr[:c]).astype(BF16) for (q, h), rr in zip(streams, r)]
        out = [rr[c:] + _dot(attn_ref[h, q, rows, :], v) for (q, h), rr, v in zip(streams, r, vnb)]
        upd = [_dot_tn(kd_ref[q, rows, sl[h]], v) for (q, h), v in zip(streams, vnb)]
        for x, (q, h) in enumerate(streams):
            o_ref[q, rows, sl[h]] = out[x].astype(o_ref.dtype)
            st_ref[q, h] = s_old[x] * dec[q][:, SM_A + h:SM_A + h + 1] + upd[x]
        return carry

    lax.fori_loop(0, cpg, chunk, 0)

    @pl.when(pl.program_id(1) == pl.num_programs(1) - 1)
    def _():
        s_out[...] = st_ref[...]


def _gdn_scan(u, w, qg, kd, attn, dec, s_meta, batch, seq):
    nc = seq // CHUNK
    nseq = 2 if batch % 2 == 0 else 1
    cpg = _pick(nc, (8, 4, 2, 1))
    ng = nc // cpg
    by_seq = lambda a: a.reshape(batch, seq, D_MODEL)
    rows = lambda: pl.BlockSpec((nseq, cpg * CHUNK, D_MODEL), lambda b, g: (b, g, 0))
    kern = functools.partial(_gdn_scan_kernel, cpg=cpg, nseq=nseq)
    o, s_fin = pl.pallas_call(
        kern,
        out_shape=(jax.ShapeDtypeStruct((batch, seq, D_MODEL), BF16),
                   jax.ShapeDtypeStruct((batch, GDN_HEADS, GDN_DK, GDN_DV), F32)),
        grid=(batch // nseq, ng),
        in_specs=[rows(), rows(), rows(), rows(),
                  pl.BlockSpec((GDN_HEADS, nseq, cpg * CHUNK, CHUNK), lambda b, g: (0, b, g, 0)),
                  pl.BlockSpec((nseq, cpg, 1, LANES), lambda b, g: (b, g, 0, 0)),
                  pl.BlockSpec((GDN_HEADS, GDN_DK, GDN_DV), lambda b, g: (0, 0, 0))],
        out_specs=(rows(),
                   pl.BlockSpec((nseq, GDN_HEADS, GDN_DK, GDN_DV), lambda b, g: (b, 0, 0, 0))),
        scratch_shapes=[pltpu.VMEM((nseq, GDN_HEADS, GDN_DK, GDN_DV), F32)],
        compiler_params=pltpu.CompilerParams(
            dimension_semantics=("parallel", "arbitrary"), vmem_limit_bytes=VMEM_LIMIT),
        name="gdn_scan",
    )(by_seq(u), by_seq(w), by_seq(qg), by_seq(kd), attn.reshape(GDN_HEADS, batch, seq, CHUNK),
      dec.reshape(batch, nc, 1, LANES), s_meta)
    return o.reshape(batch * seq, D_MODEL), s_fin


def _gdn_sample_kernel(x_ref, sm_ref, cs_ref, st_ref, conv_ref, alog_ref, dtb_ref, o_ref, s_out, *, nb):
    xc = x_ref[...] * conv_ref[CONV_W - 1:CONV_W, :]
    for j in range(CONV_W - 1):
        xc = xc + cs_ref[:, j, :] * conv_ref[j:j + 1, :]
    qs, ks, vs = _qkv_heads(xc)
    g, beta = _gate_lanes(sm_ref[...], alog_ref, dtb_ref)
    eg = jnp.exp(g)
    for h in range(GDN_HEADS):
        q_t = qs[h].T
        k_t = ks[h].T
        for b in range(nb):
            kcol = k_t[:, b:b + 1]
            s1 = st_ref[b, h] * eg[b:b + 1, SM_A + h:SM_A + h + 1]
            r = jnp.sum(s1 * kcol, axis=0, keepdims=True)
            delta = (vs[h][b:b + 1, :] - r) * beta[b:b + 1, SM_B + h:SM_B + h + 1]
            s2 = s1 + kcol * delta
            s_out[b, h] = s2
            o_ref[b:b + 1, h * GDN_DV:(h + 1) * GDN_DV] = jnp.sum(s2 * q_t[:, b:b + 1], axis=0, keepdims=True)


def _gdn_sample(proj, state_conv, state_gdn, conv_w, alog_v, dtb_v, row0):
    ns = state_gdn.shape[0]
    nb = 8
    full = lambda shape: pl.BlockSpec(shape, lambda i: (0,) * len(shape))
    kern = functools.partial(_gdn_sample_kernel, nb=nb)
    return pl.pallas_call(
        kern,
        out_shape=(jax.ShapeDtypeStruct((ns, D_MODEL), F32),
                   jax.ShapeDtypeStruct(state_gdn.shape, F32)),
        grid=(ns // nb,),
        in_specs=[
            pl.BlockSpec((nb, GDN_QKV), lambda i: (row0 // nb + i, 0)),
            pl.BlockSpec((nb, LANES), lambda i: (row0 // nb + i, P_SMALL // LANES)),
            pl.BlockSpec((nb, CONV_W - 1, GDN_QKV), lambda i: (i, 0, 0)),
            pl.BlockSpec((nb, GDN_HEADS, GDN_DK, GDN_DV), lambda i: (i, 0, 0, 0)),
            full((CONV_W, GDN_QKV)), full((1, LANES)), full((1, LANES)),
        ],
        out_specs=(pl.BlockSpec((nb, D_MODEL), lambda i: (i, 0)),
                   pl.BlockSpec((nb, GDN_HEADS, GDN_DK, GDN_DV), lambda i: (i, 0, 0, 0))),
        compiler_params=pltpu.CompilerParams(dimension_semantics=("parallel",), vmem_limit_bytes=VMEM_LIMIT),
        name="gdn_sample",
    )(proj, proj, state_conv, state_gdn, conv_w, alog_v, dtb_v)


def _q_absorb_kernel(qt_ref, wukt_ref, ql_out, qrt_out, *, lane0, ns):
    for h in range(MLA_HEADS):
        qt = qt_ref[h, 0][:, lane0:lane0 + ns]
        ql_out[h] = _dot_tn(qt, wukt_ref[h]).astype(BF16)
        qrt_out[h] = qt[NOPE_DIM:NOPE_DIM + ROPE_DIM]


def _q_absorb(qt, wukt, ns, row0):
    tm = qt.shape[-1]
    assert row0 % tm + ns <= tm and (row0 % tm) % LANES == 0
    return pl.pallas_call(
        functools.partial(_q_absorb_kernel, lane0=row0 % tm, ns=ns),
        out_shape=(jax.ShapeDtypeStruct((MLA_HEADS, ns, KV_LORA), BF16),
                   jax.ShapeDtypeStruct((MLA_HEADS, ROPE_DIM, ns), BF16)),
        grid=(1,),
        in_specs=[pl.BlockSpec((MLA_HEADS, 1, LANES, tm), lambda i: (0, row0 // tm, 0, 0)),
                  pl.BlockSpec((MLA_HEADS, LANES, KV_LORA), lambda i: (0, 0, 0))],
        out_specs=(pl.BlockSpec((MLA_HEADS, ns, KV_LORA), lambda i: (0, 0, 0)),
                   pl.BlockSpec((MLA_HEADS, ROPE_DIM, ns), lambda i: (0, 0, 0))),
        compiler_params=pltpu.CompilerParams(vmem_limit_bytes=VMEM_LIMIT),
        name="q_absorb",
    )(qt, wukt)


def _mla_sample_kernel(pt_ref, ql_ref, qr_ref, cn_ref, krn_ref, cc_hbm, cr_hbm, o_ref, cbuf, rbuf, sem,
                       *, n_pages, nsub):
    g = pl.program_id(0)

    def page_copies(seq, slot, i):
        page = pt_ref[seq * n_pages + i]
        return (pltpu.make_async_copy(cc_hbm.at[page], cbuf.at[slot, i], sem.at[slot, 0]),
                pltpu.make_async_copy(cr_hbm.at[page], rbuf.at[slot, :, pl.ds(i * PAGE_SIZE, PAGE_SIZE)],
                                      sem.at[slot, 1]))

    def start_pages(seq, slot):
        for i in range(n_pages):
            cc, cr = page_copies(seq, slot, i)
            cc.start()
            cr.start()

    def wait_pages(seq, slot):
        for i in range(n_pages):
            cc, cr = page_copies(seq, slot, i)
            cc.wait()
            cr.wait()

    def attend(slot):
        ql = ql_ref[slot]
        qr = qr_ref[slot]
        pps = n_pages // nsub
        subs = range(nsub)
        c = [cbuf[slot, i * pps:(i + 1) * pps].reshape(pps * PAGE_SIZE, KV_LORA).astype(BF16) for i in subs]
        s = [_dot_nt(ql, c[i]) + _dot(qr, rbuf[slot, :, i * pps * PAGE_SIZE:(i + 1) * pps * PAGE_SIZE].astype(BF16))
             for i in subs]
        ms = [jnp.max(s[i], axis=1, keepdims=True) for i in subs]
        p = [jnp.exp2(s[i] - ms[i]) for i in subs]
        ls = [jnp.sum(p[i], axis=1, keepdims=True) for i in subs]
        accs = [_dot(p[i].astype(BF16), c[i]) for i in subs]
        cn = cn_ref[slot]
        krn = krn_ref[slot][:, SM_KR:SM_KR + ROPE_DIM]
        ms.append(jnp.sum(ql.astype(F32) * cn, axis=1, keepdims=True)
                  + jnp.sum(qr.astype(F32) * krn, axis=1, keepdims=True))
        ls.append(jnp.ones_like(ms[-1]))
        accs.append(jnp.broadcast_to(cn, (MLA_HEADS, KV_LORA)))
        m = functools.reduce(jnp.maximum, ms)
        scale = [jnp.exp2(m_i - m) for m_i in ms]
        l = sum(a * l_i for a, l_i in zip(scale, ls))
        acc = sum(a * acc_i for a, acc_i in zip(scale, accs))
        o_ref[slot] = acc / l

    @pl.when(g == 0)
    def _():
        start_pages(0, 0)

    start_pages(2 * g + 1, 1)
    wait_pages(2 * g, 0)
    attend(0)

    @pl.when(g + 1 < pl.num_programs(0))
    def _():
        start_pages(2 * g + 2, 0)

    wait_pages(2 * g + 1, 1)
    attend(1)


def _mla_sample(page_table, ql, qr, c_new, kr_new, cache_c, cache_r):
    ns, n_pages = page_table.shape
    assert ns % 2 == 0
    kern = functools.partial(_mla_sample_kernel, n_pages=n_pages, nsub=_pick(n_pages, (8, 4, 2, 1)))
    grid_spec = pltpu.PrefetchScalarGridSpec(
        num_scalar_prefetch=1,
        grid=(ns // 2,),
        in_specs=[
            pl.BlockSpec((2, MLA_HEADS, KV_LORA), lambda b, pt: (b, 0, 0)),
            pl.BlockSpec((2, MLA_HEADS, ROPE_DIM), lambda b, pt: (b, 0, 0)),
            pl.BlockSpec((2, 1, KV_LORA), lambda b, pt: (b, 0, 0)),
            pl.BlockSpec((2, 1, LANES), lambda b, pt: (b, 0, 0)),
            pl.BlockSpec(memory_space=pl.ANY),
            pl.BlockSpec(memory_space=pl.ANY),
        ],
        out_specs=pl.BlockSpec((2, MLA_HEADS, KV_LORA), lambda b, pt: (b, 0, 0)),
        scratch_shapes=[pltpu.VMEM((2, n_pages, PAGE_SIZE, KV_LORA), F32),
                        pltpu.VMEM((2, ROPE_DIM, n_pages * PAGE_SIZE), F32),
                        pltpu.SemaphoreType.DMA((2, 2))],
    )
    return pl.pallas_call(
        kern,
        out_shape=jax.ShapeDtypeStruct((ns, MLA_HEADS, KV_LORA), F32),
        grid_spec=grid_spec,
        compiler_params=pltpu.CompilerParams(dimension_semantics=("arbitrary",), vmem_limit_bytes=VMEM_LIMIT),
        name="mla_sample",
    )(page_table.reshape(-1), ql, qr, c_new, kr_new, cache_c, cache_r)


def _o_proj_sample_kernel(ol_ref, wv_ref, o_ref):
    for p in range(MLA_HEADS // 2):
        acc = None
        for h in (2 * p, 2 * p + 1):
            part = _dot(ol_ref[h].astype(BF16), wv_ref[:, h * LANES:(h + 1) * LANES])
            acc = part if acc is None else acc + part
        o_ref[:, p * LANES:(p + 1) * LANES] = acc.astype(o_ref.dtype)


def _o_proj_sample(o_lat_t, wv):
    ns = o_lat_t.shape[1]
    return pl.pallas_call(
        _o_proj_sample_kernel,
        out_shape=jax.ShapeDtypeStruct((ns, D_MODEL), BF16),
        grid=(1,),
        in_specs=[pl.BlockSpec((MLA_HEADS, ns, KV_LORA), lambda i: (0, 0, 0)),
                  pl.BlockSpec((KV_LORA, MLA_HEADS * LANES), lambda i: (0, 0))],
        out_specs=pl.BlockSpec((ns, D_MODEL), lambda i: (0, 0)),
        compiler_params=pltpu.CompilerParams(vmem_limit_bytes=VMEM_LIMIT),
        name="o_proj_sample",
    )(o_lat_t, wv)


def _outproj_kernel(x_ref, om_ref, og_ref, z_ref, gm_ref, gg_ref, wo_ref, gn_ref, nf_ref, wr_ref, br_ref,
                    xmid_out, hf_out, route_out):
    og = og_ref[...].astype(F32)
    parts = []
    for h in range(GDN_HEADS):
        oh = og[:, h * GDN_DV:(h + 1) * GDN_DV]
        parts.append(oh * lax.rsqrt(jnp.mean(oh * oh, axis=-1, keepdims=True) + RMS_EPS))
    o_gdn = jnp.concatenate(parts, axis=1) * gn_ref[...] * _silu(z_ref[...].astype(F32))
    merged = (_sigmoid(gm_ref[...].astype(F32)) * om_ref[...].astype(F32)
              + _sigmoid(gg_ref[...].astype(F32)) * o_gdn)
    x_mid = x_ref[...] + _dot(merged.astype(BF16), wo_ref[...])
    xmid_out[...] = x_mid
    hf = _rms(x_mid, nf_ref[...]).astype(BF16)
    bits = lax.bitcast_convert_type(hf.astype(F32), jnp.uint32)
    half = D_MODEL // 2
    hf_out[...] = bits[:, half:] | (bits[:, :half] >> 16)

    logits = _dot(hf, wr_ref[...]) + br_ref[...]
    lane = lax.broadcasted_iota(jnp.int32, logits.shape, 1)
    neg = -jnp.inf
    big = 4 * LANES
    is_g = lane < N_GROUPS
    lg = jnp.where(is_g, logits, neg)
    mg = jnp.max(lg, axis=1, keepdims=True)
    grp = jnp.min(jnp.where(lg == mg, lane, big), axis=1, keepdims=True)
    gate_g = 1.0 / jnp.sum(jnp.where(is_g, jnp.exp(logits - mg), 0.0), axis=1, keepdims=True)
    e_lane = lane - N_GROUPS
    in_grp = (e_lane >= 0) & (e_lane < N_EXPERTS) & ((e_lane >> 3) == grp)
    le = jnp.where(in_grp, logits, neg)
    v1 = jnp.max(le, axis=1, keepdims=True)
    i1 = jnp.min(jnp.where(le == v1, lane, big), axis=1, keepdims=True)
    le2 = jnp.where(lane == i1, neg, le)
    v2 = jnp.max(le2, axis=1, keepdims=True)
    i2 = jnp.min(jnp.where(le2 == v2, lane, big), axis=1, keepdims=True)
    e = jnp.exp(v2 - v1)
    w1 = gate_g / (1.0 + e)
    w2 = gate_g * e / (1.0 + e)
    route = jnp.where(lane == 0, (i1 - N_GROUPS).astype(F32),
                      jnp.where(lane == 1, (i2 - N_GROUPS).astype(F32),
                                jnp.where(lane == 2, w1, jnp.where(lane == 3, w2, 0.0))))
    route_out[...] = route


def _row_tile(n, limit):
    t = limit - limit % 16
    while t >= 16:
        if n % t == 0:
            return t
        t -= 16
    raise ValueError(f"no row tile for {n}")


def _outproj(x_all, o_mla, o_gdn, gates, w_out, gn_t, norm_ffn, w_r, b_r, row0):
    n = o_mla.shape[0]
    tm = _pick(n, (1024, 512, 256, 128, 64, 32, 16))
    assert row0 % tm == 0
    r0 = row0 // tm
    full = lambda shape: pl.BlockSpec(shape, lambda i: (0,) * len(shape))
    row = lambda w: pl.BlockSpec((tm, w), lambda i: (i, 0))
    shared = lambda w, j=0: pl.BlockSpec((tm, w), lambda i, j=j: (r0 + i, j))
    return pl.pallas_call(
        _outproj_kernel,
        out_shape=(jax.ShapeDtypeStruct((n, D_MODEL), F32),
                   jax.ShapeDtypeStruct((n, D_MODEL // 2), jnp.uint32),
                   jax.ShapeDtypeStruct((n, LANES), F32)),
        grid=(n // tm,),
        in_specs=[shared(D_MODEL), row(D_MODEL), row(D_MODEL),
                  shared(D_MODEL, P_Z // D_MODEL), shared(D_MODEL, P_GM // D_MODEL), shared(D_MODEL, P_GG // D_MODEL),
                  full((D_MODEL, D_MODEL)), full((1, D_MODEL)), full((1, D_MODEL)),
                  full((D_MODEL, LANES)), full((1, LANES))],
        out_specs=(row(D_MODEL), row(D_MODEL // 2), row(LANES)),
        compiler_params=pltpu.CompilerParams(dimension_semantics=("parallel",), vmem_limit_bytes=VMEM_LIMIT),
        name="outproj_route",
    )(x_all, o_mla, o_gdn, gates, gates, gates, w_out, gn_t, norm_ffn, w_r, b_r)


def _moe_plan(eid, wgt, nsp):
    nt = eid.shape[0]
    ts = nt // nsp
    n_asg = ts * TOP_K
    e = eid.reshape(nsp, n_asg)
    ids = jnp.broadcast_to(jnp.arange(n_asg, dtype=jnp.int32), e.shape)
    _, ids_s, w_s = lax.sort((e, ids, wgt.reshape(nsp, n_asg)), dimension=1, num_keys=1, is_stable=True)
    rows_s = (ids_s % TOP_K) * (ts + 8) + ids_s // TOP_K
    counts = jnp.sum((e[..., None] == jnp.arange(N_EXPERTS, dtype=jnp.int32)).astype(jnp.int32), axis=1)
    run_start = jnp.cumsum(counts, axis=1) - counts
    tail = lambda v, dt: jnp.full((nsp, MOE_BLOCK), v, dt)
    tok_t = jnp.concatenate([ids_s // TOP_K, tail(ts, jnp.int32)], axis=1)
    rows_t = jnp.concatenate([rows_s, tail(ts, jnp.int32)], axis=1)
    w_t = jnp.concatenate([w_s, tail(0.0, F32)], axis=1)
    as_i32 = lambda a: a.astype(jnp.int32).reshape(-1)
    scalars = (as_i32(run_start), as_i32(counts), as_i32(tok_t), as_i32(rows_t), w_t.reshape(-1))
    return scalars, ts


def _split_pieces(k, ts, n_prompt, n_sample):
    lo, hi = k * ts, (k + 1) * ts
    pieces = []
    if lo < n_prompt:
        pieces.append((0, lo, 0, min(hi, n_prompt) - lo))
    if hi > n_prompt:
        start = max(lo, n_prompt)
        pieces.append((1, start - n_prompt, start - lo, hi - start))
    assert hi <= n_prompt + n_sample
    return pieces


def _moe_kernel(start_ref, cnt_ref, tok_ref, rows_ref, w_ref, hf_p, hf_q, xmid_p, xmid_q,
                wg_ref, wu_ref, wd_ref, nfin_ref, y_p, y_q, hf_s, comb, acc, xb, yb, sem,
                *, ts, rc, nsp, n_prompt, n_sample):
    s = pl.program_id(0)
    j = pl.program_id(1)
    half = D_MODEL // 2
    stride = ts + 8

    def zero_spare_rows():
        hf_s[ts:ts + 8, :] = jnp.zeros((8, half), jnp.uint32)

    def split_copies(k, what):
        cps = []
        for src, r0, l0, n in _split_pieces(k, ts, n_prompt, n_sample):
            hbm_rows, vmem_rows = pl.ds(r0, n), pl.ds(l0, n)
            if what == 0:
                cps.append(pltpu.make_async_copy((hf_p, hf_q)[src].at[hbm_rows], hf_s.at[vmem_rows], sem.at[0, src]))
            elif what == 1:
                cps.append(pltpu.make_async_copy((xmid_p, xmid_q)[src].at[hbm_rows], acc.at[vmem_rows],
                                                 sem.at[1, src]))
            else:
                cps.append(pltpu.make_async_copy(acc.at[vmem_rows], (y_p, y_q)[src].at[hbm_rows], sem.at[2, src]))
        return cps

    def for_split(fn):
        for k in range(nsp):
            @pl.when(s == k)
            def _(k=k):
                fn(k)

    def begin_split(k):
        loads = split_copies(k, 0) + split_copies(k, 1)
        for cp in loads:
            cp.start()
        zero_spare_rows()
        for cp in split_copies(k, 0):
            cp.wait()

    def finish_split(k):
        for cp in split_copies(k, 2):
            cp.start()
        for cp in split_copies(k, 2):
            cp.wait()

    @pl.when(j == 0)
    def _():
        for_split(begin_split)

    cnt = cnt_ref[s * N_EXPERTS + j]
    first = s * (ts * TOP_K + MOE_BLOCK) + start_ref[s * N_EXPERTS + j]

    def block(p0, n):
        for r in range(n):
            xb[r:r + 1, :] = hf_s[pl.ds(tok_ref[p0 + r], 1), :]
        bits = xb[0:n, :]
        lo = lax.bitcast_convert_type(bits << 16, F32).astype(BF16)
        hi = lax.bitcast_convert_type(bits & jnp.uint32(0xFFFF0000), F32).astype(BF16)
        g = _dot(lo, wg_ref[0, :half, :]) + _dot(hi, wg_ref[0, half:, :])
        u = _dot(lo, wu_ref[0, :half, :]) + _dot(hi, wu_ref[0, half:, :])
        yb[0:n, :] = _dot((_silu(g) * u).astype(BF16), wd_ref[0])
        for r in range(n):
            comb[pl.ds(rows_ref[p0 + r], 1), :] = w_ref[p0 + r] * yb[r:r + 1, :]

    def full_block(i, carry):
        block(first + i * MOE_BLOCK, MOE_BLOCK)
        return carry

    n_full = cnt // MOE_BLOCK
    rest = cnt - n_full * MOE_BLOCK
    lax.fori_loop(0, n_full, full_block, 0)

    @pl.when(rest > MOE_BLOCK // 2)
    def _():
        block(first + n_full * MOE_BLOCK, MOE_BLOCK)

    @pl.when((rest > 0) & (rest <= MOE_BLOCK // 2))
    def _():
        block(first + n_full * MOE_BLOCK, MOE_BLOCK // 2)

    @pl.when(j == N_EXPERTS - 1)
    def _():
        for_split(lambda k: [cp.wait() for cp in split_copies(k, 1)])

        def body(i, carry):
            r0 = pl.multiple_of(i * rc, 8)
            rows = pl.ds(r0, rc)
            moe = comb[rows, :]
            for k in range(1, TOP_K):
                moe = moe + comb[pl.ds(k * stride + r0, rc), :]
            acc[rows, :] = _rms(acc[rows, :] + moe, nfin_ref[...])
            return carry

        lax.fori_loop(0, ts // rc, body, 0)
        for_split(finish_split)


def _moe(plan, hf_p, hf_q, xmid_p, xmid_q, wg, wu, wd, norm_final):
    scalars, ts = plan
    n_prompt, n_sample = xmid_p.shape[0], xmid_q.shape[0]
    nsp = (n_prompt + n_sample) // ts
    rc = _row_tile(ts, 256) if ts % 16 == 0 else 8
    kern = functools.partial(_moe_kernel, ts=ts, rc=rc, nsp=nsp, n_prompt=n_prompt, n_sample=n_sample)
    hbm = pl.BlockSpec(memory_space=pl.ANY)
    expert = lambda shape: pl.BlockSpec((1,) + shape, lambda s, j, *_: (j, 0, 0))
    grid_spec = pltpu.PrefetchScalarGridSpec(
        num_scalar_prefetch=len(scalars),
        grid=(nsp, N_EXPERTS),
        in_specs=[
            hbm, hbm, hbm, hbm,
            expert((D_MODEL, D_EXPERT)), expert((D_MODEL, D_EXPERT)), expert((D_EXPERT, D_MODEL)),
            pl.BlockSpec((1, D_MODEL), lambda s, j, *_: (0, 0)),
        ],
        out_specs=(hbm, hbm),
        scratch_shapes=[
            pltpu.VMEM((ts + 8, D_MODEL // 2), jnp.uint32),
            pltpu.VMEM((TOP_K * (ts + 8), D_MODEL), F32),
            pltpu.VMEM((ts, D_MODEL), F32),
            pltpu.VMEM((MOE_BLOCK, D_MODEL // 2), jnp.uint32),
            pltpu.VMEM((MOE_BLOCK, D_MODEL), F32),
            pltpu.SemaphoreType.DMA((3, 2)),
        ],
    )
    return pl.pallas_call(
        kern,
        out_shape=(jax.ShapeDtypeStruct((n_prompt, D_MODEL), F32), jax.ShapeDtypeStruct((n_sample, D_MODEL), F32)),
        grid_spec=grid_spec,
        compiler_params=pltpu.CompilerParams(
            dimension_semantics=("arbitrary", "arbitrary"), vmem_limit_bytes=VMEM_LIMIT),
        name="moe",
    )(*scalars, hf_p, hf_q, xmid_p, xmid_q, wg, wu, wd, norm_final)


def _pack_w_in(w):
    kr = w[:, _OFF_KV + KV_LORA:_OFF_QKV]
    kr_sw = jnp.concatenate([kr[:, ROPE_DIM // 2:], kr[:, :ROPE_DIM // 2]], axis=1)
    small = jnp.concatenate([w[:, _OFF_B:_OFF_A], w[:, _OFF_A:_OFF_GM],
                             jnp.zeros((D_MODEL, SM_KR - 2 * GDN_HEADS), w.dtype), kr, kr_sw], axis=1)
    packed = jnp.concatenate([w[:, _OFF_QKV:_OFF_Z], w[:, _OFF_KV:_OFF_KV + KV_LORA], small, w[:, :Q_LORA],
                              w[:, _OFF_Z:_OFF_B], w[:, _OFF_GM:_OFF_GG], w[:, _OFF_GG:]], axis=1)
    return packed.astype(BF16)


def _pack_mla_weights(w_uq, w_uk, w_uv):
    zq = jnp.zeros((Q_LORA, MLA_HEADS, LANES - NOPE_DIM - ROPE_DIM), w_uq.dtype)
    wq = jnp.concatenate([w_uq, zq], axis=2).reshape(Q_LORA, MLA_HEADS * LANES)
    rope = w_uq[:, :, NOPE_DIM:]
    rope_sw = jnp.concatenate([rope[..., ROPE_DIM // 2:], rope[..., :ROPE_DIM // 2]], axis=2)
    wqs = jnp.concatenate([jnp.zeros((Q_LORA, MLA_HEADS, NOPE_DIM), w_uq.dtype), rope_sw, zq], axis=2)
    wqs = wqs.reshape(Q_LORA, MLA_HEADS * LANES)
    wk = jnp.concatenate([w_uk, jnp.zeros((KV_LORA, MLA_HEADS, LANES - NOPE_DIM), w_uk.dtype)], axis=2)
    wk = wk.reshape(KV_LORA, MLA_HEADS * LANES)
    zv = jnp.zeros((KV_LORA, MLA_HEADS // 2, V_DIM), w_uv.dtype)
    wv = jnp.stack([jnp.concatenate([w_uv[:, 0::2], zv], axis=2),
                    jnp.concatenate([zv, w_uv[:, 1::2]], axis=2)], axis=2)
    wv = wv.reshape(KV_LORA, MLA_HEADS * LANES)
    return wq.astype(BF16), wqs.astype(BF16), wk.astype(BF16), wv.astype(BF16)


def _rope_tables(pos):
    inv_freq = ROPE_THETA ** (-jnp.arange(0, ROPE_DIM, 2, dtype=F32) / ROPE_DIM)
    ang = pos.astype(F32)[:, None] * inv_freq[None, :]
    cos, sin = jnp.cos(ang), jnp.sin(ang)
    n = pos.shape[0]
    cos_t = jnp.concatenate([jnp.ones((n, NOPE_DIM), F32), cos, cos, jnp.zeros((n, ROPE_DIM), F32)], axis=1)
    sin_t = jnp.concatenate([jnp.zeros((n, NOPE_DIM), F32), -sin, sin, jnp.zeros((n, ROPE_DIM), F32)], axis=1)
    return cos_t, sin_t


def _head_lanes(v):
    return jnp.zeros((1, LANES), F32).at[0, SM_A:SM_A + GDN_HEADS].set(v.astype(F32))


def _moe_splits(nt):
    for nsp in (6, 3, 4, 2, 1):
        if nt % (nsp * 8) == 0:
            return nsp
    return 1


def kernel(x_prompt, x_sample, cache_kv_latent, cache_k_rope, page_table, state_conv, state_gdn, meta_tokens,
           norm_mix, w_in, q_norm, w_uq, kv_norm, w_uk, w_uv, conv_w, a_log, dt_bias, gdn_norm, w_out, norm_ffn,
           w_group, b_group, w_router, b_router, w_gate, w_up, w_down, norm_final):
    batch, seq, _ = x_prompt.shape
    ns, dec_seq, _ = x_sample.shape
    assert dec_seq == 1 and w_in.shape[0] == 1 and seq % CHUNK == 0
    n_pages = page_table.shape[1]
    n_prompt = batch * seq
    nt = n_prompt + ns
    meta_row = nt
    tq = _pick(seq, ATTN_TQ)
    n_rows = -(-(nt + N_META) // tq) * tq
    assert n_prompt % ns == 0 and nt % N_META == 0 and ns % LANES == 0 and meta_row % tq + N_META <= tq

    x_all = jnp.concatenate([x_prompt.reshape(n_prompt, D_MODEL), x_sample.reshape(ns, D_MODEL),
                             meta_tokens.astype(x_prompt.dtype),
                             jnp.zeros((n_rows - nt - N_META, D_MODEL), x_prompt.dtype)], axis=0)
    pos = jnp.concatenate([N_META + jnp.arange(seq), jnp.full((1,), n_pages * PAGE_SIZE), jnp.arange(N_META)])
    by_row = lambda t: jnp.concatenate([jnp.tile(t[:seq], (batch, 1)), jnp.broadcast_to(t[seq], (ns, LANES)),
                                        t[seq + 1:], jnp.zeros((n_rows - nt - N_META, LANES), F32)], axis=0)
    cos_t, sin_t = (by_row(t) for t in _rope_tables(pos))
    w_packed = _pack_w_in(w_in[0])
    wq, wqs, wk, wv = _pack_mla_weights(w_uq[0], w_uk[0], w_uv[0])
    wukt = jnp.concatenate([jnp.transpose(w_uk[0], (1, 2, 0)),
                            jnp.zeros((MLA_HEADS, LANES - NOPE_DIM, KV_LORA), w_uk.dtype)], axis=1).astype(BF16)
    alog_v, dtb_v = _head_lanes(a_log[0]), _head_lanes(dt_bias[0])
    cw = conv_w[0].astype(F32)
    gn_t = jnp.tile(gdn_norm[0].astype(F32), GDN_HEADS)[None]
    w_r = jnp.concatenate([w_group[0], w_router[0],
                           jnp.zeros((D_MODEL, LANES - N_GROUPS - N_EXPERTS), w_group.dtype)], axis=1).astype(BF16)
    b_r = jnp.concatenate([b_group[0], b_router[0], jnp.zeros((LANES - N_GROUPS - N_EXPERTS,), b_group.dtype)])[None]
    wg, wu, wd = w_gate[0].astype(BF16), w_up[0].astype(BF16), w_down[0].astype(BF16)

    proj, gates = _inproj(x_all, norm_mix[0][None].astype(F32), w_packed)
    qt, k, vt, ckv, krot = _mla_prep(proj, cos_t, sin_t, q_norm[0][None].astype(F32), kv_norm[0][None].astype(F32),
                                     wq.T, wqs.T, wk, wv.T, tq)

    o_mla_p = _attn_prompt(qt, k, vt, batch, seq, meta_row, tq)
    ql, qrt = _q_absorb(qt, wukt, ns, n_prompt)
    o_lat = _mla_sample(page_table, jnp.transpose(ql, (1, 0, 2)), jnp.transpose(qrt, (2, 0, 1)),
                        ckv[n_prompt:nt].reshape(ns, 1, KV_LORA), krot[n_prompt:nt].reshape(ns, 1, LANES),
                        cache_kv_latent[0], jnp.swapaxes(cache_k_rope[0], 1, 2))
    o_mla_s = _o_proj_sample(jnp.transpose(o_lat, (1, 0, 2)), wv)

    s_meta = _gdn_meta(proj, cw, alog_v, dtb_v, meta_row)
    terms = _gdn_terms(proj, cw, alog_v, dtb_v, batch, seq, meta_row)
    o_gdn_p, gdn_p = _gdn_scan(*terms, s_meta, batch, seq)
    o_gdn_s, gdn_s = _gdn_sample(proj, state_conv[0], state_gdn[0], cw, alog_v, dtb_v, n_prompt)

    tail = (gates, w_out[0].astype(BF16), gn_t, norm_ffn[0][None].astype(F32), w_r, b_r.astype(F32))
    xmid_p, hf_p, route_p = _outproj(x_all, o_mla_p, o_gdn_p, *tail, 0)
    xmid_s, hf_s, route_s = _outproj(x_all, o_mla_s, o_gdn_s, *tail, n_prompt)
    route = jnp.concatenate([route_p[:, :2 * TOP_K], route_s[:, :2 * TOP_K]], axis=0)
    plan = _moe_plan(route[:, :TOP_K].astype(jnp.int32), route[:, TOP_K:], _moe_splits(nt))
    y_p, y_s = _moe(plan, hf_p, hf_s, xmid_p, xmid_s, wg, wu, wd, norm_final[None].astype(F32))

    def with_meta(rows, width):
        meta = jnp.broadcast_to(rows[meta_row:meta_row + N_META][None], (batch, N_META, width))
        return jnp.concatenate([meta, rows[:n_prompt].reshape(batch, seq, width)], axis=1)[None]

    k_rope = krot[:, SM_KR:SM_KR + ROPE_DIM]
    conv_p = jnp.stack([proj[(b + 1) * seq - (CONV_W - 1):(b + 1) * seq, :GDN_QKV] for b in range(batch)])
    conv_s = jnp.concatenate([state_conv[0][:, 1:].astype(F32), proj[n_prompt:nt, None, :GDN_QKV]], axis=1)
    return (y_p.reshape(batch, seq, D_MODEL), y_s.reshape(ns, 1, D_MODEL),
            with_meta(ckv, KV_LORA), with_meta(k_rope, ROPE_DIM),
            ckv[n_prompt:nt].reshape(1, ns, 1, KV_LORA), k_rope[n_prompt:nt].reshape(1, ns, 1, ROPE_DIM),
            conv_p[None], conv_s[None], gdn_p[None], gdn_s[None])
```

```python
import functools

import jax
import jax.numpy as jnp
from jax import lax
from jax.experimental import pallas as pl
from jax.experimental.pallas import tpu as pltpu

F32 = jnp.float32
BF16 = jnp.bfloat16
HIGHEST = lax.Precision.HIGHEST

D_MODEL = 1024
N_META = 16
RMS_EPS = 1e-6
MLA_HEADS = 16
Q_LORA = 384
KV_LORA = 256
NOPE_DIM = 64
ROPE_DIM = 32
V_DIM = 64
ROPE_THETA = 10000.0
MLA_SCALE = (NOPE_DIM + ROPE_DIM) ** -0.5
PAGE_SIZE = 128
GDN_HEADS = 8
GDN_DK = 128
GDN_DV = 128
GDN_KEY = GDN_HEADS * GDN_DK
GDN_QKV = 3 * GDN_KEY
CONV_W = 4
CHUNK = 64
N_GROUPS = 4
EXPERTS_PER_GROUP = 8
N_EXPERTS = 32
TOP_K = 2
D_EXPERT = 256
MOE_BLOCK = 128

_OFF_KV = Q_LORA
_OFF_QKV = _OFF_KV + KV_LORA + ROPE_DIM
_OFF_Z = _OFF_QKV + GDN_QKV
_OFF_B = _OFF_Z + GDN_KEY
_OFF_A = _OFF_B + GDN_HEADS
_OFF_GM = _OFF_A + GDN_HEADS
_OFF_GG = _OFF_GM + D_MODEL
P_QKV = 0
P_KVC = 3072
P_SMALL = 3328
P_QD = 3456
P_F32 = 3840
P_Z = 0
P_GM = 1024
P_GG = 2048
P_BF16 = 3072
P_TOTAL = P_F32 + P_BF16
INPROJ_TN = 768
SM_B = 0
SM_A = 8
SM_KR = 64

LANES = 128
VMEM_LIMIT = 56 * 1024 * 1024
ATTN_TQ = (512, 256, 128)


def _pick(n, candidates):
    for c in candidates:
        if n % c == 0:
            return c
    raise ValueError(f"no tile for {n} in {candidates}")


def _dot(a, b):
    return jnp.dot(a, b, preferred_element_type=F32)


def _dot_nt(a, b):
    return lax.dot_general(a, b, (((1,), (1,)), ((), ())), preferred_element_type=F32)


def _dot_tn(a, b):
    return lax.dot_general(a, b, (((0,), (0,)), ((), ())), preferred_element_type=F32)


def _sigmoid(x):
    return 1.0 / (1.0 + jnp.exp(-x))


def _silu(x):
    return x * _sigmoid(x)


def _softplus(x):
    return jnp.maximum(x, 0.0) + jnp.log1p(jnp.exp(-jnp.abs(x)))


def _rms(x, w):
    return x * lax.rsqrt(jnp.mean(x * x, axis=-1, keepdims=True) + RMS_EPS) * w


def _inproj_kernel(x_ref, nw_ref, w_ref, of_ref, ob_ref, hn_ref):
    j = pl.program_id(1)
    nf = P_F32 // INPROJ_TN

    @pl.when(j == 0)
    def _():
        hn_ref[...] = _rms(x_ref[...], nw_ref[...]).astype(BF16)

    r = _dot(hn_ref[...], w_ref[...])

    @pl.when(j < nf)
    def _():
        of_ref[...] = r

    @pl.when(j >= nf)
    def _():
        ob_ref[...] = r.astype(BF16)


def _inproj(x_all, norm_w, w_packed):
    r = x_all.shape[0]
    tm = _pick(r, (1536, 1280, 768, 640, 512, 256, 128))
    tn = INPROJ_TN
    nf = P_F32 // tn
    return pl.pallas_call(
        _inproj_kernel,
        out_shape=(jax.ShapeDtypeStruct((r, P_F32), F32), jax.ShapeDtypeStruct((r, P_BF16), BF16)),
        grid=(r // tm, P_TOTAL // tn),
        in_specs=[
            pl.BlockSpec((tm, D_MODEL), lambda i, j: (i, 0)),
            pl.BlockSpec((1, D_MODEL), lambda i, j: (0, 0)),
            pl.BlockSpec((D_MODEL, tn), lambda i, j: (0, j)),
        ],
        out_specs=(pl.BlockSpec((tm, tn), lambda i, j: (i, jnp.minimum(j, nf - 1))),
                   pl.BlockSpec((tm, tn), lambda i, j: (i, jnp.maximum(j - nf, 0)))),
        scratch_shapes=[pltpu.VMEM((tm, D_MODEL), BF16)],
        compiler_params=pltpu.CompilerParams(
            dimension_semantics=("parallel", "arbitrary"), vmem_limit_bytes=VMEM_LIMIT),
        name="inproj",
    )(x_all, norm_w, w_packed)


LOG2E = 1.4426950408889634


def _mla_prep_kernel(qd_ref, kvc_ref, sm_ref, c_ref, s_ref, qn_ref, kvn_ref, wqt_ref, wqst_ref, wk_ref, wvt_ref,
                     qt_out, k_out, vt_out, ckv_out, kr_out):
    cos = c_ref[...]
    sin = s_ref[...]
    cos_t, sin_t = cos.T, sin.T
    qn_t = _rms(qd_ref[...], qn_ref[...]).T.astype(BF16)
    qt = _dot(wqt_ref[...], qn_t)
    qst = _dot(wqst_ref[...], qn_t)
    for h in range(MLA_HEADS):
        sl = slice(h * LANES, (h + 1) * LANES)
        qt_out[h, 0] = ((qt[sl] * cos_t + qst[sl] * sin_t) * (MLA_SCALE * LOG2E)).astype(BF16)
    ckv = _rms(kvc_ref[...], kvn_ref[...])
    ckv_out[...] = ckv
    sm = sm_ref[...]
    lane = lax.broadcasted_iota(jnp.int32, sm.shape, 1)
    cos_k = jnp.where((lane >= SM_KR) & (lane < SM_KR + ROPE_DIM), cos, 0.0)
    krot = sm * cos_k + pltpu.roll(sm, LANES - ROPE_DIM, 1) * sin
    kr_out[...] = krot
    kk = _dot(ckv.astype(BF16), wk_ref[...])
    vvt = _dot(wvt_ref[...], ckv.T.astype(BF16))
    for h in range(MLA_HEADS):
        sl = slice(h * LANES, (h + 1) * LANES)
        k_out[h] = (kk[:, sl] + krot).astype(BF16)
        vt_out[h, 0] = vvt[sl].astype(BF16)


def _mla_prep(proj, cos_t, sin_t, q_norm, kv_norm, wqt, wqst, wk, wvt, tm):
    r = proj.shape[0]
    hw = MLA_HEADS * LANES
    full = lambda shape: pl.BlockSpec(shape, lambda i: (0,) * len(shape))
    return pl.pallas_call(
        _mla_prep_kernel,
        out_shape=(
            jax.ShapeDtypeStruct((MLA_HEADS, r // tm, LANES, tm), BF16),
            jax.ShapeDtypeStruct((MLA_HEADS, r, LANES), BF16),
            jax.ShapeDtypeStruct((MLA_HEADS, r // tm, LANES, tm), BF16),
            jax.ShapeDtypeStruct((r, KV_LORA), F32),
            jax.ShapeDtypeStruct((r, LANES), F32),
        ),
        grid=(r // tm,),
        in_specs=[
            pl.BlockSpec((tm, Q_LORA), lambda i: (i, P_QD // Q_LORA)),
            pl.BlockSpec((tm, KV_LORA), lambda i: (i, P_KVC // KV_LORA)),
            pl.BlockSpec((tm, LANES), lambda i: (i, P_SMALL // LANES)),
            pl.BlockSpec((tm, LANES), lambda i: (i, 0)),
            pl.BlockSpec((tm, LANES), lambda i: (i, 0)),
            full((1, Q_LORA)), full((1, KV_LORA)),
            full((hw, Q_LORA)), full((hw, Q_LORA)), full((KV_LORA, hw)), full((hw, KV_LORA)),
        ],
        out_specs=(pl.BlockSpec((MLA_HEADS, 1, LANES, tm), lambda i: (0, i, 0, 0)),
                   pl.BlockSpec((MLA_HEADS, tm, LANES), lambda i: (0, i, 0)),
                   pl.BlockSpec((MLA_HEADS, 1, LANES, tm), lambda i: (0, i, 0, 0)),
                   pl.BlockSpec((tm, KV_LORA), lambda i: (i, 0)),
                   pl.BlockSpec((tm, LANES), lambda i: (i, 0))),
        compiler_params=pltpu.CompilerParams(dimension_semantics=("parallel",), vmem_limit_bytes=VMEM_LIMIT),
        name="mla_prep",
    )(proj, proj, proj, cos_t, sin_t, q_norm, kv_norm, wqt, wqst, wk, wvt)


def _attn_prompt_kernel(qt_ref, k_ref, vt_ref, km_ref, vmt_ref, o_ref, *, tq, meta_lane, nq, tiles_per_step):
    i = pl.program_id(2)
    tiles = (i,) if tiles_per_step == 1 else (i, nq - 1 - i)
    streams = [(t, h) for t in tiles for h in (0, 1)]
    half = tq // 2
    qt = [qt_ref[h, t] for t, h in streams]

    def update(qts, k_blk, vt_blk, carry, mask):
        ms, ls, accs = carry
        n = range(len(qts))
        s = [_dot(k_blk[x], qts[x]) for x in n]
        if mask is not None:
            s = [jnp.where(mask, v, -1e30) for v in s]
        m_new = [jnp.maximum(ms[x], jnp.max(s[x], axis=0, keepdims=True)) for x in n]
        a = [jnp.exp2(ms[x] - m_new[x]) for x in n]
        p = [jnp.exp2(s[x] - m_new[x]) for x in n]
        l_new = [a[x] * ls[x] + jnp.sum(p[x], axis=0, keepdims=True) for x in n]
        acc_new = [a[x] * accs[x] + _dot(vt_blk[x], p[x].astype(BF16)) for x in n]
        return m_new, l_new, acc_new

    s0 = [_dot(km_ref[h], q) for (_, h), q in zip(streams, qt)]
    m = [jnp.max(v, axis=0, keepdims=True) for v in s0]
    p0 = [jnp.exp2(v - mx) for v, mx in zip(s0, m)]
    l = [jnp.sum(v, axis=0, keepdims=True) for v in p0]
    acc = [_dot(vmt_ref[h, 0][:, meta_lane:meta_lane + N_META], v.astype(BF16)) for (_, h), v in zip(streams, p0)]

    for ti, t in enumerate(tiles):
        own = slice(2 * ti, 2 * ti + 2)

        def body(j, carry, own=own):
            rows = pl.ds(pl.multiple_of(j * tq, tq), tq)
            return update(qt[own], [k_ref[h, rows, :] for h in (0, 1)], [vt_ref[h, j] for h in (0, 1)], carry, None)

        m[own], l[own], acc[own] = lax.fori_loop(0, t, body, (m[own], l[own], acc[own]))

    key = lax.broadcasted_iota(jnp.int32, (half, tq), 0)
    qry = lax.broadcasted_iota(jnp.int32, (half, tq), 1)
    off = [pl.multiple_of(t * tq, tq) for t, _ in streams]
    vt_d = [vt_ref[h, t] for t, h in streams]
    m, l, acc = update(qt, [k_ref[h, pl.ds(o, half), :] for (_, h), o in zip(streams, off)],
                       [v[:, :half] for v in vt_d], (m, l, acc), key <= qry)
    late = lambda xs: [v[:, half:] for v in xs]
    mb, lb, accb = update(late(qt), [k_ref[h, pl.ds(pl.multiple_of(o + half, half), half), :]
                                     for (_, h), o in zip(streams, off)],
                          late(vt_d), (late(m), late(l), late(acc)), (key <= qry)[:, :half])
    ot = [jnp.concatenate([acc[x][:, :half] / l[x][:, :half], accb[x] / lb[x]], axis=1) for x in range(len(streams))]
    for ti, t in enumerate(tiles):
        o_ref[pl.ds(pl.multiple_of(t * tq, tq), tq), :] = (ot[2 * ti] + ot[2 * ti + 1]).T.astype(o_ref.dtype)


def _attn_prompt(qt, k, vt, batch, seq, meta_row, tq):
    nq = seq // tq
    tps = 2 if nq % 2 == 0 else 1
    kern = functools.partial(_attn_prompt_kernel, tq=tq, meta_lane=meta_row % tq, nq=nq, tiles_per_step=tps)
    tiled = lambda: pl.BlockSpec((2, nq, LANES, tq), lambda b, p, i: (p, b, 0, 0))
    return pl.pallas_call(
        kern,
        out_shape=jax.ShapeDtypeStruct((batch * seq, D_MODEL), BF16),
        grid=(batch, MLA_HEADS // 2, nq // tps),
        in_specs=[
            tiled(),
            pl.BlockSpec((2, seq, LANES), lambda b, p, i: (p, b, 0)),
            tiled(),
            pl.BlockSpec((2, N_META, LANES), lambda b, p, i: (p, meta_row // N_META, 0)),
            pl.BlockSpec((2, 1, LANES, tq), lambda b, p, i: (p, meta_row // tq, 0, 0)),
        ],
        out_specs=pl.BlockSpec((seq, LANES), lambda b, p, i: (b, p)),
        compiler_params=pltpu.CompilerParams(
            dimension_semantics=("parallel", "parallel", "arbitrary"), vmem_limit_bytes=VMEM_LIMIT),
        name="attn_prompt",
    )(qt, k, vt, k, vt)


def _gate_lanes(sm, alog_ref, dtb_ref):
    g = -jnp.exp(alog_ref[...]) * _softplus(sm + dtb_ref[...])
    beta = _sigmoid(sm)
    return g, beta


def _qkv_heads(xc):
    xf = _silu(xc)
    qs, ks, vs = [], [], []
    for h in range(GDN_HEADS):
        q = xf[:, h * GDN_DK:(h + 1) * GDN_DK]
        k = xf[:, GDN_KEY + h * GDN_DK:GDN_KEY + (h + 1) * GDN_DK]
        qs.append(q * lax.rsqrt(jnp.sum(q * q, axis=-1, keepdims=True) + RMS_EPS) * (GDN_DK ** -0.5))
        ks.append(k * lax.rsqrt(jnp.sum(k * k, axis=-1, keepdims=True) + RMS_EPS))
        vs.append(xf[:, 2 * GDN_KEY + h * GDN_DV:2 * GDN_KEY + (h + 1) * GDN_DV])
    return qs, ks, vs


def _split_bf16(x):
    hi = x.astype(BF16)
    return hi, (x - hi.astype(F32)).astype(BF16)


def _dot_split(a, b):
    return _dot(a[0], b[0]) + (_dot(a[0], b[1]) + _dot(a[1], b[0]))


def _unit_lower_inverses(mats, c):
    row = lax.broadcasted_iota(jnp.int32, (c, c), 0)
    col = lax.broadcasted_iota(jnp.int32, (c, c), 1)
    eye = jnp.where(row == col, 1.0, 0.0)
    ps = [-a for a in mats]
    ts = [eye + p for p in ps]
    span = 2
    while span < c:
        psp = [_split_bf16(p) for p in ps]
        ps = [_dot_split(p, p) for p in psp]
        psp = [_split_bf16(p) for p in ps]
        ts = [t + _dot_split(p, _split_bf16(t)) for p, t in zip(psp, ts)]
        span *= 2
    return ts


def _gdn_chunk_terms(xs, sm, conv_ref, alog_ref, dtb_ref, c):
    heads = range(GDN_HEADS)
    xc = xs[0] * conv_ref[0:1, :]
    for j in range(1, CONV_W):
        xc = xc + xs[j] * conv_ref[j:j + 1, :]
    qs, ks, vs = _qkv_heads(xc)
    g, beta = _gate_lanes(sm, alog_ref, dtb_ref)
    row = lax.broadcasted_iota(jnp.int32, (c, c), 0)
    col = lax.broadcasted_iota(jnp.int32, (c, c), 1)
    causal = col <= row
    strict = col < row
    gcum = jnp.dot(jnp.where(causal, 1.0, 0.0), g, precision=HIGHEST, preferred_element_type=F32)
    gcum_t = lax.dot_general(g, jnp.where(col >= row, 1.0, 0.0), (((0,), (0,)), ((), ())),
                             precision=HIGHEST, preferred_element_type=F32)
    gc = [gcum[:, SM_A + h:SM_A + h + 1] for h in heads]
    gr = [gcum_t[SM_A + h:SM_A + h + 1, :] for h in heads]
    bc = [beta[:, SM_B + h:SM_B + h + 1] for h in heads]
    decay = [jnp.where(causal, jnp.exp(jnp.where(causal, gc[h] - gr[h], 0.0)), 0.0) for h in heads]
    kb = [ks[h] * bc[h] for h in heads]
    kbf = [ks[h].astype(BF16) for h in heads]
    a = [jnp.where(strict, _dot_nt(kb[h].astype(BF16), kbf[h]) * decay[h], 0.0) for h in heads]
    t = _unit_lower_inverses(a, c)
    eg = [jnp.exp(gc[h]) for h in heads]
    sol = [_dot(t[h].astype(BF16), jnp.concatenate([vs[h] * bc[h], kb[h] * eg[h]], axis=1).astype(BF16))
           for h in heads]
    u = [sol[h][:, :GDN_DV] for h in heads]
    w = [sol[h][:, GDN_DV:] for h in heads]
    attn = [jnp.where(causal, _dot_nt(qs[h].astype(BF16), kbf[h]) * decay[h], 0.0) for h in heads]
    qg = [qs[h] * eg[h] for h in heads]
    g_last = gcum[c - 1:c, :]
    kd = [ks[h] * jnp.exp(g_last[:, SM_A + h:SM_A + h + 1] - gc[h]) for h in heads]
    return u, w, qg, kd, attn, jnp.exp(g_last)


def _conv_shifts(hist, x):
    xe = jnp.concatenate([hist, x], axis=0)
    return [pltpu.roll(xe, d, 0)[8:] for d in range(CONV_W - 1, 0, -1)] + [x]


def _gdn_meta_kernel(x_ref, sm_ref, conv_ref, alog_ref, dtb_ref, s_out):
    xs = _conv_shifts(jnp.zeros((8, GDN_QKV), F32), x_ref[...])
    u, _, _, kd, _, _ = _gdn_chunk_terms(xs, sm_ref[...], conv_ref, alog_ref, dtb_ref, N_META)
    for h in range(GDN_HEADS):
        s_out[h] = _dot_tn(kd[h].astype(BF16), u[h].astype(BF16))


def _gdn_meta(proj, conv_w, alog_v, dtb_v, meta_row):
    full = lambda shape: pl.BlockSpec(shape, lambda i: (0,) * len(shape))
    return pl.pallas_call(
        _gdn_meta_kernel,
        out_shape=jax.ShapeDtypeStruct((GDN_HEADS, GDN_DK, GDN_DV), F32),
        grid=(1,),
        in_specs=[
            pl.BlockSpec((N_META, GDN_QKV), lambda i: (meta_row // N_META, 0)),
            pl.BlockSpec((N_META, LANES), lambda i: (meta_row // N_META, P_SMALL // LANES)),
            full((CONV_W, GDN_QKV)), full((1, LANES)), full((1, LANES)),
        ],
        out_specs=full((GDN_HEADS, GDN_DK, GDN_DV)),
        compiler_params=pltpu.CompilerParams(vmem_limit_bytes=VMEM_LIMIT),
        name="gdn_meta",
    )(proj, proj, conv_w, alog_v, dtb_v)


def _gdn_terms_kernel(x_ref, hist_ref, sm_ref, conv_ref, alog_ref, dtb_ref,
                      u_out, w_out, qg_out, kd_out, attn_out, dec_out, *, cps):
    c = CHUNK
    hist = hist_ref[...]
    for cc in range(cps):
        rows = slice(cc * c, (cc + 1) * c)
        x = x_ref[rows, :]
        u, w, qg, kd, attn, dec = _gdn_chunk_terms(_conv_shifts(hist, x), sm_ref[rows, :], conv_ref, alog_ref,
                                                   dtb_ref, c)
        for h in range(GDN_HEADS):
            sl = slice(h * GDN_DV, (h + 1) * GDN_DV)
            u_out[rows, sl] = u[h]
            w_out[rows, sl] = w[h].astype(BF16)
            qg_out[rows, sl] = qg[h].astype(BF16)
            kd_out[rows, sl] = kd[h].astype(BF16)
            attn_out[h, rows, :] = attn[h].astype(BF16)
        dec_out[cc] = dec
        hist = x[c - 8:]


def _gdn_terms(proj, conv_w, alog_v, dtb_v, batch, seq, meta_row):
    nc = seq // CHUNK
    n = batch * seq
    cps = _pick(nc, (4, 2, 1))
    step = cps * CHUNK
    spb = nc // cps
    full = lambda shape: pl.BlockSpec(shape, lambda i: (0,) * len(shape))
    rows = lambda: pl.BlockSpec((step, D_MODEL), lambda i: (i, 0))

    def hist_index(i):
        return (jnp.where(i % spb == 0, (meta_row + N_META) // 8, i * (step // 8)) - 1, 0)

    return pl.pallas_call(
        functools.partial(_gdn_terms_kernel, cps=cps),
        out_shape=(jax.ShapeDtypeStruct((n, D_MODEL), F32),
                   jax.ShapeDtypeStruct((n, D_MODEL), BF16),
                   jax.ShapeDtypeStruct((n, D_MODEL), BF16),
                   jax.ShapeDtypeStruct((n, D_MODEL), BF16),
                   jax.ShapeDtypeStruct((GDN_HEADS, n, CHUNK), BF16),
                   jax.ShapeDtypeStruct((batch * nc, 1, LANES), F32)),
        grid=(batch * spb,),
        in_specs=[
            pl.BlockSpec((step, GDN_QKV), lambda i: (i, 0)),
            pl.BlockSpec((8, GDN_QKV), hist_index),
            pl.BlockSpec((step, LANES), lambda i: (i, P_SMALL // LANES)),
            full((CONV_W, GDN_QKV)), full((1, LANES)), full((1, LANES)),
        ],
        out_specs=(rows(), rows(), rows(), rows(),
                   pl.BlockSpec((GDN_HEADS, step, CHUNK), lambda i: (0, i, 0)),
                   pl.BlockSpec((cps, 1, LANES), lambda i: (i, 0, 0))),
        compiler_params=pltpu.CompilerParams(dimension_semantics=("parallel",), vmem_limit_bytes=VMEM_LIMIT),
        name="gdn_terms",
    )(proj, proj, proj, conv_w, alog_v, dtb_v)


def _gdn_scan_kernel(u_ref, w_ref, qg_ref, kd_ref, attn_ref, dec_ref, s0_ref, o_ref, s_out, st_ref, *, cpg, nseq):
    c = CHUNK
    streams = [(q, h) for q in range(nseq) for h in range(GDN_HEADS)]
    sl = [slice(h * GDN_DV, (h + 1) * GDN_DV) for h in range(GDN_HEADS)]

    @pl.when(pl.program_id(1) == 0)
    def _():
        for q in range(nseq):
            st_ref[q] = s0_ref[...]

    def chunk(ci, carry):
        rows = pl.ds(pl.multiple_of(ci * c, c), c)
        dec = [dec_ref[q, ci] for q in range(nseq)]
        s_old = [st_ref[q, h] for q, h in streams]
        sb = [s.astype(BF16) for s in s_old]
        lhs = [jnp.concatenate([w_ref[q, rows, sl[h]], qg_ref[q, rows, sl[h]]], axis=0) for q, h in streams]
        r = [_dot(a, b) for a, b in zip(lhs, sb)]
        vnb = [(u_ref[q, rows, sl[h]] - rr[:c]).astype(BF16) for (q, h), rr in zip(streams, r)]
        out = [rr[c:] + _dot(attn_ref[h, q, rows, :], v) for (q, h), rr, v in zip(streams, r, vnb)]
        upd = [_dot_tn(kd_ref[q, rows, sl[h]], v) for (q, h), v in zip(streams, vnb)]
        for x, (q, h) in enumerate(streams):
            o_ref[q, rows, sl[h]] = out[x].astype(o_ref.dtype)
            st_ref[q, h] = s_old[x] * dec[q][:, SM_A + h:SM_A + h + 1] + upd[x]
        return carry

    lax.fori_loop(0, cpg, chunk, 0)

    @pl.when(pl.program_id(1) == pl.num_programs(1) - 1)
    def _():
        s_out[...] = st_ref[...]


def _gdn_scan(u, w, qg, kd, attn, dec, s_meta, batch, seq):
    nc = seq // CHUNK
    nseq = 2 if batch % 2 == 0 else 1
    cpg = _pick(nc, (8, 4, 2, 1))
    ng = nc // cpg
    by_seq = lambda a: a.reshape(batch, seq, D_MODEL)
    rows = lambda: pl.BlockSpec((nseq, cpg * CHUNK, D_MODEL), lambda b, g: (b, g, 0))
    kern = functools.partial(_gdn_scan_kernel, cpg=cpg, nseq=nseq)
    o, s_fin = pl.pallas_call(
        kern,
        out_shape=(jax.ShapeDtypeStruct((batch, seq, D_MODEL), BF16),
                   jax.ShapeDtypeStruct((batch, GDN_HEADS, GDN_DK, GDN_DV), F32)),
        grid=(batch // nseq, ng),
        in_specs=[rows(), rows(), rows(), rows(),
                  pl.BlockSpec((GDN_HEADS, nseq, cpg * CHUNK, CHUNK), lambda b, g: (0, b, g, 0)),
                  pl.BlockSpec((nseq, cpg, 1, LANES), lambda b, g: (b, g, 0, 0)),
                  pl.BlockSpec((GDN_HEADS, GDN_DK, GDN_DV), lambda b, g: (0, 0, 0))],
        out_specs=(rows(),
                   pl.BlockSpec((nseq, GDN_HEADS, GDN_DK, GDN_DV), lambda b, g: (b, 0, 0, 0))),
        scratch_shapes=[pltpu.VMEM((nseq, GDN_HEADS, GDN_DK, GDN_DV), F32)],
        compiler_params=pltpu.CompilerParams(
            dimension_semantics=("parallel", "arbitrary"), vmem_limit_bytes=VMEM_LIMIT),
        name="gdn_scan",
    )(by_seq(u), by_seq(w), by_seq(qg), by_seq(kd), attn.reshape(GDN_HEADS, batch, seq, CHUNK),
      dec.reshape(batch, nc, 1, LANES), s_meta)
    return o.reshape(batch * seq, D_MODEL), s_fin


def _gdn_sample_kernel(x_ref, sm_ref, cs_ref, st_ref, conv_ref, alog_ref, dtb_ref, o_ref, s_out, *, nb):
    xc = x_ref[...] * conv_ref[CONV_W - 1:CONV_W, :]
    for j in range(CONV_W - 1):
        xc = xc + cs_ref[:, j, :] * conv_ref[j:j + 1, :]
    qs, ks, vs = _qkv_heads(xc)
    g, beta = _gate_lanes(sm_ref[...], alog_ref, dtb_ref)
    eg = jnp.exp(g)
    for h in range(GDN_HEADS):
        q_t = qs[h].T
        k_t = ks[h].T
        for b in range(nb):
            kcol = k_t[:, b:b + 1]
            s1 = st_ref[b, h] * eg[b:b + 1, SM_A + h:SM_A + h + 1]
            r = jnp.sum(s1 * kcol, axis=0, keepdims=True)
            delta = (vs[h][b:b + 1, :] - r) * beta[b:b + 1, SM_B + h:SM_B + h + 1]
            s2 = s1 + kcol * delta
            s_out[b, h] = s2
            o_ref[b:b + 1, h * GDN_DV:(h + 1) * GDN_DV] = jnp.sum(s2 * q_t[:, b:b + 1], axis=0, keepdims=True)


def _gdn_sample(proj, state_conv, state_gdn, conv_w, alog_v, dtb_v, row0):
    ns = state_gdn.shape[0]
    nb = 8
    full = lambda shape: pl.BlockSpec(shape, lambda i: (0,) * len(shape))
    kern = functools.partial(_gdn_sample_kernel, nb=nb)
    return pl.pallas_call(
        kern,
        out_shape=(jax.ShapeDtypeStruct((ns, D_MODEL), F32),
                   jax.ShapeDtypeStruct(state_gdn.shape, F32)),
        grid=(ns // nb,),
        in_specs=[
            pl.BlockSpec((nb, GDN_QKV), lambda i: (row0 // nb + i, 0)),
            pl.BlockSpec((nb, LANES), lambda i: (row0 // nb + i, P_SMALL // LANES)),
            pl.BlockSpec((nb, CONV_W - 1, GDN_QKV), lambda i: (i, 0, 0)),
            pl.BlockSpec((nb, GDN_HEADS, GDN_DK, GDN_DV), lambda i: (i, 0, 0, 0)),
            full((CONV_W, GDN_QKV)), full((1, LANES)), full((1, LANES)),
        ],
        out_specs=(pl.BlockSpec((nb, D_MODEL), lambda i: (i, 0)),
                   pl.BlockSpec((nb, GDN_HEADS, GDN_DK, GDN_DV), lambda i: (i, 0, 0, 0))),
        compiler_params=pltpu.CompilerParams(dimension_semantics=("parallel",), vmem_limit_bytes=VMEM_LIMIT),
        name="gdn_sample",
    )(proj, proj, state_conv, state_gdn, conv_w, alog_v, dtb_v)


def _q_absorb_kernel(qt_ref, wukt_ref, ql_out, qrt_out, *, lane0, ns):
    for h in range(MLA_HEADS):
        qt = qt_ref[h, 0][:, lane0:lane0 + ns]
        ql_out[h] = _dot_tn(qt, wukt_ref[h]).astype(BF16)
        qrt_out[h] = qt[NOPE_DIM:NOPE_DIM + ROPE_DIM]


def _q_absorb(qt, wukt, ns, row0):
    tm = qt.shape[-1]
    assert row0 % tm + ns <= tm and (row0 % tm) % LANES == 0
    return pl.pallas_call(
        functools.partial(_q_absorb_kernel, lane0=row0 % tm, ns=ns),
        out_shape=(jax.ShapeDtypeStruct((MLA_HEADS, ns, KV_LORA), BF16),
                   jax.ShapeDtypeStruct((MLA_HEADS, ROPE_DIM, ns), BF16)),
        grid=(1,),
        in_specs=[pl.BlockSpec((MLA_HEADS, 1, LANES, tm), lambda i: (0, row0 // tm, 0, 0)),
                  pl.BlockSpec((MLA_HEADS, LANES, KV_LORA), lambda i: (0, 0, 0))],
        out_specs=(pl.BlockSpec((MLA_HEADS, ns, KV_LORA), lambda i: (0, 0, 0)),
                   pl.BlockSpec((MLA_HEADS, ROPE_DIM, ns), lambda i: (0, 0, 0))),
        compiler_params=pltpu.CompilerParams(vmem_limit_bytes=VMEM_LIMIT),
        name="q_absorb",
    )(qt, wukt)


def _mla_sample_kernel(pt_ref, ql_ref, qr_ref, cn_ref, krn_ref, cc_hbm, cr_hbm, o_ref, cbuf, rbuf, sem,
                       *, n_pages, nsub):
    g = pl.program_id(0)

    def page_copies(seq, slot, i):
        page = pt_ref[seq * n_pages + i]
        return (pltpu.make_async_copy(cc_hbm.at[page], cbuf.at[slot, i], sem.at[slot, 0]),
                pltpu.make_async_copy(cr_hbm.at[page], rbuf.at[slot, :, pl.ds(i * PAGE_SIZE, PAGE_SIZE)],
                                      sem.at[slot, 1]))

    def start_pages(seq, slot):
        for i in range(n_pages):
            cc, cr = page_copies(seq, slot, i)
            cc.start(priority=0)
            cr.start(priority=1)

    def wait_pages(seq, slot):
        for i in range(n_pages):
            cc, cr = page_copies(seq, slot, i)
            cc.wait()
            cr.wait()

    def attend(slot):
        ql = ql_ref[slot]
        qr = qr_ref[slot]
        pps = n_pages // nsub
        subs = range(nsub)
        c = [cbuf[slot, i * pps:(i + 1) * pps].reshape(pps * PAGE_SIZE, KV_LORA).astype(BF16) for i in subs]
        s = [_dot_nt(ql, c[i]) + _dot(qr, rbuf[slot, :, i * pps * PAGE_SIZE:(i + 1) * pps * PAGE_SIZE].astype(BF16))
             for i in subs]
        ms = [jnp.max(s[i], axis=1, keepdims=True) for i in subs]
        p = [jnp.exp2(s[i] - ms[i]) for i in subs]
        ls = [jnp.sum(p[i], axis=1, keepdims=True) for i in subs]
        accs = [_dot(p[i].astype(BF16), c[i]) for i in subs]
        cn = cn_ref[slot]
        krn = krn_ref[slot][:, SM_KR:SM_KR + ROPE_DIM]
        ms.append(jnp.sum(ql.astype(F32) * cn, axis=1, keepdims=True)
                  + jnp.sum(qr.astype(F32) * krn, axis=1, keepdims=True))
        ls.append(jnp.ones_like(ms[-1]))
        accs.append(jnp.broadcast_to(cn, (MLA_HEADS, KV_LORA)))
        m = functools.reduce(jnp.maximum, ms)
        scale = [jnp.exp2(m_i - m) for m_i in ms]
        l = sum(a * l_i for a, l_i in zip(scale, ls))
        acc = sum(a * acc_i for a, acc_i in zip(scale, accs))
        o_ref[slot] = acc / l

    @pl.when(g == 0)
    def _():
        start_pages(0, 0)

    start_pages(2 * g + 1, 1)
    wait_pages(2 * g, 0)
    attend(0)

    @pl.when(g + 1 < pl.num_programs(0))
    def _():
        start_pages(2 * g + 2, 0)

    wait_pages(2 * g + 1, 1)
    attend(1)


def _mla_sample(page_table, ql, qr, c_new, kr_new, cache_c, cache_r):
    ns, n_pages = page_table.shape
    assert ns % 2 == 0
    kern = functools.partial(_mla_sample_kernel, n_pages=n_pages, nsub=_pick(n_pages, (8, 4, 2, 1)))
    grid_spec = pltpu.PrefetchScalarGridSpec(
        num_scalar_prefetch=1,
        grid=(ns // 2,),
        in_specs=[
            pl.BlockSpec((2, MLA_HEADS, KV_LORA), lambda b, pt: (b, 0, 0)),
            pl.BlockSpec((2, MLA_HEADS, ROPE_DIM), lambda b, pt: (b, 0, 0)),
            pl.BlockSpec((2, 1, KV_LORA), lambda b, pt: (b, 0, 0)),
            pl.BlockSpec((2, 1, LANES), lambda b, pt: (b, 0, 0)),
            pl.BlockSpec(memory_space=pl.ANY),
            pl.BlockSpec(memory_space=pl.ANY),
        ],
        out_specs=pl.BlockSpec((2, MLA_HEADS, KV_LORA), lambda b, pt: (b, 0, 0)),
        scratch_shapes=[pltpu.VMEM((2, n_pages, PAGE_SIZE, KV_LORA), F32),
                        pltpu.VMEM((2, ROPE_DIM, n_pages * PAGE_SIZE), F32),
                        pltpu.SemaphoreType.DMA((2, 2))],
    )
    return pl.pallas_call(
        kern,
        out_shape=jax.ShapeDtypeStruct((ns, MLA_HEADS, KV_LORA), F32),
        grid_spec=grid_spec,
        compiler_params=pltpu.CompilerParams(dimension_semantics=("arbitrary",), vmem_limit_bytes=VMEM_LIMIT),
        name="mla_sample",
    )(page_table.reshape(-1), ql, qr, c_new, kr_new, cache_c, cache_r)


def _o_proj_sample_kernel(ol_ref, wv_ref, o_ref):
    for p in range(MLA_HEADS // 2):
        acc = None
        for h in (2 * p, 2 * p + 1):
            part = _dot(ol_ref[h].astype(BF16), wv_ref[:, h * LANES:(h + 1) * LANES])
            acc = part if acc is None else acc + part
        o_ref[:, p * LANES:(p + 1) * LANES] = acc.astype(o_ref.dtype)


def _o_proj_sample(o_lat_t, wv):
    ns = o_lat_t.shape[1]
    return pl.pallas_call(
        _o_proj_sample_kernel,
        out_shape=jax.ShapeDtypeStruct((ns, D_MODEL), BF16),
        grid=(1,),
        in_specs=[pl.BlockSpec((MLA_HEADS, ns, KV_LORA), lambda i: (0, 0, 0)),
                  pl.BlockSpec((KV_LORA, MLA_HEADS * LANES), lambda i: (0, 0))],
        out_specs=pl.BlockSpec((ns, D_MODEL), lambda i: (0, 0)),
        compiler_params=pltpu.CompilerParams(vmem_limit_bytes=VMEM_LIMIT),
        name="o_proj_sample",
    )(o_lat_t, wv)


def _outproj_kernel(x_ref, om_ref, og_ref, z_ref, gm_ref, gg_ref, wo_ref, gn_ref, nf_ref, wr_ref, br_ref,
                    xmid_out, hf_out, route_out):
    og = og_ref[...].astype(F32)
    parts = []
    for h in range(GDN_HEADS):
        oh = og[:, h * GDN_DV:(h + 1) * GDN_DV]
        parts.append(oh * lax.rsqrt(jnp.mean(oh * oh, axis=-1, keepdims=True) + RMS_EPS))
    o_gdn = jnp.concatenate(parts, axis=1) * gn_ref[...] * _silu(z_ref[...].astype(F32))
    merged = (_sigmoid(gm_ref[...].astype(F32)) * om_ref[...].astype(F32)
              + _sigmoid(gg_ref[...].astype(F32)) * o_gdn)
    x_mid = x_ref[...] + _dot(merged.astype(BF16), wo_ref[...])
    xmid_out[...] = x_mid
    hf = _rms(x_mid, nf_ref[...]).astype(BF16)
    bits = lax.bitcast_convert_type(hf.astype(F32), jnp.uint32)
    half = D_MODEL // 2
    hf_out[...] = bits[:, half:] | (bits[:, :half] >> 16)

    logits = _dot(hf, wr_ref[...]) + br_ref[...]
    lane = lax.broadcasted_iota(jnp.int32, logits.shape, 1)
    neg = -jnp.inf
    big = 4 * LANES
    is_g = lane < N_GROUPS
    lg = jnp.where(is_g, logits, neg)
    mg = jnp.max(lg, axis=1, keepdims=True)
    grp = jnp.min(jnp.where(lg == mg, lane, big), axis=1, keepdims=True)
    gate_g = 1.0 / jnp.sum(jnp.where(is_g, jnp.exp(logits - mg), 0.0), axis=1, keepdims=True)
    e_lane = lane - N_GROUPS
    in_grp = (e_lane >= 0) & (e_lane < N_EXPERTS) & ((e_lane >> 3) == grp)
    le = jnp.where(in_grp, logits, neg)
    v1 = jnp.max(le, axis=1, keepdims=True)
    i1 = jnp.min(jnp.where(le == v1, lane, big), axis=1, keepdims=True)
    le2 = jnp.where(lane == i1, neg, le)
    v2 = jnp.max(le2, axis=1, keepdims=True)
    i2 = jnp.min(jnp.where(le2 == v2, lane, big), axis=1, keepdims=True)
    e = jnp.exp(v2 - v1)
    w1 = gate_g / (1.0 + e)
    w2 = gate_g * e / (1.0 + e)
    route = jnp.where(lane == 0, (i1 - N_GROUPS).astype(F32),
                      jnp.where(lane == 1, (i2 - N_GROUPS).astype(F32),
                                jnp.where(lane == 2, w1, jnp.where(lane == 3, w2, 0.0))))
    route_out[...] = route


def _row_tile(n, limit):
    t = limit - limit % 16
    while t >= 16:
        if n % t == 0:
            return t
        t -= 16
    raise ValueError(f"no row tile for {n}")


def _outproj(x_all, o_mla, o_gdn, gates, w_out, gn_t, norm_ffn, w_r, b_r, row0):
    n = o_mla.shape[0]
    tm = _pick(n, (1024, 512, 256, 128, 64, 32, 16))
    assert row0 % tm == 0
    r0 = row0 // tm
    full = lambda shape: pl.BlockSpec(shape, lambda i: (0,) * len(shape))
    row = lambda w: pl.BlockSpec((tm, w), lambda i: (i, 0))
    shared = lambda w, j=0: pl.BlockSpec((tm, w), lambda i, j=j: (r0 + i, j))
    return pl.pallas_call(
        _outproj_kernel,
        out_shape=(jax.ShapeDtypeStruct((n, D_MODEL), F32),
                   jax.ShapeDtypeStruct((n, D_MODEL // 2), jnp.uint32),
                   jax.ShapeDtypeStruct((n, LANES), F32)),
        grid=(n // tm,),
        in_specs=[shared(D_MODEL), row(D_MODEL), row(D_MODEL),
                  shared(D_MODEL, P_Z // D_MODEL), shared(D_MODEL, P_GM // D_MODEL), shared(D_MODEL, P_GG // D_MODEL),
                  full((D_MODEL, D_MODEL)), full((1, D_MODEL)), full((1, D_MODEL)),
                  full((D_MODEL, LANES)), full((1, LANES))],
        out_specs=(row(D_MODEL), row(D_MODEL // 2), row(LANES)),
        compiler_params=pltpu.CompilerParams(dimension_semantics=("parallel",), vmem_limit_bytes=VMEM_LIMIT),
        name="outproj_route",
    )(x_all, o_mla, o_gdn, gates, gates, gates, w_out, gn_t, norm_ffn, w_r, b_r)


def _moe_plan(eid, wgt, nsp):
    nt = eid.shape[0]
    ts = nt // nsp
    n_asg = ts * TOP_K
    e = eid.reshape(nsp, n_asg)
    ids = jnp.broadcast_to(jnp.arange(n_asg, dtype=jnp.int32), e.shape)
    _, ids_s, w_s = lax.sort((e, ids, wgt.reshape(nsp, n_asg)), dimension=1, num_keys=1, is_stable=True)
    rows_s = (ids_s % TOP_K) * (ts + 8) + ids_s // TOP_K
    counts = jnp.sum((e[..., None] == jnp.arange(N_EXPERTS, dtype=jnp.int32)).astype(jnp.int32), axis=1)
    run_start = jnp.cumsum(counts, axis=1) - counts
    tail = lambda v, dt: jnp.full((nsp, MOE_BLOCK), v, dt)
    tok_t = jnp.concatenate([ids_s // TOP_K, tail(ts, jnp.int32)], axis=1)
    rows_t = jnp.concatenate([rows_s, tail(ts, jnp.int32)], axis=1)
    w_t = jnp.concatenate([w_s, tail(0.0, F32)], axis=1)
    as_i32 = lambda a: a.astype(jnp.int32).reshape(-1)
    scalars = (as_i32(run_start), as_i32(counts), as_i32(tok_t), as_i32(rows_t), w_t.reshape(-1))
    return scalars, ts


def _split_pieces(k, ts, n_prompt, n_sample):
    lo, hi = k * ts, (k + 1) * ts
    pieces = []
    if lo < n_prompt:
        pieces.append((0, lo, 0, min(hi, n_prompt) - lo))
    if hi > n_prompt:
        start = max(lo, n_prompt)
        pieces.append((1, start - n_prompt, start - lo, hi - start))
    assert hi <= n_prompt + n_sample
    return pieces


def _moe_kernel(start_ref, cnt_ref, tok_ref, rows_ref, w_ref, hf_p, hf_q, xmid_p, xmid_q,
                wg_ref, wu_ref, wd_ref, nfin_ref, y_p, y_q, hf_s, comb, acc, xb, yb, sem,
                *, ts, rc, nsp, n_prompt, n_sample):
    s = pl.program_id(0)
    j = pl.program_id(1)
    half = D_MODEL // 2
    stride = ts + 8

    def zero_spare_rows():
        hf_s[ts:ts + 8, :] = jnp.zeros((8, half), jnp.uint32)

    def split_copies(k, what):
        cps = []
        for src, r0, l0, n in _split_pieces(k, ts, n_prompt, n_sample):
            hbm_rows, vmem_rows = pl.ds(r0, n), pl.ds(l0, n)
            if what == 0:
                cps.append(pltpu.make_async_copy((hf_p, hf_q)[src].at[hbm_rows], hf_s.at[vmem_rows], sem.at[0, src]))
            elif what == 1:
                cps.append(pltpu.make_async_copy((xmid_p, xmid_q)[src].at[hbm_rows], acc.at[vmem_rows],
                                                 sem.at[1, src]))
            else:
                cps.append(pltpu.make_async_copy(acc.at[vmem_rows], (y_p, y_q)[src].at[hbm_rows], sem.at[2, src]))
        return cps

    def for_split(fn):
        for k in range(nsp):
            @pl.when(s == k)
            def _(k=k):
                fn(k)

    def begin_split(k):
        loads = split_copies(k, 0) + split_copies(k, 1)
        for cp in loads:
            cp.start()
        zero_spare_rows()
        for cp in split_copies(k, 0):
            cp.wait()

    def finish_split(k):
        for cp in split_copies(k, 2):
            cp.start()
        for cp in split_copies(k, 2):
            cp.wait()

    @pl.when(j == 0)
    def _():
        for_split(begin_split)

    cnt = cnt_ref[s * N_EXPERTS + j]
    first = s * (ts * TOP_K + MOE_BLOCK) + start_ref[s * N_EXPERTS + j]

    def block(p0, n):
        for r in range(n):
            xb[r:r + 1, :] = hf_s[pl.ds(tok_ref[p0 + r], 1), :]
        bits = xb[0:n, :]
        lo = lax.bitcast_convert_type(bits << 16, F32).astype(BF16)
        hi = lax.bitcast_convert_type(bits & jnp.uint32(0xFFFF0000), F32).astype(BF16)
        g = _dot(lo, wg_ref[0, :half, :]) + _dot(hi, wg_ref[0, half:, :])
        u = _dot(lo, wu_ref[0, :half, :]) + _dot(hi, wu_ref[0, half:, :])
        yb[0:n, :] = _dot((_silu(g) * u).astype(BF16), wd_ref[0])
        for r in range(n):
            comb[pl.ds(rows_ref[p0 + r], 1), :] = w_ref[p0 + r] * yb[r:r + 1, :]

    def full_block(i, carry):
        block(first + i * MOE_BLOCK, MOE_BLOCK)
        return carry

    n_full = cnt // MOE_BLOCK
    rest = cnt - n_full * MOE_BLOCK
    lax.fori_loop(0, n_full, full_block, 0)

    @pl.when(rest > MOE_BLOCK // 2)
    def _():
        block(first + n_full * MOE_BLOCK, MOE_BLOCK)

    @pl.when((rest > 0) & (rest <= MOE_BLOCK // 2))
    def _():
        block(first + n_full * MOE_BLOCK, MOE_BLOCK // 2)

    @pl.when(j == N_EXPERTS - 1)
    def _():
        for_split(lambda k: [cp.wait() for cp in split_copies(k, 1)])

        def body(i, carry):
            r0 = pl.multiple_of(i * rc, 8)
            rows = pl.ds(r0, rc)
            moe = comb[rows, :]
            for k in range(1, TOP_K):
                moe = moe + comb[pl.ds(k * stride + r0, rc), :]
            acc[rows, :] = _rms(acc[rows, :] + moe, nfin_ref[...])
            return carry

        lax.fori_loop(0, ts // rc, body, 0)
        for_split(finish_split)


def _moe(plan, hf_p, hf_q, xmid_p, xmid_q, wg, wu, wd, norm_final):
    scalars, ts = plan
    n_prompt, n_sample = xmid_p.shape[0], xmid_q.shape[0]
    nsp = (n_prompt + n_sample) // ts
    rc = _row_tile(ts, 256) if ts % 16 == 0 else 8
    kern = functools.partial(_moe_kernel, ts=ts, rc=rc, nsp=nsp, n_prompt=n_prompt, n_sample=n_sample)
    hbm = pl.BlockSpec(memory_space=pl.ANY)
    expert = lambda shape: pl.BlockSpec((1,) + shape, lambda s, j, *_: (j, 0, 0))
    grid_spec = pltpu.PrefetchScalarGridSpec(
        num_scalar_prefetch=len(scalars),
        grid=(nsp, N_EXPERTS),
        in_specs=[
            hbm, hbm, hbm, hbm,
            expert((D_MODEL, D_EXPERT)), expert((D_MODEL, D_EXPERT)), expert((D_EXPERT, D_MODEL)),
            pl.BlockSpec((1, D_MODEL), lambda s, j, *_: (0, 0)),
        ],
        out_specs=(hbm, hbm),
        scratch_shapes=[
            pltpu.VMEM((ts + 8, D_MODEL // 2), jnp.uint32),
            pltpu.VMEM((TOP_K * (ts + 8), D_MODEL), F32),
            pltpu.VMEM((ts, D_MODEL), F32),
            pltpu.VMEM((MOE_BLOCK, D_MODEL // 2), jnp.uint32),
            pltpu.VMEM((MOE_BLOCK, D_MODEL), F32),
            pltpu.SemaphoreType.DMA((3, 2)),
        ],
    )
    return pl.pallas_call(
        kern,
        out_shape=(jax.ShapeDtypeStruct((n_prompt, D_MODEL), F32), jax.ShapeDtypeStruct((n_sample, D_MODEL), F32)),
        grid_spec=grid_spec,
        compiler_params=pltpu.CompilerParams(
            dimension_semantics=("arbitrary", "arbitrary"), vmem_limit_bytes=VMEM_LIMIT),
        name="moe",
    )(*scalars, hf_p, hf_q, xmid_p, xmid_q, wg, wu, wd, norm_final)


def _pack_w_in(w):
    kr = w[:, _OFF_KV + KV_LORA:_OFF_QKV]
    kr_sw = jnp.concatenate([kr[:, ROPE_DIM // 2:], kr[:, :ROPE_DIM // 2]], axis=1)
    small = jnp.concatenate([w[:, _OFF_B:_OFF_A], w[:, _OFF_A:_OFF_GM],
                             jnp.zeros((D_MODEL, SM_KR - 2 * GDN_HEADS), w.dtype), kr, kr_sw], axis=1)
    packed = jnp.concatenate([w[:, _OFF_QKV:_OFF_Z], w[:, _OFF_KV:_OFF_KV + KV_LORA], small, w[:, :Q_LORA],
                              w[:, _OFF_Z:_OFF_B], w[:, _OFF_GM:_OFF_GG], w[:, _OFF_GG:]], axis=1)
    return packed.astype(BF16)


def _pack_mla_weights(w_uq, w_uk, w_uv):
    zq = jnp.zeros((Q_LORA, MLA_HEADS, LANES - NOPE_DIM - ROPE_DIM), w_uq.dtype)
    wq = jnp.concatenate([w_uq, zq], axis=2).reshape(Q_LORA, MLA_HEADS * LANES)
    rope = w_uq[:, :, NOPE_DIM:]
    rope_sw = jnp.concatenate([rope[..., ROPE_DIM // 2:], rope[..., :ROPE_DIM // 2]], axis=2)
    wqs = jnp.concatenate([jnp.zeros((Q_LORA, MLA_HEADS, NOPE_DIM), w_uq.dtype), rope_sw, zq], axis=2)
    wqs = wqs.reshape(Q_LORA, MLA_HEADS * LANES)
    wk = jnp.concatenate([w_uk, jnp.zeros((KV_LORA, MLA_HEADS, LANES - NOPE_DIM), w_uk.dtype)], axis=2)
    wk = wk.reshape(KV_LORA, MLA_HEADS * LANES)
    zv = jnp.zeros((KV_LORA, MLA_HEADS // 2, V_DIM), w_uv.dtype)
    wv = jnp.stack([jnp.concatenate([w_uv[:, 0::2], zv], axis=2),
                    jnp.concatenate([zv, w_uv[:, 1::2]], axis=2)], axis=2)
    wv = wv.reshape(KV_LORA, MLA_HEADS * LANES)
    return wq.astype(BF16), wqs.astype(BF16), wk.astype(BF16), wv.astype(BF16)


def _rope_tables(pos):
    inv_freq = ROPE_THETA ** (-jnp.arange(0, ROPE_DIM, 2, dtype=F32) / ROPE_DIM)
    ang = pos.astype(F32)[:, None] * inv_freq[None, :]
    cos, sin = jnp.cos(ang), jnp.sin(ang)
    n = pos.shape[0]
    cos_t = jnp.concatenate([jnp.ones((n, NOPE_DIM), F32), cos, cos, jnp.zeros((n, ROPE_DIM), F32)], axis=1)
    sin_t = jnp.concatenate([jnp.zeros((n, NOPE_DIM), F32), -sin, sin, jnp.zeros((n, ROPE_DIM), F32)], axis=1)
    return cos_t, sin_t


def _head_lanes(v):
    return jnp.zeros((1, LANES), F32).at[0, SM_A:SM_A + GDN_HEADS].set(v.astype(F32))


def _moe_splits(nt):
    for nsp in (6, 3, 4, 2, 1):
        if nt % (nsp * 8) == 0:
            return nsp
    return 1


def kernel(x_prompt, x_sample, cache_kv_latent, cache_k_rope, page_table, state_conv, state_gdn, meta_tokens,
           norm_mix, w_in, q_norm, w_uq, kv_norm, w_uk, w_uv, conv_w, a_log, dt_bias, gdn_norm, w_out, norm_ffn,
           w_group, b_group, w_router, b_router, w_gate, w_up, w_down, norm_final):
    batch, seq, _ = x_prompt.shape
    ns, dec_seq, _ = x_sample.shape
    assert dec_seq == 1 and w_in.shape[0] == 1 and seq % CHUNK == 0
    n_pages = page_table.shape[1]
    n_prompt = batch * seq
    nt = n_prompt + ns
    meta_row = nt
    tq = _pick(seq, ATTN_TQ)
    n_rows = -(-(nt + N_META) // tq) * tq
    assert n_prompt % ns == 0 and nt % N_META == 0 and ns % LANES == 0 and meta_row % tq + N_META <= tq

    x_all = jnp.concatenate([x_prompt.reshape(n_prompt, D_MODEL), x_sample.reshape(ns, D_MODEL),
                             meta_tokens.astype(x_prompt.dtype),
                             jnp.zeros((n_rows - nt - N_META, D_MODEL), x_prompt.dtype)], axis=0)
    pos = jnp.concatenate([N_META + jnp.arange(seq), jnp.full((1,), n_pages * PAGE_SIZE), jnp.arange(N_META)])
    by_row = lambda t: jnp.concatenate([jnp.tile(t[:seq], (batch, 1)), jnp.broadcast_to(t[seq], (ns, LANES)),
                                        t[seq + 1:], jnp.zeros((n_rows - nt - N_META, LANES), F32)], axis=0)
    cos_t, sin_t = (by_row(t) for t in _rope_tables(pos))
    w_packed = _pack_w_in(w_in[0])
    wq, wqs, wk, wv = _pack_mla_weights(w_uq[0], w_uk[0], w_uv[0])
    wukt = jnp.concatenate([jnp.transpose(w_uk[0], (1, 2, 0)),
                            jnp.zeros((MLA_HEADS, LANES - NOPE_DIM, KV_LORA), w_uk.dtype)], axis=1).astype(BF16)
    alog_v, dtb_v = _head_lanes(a_log[0]), _head_lanes(dt_bias[0])
    cw = conv_w[0].astype(F32)
    gn_t = jnp.tile(gdn_norm[0].astype(F32), GDN_HEADS)[None]
    w_r = jnp.concatenate([w_group[0], w_router[0],
                           jnp.zeros((D_MODEL, LANES - N_GROUPS - N_EXPERTS), w_group.dtype)], axis=1).astype(BF16)
    b_r = jnp.concatenate([b_group[0], b_router[0], jnp.zeros((LANES - N_GROUPS - N_EXPERTS,), b_group.dtype)])[None]
    wg, wu, wd = w_gate[0].astype(BF16), w_up[0].astype(BF16), w_down[0].astype(BF16)

    proj, gates = _inproj(x_all, norm_mix[0][None].astype(F32), w_packed)
    qt, k, vt, ckv, krot = _mla_prep(proj, cos_t, sin_t, q_norm[0][None].astype(F32), kv_norm[0][None].astype(F32),
                                     wq.T, wqs.T, wk, wv.T, tq)

    o_mla_p = _attn_prompt(qt, k, vt, batch, seq, meta_row, tq)
    ql, qrt = _q_absorb(qt, wukt, ns, n_prompt)
    o_lat = _mla_sample(page_table, jnp.transpose(ql, (1, 0, 2)), jnp.transpose(qrt, (2, 0, 1)),
                        ckv[n_prompt:nt].reshape(ns, 1, KV_LORA), krot[n_prompt:nt].reshape(ns, 1, LANES),
                        cache_kv_latent[0], jnp.swapaxes(cache_k_rope[0], 1, 2))
    o_mla_s = _o_proj_sample(jnp.transpose(o_lat, (1, 0, 2)), wv)

    s_meta = _gdn_meta(proj, cw, alog_v, dtb_v, meta_row)
    terms = _gdn_terms(proj, cw, alog_v, dtb_v, batch, seq, meta_row)
    o_gdn_p, gdn_p = _gdn_scan(*terms, s_meta, batch, seq)
    o_gdn_s, gdn_s = _gdn_sample(proj, state_conv[0], state_gdn[0], cw, alog_v, dtb_v, n_prompt)

    tail = (gates, w_out[0].astype(BF16), gn_t, norm_ffn[0][None].astype(F32), w_r, b_r.astype(F32))
    xmid_p, hf_p, route_p = _outproj(x_all, o_mla_p, o_gdn_p, *tail, 0)
    xmid_s, hf_s, route_s = _outproj(x_all, o_mla_s, o_gdn_s, *tail, n_prompt)
    route = jnp.concatenate([route_p[:, :2 * TOP_K], route_s[:, :2 * TOP_K]], axis=0)
    plan = _moe_plan(route[:, :TOP_K].astype(jnp.int32), route[:, TOP_K:], _moe_splits(nt))
    y_p, y_s = _moe(plan, hf_p, hf_s, xmid_p, xmid_s, wg, wu, wd, norm_final[None].astype(F32))

    def with_meta(rows, width):
        meta = jnp.broadcast_to(rows[meta_row:meta_row + N_META][None], (batch, N_META, width))
        return jnp.concatenate([meta, rows[:n_prompt].reshape(batch, seq, width)], axis=1)[None]

    k_rope = krot[:, SM_KR:SM_KR + ROPE_DIM]
    conv_p = jnp.stack([proj[(b + 1) * seq - (CONV_W - 1):(b + 1) * seq, :GDN_QKV] for b in range(batch)])
    conv_s = jnp.concatenate([state_conv[0][:, 1:].astype(F32), proj[n_prompt:nt, None, :GDN_QKV]], axis=1)
    return (y_p.reshape(batch, seq, D_MODEL), y_s.reshape(ns, 1, D_MODEL),
            with_meta(ckv, KV_LORA), with_meta(k_rope, ROPE_DIM),
            ckv[n_prompt:nt].reshape(1, ns, 1, KV_LORA), k_rope[n_prompt:nt].reshape(1, ns, 1, ROPE_DIM),
            conv_p[None], conv_s[None], gdn_p[None], gdn_s[None])
```
